```python
import math
import jax, jax.numpy as jnp
from jax import lax
import numpy as np

D_MODEL = 1024
BATCH = 2
SEQ = 8192
DEPTH = 2

GRID_W = 64
CTX_LEN = 256
N_BRANCH = 3
D_BR = D_MODEL // 2
HY_ORDER = 2
HY_BANDS = 16
HY_POS_DIM = 1 + 2 * HY_BANDS
HY_FILTER_HID = 64
HY_DECAY_SLOW = 3.07
HY_DECAY_FAST = 15.35
FN_GROUPS = 4
FN_GROUP_W = D_BR // FN_GROUPS
ML_HEADS = 4
ML_HD = D_BR // ML_HEADS
ML_CHUNK = 128
ML_FGATE_LO = 3.0
ML_FGATE_HI = 6.0
P_HY = 3 * D_BR
P_FN = D_BR
P_ML = 4 * D_BR
P_MLG = 4 * ML_HEADS
P_GATE = N_BRANCH * D_MODEL
OFF_FN = P_HY
OFF_ML = OFF_FN + P_FN
OFF_MLG = OFF_ML + P_ML
OFF_GATE = OFF_MLG + P_MLG
P_TOT = OFF_GATE + P_GATE
MOE_GROUPS = 4
MOE_PER_GROUP = 4
MOE_EXPERTS = MOE_GROUPS * MOE_PER_GROUP
MOE_TOPK = 2
EXPERT_HID = D_MODEL // 4
EPS = 1e-6

kernel_name = "hybrid_hyena_fnet_mlstm_hmoe_dit"


def rmsnorm(x, w):
    xf = x.astype(jnp.float32)
    y = xf * lax.rsqrt(jnp.mean(xf * xf, axis=-1, keepdims=True) + EPS)
    return (y * w.astype(jnp.float32)).astype(x.dtype)


def modulate(h, shift, scale):
    return h * (1 + scale) + shift


def dwconv_grid(u, w, b, grid):
    B, L, C = u.shape
    rows, width = grid
    img = u.reshape(B, rows, width, C)
    out = lax.conv_general_dilated(img, w[:, :, None, :].astype(u.dtype), (1, 1), "SAME",
                                   dimension_numbers=("NHWC", "HWIO", "NHWC"),
                                   feature_group_count=C)
    return out.reshape(B, L, C) + b.astype(u.dtype)


def hyena_filter(L, lp):
    f32 = jnp.float32
    t = jnp.arange(L, dtype=f32) / L
    bands = jnp.linspace(1e-4, HY_BANDS - 1, HY_BANDS, dtype=f32)
    ang = (2 * math.pi) * t[:, None] * bands[None, :]
    feats = jnp.concatenate([t[:, None], jnp.cos(ang), jnp.sin(ang)], axis=-1)
    freq = lp["hy_f_freq"].astype(f32)
    h = jnp.sin(freq * (feats @ lp["hy_f_w1"].astype(f32) + lp["hy_f_b1"].astype(f32)))
    h = jnp.sin(freq * (h @ lp["hy_f_w2"].astype(f32) + lp["hy_f_b2"].astype(f32)))
    h = (h @ lp["hy_f_w3"].astype(f32)).reshape(L, HY_ORDER, 2, D_BR)
    h = h * jnp.exp(-t[:, None, None, None] * jnp.abs(lp["hy_decay"].astype(f32)))
    fwd, bwd = h[:, :, 0], h[:, :, 1]
    k = jnp.concatenate([fwd, jnp.zeros_like(fwd[:1]), jnp.flip(bwd[1:], axis=0)], axis=0)
    k = k * lax.rsqrt(jnp.sum(k * k, axis=0, keepdims=True) + EPS)
    return jnp.fft.rfft(k, axis=0)


def fft_longconv(u, kf, skip):
    L = u.shape[1]
    uf = jnp.fft.rfft(u, n=2 * L, axis=1)
    y = jnp.fft.irfft(uf * kf[None], n=2 * L, axis=1)[:, :L]
    return y + u * skip


def hyena_branch(zh, grid, lp):
    L = zh.shape[1]
    u = dwconv_grid(zh, lp["hy_conv_w"], lp["hy_conv_b"], grid).astype(jnp.float32)
    v, x1, x2 = jnp.split(u, 3, axis=-1)
    kf = hyena_filter(L, lp)
    skip = lp["hy_skip"].astype(jnp.float32)
    z = x1 * fft_longconv(v, kf[:, 0], skip[0])
    return x2 * fft_longconv(z, kf[:, 1], skip[1])


def fnet_branch(zf):
    B, L, _ = zf.shape
    u = zf.astype(jnp.float32).reshape(B, L, FN_GROUPS, FN_GROUP_W)
    y = jnp.fft.fft2(u, axes=(1, 3), norm="ortho").real
    return y.reshape(B, L, D_BR)


def zero_state(batch):
    f32 = jnp.float32
    return (jnp.zeros((batch, ML_HEADS, ML_HD, ML_HD), f32),
            jnp.zeros((batch, ML_HEADS, ML_HD), f32),
            jnp.zeros((batch, ML_HEADS), f32))


def mlstm_scan(q, k, v, i_pre, f_pre, state):
    B, H, L, d = q.shape
    T = ML_CHUNK
    nc = L // T

    def chunks(a):
        return jnp.moveaxis(a.reshape((B, H, nc, T) + a.shape[3:]), 2, 0)

    q = q * (d ** -0.5)
    logf = jax.nn.log_sigmoid(f_pre)
    mask = jnp.tril(jnp.ones((T, T), dtype=bool))

    def step(carry, inp):
        C, n, m = carry
        qc, kc, vc, ic, lfc = inp
        b = jnp.cumsum(lfc, axis=-1)
        Dm = jnp.where(mask, b[..., :, None] - b[..., None, :] + ic[..., None, :], -jnp.inf)
        inter = b + m[..., None]
        m_row = jnp.maximum(inter, jnp.max(Dm, axis=-1))
        w_intra = jnp.exp(Dm - m_row[..., None])
        w_inter = jnp.exp(inter - m_row)
        s = jnp.einsum("bhtk,bhsk->bhts", qc, kc) * w_intra
        num = jnp.einsum("bhts,bhsv->bhtv", s, vc) + w_inter[..., None] * jnp.einsum("bhvk,bhtk->bhtv", C, qc)
        den = jnp.sum(s, axis=-1) + w_inter * jnp.einsum("bhk,bhtk->bht", n, qc)
        den = jnp.maximum(jnp.abs(den), jnp.exp(-m_row))
        h = num / den[..., None]
        bT = b[..., -1]
        a = bT[..., None] - b + ic
        m_new = jnp.maximum(bT + m, jnp.max(a, axis=-1))
        sc = jnp.exp(a - m_new[..., None])
        decay = jnp.exp(bT + m - m_new)
        C_new = decay[..., None, None] * C + jnp.einsum("bhs,bhsv,bhsk->bhvk", sc, vc, kc)
        n_new = decay[..., None] * n + jnp.einsum("bhs,bhsk->bhk", sc, kc)
        return (C_new, n_new, m_new), h

    state, h = lax.scan(step, state, (chunks(q), chunks(k), chunks(v), chunks(i_pre), chunks(logf)))
    return jnp.moveaxis(h, 0, 2).reshape(B, H, L, d), state


def mlstm_branch(zm, zg, grid, lp, state_f, state_b):
    B, L, _ = zm.shape
    qk = jax.nn.silu(dwconv_grid(zm[..., :2 * D_BR], lp["ml_conv_w"], lp["ml_conv_b"], grid))
    v = zm[..., 2 * D_BR:3 * D_BR]
    o = zm[..., 3 * D_BR:]

    def heads(a):
        return a.reshape(B, L, ML_HEADS, ML_HD).transpose(0, 2, 1, 3).astype(jnp.float32)

    q, k, v = heads(qk[..., :D_BR]), heads(qk[..., D_BR:]), heads(v)
    g = zg.astype(jnp.float32).reshape(B, L, 4, ML_HEADS).transpose(2, 0, 3, 1)
    h_f, st_f = mlstm_scan(q, k, v, g[0], g[1], state_f)
    rev = lambda a: jnp.flip(a, axis=2)
    h_b, st_b = mlstm_scan(rev(q), rev(k), rev(v), rev(g[2]), rev(g[3]), state_b)
    h = h_f + rev(h_b)
    h = h * lax.rsqrt(jnp.mean(h * h, axis=-1, keepdims=True) + EPS)
    h = h.transpose(0, 2, 1, 3).reshape(B, L, D_BR) * lp["ml_norm_w"].astype(jnp.float32)
    return jax.nn.sigmoid(o.astype(jnp.float32)) * h, st_f, st_b


def merge_branches(y_hy, y_fn, y_ml, gate_pre, lp):
    B, L, _ = gate_pre.shape
    dt = gate_pre.dtype
    ys = jnp.stack([y_hy, y_fn, y_ml], axis=2).astype(dt)
    proj = jnp.einsum("blkc,kcd->blkd", ys, lp["w_branch"])
    g = jax.nn.sigmoid(gate_pre.astype(jnp.float32)).reshape(B, L, N_BRANCH, D_MODEL).astype(dt)
    return jnp.sum(g * proj, axis=2) @ lp["w_out"]


def token_mixer_out(z, grid, lp, ml_out):
    y_hy = hyena_branch(z[..., :OFF_FN], grid, lp)
    y_fn = fnet_branch(z[..., OFF_FN:OFF_ML])
    return merge_branches(y_hy, y_fn, ml_out, z[..., OFF_GATE:], lp)


def hier_moe(h, lp):
    B, L, D = h.shape
    t = h.reshape(B * L, D)
    g_logits = (t @ lp["moe_rg_w"] + lp["moe_rg_b"]).astype(jnp.float32)
    p_top, g_sel = lax.top_k(jax.nn.softmax(g_logits, axis=-1), 1)
    e_logits = (t @ lp["moe_re_w"] + lp["moe_re_b"]).astype(jnp.float32).reshape(B * L, MOE_GROUPS, MOE_PER_GROUP)
    g_onehot = jax.nn.one_hot(g_sel[:, 0], MOE_GROUPS, dtype=jnp.float32)
    e_in_group = jnp.sum(e_logits * g_onehot[:, :, None], axis=1)
    e_top, e_sel = lax.top_k(e_in_group, MOE_TOPK)
    w_sel = p_top * jax.nn.softmax(e_top, axis=-1)
    e_idx = g_sel * MOE_PER_GROUP + e_sel
    combine = jnp.sum(jax.nn.one_hot(e_idx, MOE_EXPERTS, dtype=jnp.float32) * w_sel[..., None], axis=1)
    hg = jnp.einsum("nd,edh->neh", t, lp["moe_w_gate"])
    hu = jnp.einsum("nd,edh->neh", t, lp["moe_w_up"])
    a = jax.nn.silu(hg) * hu * combine[..., None].astype(t.dtype)
    return jnp.einsum("neh,ehd->nd", a, lp["moe_w_down"]).reshape(B, L, D)


def setup_inputs(seed: int = 0) -> dict:
    key = jax.random.key(seed)
    ks = iter(jax.random.split(key, 40))
    f32 = jnp.float32

    def nrm(shape, scale):
        return jax.random.normal(next(ks), shape, f32) * scale

    x = nrm((BATCH, SEQ, D_MODEL), 1.0)
    c = nrm((BATCH, D_MODEL), 1.0)
    ctx = nrm((BATCH, CTX_LEN, D_MODEL), 1.0)
    c_ctx = nrm((D_MODEL,), 1.0)
    ada_w = nrm((DEPTH, D_MODEL, 6 * D_MODEL), 0.5 * D_MODEL ** -0.5)
    ada_b = nrm((DEPTH, 6 * D_MODEL), 0.02)
    norm1_w = 1.0 + nrm((DEPTH, D_MODEL), 0.02)
    norm2_w = 1.0 + nrm((DEPTH, D_MODEL), 0.02)
    w_in = nrm((DEPTH, D_MODEL, P_TOT), D_MODEL ** -0.5)
    lin = jnp.linspace(ML_FGATE_LO, ML_FGATE_HI, ML_HEADS, dtype=f32)
    zh = jnp.zeros((ML_HEADS,), f32)
    fbias = jnp.concatenate([zh, lin, zh, lin])
    b_in = nrm((DEPTH, P_TOT), 0.02).at[:, OFF_MLG:OFF_GATE].add(fbias)
    hy_conv_w = nrm((DEPTH, 3, 3, P_HY), 1.0 / 3.0)
    hy_conv_b = nrm((DEPTH, P_HY), 0.02)
    hy_f_w1 = nrm((DEPTH, HY_POS_DIM, HY_FILTER_HID), HY_POS_DIM ** -0.5)
    hy_f_b1 = nrm((DEPTH, HY_FILTER_HID), 0.02)
    hy_f_w2 = nrm((DEPTH, HY_FILTER_HID, HY_FILTER_HID), HY_FILTER_HID ** -0.5)
    hy_f_b2 = nrm((DEPTH, HY_FILTER_HID), 0.02)
    hy_f_w3 = nrm((DEPTH, HY_FILTER_HID, HY_ORDER * 2 * D_BR), HY_FILTER_HID ** -0.5)
    hy_f_freq = 1.0 + nrm((DEPTH, HY_FILTER_HID), 0.02)
    decay_base = jnp.linspace(HY_DECAY_SLOW, HY_DECAY_FAST, D_BR, dtype=f32)
    hy_decay = decay_base[None, None, None, :] + nrm((DEPTH, HY_ORDER, 2, D_BR), 0.1)
    hy_skip = nrm((DEPTH, HY_ORDER, D_BR), 0.5)
    ml_conv_w = nrm((DEPTH, 3, 3, 2 * D_BR), 1.0 / 3.0)
    ml_conv_b = nrm((DEPTH, 2 * D_BR), 0.02)
    ml_norm_w = 1.0 + nrm((DEPTH, D_BR), 0.02)
    w_branch = nrm((DEPTH, N_BRANCH, D_BR, D_MODEL), D_BR ** -0.5)
    w_out = nrm((DEPTH, D_MODEL, D_MODEL), D_MODEL ** -0.5)
    moe_rg_w = nrm((DEPTH, D_MODEL, MOE_GROUPS), D_MODEL ** -0.5)
    moe_rg_b = nrm((DEPTH, MOE_GROUPS), 0.01)
    moe_re_w = nrm((DEPTH, D_MODEL, MOE_EXPERTS), D_MODEL ** -0.5)
    moe_re_b = nrm((DEPTH, MOE_EXPERTS), 0.01)
    moe_w_gate = nrm((DEPTH, MOE_EXPERTS, D_MODEL, EXPERT_HID), D_MODEL ** -0.5)
    moe_w_up = nrm((DEPTH, MOE_EXPERTS, D_MODEL, EXPERT_HID), D_MODEL ** -0.5)
    moe_w_down = nrm((DEPTH, MOE_EXPERTS, EXPERT_HID, D_MODEL), EXPERT_HID ** -0.5)
    norm_f_w = 1.0 + nrm((D_MODEL,), 0.02)
    return {"x": x, "c": c, "ctx": ctx, "c_ctx": c_ctx, "ada_w": ada_w, "ada_b": ada_b,
            "norm1_w": norm1_w, "norm2_w": norm2_w, "w_in": w_in, "b_in": b_in,
            "hy_conv_w": hy_conv_w, "hy_conv_b": hy_conv_b, "hy_f_w1": hy_f_w1, "hy_f_b1": hy_f_b1,
            "hy_f_w2": hy_f_w2, "hy_f_b2": hy_f_b2, "hy_f_w3": hy_f_w3, "hy_f_freq": hy_f_freq,
            "hy_decay": hy_decay, "hy_skip": hy_skip, "ml_conv_w": ml_conv_w, "ml_conv_b": ml_conv_b,
            "ml_norm_w": ml_norm_w, "w_branch": w_branch, "w_out": w_out,
            "moe_rg_w": moe_rg_w, "moe_rg_b": moe_rg_b, "moe_re_w": moe_re_w, "moe_re_b": moe_re_b,
            "moe_w_gate": moe_w_gate, "moe_w_up": moe_w_up, "moe_w_down": moe_w_down,
            "norm_f_w": norm_f_w}


def reference(x, c, ctx, c_ctx, ada_w, ada_b, norm1_w, norm2_w, w_in, b_in,
              hy_conv_w, hy_conv_b, hy_f_w1, hy_f_b1, hy_f_w2, hy_f_b2, hy_f_w3, hy_f_freq,
              hy_decay, hy_skip, ml_conv_w, ml_conv_b, ml_norm_w, w_branch, w_out,
              moe_rg_w, moe_rg_b, moe_re_w, moe_re_b, moe_w_gate, moe_w_up, moe_w_down,
              norm_f_w):
    B = x.shape[0]
    rows = x.shape[1] // GRID_W
    lat_grid = (rows, GRID_W)
    ctx_grid = (1, ctx.shape[1])
    for l in range(DEPTH):
        lp = {"hy_conv_w": hy_conv_w[l], "hy_conv_b": hy_conv_b[l], "hy_f_w1": hy_f_w1[l],
              "hy_f_b1": hy_f_b1[l], "hy_f_w2": hy_f_w2[l], "hy_f_b2": hy_f_b2[l],
              "hy_f_w3": hy_f_w3[l], "hy_f_freq": hy_f_freq[l], "hy_decay": hy_decay[l],
              "hy_skip": hy_skip[l], "ml_conv_w": ml_conv_w[l], "ml_conv_b": ml_conv_b[l],
              "ml_norm_w": ml_norm_w[l], "w_branch": w_branch[l], "w_out": w_out[l],
              "moe_rg_w": moe_rg_w[l], "moe_rg_b": moe_rg_b[l], "moe_re_w": moe_re_w[l],
              "moe_re_b": moe_re_b[l], "moe_w_gate": moe_w_gate[l], "moe_w_up": moe_w_up[l],
              "moe_w_down": moe_w_down[l]}
        last = l == DEPTH - 1
        mod_l = jnp.split((jax.nn.silu(c) @ ada_w[l] + ada_b[l])[:, None, :], 6, axis=-1)
        mod_c = jnp.split((jax.nn.silu(c_ctx) @ ada_w[l] + ada_b[l])[None, None, :], 6, axis=-1)
        zc = modulate(rmsnorm(ctx, norm1_w[l]), mod_c[0], mod_c[1]) @ w_in[l] + b_in[l]
        zl = modulate(rmsnorm(x, norm1_w[l]), mod_l[0], mod_l[1]) @ w_in[l] + b_in[l]
        st0 = zero_state(B)
        ml_c, st_f, st_b = mlstm_branch(zc[..., OFF_ML:OFF_MLG], zc[..., OFF_MLG:OFF_GATE], ctx_grid, lp, st0, st0)
        ml_l, _, _ = mlstm_branch(zl[..., OFF_ML:OFF_MLG], zl[..., OFF_MLG:OFF_GATE], lat_grid, lp, st_f, st_b)
        x = x + mod_l[2] * token_mixer_out(zl, lat_grid, lp, ml_l)
        x = x + mod_l[5] * hier_moe(modulate(rmsnorm(x, norm2_w[l]), mod_l[3], mod_l[4]), lp)
        if not last:
            ctx = ctx + mod_c[2] * token_mixer_out(zc, ctx_grid, lp, ml_c)
            ctx = ctx + mod_c[5] * hier_moe(modulate(rmsnorm(ctx, norm2_w[l]), mod_c[3], mod_c[4]), lp)
    return rmsnorm(x, norm_f_w)
```

```python
import functools
import math

import numpy as np
import jax
import jax.numpy as jnp
from jax import lax
from jax.experimental import pallas as pl
from jax.experimental.pallas import tpu as pltpu

F32 = jnp.float32
BF16 = jnp.bfloat16

D_MODEL = 1024
D_BR = 512
GRID_W = 64
LANES = 128
CB = 8
HY_ORDER = 2
HY_BANDS = 16
FN_GROUPS = 4
ML_HEADS = 4
MOE_GROUPS = 4
MOE_PER_GROUP = 4
MOE_EXPERTS = 16
EXPERT_HID = 256
EPS = 1e-6
VMEM_LIMIT = 56 * 1024 * 1024


def _params(sem):
    return pltpu.CompilerParams(dimension_semantics=sem, vmem_limit_bytes=VMEM_LIMIT)


def _bdot(a, b):
    return jnp.dot(a.astype(BF16), b.astype(BF16), preferred_element_type=F32)


def _split3(x):
    hi = x.astype(BF16)
    r1 = x - hi.astype(F32)
    mid = r1.astype(BF16)
    lo = (r1 - mid.astype(F32)).astype(BF16)
    return hi, mid, lo


def _dot_f32ish(x, w):
    xh, xm, xl = _split3(x)
    wh, wm, wl = _split3(w)
    d = lambda a, b: jnp.dot(a, b, preferred_element_type=F32)
    return (d(xh, wh) + (d(xh, wm) + d(xm, wh))) + (d(xm, wm) + d(xh, wl) + d(xl, wh))


def _swap_halves(x):
    return jnp.concatenate([x[..., LANES:], x[..., :LANES]], axis=-1)


def _mods_kernel(c_ref, w_ref, b_ref, o_ref):
    c = c_ref[...]
    s = c * jax.nn.sigmoid(c)
    o_ref[...] = _dot_f32ish(s, w_ref[...]) + b_ref[...]


def _mods(cvec, ada_w, ada_b):
    depth, d, n6 = ada_w.shape
    tn = 1536
    return pl.pallas_call(
        _mods_kernel,
        grid=(depth, n6 // tn),
        in_specs=[pl.BlockSpec((8, d), lambda l, j: (0, 0)),
                  pl.BlockSpec((None, d, tn), lambda l, j: (l, 0, j)),
                  pl.BlockSpec((None, 1, tn), lambda l, j: (l, 0, j))],
        out_specs=pl.BlockSpec((None, 8, tn), lambda l, j: (l, 0, j)),
        out_shape=jax.ShapeDtypeStruct((depth, 8, n6), F32),
        compiler_params=_params(("arbitrary", "arbitrary")),
        name="adaln_mods",
    )(cvec, ada_w, ada_b.reshape(depth, 1, n6))


def _norm_mod_kernel(x_ref, w_ref, sh_ref, sc_ref, o_ref):
    x = x_ref[...]
    y = x * lax.rsqrt(jnp.mean(x * x, axis=-1, keepdims=True) + EPS) * w_ref[...]
    o_ref[...] = (y * (1.0 + sc_ref[...]) + sh_ref[...]).astype(o_ref.dtype)


def _seg_fn(n_lat_tiles, tiles_per_batch, n_batch):
    def seg(i):
        return jnp.where(i < n_lat_tiles, i // tiles_per_batch, n_batch)
    return seg


def _norm_mod(x, w, mods3, col_shift, col_scale, seg, tm):
    nt, d = x.shape
    return pl.pallas_call(
        _norm_mod_kernel,
        grid=(nt // tm,),
        in_specs=[pl.BlockSpec((tm, d), lambda i: (i, 0)),
                  pl.BlockSpec((1, d), lambda i: (0, 0)),
                  pl.BlockSpec((None, 1, d), lambda i: (seg(i), 0, col_shift)),
                  pl.BlockSpec((None, 1, d), lambda i: (seg(i), 0, col_scale))],
        out_specs=pl.BlockSpec((tm, d), lambda i: (i, 0)),
        out_shape=jax.ShapeDtypeStruct((nt, d), BF16),
        compiler_params=_params(("arbitrary",)),
        name="norm_mod",
    )(x, w.reshape(1, d), mods3, mods3)


def _mm_tm_kernel(x_ref, w_ref, b_ref, o_ref):
    o_ref[...] = jnp.dot(x_ref[...], w_ref[...], preferred_element_type=F32) + b_ref[...]


def _pick(n, cands):
    for c in cands:
        if n % c == 0:
            return c
    raise ValueError(f"no tile for {n} in {cands}")


def _mm_tm(xn, w, b):
    nt, k = xn.shape
    n = w.shape[1]
    tm = _pick(nt, (1056, 1024, 768, 512, 256))
    tn = _pick(n, (1408, 1024, 512, 384, 256, 128))
    return pl.pallas_call(
        _mm_tm_kernel,
        grid=(n // tn, nt // tm),
        in_specs=[pl.BlockSpec((tm, k), lambda j, i: (i, 0)),
                  pl.BlockSpec((k, tn), lambda j, i: (0, j)),
                  pl.BlockSpec((1, tn), lambda j, i: (0, j))],
        out_specs=pl.BlockSpec((tm, tn), lambda j, i: (i, j)),
        out_shape=jax.ShapeDtypeStruct((nt, n), F32),
        compiler_params=_params(("arbitrary", "arbitrary")),
        name="inproj_token_major",
    )(xn, w, b.reshape(1, n))


def _mm_slab_kernel(w_ref, x_ref, b_ref, o_ref, *, slabs):
    w = w_ref[...]
    b = b_ref[...]
    for s in range(slabs):
        xs = x_ref[s * LANES:(s + 1) * LANES, :]
        o_ref[s] = lax.dot_general(w, xs, (((1,), (1,)), ((), ())), preferred_element_type=F32) + b


def _mm_slab(xn, wt, b):
    nt, k = xn.shape
    c = wt.shape[0]
    ns = nt // LANES
    ts = _pick(ns, (12, 11, 8, 6, 4, 3, 2, 1))
    tc = _pick(c, (512, 256, 128))
    return pl.pallas_call(
        functools.partial(_mm_slab_kernel, slabs=ts),
        grid=(c // tc, ns // ts),
        in_specs=[pl.BlockSpec((tc, k), lambda j, i: (j, 0)),
                  pl.BlockSpec((ts * LANES, k), lambda j, i: (i, 0)),
                  pl.BlockSpec((tc, 1), lambda j, i: (j, 0))],
        out_specs=pl.BlockSpec((ts, tc, LANES), lambda j, i: (i, j, 0)),
        out_shape=jax.ShapeDtypeStruct((ns, c, LANES), F32),
        compiler_params=_params(("arbitrary", "arbitrary")),
        name="inproj_slab",
    )(wt, xn, b.reshape(c, 1))


def _conv_taps(rows, width):
    taps = []
    for dr in (-1, 0, 1):
        if rows == 1 and dr != 0:
            continue
        for dw in (-1, 0, 1):
            taps.append((dr, dw))
    return taps


def _conv_masks(rows, width, taps):
    n = np.arange(rows * width)
    r, w = n // width, n % width
    m = np.stack([((r + dr >= 0) & (r + dr < rows) & (w + dw >= 0) & (w + dw < width)).astype(np.float32)
                  for dr, dw in taps])
    return m.reshape(len(taps), rows * width // LANES, 1, LANES)


def _conv_kernel(x_ref, w_ref, m_ref, b_ref, o_ref, *, deltas, n_slabs):
    ct = x_ref.shape[1]
    lane = lax.broadcasted_iota(jnp.int32, (ct, LANES), 1)
    bias = b_ref[...]

    def body(a, carry):
        x0 = x_ref[a]
        xm = x_ref[jnp.maximum(a - 1, 0)]
        xp = x_ref[jnp.minimum(a + 1, n_slabs - 1)]
        acc = jnp.zeros((ct, LANES), F32) + bias
        for t, delta in enumerate(deltas):
            if delta == 0:
                src = x0
            elif delta > 0:
                sh = LANES - delta
                src = jnp.where(lane < sh, pltpu.roll(x0, sh, 1), pltpu.roll(xp, sh, 1))
            else:
                sh = -delta
                src = jnp.where(lane >= sh, pltpu.roll(x0, sh, 1), pltpu.roll(xm, sh, 1))
            acc = acc + src * (w_ref[t] * m_ref[t, a])
        o_ref[a] = acc
        return carry

    lax.fori_loop(0, n_slabs, body, 0)


def _conv(z_s, w9, bias, *, rows, width, n_batch, slab0, chan_lo, chan_n):
    seq = rows * width
    a_n = seq // LANES
    taps = _conv_taps(rows, width)
    deltas = tuple(dr * width + dw for dr, dw in taps)
    assert all(abs(dl) < LANES for dl in deltas)
    tap_ids = [(dr + 1) * 3 + (dw + 1) for dr, dw in taps]
    w_t = w9[jnp.array(tap_ids)][:, :, None]
    masks = jnp.asarray(_conv_masks(rows, width, taps))
    ct = 128
    assert chan_lo % ct == 0 and chan_n % ct == 0 and slab0 % a_n == 0
    nt_ = len(taps)
    return pl.pallas_call(
        functools.partial(_conv_kernel, deltas=deltas, n_slabs=a_n),
        grid=(n_batch, chan_n // ct),
        in_specs=[pl.BlockSpec((a_n, ct, LANES), lambda b, j: (slab0 // a_n + b, chan_lo // ct + j, 0)),
                  pl.BlockSpec((nt_, ct, 1), lambda b, j: (0, j, 0)),
                  pl.BlockSpec((nt_, a_n, 1, LANES), lambda b, j: (0, 0, 0, 0)),
                  pl.BlockSpec((ct, 1), lambda b, j: (j, 0))],
        out_specs=pl.BlockSpec((a_n, ct, LANES), lambda b, j: (b, j, 0)),
        out_shape=jax.ShapeDtypeStruct((n_batch * a_n, chan_n, LANES), F32),
        compiler_params=_params(("arbitrary", "arbitrary")),
        name=f"dwconv_{rows}x{width}",
    )(z_s, w_t, masks, bias.reshape(chan_n, 1))


def _dft_consts(a_in, na):
    n = na * LANES
    k = np.arange(na)[:, None]
    a = np.arange(a_in)[None, :]
    ang = 2 * np.pi * (k * a % na) / na
    fa = np.concatenate([np.cos(ang), -np.sin(ang)], axis=0)
    r = np.arange(LANES)
    ang_t = 2 * np.pi * (np.arange(na)[:, None] * r[None, :] % n) / n
    tr, ti = np.cos(ang_t), -np.sin(ang_t)
    ta = np.concatenate([tr, tr], axis=1)
    tb = np.concatenate([-ti, ti], axis=1)
    ang2 = 2 * np.pi * (r[:, None] * r[None, :] % LANES) / LANES
    c2, s2 = np.cos(ang2), np.sin(ang2)
    g2 = np.block([[c2, -s2], [s2, c2]])
    g2i = np.block([[c2, s2], [-s2, c2]])
    ang_i = 2 * np.pi * (np.arange(a_in)[:, None] * np.arange(na)[None, :] % na) / na
    ci, si = np.cos(ang_i) / n, -np.sin(ang_i) / n
    f = lambda v, dt: jnp.asarray(v, dtype=dt)
    return dict(fa=f(fa, F32), ta=f(ta, F32), tb=f(tb, F32), g2=f(g2, F32), g2i=f(g2i, F32),
                ci=f(ci, F32), si=f(si, F32))


def _fwd_slab_stage(m, fa, ta, tb, na):
    pp = jnp.dot(fa, m.astype(BF16), preferred_element_type=F32)
    p = jnp.concatenate([pp[:na], pp[na:]], axis=1)
    return p * ta + _swap_halves(p) * tb


def _cmul(x, kf):
    kr, ki = kf[..., :LANES], kf[..., LANES:]
    ka = jnp.concatenate([kr, kr], axis=-1)
    kb = jnp.concatenate([-ki, ki], axis=-1)
    return x * ka + _swap_halves(x) * kb


def _filt_kernel(k_ref, fa_ref, ta_ref, tb_ref, g2_ref, o_ref, p_buf, *, na):
    fa, ta, tb = fa_ref[...].astype(BF16), ta_ref[...], tb_ref[...]
    for c in range(CB):
        p_buf[c] = _fwd_slab_stage(k_ref[c], fa, ta, tb, na)
    x = _bdot(p_buf[...].reshape(CB * na, 2 * LANES), g2_ref[...])
    o_ref[...] = x.reshape(CB, na, 2 * LANES)


def _filter_spectrum(k_s, na):
    nc = k_s.shape[0]
    cs = _dft_consts(na, na)
    const = lambda shp: pl.BlockSpec(shp, lambda j: (0,) * len(shp))
    return pl.pallas_call(
        functools.partial(_filt_kernel, na=na),
        grid=(nc // CB,),
        in_specs=[pl.BlockSpec((CB, na, LANES), lambda j: (j, 0, 0)),
                  const((2 * na, na)), const((na, 2 * LANES)), const((na, 2 * LANES)),
                  const((2 * LANES, 2 * LANES))],
        out_specs=pl.BlockSpec((CB, na, 2 * LANES), lambda j: (j, 0, 0)),
        out_shape=jax.ShapeDtypeStruct((nc, na, 2 * LANES), F32),
        scratch_shapes=[pltpu.VMEM((CB, na, 2 * LANES), F32)],
        compiler_params=_params(("arbitrary",)),
        name="hyena_filter_dft",
    )(k_s, cs["fa"], cs["ta"], cs["tb"], cs["g2"])


def _hyena_taps(seq, na, lp):
    n = na * LANES
    hi = lax.Precision.HIGHEST
    t = jnp.arange(seq, dtype=F32) / seq
    bands = jnp.linspace(1e-4, HY_BANDS - 1, HY_BANDS, dtype=F32)
    ang = (2 * math.pi) * t[:, None] * bands[None, :]
    feats = jnp.concatenate([t[:, None], jnp.cos(ang), jnp.sin(ang)], axis=-1)
    freq = lp["hy_f_freq"]
    h = jnp.sin(freq * (jnp.dot(feats, lp["hy_f_w1"], precision=hi) + lp["hy_f_b1"]))
    h = jnp.sin(freq * (jnp.dot(h, lp["hy_f_w2"], precision=hi) + lp["hy_f_b2"]))
    h = jnp.dot(h, lp["hy_f_w3"], precision=hi).reshape(seq, HY_ORDER, 2, D_BR)
    h = h * jnp.exp(-t[:, None, None, None] * jnp.abs(lp["hy_decay"]))
    fwd, bwd = h[:, :, 0], h[:, :, 1]
    k = jnp.concatenate([fwd, jnp.zeros((n - 2 * seq + 1, HY_ORDER, D_BR), F32), jnp.flip(bwd[1:], axis=0)], axis=0)
    k = k * lax.rsqrt(jnp.sum(k * k, axis=0, keepdims=True) + EPS)
    return jnp.transpose(k, (1, 2, 0)).reshape(HY_ORDER * D_BR, na, LANES)


def _hyena_kernel(v_ref, x1_ref, x2_ref, kf_ref, skip_ref, fa_ref, ta_ref, tb_ref, g2_ref, g2i_ref,
                  ci_ref, si_ref, o_ref, p_buf, z_buf, *, a_in, na):
    fa, ta, tb = fa_ref[...].astype(BF16), ta_ref[...], tb_ref[...]
    ci, si = ci_ref[...].astype(BF16), si_ref[...].astype(BF16)

    def spectral(order):
        x = _bdot(p_buf[...].reshape(CB * na, 2 * LANES), g2_ref[...])
        y = _cmul(x, kf_ref[order].reshape(CB * na, 2 * LANES))
        bm = _bdot(y, g2i_ref[...]).reshape(CB, na, 2 * LANES)
        p_buf[...] = bm * ta - _swap_halves(bm) * tb

    def conv_out(c):
        bb = p_buf[c]
        return (jnp.dot(ci, bb[:, :LANES].astype(BF16), preferred_element_type=F32)
                + jnp.dot(si, bb[:, LANES:].astype(BF16), preferred_element_type=F32))

    for c in range(CB):
        p_buf[c] = _fwd_slab_stage(v_ref[:, c, :], fa, ta, tb, na)
    spectral(0)
    for c in range(CB):
        z = x1_ref[:, c, :] * (conv_out(c) + v_ref[:, c, :] * skip_ref[0, c])
        z_buf[c] = z
    for c in range(CB):
        p_buf[c] = _fwd_slab_stage(z_buf[c], fa, ta, tb, na)
    spectral(1)
    for c in range(CB):
        o_ref[:, c, :] = x2_ref[:, c, :] * (conv_out(c) + z_buf[c] * skip_ref[1, c])


def _hyena(u_s, kf, skip, *, a_in, na, n_batch):
    cs = _dft_consts(a_in, na)
    nblk = D_BR // CB
    const = lambda shp: pl.BlockSpec(shp, lambda b, j: (0,) * len(shp))
    skip_b = jnp.broadcast_to(skip[:, :, None, None], (HY_ORDER, D_BR, 1, LANES))
    return pl.pallas_call(
        functools.partial(_hyena_kernel, a_in=a_in, na=na),
        grid=(n_batch, nblk),
        in_specs=[pl.BlockSpec((a_in, CB, LANES), lambda b, j: (b, j, 0)),
                  pl.BlockSpec((a_in, CB, LANES), lambda b, j: (b, nblk + j, 0)),
                  pl.BlockSpec((a_in, CB, LANES), lambda b, j: (b, 2 * nblk + j, 0)),
                  pl.BlockSpec((HY_ORDER, CB, na, 2 * LANES), lambda b, j: (0, j, 0, 0)),
                  pl.BlockSpec((HY_ORDER, CB, 1, LANES), lambda b, j: (0, j, 0, 0)),
                  const((2 * na, a_in)), const((na, 2 * LANES)), const((na, 2 * LANES)),
                  const((2 * LANES, 2 * LANES)), const((2 * LANES, 2 * LANES)),
                  const((a_in, na)), const((a_in, na))],
        out_specs=pl.BlockSpec((a_in, CB, LANES), lambda b, j: (b, j, 0)),
        out_shape=jax.ShapeDtypeStruct((n_batch * a_in, D_BR, LANES), F32),
        scratch_shapes=[pltpu.VMEM((CB, na, 2 * LANES), F32), pltpu.VMEM((CB, a_in, LANES), F32)],
        compiler_params=_params(("arbitrary", "arbitrary")),
        name=f"hyena_longconv_{a_in}",
    )(u_s, u_s, u_s, kf, skip_b, cs["fa"], cs["ta"], cs["tb"], cs["g2"], cs["g2i"], cs["ci"], cs["si"])


def _chan_dft_mats():
    r = np.arange(LANES)
    ang = 2 * np.pi * (r[:, None] * r[None, :] % LANES) / LANES
    return np.cos(ang), np.sin(ang)


def _fn_mix_kernel(u_ref, cs_ref, o_ref, *, n_slabs):
    w = cs_ref[...].astype(BF16)

    def body(a, carry):
        o_ref[a] = jnp.dot(w, u_ref[a].astype(BF16), preferred_element_type=F32)
        return carry

    lax.fori_loop(0, n_slabs, body, 0)


def _fn_mix(z_s, *, a_n, n_batch, chan_lo):
    c, s = _chan_dft_mats()
    w = jnp.asarray(np.concatenate([c, s], axis=0), dtype=F32)
    g0 = chan_lo // LANES
    return pl.pallas_call(
        functools.partial(_fn_mix_kernel, n_slabs=a_n),
        grid=(n_batch, FN_GROUPS),
        in_specs=[pl.BlockSpec((a_n, LANES, LANES), lambda b, g: (b, g0 + g, 0)),
                  pl.BlockSpec((2 * LANES, LANES), lambda b, g: (0, 0))],
        out_specs=pl.BlockSpec((a_n, 2 * LANES, LANES), lambda b, g: (b, g, 0)),
        out_shape=jax.ShapeDtypeStruct((n_batch * a_n, 2 * D_BR, LANES), F32),
        compiler_params=_params(("arbitrary", "arbitrary")),
        name="fnet_channel_dft",
    )(z_s, w)


def _fn_seq_kernel(p_ref, q_ref, fa_ref, tr_ref, ti_ref, g_ref, o_ref, a_buf, *, a_n, scale):
    fa, tr, ti = fa_ref[...].astype(BF16), tr_ref[...], ti_ref[...]
    for c in range(CB):
        r1 = jnp.dot(fa, p_ref[:, c, :].astype(BF16), preferred_element_type=F32)
        r2 = jnp.dot(fa, q_ref[:, c, :].astype(BF16), preferred_element_type=F32)
        ar = r1[:a_n] - r2[a_n:]
        ai = -(r2[:a_n] + r1[a_n:])
        a_buf[c] = jnp.concatenate([ar * tr - ai * ti, ar * ti + ai * tr], axis=1)
    y = _bdot(a_buf[...].reshape(CB * a_n, 2 * LANES), g_ref[...]) * scale
    o_ref[...] = y.reshape(CB, a_n, LANES)


def _fn_seq(pq, *, a_n, n_batch):
    seq = a_n * LANES
    k = np.arange(a_n)
    ang = 2 * np.pi * (k[:, None] * k[None, :] % a_n) / a_n
    fa = np.concatenate([np.cos(ang), np.sin(ang)], axis=0)
    r = np.arange(LANES)
    ang_t = 2 * np.pi * (k[:, None] * r[None, :] % seq) / seq
    c2, s2 = _chan_dft_mats()
    g = np.concatenate([c2, s2], axis=0)
    nblk = LANES // CB
    const = lambda shp: pl.BlockSpec(shp, lambda b, j: (0,) * len(shp))

    def chan_blk(j, off):
        return (j // nblk) * (2 * nblk) + off * nblk + j % nblk

    return pl.pallas_call(
        functools.partial(_fn_seq_kernel, a_n=a_n, scale=1.0 / math.sqrt(seq * LANES)),
        grid=(n_batch, D_BR // CB),
        in_specs=[pl.BlockSpec((a_n, CB, LANES), lambda b, j: (b, chan_blk(j, 0), 0)),
                  pl.BlockSpec((a_n, CB, LANES), lambda b, j: (b, chan_blk(j, 1), 0)),
                  const((2 * a_n, a_n)), const((a_n, LANES)), const((a_n, LANES)), const((2 * LANES, LANES))],
        out_specs=pl.BlockSpec((None, CB, a_n, LANES), lambda b, j: (b, j, 0, 0)),
        out_shape=jax.ShapeDtypeStruct((n_batch, D_BR, a_n, LANES), F32),
        scratch_shapes=[pltpu.VMEM((CB, a_n, 2 * LANES), F32)],
        compiler_params=_params(("arbitrary", "arbitrary")),
        name="fnet_sequence_dft",
    )(pq, pq, jnp.asarray(fa, F32), jnp.asarray(np.cos(ang_t), F32), jnp.asarray(-np.sin(ang_t), F32),
      jnp.asarray(g, F32))


def _fn_small_kernel(u_ref, cw_ref, sw_ref, cl_ref, sl_ref, o_ref, *, a_n, scale):
    u = jnp.concatenate([u_ref[a].T for a in range(a_n)], axis=0)
    p = _bdot(u, cw_ref[...])
    q = _bdot(u, sw_ref[...])
    o_ref[...] = (_bdot(cl_ref[...], p) - _bdot(sl_ref[...], q)) * scale


def _fn_small(z_s, *, a_n, n_batch, slab0, chan_lo):
    seq = a_n * LANES
    cw, sw = _chan_dft_mats()
    n = np.arange(seq)
    ang = 2 * np.pi * (n[:, None] * n[None, :] % seq) / seq
    const = lambda shp: pl.BlockSpec(shp, lambda b, g: (0,) * len(shp))
    g0 = chan_lo // LANES
    return pl.pallas_call(
        functools.partial(_fn_small_kernel, a_n=a_n, scale=1.0 / math.sqrt(seq * LANES)),
        grid=(n_batch, FN_GROUPS),
        in_specs=[pl.BlockSpec((a_n, LANES, LANES), lambda b, g: (slab0 // a_n + b, g0 + g, 0)),
                  const((LANES, LANES)), const((LANES, LANES)), const((seq, seq)), const((seq, seq))],
        out_specs=pl.BlockSpec((None, seq, LANES), lambda b, g: (b, 0, g)),
        out_shape=jax.ShapeDtypeStruct((n_batch, seq, D_BR), F32),
        compiler_params=_params(("arbitrary", "arbitrary")),
        name="fnet_short",
    )(z_s, jnp.asarray(cw, F32), jnp.asarray(sw, F32), jnp.asarray(np.cos(ang), F32), jnp.asarray(np.sin(ang), F32))


def _log_sigmoid(x):
    return jnp.minimum(x, 0.0) - jnp.log(1.0 + jnp.exp(-jnp.abs(x)))


def _exact_tri_dot(tri, x, tri_on_left):
    h, m, l = _split3(x)
    if tri_on_left:
        d = lambda p: jnp.dot(tri, p, preferred_element_type=F32)
    else:
        d = lambda p: jnp.dot(p, tri, preferred_element_type=F32)
    return d(h) + d(m) + d(l)


def _mlstm_dir(q_ref, k_ref, v_ref, g_ref, gt_ref, h_ref, c_st, n_st, m_st, *, d, reverse):
    t = LANES
    hd = LANES
    row = lax.broadcasted_iota(jnp.int32, (t, t), 0)
    col = lax.broadcasted_iota(jnp.int32, (t, t), 1)
    lower = (col <= row)
    tri = jnp.where(lower, 1.0, 0.0).astype(BF16)
    tri_t = jnp.where(col >= row, 1.0, 0.0).astype(BF16)
    mask = (col >= row) if reverse else lower
    g = g_ref[...]
    gt = gt_ref[...]
    lf_c = _log_sigmoid(g)
    lf_r = _log_sigmoid(gt)
    if reverse:
        b_c = _exact_tri_dot(tri_t, lf_c, True)
        b_r = _exact_tri_dot(tri, lf_r, False)
    else:
        b_c = _exact_tri_dot(tri, lf_c, True)
        b_r = _exact_tri_dot(tri_t, lf_r, False)
    i_off = 2 * ML_HEADS * d
    f_off = i_off + ML_HEADS
    for h in range(ML_HEADS):
        sl = slice(h * hd, (h + 1) * hd)
        q = q_ref[:, sl]
        q = (q * jax.nn.sigmoid(q)) * (hd ** -0.5)
        k = k_ref[:, sl]
        k = k * jax.nn.sigmoid(k)
        v = v_ref[:, sl]
        bc = b_c[:, f_off + h:f_off + h + 1]
        br = b_r[f_off + h:f_off + h + 1, :]
        ic = g[:, i_off + h:i_off + h + 1]
        ir = gt[i_off + h:i_off + h + 1, :]
        m_prev = m_st[h][:, :1]
        ct = c_st[h]
        n_prev = n_st[h]
        dm = jnp.where(mask, bc - br + ir, -jnp.inf)
        inter = bc + m_prev
        m_row = jnp.maximum(inter, jnp.max(dm, axis=-1, keepdims=True))
        w_intra = jnp.exp(dm - m_row)
        w_inter = jnp.exp(inter - m_row)
        qb, kb, vb = q.astype(BF16), k.astype(BF16), v.astype(BF16)
        s = lax.dot_general(qb, kb, (((1,), (1,)), ((), ())), preferred_element_type=F32) * w_intra
        num = jnp.dot(s.astype(BF16), vb, preferred_element_type=F32) + w_inter * jnp.dot(qb, ct.astype(BF16), preferred_element_type=F32)
        den = jnp.sum(s, axis=-1, keepdims=True) + w_inter * jnp.sum(q * n_prev, axis=-1, keepdims=True)
        den = jnp.maximum(jnp.abs(den), jnp.exp(-m_row))
        h_ref[:, sl] = num / den
        if reverse:
            b_tot_c, b_tot_r = bc[:1, :], br[:, :1]
        else:
            b_tot_c, b_tot_r = bc[t - 1:, :], br[:, t - 1:]
        a_c = b_tot_c - bc + ic
        a_r = b_tot_r - br + ir
        m_new = jnp.maximum(b_tot_c + m_prev, jnp.max(a_r, axis=-1, keepdims=True))
        sc = jnp.exp(a_c - m_new)
        decay = jnp.exp(b_tot_c + m_prev - m_new)
        ks = k * sc
        c_st[h] = decay * ct + lax.dot_general(ks.astype(BF16), vb, (((0,), (0,)), ((), ())), preferred_element_type=F32)
        n_st[h] = decay * n_prev + jnp.sum(ks, axis=0, keepdims=True)
        m_st[h] = jnp.broadcast_to(m_new, (1, LANES))


def _mlstm_kernel(qf, kf, vf, gf, gtf, qb, kb, vb, gb, gtb, hf_ref, hb_ref, c_st, n_st, m_st):
    @pl.when(pl.program_id(1) == 0)
    def _():
        c_st[...] = jnp.zeros(c_st.shape, F32)
        n_st[...] = jnp.zeros(n_st.shape, F32)
        m_st[...] = jnp.zeros(m_st.shape, F32)

    _mlstm_dir(qf, kf, vf, gf, gtf, hf_ref, c_st.at[0], n_st.at[0], m_st.at[0], d=0, reverse=False)
    _mlstm_dir(qb, kb, vb, gb, gtb, hb_ref, c_st.at[1], n_st.at[1], m_st.at[1], d=1, reverse=True)


def _mlstm(qk_tm, z_tm, gt, *, n_batch, lat_chunks, ctx_chunks, v_col, g_col):
    nt = qk_tm.shape[0]
    n_lat = n_batch * lat_chunks
    nsteps = ctx_chunks + lat_chunks

    def glob(b, p):
        return jnp.where(p < ctx_chunks, n_lat + b * ctx_chunks + p, b * lat_chunks + p - ctx_chunks)

    def fwd(b, j):
        return glob(b, j)

    def bwd(b, j):
        p = jnp.where(j < ctx_chunks, ctx_chunks - 1 - j, ctx_chunks + lat_chunks - 1 - (j - ctx_chunks))
        return glob(b, p)

    def specs(ix):
        return [pl.BlockSpec((LANES, D_BR), lambda b, j: (ix(b, j), 0)),
                pl.BlockSpec((LANES, D_BR), lambda b, j: (ix(b, j), 1)),
                pl.BlockSpec((LANES, D_BR), lambda b, j: (ix(b, j), v_col)),
                pl.BlockSpec((LANES, LANES), lambda b, j: (ix(b, j), g_col)),
                pl.BlockSpec((16, LANES), lambda b, j: (0, ix(b, j)))]

    out_sd = jax.ShapeDtypeStruct((nt, D_BR), F32)
    return pl.pallas_call(
        _mlstm_kernel,
        grid=(n_batch, nsteps),
        in_specs=specs(fwd) + specs(bwd),
        out_specs=[pl.BlockSpec((LANES, D_BR), lambda b, j: (fwd(b, j), 0)),
                   pl.BlockSpec((LANES, D_BR), lambda b, j: (bwd(b, j), 0))],
        out_shape=[out_sd, out_sd],
        scratch_shapes=[pltpu.VMEM((2, ML_HEADS, LANES, LANES), F32),
                        pltpu.VMEM((2, ML_HEADS, 1, LANES), F32),
                        pltpu.VMEM((2, ML_HEADS, 1, LANES), F32)],
        compiler_params=_params(("arbitrary", "arbitrary")),
        name="mlstm_bidir",
    )(qk_tm, qk_tm, z_tm, z_tm, gt, qk_tm, qk_tm, z_tm, z_tm, gt)


def _merge_kernel(yh_ref, yf_ref, hf_ref, hb_ref, o_ref, g0_ref, g1_ref, g2_ref, x_ref, gate_ref,
                  wb_ref, wo_ref, nw_ref, out_ref):
    hd = LANES
    h = hf_ref[...] + hb_ref[...]
    parts = []
    for i in range(ML_HEADS):
        hh = h[:, i * hd:(i + 1) * hd]
        parts.append(hh * lax.rsqrt(jnp.mean(hh * hh, axis=-1, keepdims=True) + EPS))
    y_ml = jax.nn.sigmoid(o_ref[...]) * (jnp.concatenate(parts, axis=1) * nw_ref[...])
    acc = jax.nn.sigmoid(g0_ref[...]) * _bdot(yh_ref[...], wb_ref[0])
    acc = acc + jax.nn.sigmoid(g1_ref[...]) * _bdot(yf_ref[...], wb_ref[1])
    acc = acc + jax.nn.sigmoid(g2_ref[...]) * _bdot(y_ml, wb_ref[2])
    out_ref[...] = x_ref[...] + gate_ref[...] * _bdot(acc, wo_ref[...])


def _merge(yh, yf, hf, hb, z_tm, x, mods3, wb, wo, nw, *, seg, tm, o_col, gate_col0):
    nt, d = x.shape
    tok = lambda w, cidx: pl.BlockSpec((tm, w), lambda i: (i, cidx))
    return pl.pallas_call(
        _merge_kernel,
        grid=(nt // tm,),
        in_specs=[tok(D_BR, 0), tok(D_BR, 0), tok(D_BR, 0), tok(D_BR, 0),
                  tok(D_BR, o_col),
                  tok(d, gate_col0), tok(d, gate_col0 + 1), tok(d, gate_col0 + 2),
                  tok(d, 0),
                  pl.BlockSpec((None, 1, d), lambda i: (seg(i), 0, 2)),
                  pl.BlockSpec((3, D_BR, d), lambda i: (0, 0, 0)),
                  pl.BlockSpec((d, d), lambda i: (0, 0)),
                  pl.BlockSpec((1, D_BR), lambda i: (0, 0))],
        out_specs=pl.BlockSpec((tm, d), lambda i: (i, 0)),
        out_shape=jax.ShapeDtypeStruct((nt, d), F32),
        compiler_params=_params(("arbitrary",)),
        name="merge_branches",
    )(yh, yf, hf, hb, z_tm, z_tm, z_tm, z_tm, x, mods3, wb, wo, nw.reshape(1, D_BR))


def _router_kernel(x_ref, w_ref, sh_ref, sc_ref, rw_ref, rb_ref, xn_ref, comb_ref):
    x = x_ref[...]
    y = x * lax.rsqrt(jnp.mean(x * x, axis=-1, keepdims=True) + EPS) * w_ref[...]
    t = y * (1.0 + sc_ref[...]) + sh_ref[...]
    xn_ref[...] = t.astype(BF16)
    logits = _dot_f32ish(t, rw_ref[...]) + rb_ref[...]
    col = lax.broadcasted_iota(jnp.int32, logits.shape, 1)
    big = jnp.int32(1 << 20)
    ninf = -jnp.inf
    is_g = col < MOE_GROUPS
    gl = jnp.where(is_g, logits, ninf)
    gmax = jnp.max(gl, axis=-1, keepdims=True)
    g_sel = jnp.min(jnp.where(is_g & (gl == gmax), col, big), axis=-1, keepdims=True)
    p_top = 1.0 / jnp.sum(jnp.where(is_g, jnp.exp(gl - gmax), 0.0), axis=-1, keepdims=True)
    lo = MOE_GROUPS + g_sel * MOE_PER_GROUP
    in_grp = (col >= lo) & (col < lo + MOE_PER_GROUP)
    e1v = jnp.where(in_grp, logits, ninf)
    top1 = jnp.max(e1v, axis=-1, keepdims=True)
    idx1 = jnp.min(jnp.where(in_grp & (e1v == top1), col, big), axis=-1, keepdims=True)
    e2v = jnp.where(col == idx1, ninf, e1v)
    top2 = jnp.max(e2v, axis=-1, keepdims=True)
    idx2 = jnp.min(jnp.where(in_grp & (col != idx1) & (e2v == top2), col, big), axis=-1, keepdims=True)
    ex = jnp.exp(top2 - top1)
    s1 = 1.0 / (1.0 + ex)
    comb_ref[...] = jnp.where(col == idx1, p_top * s1, 0.0) + jnp.where(col == idx2, p_top * (ex * s1), 0.0)


def _router(x, w, mods3, rw, rb, *, seg, tm):
    nt, d = x.shape
    return pl.pallas_call(
        _router_kernel,
        grid=(nt // tm,),
        in_specs=[pl.BlockSpec((tm, d), lambda i: (i, 0)),
                  pl.BlockSpec((1, d), lambda i: (0, 0)),
                  pl.BlockSpec((None, 1, d), lambda i: (seg(i), 0, 3)),
                  pl.BlockSpec((None, 1, d), lambda i: (seg(i), 0, 4)),
                  pl.BlockSpec((d, LANES), lambda i: (0, 0)),
                  pl.BlockSpec((1, LANES), lambda i: (0, 0))],
        out_specs=[pl.BlockSpec((tm, d), lambda i: (i, 0)), pl.BlockSpec((tm, LANES), lambda i: (i, 0))],
        out_shape=[jax.ShapeDtypeStruct((nt, d), BF16), jax.ShapeDtypeStruct((nt, LANES), F32)],
        compiler_params=_params(("arbitrary",)),
        name="norm_router",
    )(x, w.reshape(1, d), mods3, mods3, rw, rb)


def _moe_kernel(xn_ref, comb_ref, wg_ref, wu_ref, wd_ref, x_ref, gate_ref, o_ref, acc_ref):
    e = pl.program_id(1)

    @pl.when(e == 0)
    def _():
        acc_ref[...] = jnp.zeros(acc_ref.shape, F32)

    xn = xn_ref[...]
    comb = comb_ref[...]
    col = lax.broadcasted_iota(jnp.int32, comb.shape, 1)
    cw = jnp.sum(jnp.where(col == e + MOE_GROUPS, comb, 0.0), axis=-1, keepdims=True)
    hg = jnp.dot(xn, wg_ref[...], preferred_element_type=F32)
    hu = jnp.dot(xn, wu_ref[...], preferred_element_type=F32)
    a = (hg * jax.nn.sigmoid(hg)) * hu * cw
    acc_ref[...] += jnp.dot(a.astype(BF16), wd_ref[...], preferred_element_type=F32)

    @pl.when(e == MOE_EXPERTS - 1)
    def _():
        o_ref[...] = x_ref[...] + gate_ref[...] * acc_ref[...]


def _moe(xn, comb, wg, wu, wd, x, mods3, *, seg, tm):
    nt, d = x.shape
    return pl.pallas_call(
        _moe_kernel,
        grid=(nt // tm, MOE_EXPERTS),
        in_specs=[pl.BlockSpec((tm, d), lambda i, e: (i, 0)),
                  pl.BlockSpec((tm, LANES), lambda i, e: (i, 0)),
                  pl.BlockSpec((None, d, EXPERT_HID), lambda i, e: (e, 0, 0)),
                  pl.BlockSpec((None, d, EXPERT_HID), lambda i, e: (e, 0, 0)),
                  pl.BlockSpec((None, EXPERT_HID, d), lambda i, e: (e, 0, 0)),
                  pl.BlockSpec((tm, d), lambda i, e: (i, 0)),
                  pl.BlockSpec((None, 1, d), lambda i, e: (seg(i), 0, 5))],
        out_specs=pl.BlockSpec((tm, d), lambda i, e: (i, 0)),
        out_shape=jax.ShapeDtypeStruct((nt, d), F32),
        scratch_shapes=[pltpu.VMEM((tm, d), F32)],
        compiler_params=_params(("arbitrary", "arbitrary")),
        name="moe_experts",
    )(xn, comb, wg, wu, wd, x, mods3)


def _final_norm_kernel(x_ref, w_ref, o_ref):
    x = x_ref[...]
    o_ref[...] = x * lax.rsqrt(jnp.mean(x * x, axis=-1, keepdims=True) + EPS) * w_ref[...]


def _final_norm(x, w, n_rows, tm):
    d = x.shape[1]
    return pl.pallas_call(
        _final_norm_kernel,
        grid=(n_rows // tm,),
        in_specs=[pl.BlockSpec((tm, d), lambda i: (i, 0)), pl.BlockSpec((1, d), lambda i: (0, 0))],
        out_specs=pl.BlockSpec((tm, d), lambda i: (i, 0)),
        out_shape=jax.ShapeDtypeStruct((n_rows, d), F32),
        compiler_params=_params(("arbitrary",)),
        name="final_norm",
    )(x, w.reshape(1, d))


def _slab_to_tm(y_s):
    ns, c, _ = y_s.shape
    return jnp.transpose(y_s, (0, 2, 1)).reshape(ns * LANES, c)


def kernel(x, c, ctx, c_ctx, ada_w, ada_b, norm1_w, norm2_w, w_in, b_in, hy_conv_w, hy_conv_b, hy_f_w1, hy_f_b1, hy_f_w2, hy_f_b2, hy_f_w3, hy_f_freq, hy_decay, hy_skip, ml_conv_w, ml_conv_b, ml_norm_w, w_branch, w_out, moe_rg_w, moe_rg_b, moe_re_w, moe_re_b, moe_w_gate, moe_w_up, moe_w_down, norm_f_w):
    nb, seq, d = x.shape
    lc = ctx.shape[1]
    depth = ada_w.shape[0]
    assert d == D_MODEL and seq % (GRID_W * 2) == 0 and lc % LANES == 0 and nb + 1 <= 8
    rows = seq // GRID_W
    a_lat = seq // LANES
    a_ctx = lc // LANES
    n_lat = nb * seq
    nt = n_lat + nb * lc
    tm = 256
    assert seq % tm == 0 and (nb * lc) % tm == 0
    seg = _seg_fn(n_lat // tm, seq // tm, nb)
    tm_moe = 512 if (seq % 512 == 0 and (nb * lc) % 512 == 0) else tm
    seg_moe = _seg_fn(n_lat // tm_moe, seq // tm_moe, nb)
    a_ctx_pad = 16
    assert a_ctx <= a_ctx_pad // 2 or a_ctx == a_ctx_pad

    xs = jnp.concatenate([x.reshape(n_lat, d), ctx.reshape(nb * lc, d)], axis=0)
    cvec = jnp.zeros((8, d), F32).at[:nb].set(c).at[nb].set(c_ctx)
    mods = _mods(cvec, ada_w, ada_b)

    o_fn, o_ml, o_mlg, o_gate = 3 * D_BR, 4 * D_BR, 8 * D_BR, 8 * D_BR + 4 * ML_HEADS
    pad_g = LANES - 4 * ML_HEADS

    for l in range(depth):
        lp = {"hy_f_w1": hy_f_w1[l], "hy_f_b1": hy_f_b1[l], "hy_f_w2": hy_f_w2[l], "hy_f_b2": hy_f_b2[l],
              "hy_f_w3": hy_f_w3[l], "hy_f_freq": hy_f_freq[l], "hy_decay": hy_decay[l]}
        mods3 = mods[l].reshape(8, 1, 6 * d)
        wl, bl = w_in[l], b_in[l]
        w_cm = jnp.concatenate([wl[:, :o_fn], wl[:, o_ml:o_ml + 2 * D_BR], wl[:, o_fn:o_ml]], axis=1)
        b_cm = jnp.concatenate([bl[:o_fn], bl[o_ml:o_ml + 2 * D_BR], bl[o_fn:o_ml]])
        w_tm = jnp.concatenate([wl[:, o_ml + 2 * D_BR:o_mlg], wl[:, o_gate:], wl[:, o_mlg:o_gate],
                                jnp.zeros((d, pad_g), F32)], axis=1)
        b_tm = jnp.concatenate([bl[o_ml + 2 * D_BR:o_mlg], bl[o_gate:], bl[o_mlg:o_gate], jnp.zeros((pad_g,), F32)])
        c_hy, c_qk, c_fn = 0, 3 * D_BR, 5 * D_BR
        g_col = (2 * D_BR + 3 * d) // LANES

        xn = _norm_mod(xs, norm1_w[l], mods3, 0, 1, seg, tm)
        z_tm = _mm_tm(xn, w_tm.astype(BF16), b_tm)
        z_s = _mm_slab(xn, w_cm.T.astype(BF16), b_cm)

        conv_w = jnp.concatenate([hy_conv_w[l].reshape(9, 3 * D_BR), ml_conv_w[l].reshape(9, 2 * D_BR)], axis=1)
        conv_b = jnp.concatenate([hy_conv_b[l], ml_conv_b[l]])
        u_lat = _conv(z_s, conv_w, conv_b, rows=rows, width=GRID_W, n_batch=nb, slab0=0,
                      chan_lo=0, chan_n=5 * D_BR)
        u_ctx = _conv(z_s, conv_w, conv_b, rows=1, width=lc, n_batch=nb, slab0=nb * a_lat,
                      chan_lo=0, chan_n=5 * D_BR)

        kf_lat = _filter_spectrum(_hyena_taps(seq, 2 * a_lat, lp), 2 * a_lat).reshape(HY_ORDER, D_BR, 2 * a_lat, 2 * LANES)
        yh_lat = _hyena(u_lat, kf_lat, hy_skip[l], a_in=a_lat, na=2 * a_lat, n_batch=nb)
        kf_ctx = _filter_spectrum(_hyena_taps(lc, 2 * a_ctx_pad, lp), 2 * a_ctx_pad).reshape(HY_ORDER, D_BR, 2 * a_ctx_pad, 2 * LANES)
        u_ctx_p = jnp.pad(u_ctx[:, :3 * D_BR].reshape(nb, a_ctx, 3 * D_BR, LANES),
                          ((0, 0), (0, a_ctx_pad - a_ctx), (0, 0), (0, 0))).reshape(nb * a_ctx_pad, 3 * D_BR, LANES)
        yh_ctx = _hyena(u_ctx_p, kf_ctx, hy_skip[l], a_in=a_ctx_pad, na=2 * a_ctx_pad, n_batch=nb)
        yh_ctx = yh_ctx.reshape(nb, a_ctx_pad, D_BR, LANES)[:, :a_ctx].reshape(nb * a_ctx, D_BR, LANES)
        yh_tm = _slab_to_tm(jnp.concatenate([yh_lat, yh_ctx], axis=0))

        pq = _fn_mix(z_s, a_n=a_lat, n_batch=nb, chan_lo=c_fn)
        yk = _fn_seq(pq, a_n=a_lat, n_batch=nb)
        yf_lat = jnp.transpose(yk, (0, 3, 2, 1)).reshape(n_lat, D_BR)
        yf_ctx = _fn_small(z_s, a_n=a_ctx, n_batch=nb, slab0=nb * a_lat, chan_lo=c_fn).reshape(nb * lc, D_BR)
        yf_tm = jnp.concatenate([yf_lat, yf_ctx], axis=0)

        qk_tm = _slab_to_tm(jnp.concatenate([u_lat[:, c_qk:], u_ctx[:, c_qk:]], axis=0))
        gt = jnp.transpose(z_tm[:, g_col * LANES:g_col * LANES + 16])
        h_f, h_b = _mlstm(qk_tm, z_tm, gt, n_batch=nb, lat_chunks=a_lat, ctx_chunks=a_ctx, v_col=0, g_col=g_col)

        xs = _merge(yh_tm, yf_tm, h_f, h_b, z_tm, xs, mods3, w_branch[l].astype(BF16), w_out[l].astype(BF16),
                    ml_norm_w[l], seg=seg, tm=tm, o_col=1, gate_col0=1)

        rw = jnp.concatenate([moe_rg_w[l], moe_re_w[l], jnp.zeros((d, LANES - MOE_GROUPS - MOE_EXPERTS), F32)], axis=1)
        rb = jnp.concatenate([moe_rg_b[l], moe_re_b[l], jnp.zeros((LANES - MOE_GROUPS - MOE_EXPERTS,), F32)]).reshape(1, LANES)
        xn2, comb = _router(xs, norm2_w[l], mods3, rw, rb, seg=seg, tm=tm)
        xs = _moe(xn2, comb, moe_w_gate[l].astype(BF16), moe_w_up[l].astype(BF16), moe_w_down[l].astype(BF16),
                  xs, mods3, seg=seg_moe, tm=tm_moe)

    return _final_norm(xs, norm_f_w, n_lat, tm).reshape(nb, seq, d)
```

```python
import functools
import math

import numpy as np
import jax
import jax.numpy as jnp
from jax import lax
from jax.experimental import pallas as pl
from jax.experimental.pallas import tpu as pltpu

F32 = jnp.float32
BF16 = jnp.bfloat16

D_MODEL = 1024
D_BR = 512
GRID_W = 64
LANES = 128
CB = 8
HY_ORDER = 2
HY_BANDS = 16
FN_GROUPS = 4
ML_HEADS = 4
MOE_GROUPS = 4
MOE_PER_GROUP = 4
MOE_EXPERTS = 16
EXPERT_HID = 256
EPS = 1e-6
VMEM_LIMIT = 56 * 1024 * 1024


def _params(sem):
    return pltpu.CompilerParams(dimension_semantics=sem, vmem_limit_bytes=VMEM_LIMIT)


def _bdot(a, b):
    return jnp.dot(a.astype(BF16), b.astype(BF16), preferred_element_type=F32)


def _split3(x):
    hi = x.astype(BF16)
    r1 = x - hi.astype(F32)
    mid = r1.astype(BF16)
    lo = (r1 - mid.astype(F32)).astype(BF16)
    return hi, mid, lo


def _dot_f32ish(x, w):
    xh, xm, xl = _split3(x)
    wh, wm, wl = _split3(w)
    d = lambda a, b: jnp.dot(a, b, preferred_element_type=F32)
    return (d(xh, wh) + (d(xh, wm) + d(xm, wh))) + (d(xm, wm) + d(xh, wl) + d(xl, wh))


def _swap_halves(x):
    return jnp.concatenate([x[..., LANES:], x[..., :LANES]], axis=-1)


def _mods_kernel(c_ref, w_ref, b_ref, o_ref):
    c = c_ref[...]
    s = c * jax.nn.sigmoid(c)
    o_ref[...] = _dot_f32ish(s, w_ref[...]) + b_ref[...]


def _mods(cvec, ada_w, ada_b):
    depth, d, n6 = ada_w.shape
    tn = 1536
    return pl.pallas_call(
        _mods_kernel,
        grid=(depth, n6 // tn),
        in_specs=[pl.BlockSpec((8, d), lambda l, j: (0, 0)),
                  pl.BlockSpec((None, d, tn), lambda l, j: (l, 0, j)),
                  pl.BlockSpec((None, 1, tn), lambda l, j: (l, 0, j))],
        out_specs=pl.BlockSpec((None, 8, tn), lambda l, j: (l, 0, j)),
        out_shape=jax.ShapeDtypeStruct((depth, 8, n6), F32),
        compiler_params=_params(("arbitrary", "arbitrary")),
        name="adaln_mods",
    )(cvec, ada_w, ada_b.reshape(depth, 1, n6))


def _norm_mod_kernel(x_ref, w_ref, sh_ref, sc_ref, o_ref):
    x = x_ref[...]
    y = x * lax.rsqrt(jnp.mean(x * x, axis=-1, keepdims=True) + EPS) * w_ref[...]
    o_ref[...] = (y * (1.0 + sc_ref[...]) + sh_ref[...]).astype(o_ref.dtype)


def _seg_fn(n_lat_tiles, tiles_per_batch, n_batch):
    def seg(i):
        return jnp.where(i < n_lat_tiles, i // tiles_per_batch, n_batch)
    return seg


def _norm_mod(x, w, mods3, col_shift, col_scale, seg, tm):
    nt, d = x.shape
    return pl.pallas_call(
        _norm_mod_kernel,
        grid=(nt // tm,),
        in_specs=[pl.BlockSpec((tm, d), lambda i: (i, 0)),
                  pl.BlockSpec((1, d), lambda i: (0, 0)),
                  pl.BlockSpec((None, 1, d), lambda i: (seg(i), 0, col_shift)),
                  pl.BlockSpec((None, 1, d), lambda i: (seg(i), 0, col_scale))],
        out_specs=pl.BlockSpec((tm, d), lambda i: (i, 0)),
        out_shape=jax.ShapeDtypeStruct((nt, d), BF16),
        compiler_params=_params(("arbitrary",)),
        name="norm_mod",
    )(x, w.reshape(1, d), mods3, mods3)


def _mm_tm_kernel(x_ref, w_ref, b_ref, o_ref):
    o_ref[...] = jnp.dot(x_ref[...], w_ref[...], preferred_element_type=F32) + b_ref[...]


def _pick(n, cands):
    for c in cands:
        if n % c == 0:
            return c
    raise ValueError(f"no tile for {n} in {cands}")


def _mm_tm(xn, w, b):
    nt, k = xn.shape
    n = w.shape[1]
    tm = _pick(nt, (1056, 1024, 768, 512, 256))
    tn = _pick(n, (1408, 1024, 512, 384, 256, 128))
    return pl.pallas_call(
        _mm_tm_kernel,
        grid=(n // tn, nt // tm),
        in_specs=[pl.BlockSpec((tm, k), lambda j, i: (i, 0)),
                  pl.BlockSpec((k, tn), lambda j, i: (0, j)),
                  pl.BlockSpec((1, tn), lambda j, i: (0, j))],
        out_specs=pl.BlockSpec((tm, tn), lambda j, i: (i, j)),
        out_shape=jax.ShapeDtypeStruct((nt, n), F32),
        compiler_params=_params(("arbitrary", "arbitrary")),
        name="inproj_token_major",
    )(xn, w, b.reshape(1, n))


def _mm_slab_kernel(w_ref, x_ref, b_ref, o_ref, *, slabs):
    w = w_ref[...]
    b = b_ref[...]
    for s in range(slabs):
        xs = x_ref[s * LANES:(s + 1) * LANES, :]
        o_ref[s] = lax.dot_general(w, xs, (((1,), (1,)), ((), ())), preferred_element_type=F32) + b


def _mm_slab(xn, wt, b):
    nt, k = xn.shape
    c = wt.shape[0]
    ns = nt // LANES
    ts = _pick(ns, (12, 11, 8, 6, 4, 3, 2, 1))
    tc = _pick(c, (512, 256, 128))
    return pl.pallas_call(
        functools.partial(_mm_slab_kernel, slabs=ts),
        grid=(c // tc, ns // ts),
        in_specs=[pl.BlockSpec((tc, k), lambda j, i: (j, 0)),
                  pl.BlockSpec((ts * LANES, k), lambda j, i: (i, 0)),
                  pl.BlockSpec((tc, 1), lambda j, i: (j, 0))],
        out_specs=pl.BlockSpec((ts, tc, LANES), lambda j, i: (i, j, 0)),
        out_shape=jax.ShapeDtypeStruct((ns, c, LANES), F32),
        compiler_params=_params(("arbitrary", "arbitrary")),
        name="inproj_slab",
    )(wt, xn, b.reshape(c, 1))


def _conv_taps(rows, width):
    taps = []
    for dr in (-1, 0, 1):
        if rows == 1 and dr != 0:
            continue
        for dw in (-1, 0, 1):
            taps.append((dr, dw))
    return taps


def _conv_kernel(x_ref, w_ref, b_ref, o_ref, *, taps, width, n_slabs):
    ct = x_ref.shape[1]
    lane = lax.broadcasted_iota(jnp.int32, (ct, LANES), 1)
    bias = jnp.zeros((ct, LANES), F32) + b_ref[...]
    planes = []
    for t, (dr, dw) in enumerate(taps):
        w = w_ref[t]
        if width < LANES and dw != 0:
            col = lane % width + dw
            w = jnp.where((col >= 0) & (col < width), w, 0.0)
        planes.append((dr * width + dw, w))

    def body(a, carry):
        x0 = x_ref[a]
        xm = jnp.where(a > 0, x_ref[jnp.maximum(a - 1, 0)], 0.0)
        xp = jnp.where(a < n_slabs - 1, x_ref[jnp.minimum(a + 1, n_slabs - 1)], 0.0)
        acc = bias
        for delta, w in planes:
            if delta == 0:
                src = x0
            elif delta > 0:
                src = pltpu.roll(jnp.where(lane >= delta, x0, xp), LANES - delta, 1)
            else:
                src = pltpu.roll(jnp.where(lane < LANES + delta, x0, xm), -delta, 1)
            acc = acc + src * w
        o_ref[a] = acc
        return carry

    lax.fori_loop(0, n_slabs, body, 0, unroll=2 if n_slabs % 2 == 0 else 1)


def _conv(z_s, w9, bias, *, rows, width, n_batch, slab0, chan_lo, chan_n):
    seq = rows * width
    a_n = seq // LANES
    taps = tuple(_conv_taps(rows, width))
    assert all(abs(dr * width + dw) < LANES for dr, dw in taps)
    assert LANES % width == 0 or (rows == 1 and width % LANES == 0)
    tap_ids = [(dr + 1) * 3 + (dw + 1) for dr, dw in taps]
    w_t = jnp.broadcast_to(w9[jnp.array(tap_ids)][:, :, None], (len(taps), chan_n, LANES))
    ct = 64
    assert chan_lo % ct == 0 and chan_n % ct == 0 and slab0 % a_n == 0
    nt_ = len(taps)
    return pl.pallas_call(
        functools.partial(_conv_kernel, taps=taps, width=width, n_slabs=a_n),
        grid=(n_batch, chan_n // ct),
        in_specs=[pl.BlockSpec((a_n, ct, LANES), lambda b, j: (slab0 // a_n + b, chan_lo // ct + j, 0)),
                  pl.BlockSpec((nt_, ct, LANES), lambda b, j: (0, j, 0)),
                  pl.BlockSpec((ct, 1), lambda b, j: (j, 0))],
        out_specs=pl.BlockSpec((a_n, ct, LANES), lambda b, j: (b, j, 0)),
        out_shape=jax.ShapeDtypeStruct((n_batch * a_n, chan_n, LANES), F32),
        compiler_params=_params(("arbitrary", "arbitrary")),
        name=f"dwconv_{rows}x{width}",
    )(z_s, w_t, bias.reshape(chan_n, 1))


def _dft_consts(a_in, na):
    n = na * LANES
    k = np.arange(na)[:, None]
    a = np.arange(a_in)[None, :]
    ang = 2 * np.pi * (k * a % na) / na
    fa = np.concatenate([np.cos(ang), -np.sin(ang)], axis=0)
    r = np.arange(LANES)
    ang_t = 2 * np.pi * (np.arange(na)[:, None] * r[None, :] % n) / n
    tr, ti = np.cos(ang_t), -np.sin(ang_t)
    ta = np.concatenate([tr, tr], axis=1)
    tb = np.concatenate([-ti, ti], axis=1)
    ang2 = 2 * np.pi * (r[:, None] * r[None, :] % LANES) / LANES
    c2, s2 = np.cos(ang2), np.sin(ang2)
    g2 = np.block([[c2, -s2], [s2, c2]])
    g2i = np.block([[c2, s2], [-s2, c2]])
    ang_i = 2 * np.pi * (np.arange(a_in)[:, None] * np.arange(na)[None, :] % na) / na
    ci, si = np.cos(ang_i) / n, -np.sin(ang_i) / n
    f = lambda v, dt: jnp.asarray(v, dtype=dt)
    return dict(fa=f(fa, F32), ta=f(ta, F32), tb=f(tb, F32), g2=f(g2, F32), g2i=f(g2i, F32),
                ci=f(ci, F32), si=f(si, F32))


def _fwd_slab_stage(m, fa, ta, tb, na):
    pp = jnp.dot(fa, m.astype(BF16), preferred_element_type=F32)
    p = jnp.concatenate([pp[:na], pp[na:]], axis=1)
    return p * ta + _swap_halves(p) * tb


def _cmul(x, kf):
    kr, ki = kf[..., :LANES], kf[..., LANES:]
    ka = jnp.concatenate([kr, kr], axis=-1)
    kb = jnp.concatenate([-ki, ki], axis=-1)
    return x * ka + _swap_halves(x) * kb


def _chan_load(ref, c):
    n, cb, _ = ref.shape
    return ref.reshape(n * cb, LANES)[pl.ds(c, n, stride=cb), :]


def _chan_store(ref, c, val):
    ref[:, c, :] = val


def _filt_kernel(k_ref, fa_ref, ta_ref, tb_ref, g2_ref, o_ref, p_buf, *, na):
    fa, ta, tb = fa_ref[...].astype(BF16), ta_ref[...], tb_ref[...]
    scales = []
    for c in range(CB):
        m = _chan_load(k_ref, c)
        ss = jnp.sum(jnp.sum(m * m, axis=1, keepdims=True), axis=0, keepdims=True)
        scales.append(lax.rsqrt(ss + EPS))
        p_buf[c] = _fwd_slab_stage(m, fa, ta, tb, na)
    x = _bdot(p_buf[...].reshape(CB * na, 2 * LANES), g2_ref[...]).reshape(CB, na, 2 * LANES)
    for c in range(CB):
        o_ref[c] = x[c] * scales[c]


def _filter_spectrum(k_s, na):
    nc = k_s.shape[1]
    cs = _dft_consts(na, na)
    const = lambda shp: pl.BlockSpec(shp, lambda j: (0,) * len(shp))
    return pl.pallas_call(
        functools.partial(_filt_kernel, na=na),
        grid=(nc // CB,),
        in_specs=[pl.BlockSpec((na, CB, LANES), lambda j: (0, j, 0)),
                  const((2 * na, na)), const((na, 2 * LANES)), const((na, 2 * LANES)),
                  const((2 * LANES, 2 * LANES))],
        out_specs=pl.BlockSpec((CB, na, 2 * LANES), lambda j: (j, 0, 0)),
        out_shape=jax.ShapeDtypeStruct((nc, na, 2 * LANES), F32),
        scratch_shapes=[pltpu.VMEM((CB, na, 2 * LANES), F32)],
        compiler_params=_params(("arbitrary",)),
        name="hyena_filter_dft",
    )(k_s, cs["fa"], cs["ta"], cs["tb"], cs["g2"])


def _dot_f32ish_k(w, h):
    wh, wm, _ = _split3(w)
    hh, hm, _ = _split3(h)
    lhs = jnp.concatenate([wh, wh, wm], axis=1)
    rhs = jnp.concatenate([hh, hm, hh], axis=0)
    return jnp.dot(lhs, rhs, preferred_element_type=F32)


def _taps_kernel(bands_ref, w1t_ref, w1c_ref, w1s_ref, b1_ref, w2_ref, b2_ref, fq_ref, w3_ref, dec_ref, o_ref,
                 *, seq, a_seq, na, spb):
    step = pl.program_id(0)
    lane = lax.broadcasted_iota(jnp.int32, (1, LANES), 1)
    fq = fq_ref[...]
    n_total = na * LANES
    for i in range(spb):
        a = step * spb + i
        is_f = a < a_seq
        is_b = a >= na - a_seq
        live = jnp.logical_or(is_f, is_b)

        @pl.when(live)
        def _():
            n = a * LANES + lane
            pos = jnp.where(is_f, n, n_total - n)
            valid = (n > jnp.where(is_f, -1, n_total - seq)) & (n < jnp.where(is_f, seq, n_total))
            t = pos.astype(F32) / seq
            ang = ((2 * math.pi) * t) * bands_ref[...]
            pre = (w1t_ref[...] * t + _dot_f32ish(w1c_ref[...], jnp.cos(ang))
                   + _dot_f32ish(w1s_ref[...], jnp.sin(ang)))
            h = jnp.sin(fq * (pre + b1_ref[...]))
            h = jnp.sin(fq * (_dot_f32ish(w2_ref[...], h) + b2_ref[...]))
            d = jnp.where(is_f, 0, 1)
            k = _dot_f32ish_k(w3_ref[d], h) * jnp.exp(-t * jnp.abs(dec_ref[d]))
            o_ref[i] = jnp.where(valid, k, 0.0)

        @pl.when(jnp.logical_not(live))
        def _():
            o_ref[i] = jnp.zeros(o_ref.shape[1:], F32)


def _hyena_taps(seq, na, lp):
    nc = HY_ORDER * D_BR
    hid = lp["hy_f_w2"].shape[0]
    w1 = lp["hy_f_w1"]
    col = lambda v: v.reshape(-1, 1)
    bands = col(jnp.linspace(1e-4, HY_BANDS - 1, HY_BANDS, dtype=F32))
    w3 = jnp.transpose(lp["hy_f_w3"].T.reshape(HY_ORDER, 2, D_BR, hid), (1, 0, 2, 3)).reshape(2, nc, hid)
    dec = jnp.broadcast_to(jnp.transpose(lp["hy_decay"], (1, 0, 2)).reshape(2, nc, 1), (2, nc, LANES))
    spb = 8
    assert na % spb == 0
    args = (bands, col(w1[0]), w1[1:1 + HY_BANDS].T, w1[1 + HY_BANDS:].T, col(lp["hy_f_b1"]), lp["hy_f_w2"].T,
            col(lp["hy_f_b2"]), col(lp["hy_f_freq"]), w3, dec)
    full = lambda v: pl.BlockSpec(v.shape, lambda s: (0,) * v.ndim)
    return pl.pallas_call(
        functools.partial(_taps_kernel, seq=seq, a_seq=seq // LANES, na=na, spb=spb),
        grid=(na // spb,),
        in_specs=[full(v) for v in args],
        out_specs=pl.BlockSpec((spb, nc, LANES), lambda s: (s, 0, 0)),
        out_shape=jax.ShapeDtypeStruct((na, nc, LANES), F32),
        compiler_params=_params(("arbitrary",)),
        name=f"hyena_filter_taps_{na}",
    )(*args)


def _hyena_kernel(v_ref, x1_ref, x2_ref, kf_ref, skip_ref, fa_ref, ta_ref, tb_ref, g2_ref, g2i_ref,
                  ci_ref, si_ref, o_ref, p_buf, z_buf, *, a_in, na):
    fa, ta, tb = fa_ref[...].astype(BF16), ta_ref[...], tb_ref[...]
    ci, si = ci_ref[...].astype(BF16), si_ref[...].astype(BF16)

    def spectral(order):
        x = _bdot(p_buf[...].reshape(CB * na, 2 * LANES), g2_ref[...])
        y = _cmul(x, kf_ref[order].reshape(CB * na, 2 * LANES))
        bm = _bdot(y, g2i_ref[...]).reshape(CB, na, 2 * LANES)
        p_buf[...] = bm * ta - _swap_halves(bm) * tb

    def conv_out(c):
        bb = p_buf[c]
        return (jnp.dot(ci, bb[:, :LANES].astype(BF16), preferred_element_type=F32)
                + jnp.dot(si, bb[:, LANES:].astype(BF16), preferred_element_type=F32))

    for c in range(CB):
        p_buf[c] = _fwd_slab_stage(_chan_load(v_ref, c), fa, ta, tb, na)
    spectral(0)
    for c in range(CB):
        z_buf[c] = _chan_load(x1_ref, c) * (conv_out(c) + _chan_load(v_ref, c) * skip_ref[0, c])
    for c in range(CB):
        p_buf[c] = _fwd_slab_stage(z_buf[c], fa, ta, tb, na)
    spectral(1)
    for c in range(CB):
        _chan_store(o_ref, c, _chan_load(x2_ref, c) * (conv_out(c) + z_buf[c] * skip_ref[1, c]))


def _hyena(u_s, kf, skip, *, a_in, na, n_batch):
    cs = _dft_consts(a_in, na)
    nblk = D_BR // CB
    const = lambda shp: pl.BlockSpec(shp, lambda j, b: (0,) * len(shp))
    skip_b = jnp.broadcast_to(skip[:, :, None, None], (HY_ORDER, D_BR, 1, LANES))
    return pl.pallas_call(
        functools.partial(_hyena_kernel, a_in=a_in, na=na),
        grid=(nblk, n_batch),
        in_specs=[pl.BlockSpec((a_in, CB, LANES), lambda j, b: (b, j, 0)),
                  pl.BlockSpec((a_in, CB, LANES), lambda j, b: (b, nblk + j, 0)),
                  pl.BlockSpec((a_in, CB, LANES), lambda j, b: (b, 2 * nblk + j, 0)),
                  pl.BlockSpec((HY_ORDER, CB, na, 2 * LANES), lambda j, b: (0, j, 0, 0)),
                  pl.BlockSpec((HY_ORDER, CB, 1, LANES), lambda j, b: (0, j, 0, 0)),
                  const((2 * na, a_in)), const((na, 2 * LANES)), const((na, 2 * LANES)),
                  const((2 * LANES, 2 * LANES)), const((2 * LANES, 2 * LANES)),
                  const((a_in, na)), const((a_in, na))],
        out_specs=pl.BlockSpec((a_in, CB, LANES), lambda j, b: (b, j, 0)),
        out_shape=jax.ShapeDtypeStruct((n_batch * a_in, D_BR, LANES), F32),
        scratch_shapes=[pltpu.VMEM((CB, na, 2 * LANES), F32), pltpu.VMEM((CB, a_in, LANES), F32)],
        compiler_params=_params(("arbitrary", "arbitrary")),
        name=f"hyena_longconv_{a_in}",
    )(u_s, u_s, u_s, kf, skip_b, cs["fa"], cs["ta"], cs["tb"], cs["g2"], cs["g2i"], cs["ci"], cs["si"])


def _chan_dft_mats():
    r = np.arange(LANES)
    ang = 2 * np.pi * (r[:, None] * r[None, :] % LANES) / LANES
    return np.cos(ang), np.sin(ang)


def _fn_mix_kernel(u_ref, cs_ref, o_ref, *, n_slabs):
    w = cs_ref[...].astype(BF16)

    def body(a, carry):
        o_ref[a] = jnp.dot(w, u_ref[a].astype(BF16), preferred_element_type=F32)
        return carry

    lax.fori_loop(0, n_slabs, body, 0)


def _fn_mix(z_s, *, a_n, n_batch, chan_lo):
    c, s = _chan_dft_mats()
    w = jnp.asarray(np.concatenate([c, s], axis=0), dtype=F32)
    g0 = chan_lo // LANES
    return pl.pallas_call(
        functools.partial(_fn_mix_kernel, n_slabs=a_n),
        grid=(n_batch, FN_GROUPS),
        in_specs=[pl.BlockSpec((a_n, LANES, LANES), lambda b, g: (b, g0 + g, 0)),
                  pl.BlockSpec((2 * LANES, LANES), lambda b, g: (0, 0))],
        out_specs=pl.BlockSpec((a_n, 2 * LANES, LANES), lambda b, g: (b, g, 0)),
        out_shape=jax.ShapeDtypeStruct((n_batch * a_n, 2 * D_BR, LANES), F32),
        compiler_params=_params(("arbitrary", "arbitrary")),
        name="fnet_channel_dft",
    )(z_s, w)


def _fn_seq_kernel(p_ref, q_ref, fa_ref, tr_ref, ti_ref, g_ref, o_ref, a_buf, *, a_n, scale):
    fa, tr, ti = fa_ref[...].astype(BF16), tr_ref[...], ti_ref[...]
    for c in range(CB):
        r1 = jnp.dot(fa, _chan_load(p_ref, c).astype(BF16), preferred_element_type=F32)
        r2 = jnp.dot(fa, _chan_load(q_ref, c).astype(BF16), preferred_element_type=F32)
        ar = r1[:a_n] - r2[a_n:]
        ai = -(r2[:a_n] + r1[a_n:])
        a_buf[c] = jnp.concatenate([ar * tr - ai * ti, ar * ti + ai * tr], axis=1)
    y = _bdot(a_buf[...].reshape(CB * a_n, 2 * LANES), g_ref[...]) * scale
    o_ref[...] = y.reshape(CB, a_n, LANES)


def _fn_seq(pq, *, a_n, n_batch):
    seq = a_n * LANES
    k = np.arange(a_n)
    ang = 2 * np.pi * (k[:, None] * k[None, :] % a_n) / a_n
    fa = np.concatenate([np.cos(ang), np.sin(ang)], axis=0)
    r = np.arange(LANES)
    ang_t = 2 * np.pi * (k[:, None] * r[None, :] % seq) / seq
    c2, s2 = _chan_dft_mats()
    g = np.concatenate([c2, s2], axis=0)
    nblk = LANES // CB
    const = lambda shp: pl.BlockSpec(shp, lambda b, j: (0,) * len(shp))

    def chan_blk(j, off):
        return (j // nblk) * (2 * nblk) + off * nblk + j % nblk

    return pl.pallas_call(
        functools.partial(_fn_seq_kernel, a_n=a_n, scale=1.0 / math.sqrt(seq * LANES)),
        grid=(n_batch, D_BR // CB),
        in_specs=[pl.BlockSpec((a_n, CB, LANES), lambda b, j: (b, chan_blk(j, 0), 0)),
                  pl.BlockSpec((a_n, CB, LANES), lambda b, j: (b, chan_blk(j, 1), 0)),
                  const((2 * a_n, a_n)), const((a_n, LANES)), const((a_n, LANES)), const((2 * LANES, LANES))],
        out_specs=pl.BlockSpec((None, CB, a_n, LANES), lambda b, j: (b, j, 0, 0)),
        out_shape=jax.ShapeDtypeStruct((n_batch, D_BR, a_n, LANES), F32),
        scratch_shapes=[pltpu.VMEM((CB, a_n, 2 * LANES), F32)],
        compiler_params=_params(("arbitrary", "arbitrary")),
        name="fnet_sequence_dft",
    )(pq, pq, jnp.asarray(fa, F32), jnp.asarray(np.cos(ang_t), F32), jnp.asarray(-np.sin(ang_t), F32),
      jnp.asarray(g, F32))


def _fn_small_kernel(u_ref, cw_ref, sw_ref, cl_ref, sl_ref, o_ref, *, a_n, scale):
    u = jnp.concatenate([u_ref[a].T for a in range(a_n)], axis=0)
    p = _bdot(u, cw_ref[...])
    q = _bdot(u, sw_ref[...])
    o_ref[...] = (_bdot(cl_ref[...], p) - _bdot(sl_ref[...], q)) * scale


def _fn_small(z_s, *, a_n, n_batch, slab0, chan_lo):
    seq = a_n * LANES
    cw, sw = _chan_dft_mats()
    n = np.arange(seq)
    ang = 2 * np.pi * (n[:, None] * n[None, :] % seq) / seq
    const = lambda shp: pl.BlockSpec(shp, lambda b, g: (0,) * len(shp))
    g0 = chan_lo // LANES
    return pl.pallas_call(
        functools.partial(_fn_small_kernel, a_n=a_n, scale=1.0 / math.sqrt(seq * LANES)),
        grid=(n_batch, FN_GROUPS),
        in_specs=[pl.BlockSpec((a_n, LANES, LANES), lambda b, g: (slab0 // a_n + b, g0 + g, 0)),
                  const((LANES, LANES)), const((LANES, LANES)), const((seq, seq)), const((seq, seq))],
        out_specs=pl.BlockSpec((None, seq, LANES), lambda b, g: (b, 0, g)),
        out_shape=jax.ShapeDtypeStruct((n_batch, seq, D_BR), F32),
        compiler_params=_params(("arbitrary", "arbitrary")),
        name="fnet_short",
    )(z_s, jnp.asarray(cw, F32), jnp.asarray(sw, F32), jnp.asarray(np.cos(ang), F32), jnp.asarray(np.sin(ang), F32))


def _log_sigmoid(x):
    return jnp.minimum(x, 0.0) - jnp.log(1.0 + jnp.exp(-jnp.abs(x)))


def _exact_tri_dot(tri, x, tri_on_left):
    h, m, l = _split3(x)
    if tri_on_left:
        d = lambda p: jnp.dot(tri, p, preferred_element_type=F32)
    else:
        d = lambda p: jnp.dot(p, tri, preferred_element_type=F32)
    return d(h) + d(m) + d(l)


def _mlstm_dir(q_ref, k_ref, v_ref, g_ref, gt_ref, h_ref, c_st, n_st, m_st, *, d, reverse):
    t = LANES
    hd = LANES
    row = lax.broadcasted_iota(jnp.int32, (t, t), 0)
    col = lax.broadcasted_iota(jnp.int32, (t, t), 1)
    lower = (col <= row)
    tri = jnp.where(lower, 1.0, 0.0).astype(BF16)
    tri_t = jnp.where(col >= row, 1.0, 0.0).astype(BF16)
    mask = (col >= row) if reverse else lower
    g = g_ref[...]
    gt = gt_ref[...]
    lf_c = _log_sigmoid(g)
    lf_r = _log_sigmoid(gt)
    if reverse:
        b_c = _exact_tri_dot(tri_t, lf_c, True)
        b_r = _exact_tri_dot(tri, lf_r, False)
    else:
        b_c = _exact_tri_dot(tri, lf_c, True)
        b_r = _exact_tri_dot(tri_t, lf_r, False)
    i_off = 2 * ML_HEADS * d
    f_off = i_off + ML_HEADS
    for h in range(ML_HEADS):
        sl = slice(h * hd, (h + 1) * hd)
        q = q_ref[:, sl]
        q = (q * jax.nn.sigmoid(q)) * (hd ** -0.5)
        k = k_ref[:, sl]
        k = k * jax.nn.sigmoid(k)
        v = v_ref[:, sl]
        bc = b_c[:, f_off + h:f_off + h + 1]
        br = b_r[f_off + h:f_off + h + 1, :]
        ic = g[:, i_off + h:i_off + h + 1]
        ir = gt[i_off + h:i_off + h + 1, :]
        m_prev = m_st[h][:, :1]
        ct = c_st[h]
        n_prev = n_st[h]
        dm = jnp.where(mask, bc - br + ir, -jnp.inf)
        inter = bc + m_prev
        m_row = jnp.maximum(inter, jnp.max(dm, axis=-1, keepdims=True))
        w_intra = jnp.exp(dm - m_row)
        w_inter = jnp.exp(inter - m_row)
        qb, kb, vb = q.astype(BF16), k.astype(BF16), v.astype(BF16)
        s = lax.dot_general(qb, kb, (((1,), (1,)), ((), ())), preferred_element_type=F32) * w_intra
        num = jnp.dot(s.astype(BF16), vb, preferred_element_type=F32) + w_inter * jnp.dot(qb, ct.astype(BF16), preferred_element_type=F32)
        den = jnp.sum(s, axis=-1, keepdims=True) + w_inter * jnp.sum(q * n_prev, axis=-1, keepdims=True)
        den = jnp.maximum(jnp.abs(den), jnp.exp(-m_row))
        h_ref[:, sl] = num / den
        if reverse:
            b_tot_c, b_tot_r = bc[:1, :], br[:, :1]
        else:
            b_tot_c, b_tot_r = bc[t - 1:, :], br[:, t - 1:]
        a_c = b_tot_c - bc + ic
        a_r = b_tot_r - br + ir
        m_new = jnp.maximum(b_tot_c + m_prev, jnp.max(a_r, axis=-1, keepdims=True))
        sc = jnp.exp(a_c - m_new)
        decay = jnp.exp(b_tot_c + m_prev - m_new)
        ks = k * sc
        c_st[h] = decay * ct + lax.dot_general(ks.astype(BF16), vb, (((0,), (0,)), ((), ())), preferred_element_type=F32)
        n_st[h] = decay * n_prev + jnp.sum(ks, axis=0, keepdims=True)
        m_st[h] = jnp.broadcast_to(m_new, (1, LANES))


def _mlstm_kernel(qf, kf, vf, gf, gtf, qb, kb, vb, gb, gtb, hf_ref, hb_ref, c_st, n_st, m_st):
    @pl.when(pl.program_id(1) == 0)
    def _():
        c_st[...] = jnp.zeros(c_st.shape, F32)
        n_st[...] = jnp.zeros(n_st.shape, F32)
        m_st[...] = jnp.zeros(m_st.shape, F32)

    _mlstm_dir(qf, kf, vf, gf, gtf, hf_ref, c_st.at[0], n_st.at[0], m_st.at[0], d=0, reverse=False)
    _mlstm_dir(qb, kb, vb, gb, gtb, hb_ref, c_st.at[1], n_st.at[1], m_st.at[1], d=1, reverse=True)


def _mlstm(qk_tm, z_tm, gt, *, n_batch, lat_chunks, ctx_chunks, v_col, g_col):
    nt = qk_tm.shape[0]
    n_lat = n_batch * lat_chunks
    nsteps = ctx_chunks + lat_chunks

    def glob(b, p):
        return jnp.where(p < ctx_chunks, n_lat + b * ctx_chunks + p, b * lat_chunks + p - ctx_chunks)

    def fwd(b, j):
        return glob(b, j)

    def bwd(b, j):
        p = jnp.where(j < ctx_chunks, ctx_chunks - 1 - j, ctx_chunks + lat_chunks - 1 - (j - ctx_chunks))
        return glob(b, p)

    def specs(ix):
        return [pl.BlockSpec((LANES, D_BR), lambda b, j: (ix(b, j), 0)),
                pl.BlockSpec((LANES, D_BR), lambda b, j: (ix(b, j), 1)),
                pl.BlockSpec((LANES, D_BR), lambda b, j: (ix(b, j), v_col)),
                pl.BlockSpec((LANES, LANES), lambda b, j: (ix(b, j), g_col)),
                pl.BlockSpec((16, LANES), lambda b, j: (0, ix(b, j)))]

    out_sd = jax.ShapeDtypeStruct((nt, D_BR), F32)
    return pl.pallas_call(
        _mlstm_kernel,
        grid=(n_batch, nsteps),
        in_specs=specs(fwd) + specs(bwd),
        out_specs=[pl.BlockSpec((LANES, D_BR), lambda b, j: (fwd(b, j), 0)),
                   pl.BlockSpec((LANES, D_BR), lambda b, j: (bwd(b, j), 0))],
        out_shape=[out_sd, out_sd],
        scratch_shapes=[pltpu.VMEM((2, ML_HEADS, LANES, LANES), F32),
                        pltpu.VMEM((2, ML_HEADS, 1, LANES), F32),
                        pltpu.VMEM((2, ML_HEADS, 1, LANES), F32)],
        compiler_params=_params(("arbitrary", "arbitrary")),
        name="mlstm_bidir",
    )(qk_tm, qk_tm, z_tm, z_tm, gt, qk_tm, qk_tm, z_tm, z_tm, gt)


def _merge_kernel(yh_ref, yf_ref, hf_ref, hb_ref, o_ref, g0_ref, g1_ref, g2_ref, x_ref, gate_ref,
                  wb_ref, wo_ref, nw_ref, out_ref):
    hd = LANES
    h = hf_ref[...] + hb_ref[...]
    parts = []
    for i in range(ML_HEADS):
        hh = h[:, i * hd:(i + 1) * hd]
        parts.append(hh * lax.rsqrt(jnp.mean(hh * hh, axis=-1, keepdims=True) + EPS))
    y_ml = jax.nn.sigmoid(o_ref[...]) * (jnp.concatenate(parts, axis=1) * nw_ref[...])
    acc = jax.nn.sigmoid(g0_ref[...]) * _bdot(yh_ref[...], wb_ref[0])
    acc = acc + jax.nn.sigmoid(g1_ref[...]) * _bdot(yf_ref[...], wb_ref[1])
    acc = acc + jax.nn.sigmoid(g2_ref[...]) * _bdot(y_ml, wb_ref[2])
    out_ref[...] = x_ref[...] + gate_ref[...] * _bdot(acc, wo_ref[...])


def _merge(yh, yf, hf, hb, z_tm, x, mods3, wb, wo, nw, *, seg, tm, o_col, gate_col0):
    nt, d = x.shape
    tok = lambda w, cidx: pl.BlockSpec((tm, w), lambda i: (i, cidx))
    return pl.pallas_call(
        _merge_kernel,
        grid=(nt // tm,),
        in_specs=[tok(D_BR, 0), tok(D_BR, 0), tok(D_BR, 0), tok(D_BR, 0),
                  tok(D_BR, o_col),
                  tok(d, gate_col0), tok(d, gate_col0 + 1), tok(d, gate_col0 + 2),
                  tok(d, 0),
                  pl.BlockSpec((None, 1, d), lambda i: (seg(i), 0, 2)),
                  pl.BlockSpec((3, D_BR, d), lambda i: (0, 0, 0)),
                  pl.BlockSpec((d, d), lambda i: (0, 0)),
                  pl.BlockSpec((1, D_BR), lambda i: (0, 0))],
        out_specs=pl.BlockSpec((tm, d), lambda i: (i, 0)),
        out_shape=jax.ShapeDtypeStruct((nt, d), F32),
        compiler_params=_params(("arbitrary",)),
        name="merge_branches",
    )(yh, yf, hf, hb, z_tm, z_tm, z_tm, z_tm, x, mods3, wb, wo, nw.reshape(1, D_BR))


def _router_kernel(x_ref, w_ref, sh_ref, sc_ref, rw_ref, rb_ref, xn_ref, comb_ref):
    x = x_ref[...]
    y = x * lax.rsqrt(jnp.mean(x * x, axis=-1, keepdims=True) + EPS) * w_ref[...]
    t = y * (1.0 + sc_ref[...]) + sh_ref[...]
    xn_ref[...] = t.astype(BF16)
    logits = _dot_f32ish(t, rw_ref[...]) + rb_ref[...]
    col = lax.broadcasted_iota(jnp.int32, logits.shape, 1)
    big = jnp.int32(1 << 20)
    ninf = -jnp.inf
    is_g = col < MOE_GROUPS
    gl = jnp.where(is_g, logits, ninf)
    gmax = jnp.max(gl, axis=-1, keepdims=True)
    g_sel = jnp.min(jnp.where(is_g & (gl == gmax), col, big), axis=-1, keepdims=True)
    p_top = 1.0 / jnp.sum(jnp.where(is_g, jnp.exp(gl - gmax), 0.0), axis=-1, keepdims=True)
    lo = MOE_GROUPS + g_sel * MOE_PER_GROUP
    in_grp = (col >= lo) & (col < lo + MOE_PER_GROUP)
    e1v = jnp.where(in_grp, logits, ninf)
    top1 = jnp.max(e1v, axis=-1, keepdims=True)
    idx1 = jnp.min(jnp.where(in_grp & (e1v == top1), col, big), axis=-1, keepdims=True)
    e2v = jnp.where(col == idx1, ninf, e1v)
    top2 = jnp.max(e2v, axis=-1, keepdims=True)
    idx2 = jnp.min(jnp.where(in_grp & (col != idx1) & (e2v == top2), col, big), axis=-1, keepdims=True)
    ex = jnp.exp(top2 - top1)
    s1 = 1.0 / (1.0 + ex)
    comb_ref[...] = jnp.where(col == idx1, p_top * s1, 0.0) + jnp.where(col == idx2, p_top * (ex * s1), 0.0)


def _router(x, w, mods3, rw, rb, *, seg, tm):
    nt, d = x.shape
    return pl.pallas_call(
        _router_kernel,
        grid=(nt // tm,),
        in_specs=[pl.BlockSpec((tm, d), lambda i: (i, 0)),
                  pl.BlockSpec((1, d), lambda i: (0, 0)),
                  pl.BlockSpec((None, 1, d), lambda i: (seg(i), 0, 3)),
                  pl.BlockSpec((None, 1, d), lambda i: (seg(i), 0, 4)),
                  pl.BlockSpec((d, LANES), lambda i: (0, 0)),
                  pl.BlockSpec((1, LANES), lambda i: (0, 0))],
        out_specs=[pl.BlockSpec((tm, d), lambda i: (i, 0)), pl.BlockSpec((tm, LANES), lambda i: (i, 0))],
        out_shape=[jax.ShapeDtypeStruct((nt, d), BF16), jax.ShapeDtypeStruct((nt, LANES), F32)],
        compiler_params=_params(("arbitrary",)),
        name="norm_router",
    )(x, w.reshape(1, d), mods3, mods3, rw, rb)


def _moe_kernel(xn_ref, comb_ref, wg_ref, wu_ref, wd_ref, x_ref, gate_ref, o_ref, acc_ref):
    e = pl.program_id(1)

    @pl.when(e == 0)
    def _():
        acc_ref[...] = jnp.zeros(acc_ref.shape, F32)

    xn = xn_ref[...]
    comb = comb_ref[...]
    col = lax.broadcasted_iota(jnp.int32, comb.shape, 1)
    cw = jnp.sum(jnp.where(col == e + MOE_GROUPS, comb, 0.0), axis=-1, keepdims=True)
    hg = jnp.dot(xn, wg_ref[...], preferred_element_type=F32)
    hu = jnp.dot(xn, wu_ref[...], preferred_element_type=F32)
    a = (hg * jax.nn.sigmoid(hg)) * hu * cw
    acc_ref[...] += jnp.dot(a.astype(BF16), wd_ref[...], preferred_element_type=F32)

    @pl.when(e == MOE_EXPERTS - 1)
    def _():
        o_ref[...] = x_ref[...] + gate_ref[...] * acc_ref[...]


def _moe(xn, comb, wg, wu, wd, x, mods3, *, seg, tm):
    nt, d = x.shape
    return pl.pallas_call(
        _moe_kernel,
        grid=(nt // tm, MOE_EXPERTS),
        in_specs=[pl.BlockSpec((tm, d), lambda i, e: (i, 0)),
                  pl.BlockSpec((tm, LANES), lambda i, e: (i, 0)),
                  pl.BlockSpec((None, d, EXPERT_HID), lambda i, e: (e, 0, 0)),
                  pl.BlockSpec((None, d, EXPERT_HID), lambda i, e: (e, 0, 0)),
                  pl.BlockSpec((None, EXPERT_HID, d), lambda i, e: (e, 0, 0)),
                  pl.BlockSpec((tm, d), lambda i, e: (i, 0)),
                  pl.BlockSpec((None, 1, d), lambda i, e: (seg(i), 0, 5))],
        out_specs=pl.BlockSpec((tm, d), lambda i, e: (i, 0)),
        out_shape=jax.ShapeDtypeStruct((nt, d), F32),
        scratch_shapes=[pltpu.VMEM((tm, d), F32)],
        compiler_params=_params(("arbitrary", "arbitrary")),
        name="moe_experts",
    )(xn, comb, wg, wu, wd, x, mods3)


def _final_norm_kernel(x_ref, w_ref, o_ref):
    x = x_ref[...]
    o_ref[...] = x * lax.rsqrt(jnp.mean(x * x, axis=-1, keepdims=True) + EPS) * w_ref[...]


def _final_norm(x, w, n_rows, tm):
    d = x.shape[1]
    return pl.pallas_call(
        _final_norm_kernel,
        grid=(n_rows // tm,),
        in_specs=[pl.BlockSpec((tm, d), lambda i: (i, 0)), pl.BlockSpec((1, d), lambda i: (0, 0))],
        out_specs=pl.BlockSpec((tm, d), lambda i: (i, 0)),
        out_shape=jax.ShapeDtypeStruct((n_rows, d), F32),
        compiler_params=_params(("arbitrary",)),
        name="final_norm",
    )(x, w.reshape(1, d))


def _slab_to_tm(y_s):
    ns, c, _ = y_s.shape
    return jnp.transpose(y_s, (0, 2, 1)).reshape(ns * LANES, c)


def kernel(x, c, ctx, c_ctx, ada_w, ada_b, norm1_w, norm2_w, w_in, b_in, hy_conv_w, hy_conv_b, hy_f_w1, hy_f_b1, hy_f_w2, hy_f_b2, hy_f_w3, hy_f_freq, hy_decay, hy_skip, ml_conv_w, ml_conv_b, ml_norm_w, w_branch, w_out, moe_rg_w, moe_rg_b, moe_re_w, moe_re_b, moe_w_gate, moe_w_up, moe_w_down, norm_f_w):
    nb, seq, d = x.shape
    lc = ctx.shape[1]
    depth = ada_w.shape[0]
    assert d == D_MODEL and seq % (GRID_W * 2) == 0 and lc % LANES == 0 and nb + 1 <= 8
    rows = seq // GRID_W
    a_lat = seq // LANES
    a_ctx = lc // LANES
    n_lat = nb * seq
    nt = n_lat + nb * lc
    tm = 256
    assert seq % tm == 0 and (nb * lc) % tm == 0
    seg = _seg_fn(n_lat // tm, seq // tm, nb)
    tm_moe = 512 if (seq % 512 == 0 and (nb * lc) % 512 == 0) else tm
    seg_moe = _seg_fn(n_lat // tm_moe, seq // tm_moe, nb)
    a_ctx_pad = 16
    assert a_ctx <= a_ctx_pad // 2 or a_ctx == a_ctx_pad

    xs = jnp.concatenate([x.reshape(n_lat, d), ctx.reshape(nb * lc, d)], axis=0)
    cvec = jnp.zeros((8, d), F32).at[:nb].set(c).at[nb].set(c_ctx)
    mods = _mods(cvec, ada_w, ada_b)

    o_fn, o_ml, o_mlg, o_gate = 3 * D_BR, 4 * D_BR, 8 * D_BR, 8 * D_BR + 4 * ML_HEADS
    pad_g = LANES - 4 * ML_HEADS

    for l in range(depth):
        lp = {"hy_f_w1": hy_f_w1[l], "hy_f_b1": hy_f_b1[l], "hy_f_w2": hy_f_w2[l], "hy_f_b2": hy_f_b2[l],
              "hy_f_w3": hy_f_w3[l], "hy_f_freq": hy_f_freq[l], "hy_decay": hy_decay[l]}
        mods3 = mods[l].reshape(8, 1, 6 * d)
        wl, bl = w_in[l], b_in[l]
        w_cm = jnp.concatenate([wl[:, :o_fn], wl[:, o_ml:o_ml + 2 * D_BR], wl[:, o_fn:o_ml]], axis=1)
        b_cm = jnp.concatenate([bl[:o_fn], bl[o_ml:o_ml + 2 * D_BR], bl[o_fn:o_ml]])
        w_tm = jnp.concatenate([wl[:, o_ml + 2 * D_BR:o_mlg], wl[:, o_gate:], wl[:, o_mlg:o_gate],
                                jnp.zeros((d, pad_g), F32)], axis=1)
        b_tm = jnp.concatenate([bl[o_ml + 2 * D_BR:o_mlg], bl[o_gate:], bl[o_mlg:o_gate], jnp.zeros((pad_g,), F32)])
        c_hy, c_qk, c_fn = 0, 3 * D_BR, 5 * D_BR
        g_col = (2 * D_BR + 3 * d) // LANES

        xn = _norm_mod(xs, norm1_w[l], mods3, 0, 1, seg, tm)
        z_tm = _mm_tm(xn, w_tm.astype(BF16), b_tm)
        z_s = _mm_slab(xn, w_cm.T.astype(BF16), b_cm)

        conv_w = jnp.concatenate([hy_conv_w[l].reshape(9, 3 * D_BR), ml_conv_w[l].reshape(9, 2 * D_BR)], axis=1)
        conv_b = jnp.concatenate([hy_conv_b[l], ml_conv_b[l]])
        u_lat = _conv(z_s, conv_w, conv_b, rows=rows, width=GRID_W, n_batch=nb, slab0=0,
                      chan_lo=0, chan_n=5 * D_BR)
        u_ctx = _conv(z_s, conv_w, conv_b, rows=1, width=lc, n_batch=nb, slab0=nb * a_lat,
                      chan_lo=0, chan_n=5 * D_BR)

        kf_lat = _filter_spectrum(_hyena_taps(seq, 2 * a_lat, lp), 2 * a_lat).reshape(HY_ORDER, D_BR, 2 * a_lat, 2 * LANES)
        yh_lat = _hyena(u_lat, kf_lat, hy_skip[l], a_in=a_lat, na=2 * a_lat, n_batch=nb)
        kf_ctx = _filter_spectrum(_hyena_taps(lc, 2 * a_ctx_pad, lp), 2 * a_ctx_pad).reshape(HY_ORDER, D_BR, 2 * a_ctx_pad, 2 * LANES)
        u_ctx_p = jnp.pad(u_ctx[:, :3 * D_BR].reshape(nb, a_ctx, 3 * D_BR, LANES),
                          ((0, 0), (0, a_ctx_pad - a_ctx), (0, 0), (0, 0))).reshape(nb * a_ctx_pad, 3 * D_BR, LANES)
        yh_ctx = _hyena(u_ctx_p, kf_ctx, hy_skip[l], a_in=a_ctx_pad, na=2 * a_ctx_pad, n_batch=nb)
        yh_ctx = yh_ctx.reshape(nb, a_ctx_pad, D_BR, LANES)[:, :a_ctx].reshape(nb * a_ctx, D_BR, LANES)
        yh_tm = _slab_to_tm(jnp.concatenate([yh_lat, yh_ctx], axis=0))

        pq = _fn_mix(z_s, a_n=a_lat, n_batch=nb, chan_lo=c_fn)
        yk = _fn_seq(pq, a_n=a_lat, n_batch=nb)
        yf_lat = jnp.transpose(yk, (0, 3, 2, 1)).reshape(n_lat, D_BR)
        yf_ctx = _fn_small(z_s, a_n=a_ctx, n_batch=nb, slab0=nb * a_lat, chan_lo=c_fn).reshape(nb * lc, D_BR)
        yf_tm = jnp.concatenate([yf_lat, yf_ctx], axis=0)

        qk_tm = _slab_to_tm(jnp.concatenate([u_lat[:, c_qk:], u_ctx[:, c_qk:]], axis=0))
        gt = jnp.transpose(z_tm[:, g_col * LANES:g_col * LANES + 16])
        h_f, h_b = _mlstm(qk_tm, z_tm, gt, n_batch=nb, lat_chunks=a_lat, ctx_chunks=a_ctx, v_col=0, g_col=g_col)

        xs = _merge(yh_tm, yf_tm, h_f, h_b, z_tm, xs, mods3, w_branch[l].astype(BF16), w_out[l].astype(BF16),
                    ml_norm_w[l], seg=seg, tm=tm, o_col=1, gate_col0=1)

        rw = jnp.concatenate([moe_rg_w[l], moe_re_w[l], jnp.zeros((d, LANES - MOE_GROUPS - MOE_EXPERTS), F32)], axis=1)
        rb = jnp.concatenate([moe_rg_b[l], moe_re_b[l], jnp.zeros((LANES - MOE_GROUPS - MOE_EXPERTS,), F32)]).reshape(1, LANES)
        xn2, comb = _router(xs, norm2_w[l], mods3, rw, rb, seg=seg, tm=tm)
        xs = _moe(xn2, comb, moe_w_gate[l].astype(BF16), moe_w_up[l].astype(BF16), moe_w_down[l].astype(BF16),
                  xs, mods3, seg=seg_moe, tm=tm_moe)

    return _final_norm(xs, norm_f_w, n_lat, tm).reshape(nb, seq, d)
```

```python
import functools
import math

import numpy as np
import jax
import jax.numpy as jnp
from jax import lax
from jax.experimental import pallas as pl
from jax.experimental.pallas import tpu as pltpu

F32 = jnp.float32
BF16 = jnp.bfloat16

D_MODEL = 1024
D_BR = 512
GRID_W = 64
LANES = 128
CB = 8
HY_ORDER = 2
HY_BANDS = 16
FN_GROUPS = 4
ML_HEADS = 4
MOE_GROUPS = 4
MOE_PER_GROUP = 4
MOE_EXPERTS = 16
EXPERT_HID = 256
EPS = 1e-6
VMEM_LIMIT = 56 * 1024 * 1024


def _params(sem):
    return pltpu.CompilerParams(dimension_semantics=sem, vmem_limit_bytes=VMEM_LIMIT)


def _bdot(a, b):
    return jnp.dot(a.astype(BF16), b.astype(BF16), preferred_element_type=F32)


def _split3(x):
    hi = x.astype(BF16)
    r1 = x - hi.astype(F32)
    mid = r1.astype(BF16)
    lo = (r1 - mid.astype(F32)).astype(BF16)
    return hi, mid, lo


def _dot_f32ish(x, w):
    xh, xm, xl = _split3(x)
    wh, wm, wl = _split3(w)
    d = lambda a, b: jnp.dot(a, b, preferred_element_type=F32)
    return (d(xh, wh) + (d(xh, wm) + d(xm, wh))) + (d(xm, wm) + d(xh, wl) + d(xl, wh))


def _swap_halves(x):
    return jnp.concatenate([x[..., LANES:], x[..., :LANES]], axis=-1)


def _mods_kernel(c_ref, w_ref, b_ref, o_ref):
    c = c_ref[...]
    s = c * jax.nn.sigmoid(c)
    o_ref[...] = _dot_f32ish(s, w_ref[...]) + b_ref[...]


def _mods(cvec, ada_w, ada_b):
    depth, d, n6 = ada_w.shape
    tn = 1536
    return pl.pallas_call(
        _mods_kernel,
        grid=(depth, n6 // tn),
        in_specs=[pl.BlockSpec((8, d), lambda l, j: (0, 0)),
                  pl.BlockSpec((None, d, tn), lambda l, j: (l, 0, j)),
                  pl.BlockSpec((None, 1, tn), lambda l, j: (l, 0, j))],
        out_specs=pl.BlockSpec((None, 8, tn), lambda l, j: (l, 0, j)),
        out_shape=jax.ShapeDtypeStruct((depth, 8, n6), F32),
        compiler_params=_params(("arbitrary", "arbitrary")),
        name="adaln_mods",
    )(cvec, ada_w, ada_b.reshape(depth, 1, n6))


def _norm_mod_kernel(x_ref, w_ref, sh_ref, sc_ref, o_ref):
    x = x_ref[...]
    y = x * lax.rsqrt(jnp.mean(x * x, axis=-1, keepdims=True) + EPS) * w_ref[...]
    o_ref[...] = (y * (1.0 + sc_ref[...]) + sh_ref[...]).astype(o_ref.dtype)


def _seg_fn(n_lat_tiles, tiles_per_batch, n_batch):
    def seg(i):
        return jnp.where(i < n_lat_tiles, i // tiles_per_batch, n_batch)
    return seg


def _norm_mod(x, w, mods3, col_shift, col_scale, seg, tm):
    nt, d = x.shape
    return pl.pallas_call(
        _norm_mod_kernel,
        grid=(nt // tm,),
        in_specs=[pl.BlockSpec((tm, d), lambda i: (i, 0)),
                  pl.BlockSpec((1, d), lambda i: (0, 0)),
                  pl.BlockSpec((None, 1, d), lambda i: (seg(i), 0, col_shift)),
                  pl.BlockSpec((None, 1, d), lambda i: (seg(i), 0, col_scale))],
        out_specs=pl.BlockSpec((tm, d), lambda i: (i, 0)),
        out_shape=jax.ShapeDtypeStruct((nt, d), BF16),
        compiler_params=_params(("arbitrary",)),
        name="norm_mod",
    )(x, w.reshape(1, d), mods3, mods3)


def _mm_tm_kernel(x_ref, w_ref, b_ref, o_ref):
    o_ref[...] = jnp.dot(x_ref[...], w_ref[...], preferred_element_type=F32) + b_ref[...]


def _pick(n, cands):
    for c in cands:
        if n % c == 0:
            return c
    raise ValueError(f"no tile for {n} in {cands}")


def _mm_tm(xn, w, b):
    nt, k = xn.shape
    n = w.shape[1]
    tm = _pick(nt, (1056, 1024, 768, 512, 256))
    tn = _pick(n, (1408, 1024, 512, 384, 256, 128))
    return pl.pallas_call(
        _mm_tm_kernel,
        grid=(n // tn, nt // tm),
        in_specs=[pl.BlockSpec((tm, k), lambda j, i: (i, 0)),
                  pl.BlockSpec((k, tn), lambda j, i: (0, j)),
                  pl.BlockSpec((1, tn), lambda j, i: (0, j))],
        out_specs=pl.BlockSpec((tm, tn), lambda j, i: (i, j)),
        out_shape=jax.ShapeDtypeStruct((nt, n), F32),
        compiler_params=_params(("arbitrary", "arbitrary")),
        name="inproj_token_major",
    )(xn, w, b.reshape(1, n))


def _mm_slab_kernel(w_ref, x_ref, b_ref, o_ref, *, slabs):
    w = w_ref[...]
    b = b_ref[...]
    step = 2 if slabs % 2 == 0 else 1
    for s in range(0, slabs, step):
        xs = x_ref[s * LANES:(s + step) * LANES, :]
        y = lax.dot_general(w, xs, (((1,), (1,)), ((), ())), preferred_element_type=F32) + b
        for i in range(step):
            o_ref[s + i] = y[:, i * LANES:(i + 1) * LANES]


def _mm_slab(xn, wt, b):
    nt, k = xn.shape
    c = wt.shape[0]
    ns = nt // LANES
    ts = _pick(ns, (12, 11, 8, 6, 4, 3, 2, 1))
    tc = _pick(c, (512, 256, 128))
    return pl.pallas_call(
        functools.partial(_mm_slab_kernel, slabs=ts),
        grid=(c // tc, ns // ts),
        in_specs=[pl.BlockSpec((tc, k), lambda j, i: (j, 0)),
                  pl.BlockSpec((ts * LANES, k), lambda j, i: (i, 0)),
                  pl.BlockSpec((tc, 1), lambda j, i: (j, 0))],
        out_specs=pl.BlockSpec((ts, tc, LANES), lambda j, i: (i, j, 0)),
        out_shape=jax.ShapeDtypeStruct((ns, c, LANES), F32),
        compiler_params=_params(("arbitrary", "arbitrary")),
        name="inproj_slab",
    )(wt, xn, b.reshape(c, 1))


def _conv_taps(rows, width):
    taps = []
    for dr in (-1, 0, 1):
        if rows == 1 and dr != 0:
            continue
        for dw in (-1, 0, 1):
            taps.append((dr, dw))
    return taps


def _conv_kernel(x_ref, w_ref, b_ref, o_ref, *, taps, width, n_slabs):
    ct = x_ref.shape[1]
    lane = lax.broadcasted_iota(jnp.int32, (ct, LANES), 1)
    bias = jnp.zeros((ct, LANES), F32) + b_ref[...]
    planes = []
    for t, (dr, dw) in enumerate(taps):
        w = w_ref[t]
        if width < LANES and dw != 0:
            col = lane % width + dw
            w = jnp.where((col >= 0) & (col < width), w, 0.0)
        planes.append((dr * width + dw, w))

    def body(a, carry):
        x0 = x_ref[a]
        xm = jnp.where(a > 0, x_ref[jnp.maximum(a - 1, 0)], 0.0)
        xp = jnp.where(a < n_slabs - 1, x_ref[jnp.minimum(a + 1, n_slabs - 1)], 0.0)
        acc = bias
        for delta, w in planes:
            if delta == 0:
                src = x0
            elif delta > 0:
                src = pltpu.roll(jnp.where(lane >= delta, x0, xp), LANES - delta, 1)
            else:
                src = pltpu.roll(jnp.where(lane < LANES + delta, x0, xm), -delta, 1)
            acc = acc + src * w
        o_ref[a] = acc
        return carry

    lax.fori_loop(0, n_slabs, body, 0, unroll=2 if n_slabs % 2 == 0 else 1)


def _conv_grid_kernel(x_ref, w_ref, b_ref, o_ref, ym_ref, yp_ref, *, width, n_slabs):
    ct = x_ref.shape[1]
    lane = lax.broadcasted_iota(jnp.int32, (ct, LANES), 1)
    col = lane % width
    wl = [jnp.where(col >= 1, w_ref[3 * i], 0.0) for i in range(3)]
    wc = [w_ref[3 * i + 1] for i in range(3)]
    wr = [jnp.where(col < width - 1, w_ref[3 * i + 2], 0.0) for i in range(3)]
    bias = jnp.zeros((ct, LANES), F32) + b_ref[...]

    def row_sums(a, carry):
        x0 = x_ref[a]
        xl = pltpu.roll(x0, 1, 1)
        xr = pltpu.roll(x0, LANES - 1, 1)
        ym_ref[a] = wl[0] * xl + wc[0] * x0 + wr[0] * xr
        o_ref[a] = bias + wl[1] * xl + wc[1] * x0 + wr[1] * xr
        yp_ref[a] = wl[2] * xl + wc[2] * x0 + wr[2] * xr
        return carry

    def combine(a, carry):
        up = jnp.where(a > 0, ym_ref[jnp.maximum(a - 1, 0)], 0.0)
        dn = jnp.where(a < n_slabs - 1, yp_ref[jnp.minimum(a + 1, n_slabs - 1)], 0.0)
        from_up = jnp.where(lane < LANES - width, ym_ref[a], up)
        from_dn = jnp.where(lane >= width, yp_ref[a], dn)
        if 2 * width == LANES:
            o_ref[a] = o_ref[a] + pltpu.roll(from_up + from_dn, width, 1)
        else:
            o_ref[a] = o_ref[a] + pltpu.roll(from_up, width, 1) + pltpu.roll(from_dn, LANES - width, 1)
        return carry

    unroll = 4 if n_slabs % 4 == 0 else 1
    lax.fori_loop(0, n_slabs, row_sums, 0, unroll=unroll)
    lax.fori_loop(0, n_slabs, combine, 0, unroll=unroll)


def _conv(z_s, w9, bias, *, rows, width, n_batch, slab0, chan_lo, chan_n):
    seq = rows * width
    a_n = seq // LANES
    taps = tuple(_conv_taps(rows, width))
    assert all(abs(dr * width + dw) < LANES for dr, dw in taps)
    assert LANES % width == 0 or (rows == 1 and width % LANES == 0)
    tap_ids = [(dr + 1) * 3 + (dw + 1) for dr, dw in taps]
    w_t = jnp.broadcast_to(w9[jnp.array(tap_ids)][:, :, None], (len(taps), chan_n, LANES))
    ct = 64
    assert chan_lo % ct == 0 and chan_n % ct == 0 and slab0 % a_n == 0
    nt_ = len(taps)
    if rows > 1:
        assert LANES % width == 0 and nt_ == 9
        body = functools.partial(_conv_grid_kernel, width=width, n_slabs=a_n)
        scratch = [pltpu.VMEM((a_n, ct, LANES), F32), pltpu.VMEM((a_n, ct, LANES), F32)]
    else:
        body = functools.partial(_conv_kernel, taps=taps, width=width, n_slabs=a_n)
        scratch = []
    return pl.pallas_call(
        body,
        scratch_shapes=scratch,
        grid=(n_batch, chan_n // ct),
        in_specs=[pl.BlockSpec((a_n, ct, LANES), lambda b, j: (slab0 // a_n + b, chan_lo // ct + j, 0)),
                  pl.BlockSpec((nt_, ct, LANES), lambda b, j: (0, j, 0)),
                  pl.BlockSpec((ct, 1), lambda b, j: (j, 0))],
        out_specs=pl.BlockSpec((a_n, ct, LANES), lambda b, j: (b, j, 0)),
        out_shape=jax.ShapeDtypeStruct((n_batch * a_n, chan_n, LANES), F32),
        compiler_params=_params(("arbitrary", "arbitrary")),
        name=f"dwconv_{rows}x{width}",
    )(z_s, w_t, bias.reshape(chan_n, 1))


def _dft_consts(a_in, na):
    n = na * LANES
    k = np.arange(na)[:, None]
    a = np.arange(a_in)[None, :]
    ang = 2 * np.pi * (k * a % na) / na
    fa = np.concatenate([np.cos(ang), -np.sin(ang)], axis=0)
    r = np.arange(LANES)
    ang_t = 2 * np.pi * (np.arange(na)[:, None] * r[None, :] % n) / n
    tr, ti = np.cos(ang_t), -np.sin(ang_t)
    ta = np.concatenate([tr, tr], axis=1)
    tb = np.concatenate([-ti, ti], axis=1)
    ang2 = 2 * np.pi * (r[:, None] * r[None, :] % LANES) / LANES
    c2, s2 = np.cos(ang2), np.sin(ang2)
    g2 = np.block([[c2, -s2], [s2, c2]])
    g2i = np.block([[c2, s2], [-s2, c2]])
    ang_i = 2 * np.pi * (np.arange(a_in)[:, None] * np.arange(na)[None, :] % na) / na
    ci, si = np.cos(ang_i) / n, -np.sin(ang_i) / n
    f = lambda v, dt: jnp.asarray(v, dtype=dt)
    return dict(fa=f(fa, F32), ta=f(ta, F32), tb=f(tb, F32), g2=f(g2, F32), g2i=f(g2i, F32),
                ci=f(ci, F32), si=f(si, F32))


def _fwd_slab_stage(m, fa, ta, tb, na):
    pp = jnp.dot(fa, m.astype(BF16), preferred_element_type=F32)
    p = jnp.concatenate([pp[:na], pp[na:]], axis=1)
    return p * ta + _swap_halves(p) * tb


def _cmul(x, kf):
    kr, ki = kf[..., :LANES], kf[..., LANES:]
    ka = jnp.concatenate([kr, kr], axis=-1)
    kb = jnp.concatenate([-ki, ki], axis=-1)
    return x * ka + _swap_halves(x) * kb


def _chan_load(ref, c):
    n, cb, _ = ref.shape
    return ref.reshape(n * cb, LANES)[pl.ds(c, n, stride=cb), :]


def _chan_store(ref, c, val):
    ref[:, c, :] = val


def _filt_kernel(k_ref, fa_ref, ta_ref, tb_ref, g2_ref, o_ref, p_buf, *, na):
    fa, ta, tb = fa_ref[...].astype(BF16), ta_ref[...], tb_ref[...]
    scales = []
    for c in range(CB):
        m = _chan_load(k_ref, c)
        ss = jnp.sum(jnp.sum(m * m, axis=1, keepdims=True), axis=0, keepdims=True)
        scales.append(lax.rsqrt(ss + EPS))
        p_buf[c] = _fwd_slab_stage(m, fa, ta, tb, na)
    x = _bdot(p_buf[...].reshape(CB * na, 2 * LANES), g2_ref[...]).reshape(CB, na, 2 * LANES)
    for c in range(CB):
        o_ref[c] = x[c] * scales[c]


def _filter_spectrum(k_s, na):
    nc = k_s.shape[1]
    cs = _dft_consts(na, na)
    const = lambda shp: pl.BlockSpec(shp, lambda j: (0,) * len(shp))
    return pl.pallas_call(
        functools.partial(_filt_kernel, na=na),
        grid=(nc // CB,),
        in_specs=[pl.BlockSpec((na, CB, LANES), lambda j: (0, j, 0)),
                  const((2 * na, na)), const((na, 2 * LANES)), const((na, 2 * LANES)),
                  const((2 * LANES, 2 * LANES))],
        out_specs=pl.BlockSpec((CB, na, 2 * LANES), lambda j: (j, 0, 0)),
        out_shape=jax.ShapeDtypeStruct((nc, na, 2 * LANES), F32),
        scratch_shapes=[pltpu.VMEM((CB, na, 2 * LANES), F32)],
        compiler_params=_params(("arbitrary",)),
        name="hyena_filter_dft",
    )(k_s, cs["fa"], cs["ta"], cs["tb"], cs["g2"])


def _dot_f32ish_k(w, h):
    wh, wm, _ = _split3(w)
    hh, hm, _ = _split3(h)
    lhs = jnp.concatenate([wh, wh, wm], axis=1)
    rhs = jnp.concatenate([hh, hm, hh], axis=0)
    return jnp.dot(lhs, rhs, preferred_element_type=F32)


def _taps_kernel(bands_ref, w1t_ref, w1c_ref, w1s_ref, b1_ref, w2_ref, b2_ref, fq_ref, w3_ref, dec_ref, o_ref,
                 *, seq, a_seq, na, spb):
    step = pl.program_id(0)
    lane = lax.broadcasted_iota(jnp.int32, (1, LANES), 1)
    fq = fq_ref[...]
    n_total = na * LANES
    for i in range(spb):
        a = step * spb + i
        is_f = a < a_seq
        is_b = a >= na - a_seq
        live = jnp.logical_or(is_f, is_b)

        @pl.when(live)
        def _():
            n = a * LANES + lane
            pos = jnp.where(is_f, n, n_total - n)
            valid = (n > jnp.where(is_f, -1, n_total - seq)) & (n < jnp.where(is_f, seq, n_total))
            t = pos.astype(F32) / seq
            ang = ((2 * math.pi) * t) * bands_ref[...]
            pre = (w1t_ref[...] * t + _dot_f32ish(w1c_ref[...], jnp.cos(ang))
                   + _dot_f32ish(w1s_ref[...], jnp.sin(ang)))
            h = jnp.sin(fq * (pre + b1_ref[...]))
            h = jnp.sin(fq * (_dot_f32ish(w2_ref[...], h) + b2_ref[...]))
            d = jnp.where(is_f, 0, 1)
            k = _dot_f32ish_k(w3_ref[d], h) * jnp.exp(-t * jnp.abs(dec_ref[d]))
            o_ref[i] = jnp.where(valid, k, 0.0)

        @pl.when(jnp.logical_not(live))
        def _():
            o_ref[i] = jnp.zeros(o_ref.shape[1:], F32)


def _hyena_taps(seq, na, lp):
    nc = HY_ORDER * D_BR
    hid = lp["hy_f_w2"].shape[0]
    w1 = lp["hy_f_w1"]
    col = lambda v: v.reshape(-1, 1)
    bands = col(jnp.linspace(1e-4, HY_BANDS - 1, HY_BANDS, dtype=F32))
    w3 = jnp.transpose(lp["hy_f_w3"].T.reshape(HY_ORDER, 2, D_BR, hid), (1, 0, 2, 3)).reshape(2, nc, hid)
    dec = jnp.broadcast_to(jnp.transpose(lp["hy_decay"], (1, 0, 2)).reshape(2, nc, 1), (2, nc, LANES))
    spb = min(8, na)
    assert na % spb == 0
    args = (bands, col(w1[0]), w1[1:1 + HY_BANDS].T, w1[1 + HY_BANDS:].T, col(lp["hy_f_b1"]), lp["hy_f_w2"].T,
            col(lp["hy_f_b2"]), col(lp["hy_f_freq"]), w3, dec)
    full = lambda v: pl.BlockSpec(v.shape, lambda s: (0,) * v.ndim)
    return pl.pallas_call(
        functools.partial(_taps_kernel, seq=seq, a_seq=seq // LANES, na=na, spb=spb),
        grid=(na // spb,),
        in_specs=[full(v) for v in args],
        out_specs=pl.BlockSpec((spb, nc, LANES), lambda s: (s, 0, 0)),
        out_shape=jax.ShapeDtypeStruct((na, nc, LANES), F32),
        compiler_params=_params(("arbitrary",)),
        name=f"hyena_filter_taps_{na}",
    )(*args)


def _hyena_kernel(v_ref, x1_ref, x2_ref, kf_ref, skip_ref, fa_ref, ta_ref, tb_ref, g2_ref, g2i_ref,
                  ci_ref, si_ref, o_ref, p_buf, z_buf, *, a_in, na):
    fa, ta, tb = fa_ref[...].astype(BF16), ta_ref[...], tb_ref[...]
    ci, si = ci_ref[...].astype(BF16), si_ref[...].astype(BF16)

    def spectral(order):
        x = _bdot(p_buf[...].reshape(CB * na, 2 * LANES), g2_ref[...])
        y = _cmul(x, kf_ref[order].reshape(CB * na, 2 * LANES))
        bm = _bdot(y, g2i_ref[...]).reshape(CB, na, 2 * LANES)
        p_buf[...] = bm * ta - _swap_halves(bm) * tb

    def conv_out(c):
        bb = p_buf[c]
        return (jnp.dot(ci, bb[:, :LANES].astype(BF16), preferred_element_type=F32)
                + jnp.dot(si, bb[:, LANES:].astype(BF16), preferred_element_type=F32))

    for c in range(CB):
        p_buf[c] = _fwd_slab_stage(_chan_load(v_ref, c), fa, ta, tb, na)
    spectral(0)
    for c in range(CB):
        z_buf[c] = _chan_load(x1_ref, c) * (conv_out(c) + _chan_load(v_ref, c) * skip_ref[0, c])
    for c in range(CB):
        p_buf[c] = _fwd_slab_stage(z_buf[c], fa, ta, tb, na)
    spectral(1)
    for c in range(CB):
        _chan_store(o_ref, c, _chan_load(x2_ref, c) * (conv_out(c) + z_buf[c] * skip_ref[1, c]))


def _hyena(u_s, kf, skip, *, a_in, na, n_batch):
    cs = _dft_consts(a_in, na)
    nblk = D_BR // CB
    const = lambda shp: pl.BlockSpec(shp, lambda j, b: (0,) * len(shp))
    skip_b = jnp.broadcast_to(skip[:, :, None, None], (HY_ORDER, D_BR, 1, LANES))
    return pl.pallas_call(
        functools.partial(_hyena_kernel, a_in=a_in, na=na),
        grid=(nblk, n_batch),
        in_specs=[pl.BlockSpec((a_in, CB, LANES), lambda j, b: (b, j, 0)),
                  pl.BlockSpec((a_in, CB, LANES), lambda j, b: (b, nblk + j, 0)),
                  pl.BlockSpec((a_in, CB, LANES), lambda j, b: (b, 2 * nblk + j, 0)),
                  pl.BlockSpec((HY_ORDER, CB, na, 2 * LANES), lambda j, b: (0, j, 0, 0)),
                  pl.BlockSpec((HY_ORDER, CB, 1, LANES), lambda j, b: (0, j, 0, 0)),
                  const((2 * na, a_in)), const((na, 2 * LANES)), const((na, 2 * LANES)),
                  const((2 * LANES, 2 * LANES)), const((2 * LANES, 2 * LANES)),
                  const((a_in, na)), const((a_in, na))],
        out_specs=pl.BlockSpec((a_in, CB, LANES), lambda j, b: (b, j, 0)),
        out_shape=jax.ShapeDtypeStruct((n_batch * a_in, D_BR, LANES), F32),
        scratch_shapes=[pltpu.VMEM((CB, na, 2 * LANES), F32), pltpu.VMEM((CB, a_in, LANES), F32)],
        compiler_params=_params(("arbitrary", "arbitrary")),
        name=f"hyena_longconv_{a_in}",
    )(u_s, u_s, u_s, kf, skip_b, cs["fa"], cs["ta"], cs["tb"], cs["g2"], cs["g2i"], cs["ci"], cs["si"])


def _slabs_to_rows(ref, n):
    return jnp.concatenate([ref[a].T for a in range(n)], axis=0)


def _hyena_short_kernel(v_ref, x1_ref, x2_ref, k0_ref, k1_ref, skip_ref, f_ref, g_ref, o_ref, *, a_n, a_k):
    seq, nf = a_n * LANES, a_k * LANES
    ff = f_ref[...].astype(BF16)
    gi = g_ref[...].astype(BF16)
    v, x1, x2 = _slabs_to_rows(v_ref, a_n), _slabs_to_rows(x1_ref, a_n), _slabs_to_rows(x2_ref, a_n)

    def longconv(u, k_ref):
        k = _slabs_to_rows(k_ref, a_k)
        s = lax.rsqrt(jnp.sum(k * k, axis=0, keepdims=True) + EPS)
        kf = jnp.dot(ff, k.astype(BF16), preferred_element_type=F32) * s
        x = jnp.dot(ff[:, :seq], u.astype(BF16), preferred_element_type=F32)
        xr, xi, kr, ki = x[:nf], x[nf:], kf[:nf], kf[nf:]
        y = jnp.concatenate([xr * kr - xi * ki, xr * ki + xi * kr], axis=0)
        return jnp.dot(gi, y.astype(BF16), preferred_element_type=F32)

    z = x1 * (longconv(v, k0_ref) + v * skip_ref[0])
    o_ref[...] = x2 * (longconv(z, k1_ref) + z * skip_ref[1])


def _hyena_short(u_s, taps_s, skip, *, a_n, n_batch):
    a_k = 2 * a_n
    seq, nf = a_n * LANES, a_k * LANES
    k = np.arange(nf)
    ang = 2 * np.pi * (k[:, None] * k[None, :] % nf) / nf
    f = np.concatenate([np.cos(ang), -np.sin(ang)], axis=0)
    g = np.concatenate([np.cos(ang[:seq]), -np.sin(ang[:seq])], axis=1) / nf
    nblk = D_BR // LANES
    const = lambda shp: pl.BlockSpec(shp, lambda b, j: (0,) * len(shp))
    return pl.pallas_call(
        functools.partial(_hyena_short_kernel, a_n=a_n, a_k=a_k),
        grid=(n_batch, nblk),
        in_specs=[pl.BlockSpec((a_n, LANES, LANES), lambda b, j: (b, j, 0)),
                  pl.BlockSpec((a_n, LANES, LANES), lambda b, j: (b, nblk + j, 0)),
                  pl.BlockSpec((a_n, LANES, LANES), lambda b, j: (b, 2 * nblk + j, 0)),
                  pl.BlockSpec((a_k, LANES, LANES), lambda b, j: (0, j, 0)),
                  pl.BlockSpec((a_k, LANES, LANES), lambda b, j: (0, nblk + j, 0)),
                  pl.BlockSpec((HY_ORDER, 1, LANES), lambda b, j: (0, 0, j)),
                  const((2 * nf, nf)), const((seq, 2 * nf))],
        out_specs=pl.BlockSpec((None, seq, LANES), lambda b, j: (b, 0, j)),
        out_shape=jax.ShapeDtypeStruct((n_batch, seq, D_BR), F32),
        compiler_params=_params(("arbitrary", "arbitrary")),
        name="hyena_short",
    )(u_s, u_s, u_s, taps_s, taps_s, skip.reshape(HY_ORDER, 1, D_BR), jnp.asarray(f, F32), jnp.asarray(g, F32))


def _chan_dft_mats():
    r = np.arange(LANES)
    ang = 2 * np.pi * (r[:, None] * r[None, :] % LANES) / LANES
    return np.cos(ang), np.sin(ang)


def _fn_mix_kernel(u_ref, cs_ref, o_ref, *, n_slabs):
    w = cs_ref[...].astype(BF16)

    def body(a, carry):
        o_ref[a] = jnp.dot(w, u_ref[a].astype(BF16), preferred_element_type=F32)
        return carry

    lax.fori_loop(0, n_slabs, body, 0)


def _fn_mix(z_s, *, a_n, n_batch, chan_lo):
    c, s = _chan_dft_mats()
    w = jnp.asarray(np.concatenate([c, s], axis=0), dtype=F32)
    g0 = chan_lo // LANES
    return pl.pallas_call(
        functools.partial(_fn_mix_kernel, n_slabs=a_n),
        grid=(n_batch, FN_GROUPS),
        in_specs=[pl.BlockSpec((a_n, LANES, LANES), lambda b, g: (b, g0 + g, 0)),
                  pl.BlockSpec((2 * LANES, LANES), lambda b, g: (0, 0))],
        out_specs=pl.BlockSpec((a_n, 2 * LANES, LANES), lambda b, g: (b, g, 0)),
        out_shape=jax.ShapeDtypeStruct((n_batch * a_n, 2 * D_BR, LANES), F32),
        compiler_params=_params(("arbitrary", "arbitrary")),
        name="fnet_channel_dft",
    )(z_s, w)


def _fn_seq_kernel(p_ref, q_ref, fa_ref, tr_ref, ti_ref, g_ref, o_ref, a_buf, *, a_n, scale):
    fa, tr, ti = fa_ref[...].astype(BF16), tr_ref[...], ti_ref[...]
    for c in range(CB):
        r1 = jnp.dot(fa, _chan_load(p_ref, c).astype(BF16), preferred_element_type=F32)
        r2 = jnp.dot(fa, _chan_load(q_ref, c).astype(BF16), preferred_element_type=F32)
        ar = r1[:a_n] - r2[a_n:]
        ai = -(r2[:a_n] + r1[a_n:])
        a_buf[c] = jnp.concatenate([ar * tr - ai * ti, ar * ti + ai * tr], axis=1)
    y = _bdot(a_buf[...].reshape(CB * a_n, 2 * LANES), g_ref[...]) * scale
    o_ref[...] = y.reshape(CB, a_n, LANES)


def _fn_seq(pq, *, a_n, n_batch):
    seq = a_n * LANES
    k = np.arange(a_n)
    ang = 2 * np.pi * (k[:, None] * k[None, :] % a_n) / a_n
    fa = np.concatenate([np.cos(ang), np.sin(ang)], axis=0)
    r = np.arange(LANES)
    ang_t = 2 * np.pi * (k[:, None] * r[None, :] % seq) / seq
    c2, s2 = _chan_dft_mats()
    g = np.concatenate([c2, s2], axis=0)
    nblk = LANES // CB
    const = lambda shp: pl.BlockSpec(shp, lambda b, j: (0,) * len(shp))

    def chan_blk(j, off):
        return (j // nblk) * (2 * nblk) + off * nblk + j % nblk

    return pl.pallas_call(
        functools.partial(_fn_seq_kernel, a_n=a_n, scale=1.0 / math.sqrt(seq * LANES)),
        grid=(n_batch, D_BR // CB),
        in_specs=[pl.BlockSpec((a_n, CB, LANES), lambda b, j: (b, chan_blk(j, 0), 0)),
                  pl.BlockSpec((a_n, CB, LANES), lambda b, j: (b, chan_blk(j, 1), 0)),
                  const((2 * a_n, a_n)), const((a_n, LANES)), const((a_n, LANES)), const((2 * LANES, LANES))],
        out_specs=pl.BlockSpec((None, CB, a_n, LANES), lambda b, j: (b, j, 0, 0)),
        out_shape=jax.ShapeDtypeStruct((n_batch, D_BR, a_n, LANES), F32),
        scratch_shapes=[pltpu.VMEM((CB, a_n, 2 * LANES), F32)],
        compiler_params=_params(("arbitrary", "arbitrary")),
        name="fnet_sequence_dft",
    )(pq, pq, jnp.asarray(fa, F32), jnp.asarray(np.cos(ang_t), F32), jnp.asarray(-np.sin(ang_t), F32),
      jnp.asarray(g, F32))


def _fn_small_kernel(u_ref, cw_ref, sw_ref, cl_ref, sl_ref, o_ref, *, a_n, scale):
    u = jnp.concatenate([u_ref[a].T for a in range(a_n)], axis=0)
    p = _bdot(u, cw_ref[...])
    q = _bdot(u, sw_ref[...])
    o_ref[...] = (_bdot(cl_ref[...], p) - _bdot(sl_ref[...], q)) * scale


def _fn_small(z_s, *, a_n, n_batch, slab0, chan_lo):
    seq = a_n * LANES
    cw, sw = _chan_dft_mats()
    n = np.arange(seq)
    ang = 2 * np.pi * (n[:, None] * n[None, :] % seq) / seq
    const = lambda shp: pl.BlockSpec(shp, lambda b, g: (0,) * len(shp))
    g0 = chan_lo // LANES
    return pl.pallas_call(
        functools.partial(_fn_small_kernel, a_n=a_n, scale=1.0 / math.sqrt(seq * LANES)),
        grid=(n_batch, FN_GROUPS),
        in_specs=[pl.BlockSpec((a_n, LANES, LANES), lambda b, g: (slab0 // a_n + b, g0 + g, 0)),
                  const((LANES, LANES)), const((LANES, LANES)), const((seq, seq)), const((seq, seq))],
        out_specs=pl.BlockSpec((None, seq, LANES), lambda b, g: (b, 0, g)),
        out_shape=jax.ShapeDtypeStruct((n_batch, seq, D_BR), F32),
        compiler_params=_params(("arbitrary", "arbitrary")),
        name="fnet_short",
    )(z_s, jnp.asarray(cw, F32), jnp.asarray(sw, F32), jnp.asarray(np.cos(ang), F32), jnp.asarray(np.sin(ang), F32))


def _log_sigmoid(x):
    return jnp.minimum(x, 0.0) - jnp.log(1.0 + jnp.exp(-jnp.abs(x)))


def _exact_tri_dot(tri, x, tri_on_left):
    h, m, l = _split3(x)
    if tri_on_left:
        d = lambda p: jnp.dot(tri, p, preferred_element_type=F32)
    else:
        d = lambda p: jnp.dot(p, tri, preferred_element_type=F32)
    return d(h) + d(m) + d(l)


def _mlstm_dir(q_ref, k_ref, v_ref, g_ref, h_ref, c_st, n_st, m_st, *, d, reverse):
    t = LANES
    hd = LANES
    row = lax.broadcasted_iota(jnp.int32, (t, t), 0)
    col = lax.broadcasted_iota(jnp.int32, (t, t), 1)
    lower = (col <= row)
    tri = jnp.where(lower, 1.0, 0.0).astype(BF16)
    tri_t = jnp.where(col >= row, 1.0, 0.0).astype(BF16)
    mask = (col >= row) if reverse else lower
    g = g_ref[...]
    gt = g.T
    lf_c = _log_sigmoid(g)
    lf_r = lf_c.T
    if reverse:
        b_c = _exact_tri_dot(tri_t, lf_c, True)
        b_r = _exact_tri_dot(tri, lf_r, False)
    else:
        b_c = _exact_tri_dot(tri, lf_c, True)
        b_r = _exact_tri_dot(tri_t, lf_r, False)
    i_off = 2 * ML_HEADS * d
    f_off = i_off + ML_HEADS
    for h in range(ML_HEADS):
        sl = slice(h * hd, (h + 1) * hd)
        q = q_ref[:, sl]
        q = (q * jax.nn.sigmoid(q)) * (hd ** -0.5)
        k = k_ref[:, sl]
        k = k * jax.nn.sigmoid(k)
        v = v_ref[:, sl]
        bc = b_c[:, f_off + h:f_off + h + 1]
        br = b_r[f_off + h:f_off + h + 1, :]
        ic = g[:, i_off + h:i_off + h + 1]
        ir = gt[i_off + h:i_off + h + 1, :]
        m_prev = m_st[h][:, :1]
        ct = c_st[h]
        n_prev = n_st[h]
        dm = jnp.where(mask, bc - br + ir, -jnp.inf)
        inter = bc + m_prev
        m_row = jnp.maximum(inter, jnp.max(dm, axis=-1, keepdims=True))
        w_intra = jnp.exp(dm - m_row)
        w_inter = jnp.exp(inter - m_row)
        qb, kb, vb = q.astype(BF16), k.astype(BF16), v.astype(BF16)
        s = lax.dot_general(qb, kb, (((1,), (1,)), ((), ())), preferred_element_type=F32) * w_intra
        num = jnp.dot(s.astype(BF16), vb, preferred_element_type=F32) + w_inter * jnp.dot(qb, ct.astype(BF16), preferred_element_type=F32)
        den = jnp.sum(s, axis=-1, keepdims=True) + w_inter * jnp.sum(q * n_prev, axis=-1, keepdims=True)
        den = jnp.maximum(jnp.abs(den), jnp.exp(-m_row))
        h_ref[:, sl] = num / den
        if reverse:
            b_tot_c, b_tot_r = bc[:1, :], br[:, :1]
        else:
            b_tot_c, b_tot_r = bc[t - 1:, :], br[:, t - 1:]
        a_c = b_tot_c - bc + ic
        a_r = b_tot_r - br + ir
        m_new = jnp.maximum(b_tot_c + m_prev, jnp.max(a_r, axis=-1, keepdims=True))
        sc = jnp.exp(a_c - m_new)
        decay = jnp.exp(b_tot_c + m_prev - m_new)
        ks = k * sc
        c_st[h] = decay * ct + lax.dot_general(ks.astype(BF16), vb, (((0,), (0,)), ((), ())), preferred_element_type=F32)
        n_st[h] = decay * n_prev + jnp.sum(ks, axis=0, keepdims=True)
        m_st[h] = jnp.broadcast_to(m_new, (1, LANES))


def _mlstm_kernel(qf, kf, vf, gf, qb, kb, vb, gb, hf_ref, hb_ref, c_st, n_st, m_st):
    @pl.when(pl.program_id(1) == 0)
    def _():
        c_st[...] = jnp.zeros(c_st.shape, F32)
        n_st[...] = jnp.zeros(n_st.shape, F32)
        m_st[...] = jnp.zeros(m_st.shape, F32)

    _mlstm_dir(qf, kf, vf, gf, hf_ref, c_st.at[0], n_st.at[0], m_st.at[0], d=0, reverse=False)
    _mlstm_dir(qb, kb, vb, gb, hb_ref, c_st.at[1], n_st.at[1], m_st.at[1], d=1, reverse=True)


def _mlstm(qk_tm, z_tm, *, n_batch, lat_chunks, ctx_chunks, v_col, g_col):
    nt = qk_tm.shape[0]
    n_lat = n_batch * lat_chunks
    nsteps = ctx_chunks + lat_chunks

    def glob(b, p):
        return jnp.where(p < ctx_chunks, n_lat + b * ctx_chunks + p, b * lat_chunks + p - ctx_chunks)

    def fwd(b, j):
        return glob(b, j)

    def bwd(b, j):
        p = jnp.where(j < ctx_chunks, ctx_chunks - 1 - j, ctx_chunks + lat_chunks - 1 - (j - ctx_chunks))
        return glob(b, p)

    def specs(ix):
        return [pl.BlockSpec((LANES, D_BR), lambda b, j: (ix(b, j), 0)),
                pl.BlockSpec((LANES, D_BR), lambda b, j: (ix(b, j), 1)),
                pl.BlockSpec((LANES, D_BR), lambda b, j: (ix(b, j), v_col)),
                pl.BlockSpec((LANES, LANES), lambda b, j: (ix(b, j), g_col))]

    out_sd = jax.ShapeDtypeStruct((nt, D_BR), F32)
    return pl.pallas_call(
        _mlstm_kernel,
        grid=(n_batch, nsteps),
        in_specs=specs(fwd) + specs(bwd),
        out_specs=[pl.BlockSpec((LANES, D_BR), lambda b, j: (fwd(b, j), 0)),
                   pl.BlockSpec((LANES, D_BR), lambda b, j: (bwd(b, j), 0))],
        out_shape=[out_sd, out_sd],
        scratch_shapes=[pltpu.VMEM((2, ML_HEADS, LANES, LANES), F32),
                        pltpu.VMEM((2, ML_HEADS, 1, LANES), F32),
                        pltpu.VMEM((2, ML_HEADS, 1, LANES), F32)],
        compiler_params=_params(("arbitrary", "arbitrary")),
        name="mlstm_bidir",
    )(qk_tm, qk_tm, z_tm, z_tm, qk_tm, qk_tm, z_tm, z_tm)


def _merge_kernel(yh_ref, yf_ref, hf_ref, hb_ref, o_ref, g0_ref, g1_ref, g2_ref, x_ref, gate_ref,
                  wb_ref, wo_ref, nw_ref, out_ref):
    hd = LANES
    h = hf_ref[...] + hb_ref[...]
    parts = []
    for i in range(ML_HEADS):
        hh = h[:, i * hd:(i + 1) * hd]
        parts.append(hh * lax.rsqrt(jnp.mean(hh * hh, axis=-1, keepdims=True) + EPS))
    y_ml = jax.nn.sigmoid(o_ref[...]) * (jnp.concatenate(parts, axis=1) * nw_ref[...])
    acc = jax.nn.sigmoid(g0_ref[...]) * _bdot(yh_ref[...], wb_ref[0])
    acc = acc + jax.nn.sigmoid(g1_ref[...]) * _bdot(yf_ref[...], wb_ref[1])
    acc = acc + jax.nn.sigmoid(g2_ref[...]) * _bdot(y_ml, wb_ref[2])
    out_ref[...] = x_ref[...] + gate_ref[...] * _bdot(acc, wo_ref[...])


def _merge(yh, yf, hf, hb, z_tm, x, mods3, wb, wo, nw, *, seg, tm, o_col, gate_col0):
    nt, d = x.shape
    tok = lambda w, cidx: pl.BlockSpec((tm, w), lambda i: (i, cidx))
    return pl.pallas_call(
        _merge_kernel,
        grid=(nt // tm,),
        in_specs=[tok(D_BR, 0), tok(D_BR, 0), tok(D_BR, 0), tok(D_BR, 0),
                  tok(D_BR, o_col),
                  tok(d, gate_col0), tok(d, gate_col0 + 1), tok(d, gate_col0 + 2),
                  tok(d, 0),
                  pl.BlockSpec((None, 1, d), lambda i: (seg(i), 0, 2)),
                  pl.BlockSpec((3, D_BR, d), lambda i: (0, 0, 0)),
                  pl.BlockSpec((d, d), lambda i: (0, 0)),
                  pl.BlockSpec((1, D_BR), lambda i: (0, 0))],
        out_specs=pl.BlockSpec((tm, d), lambda i: (i, 0)),
        out_shape=jax.ShapeDtypeStruct((nt, d), F32),
        compiler_params=_params(("arbitrary",)),
        name="merge_branches",
    )(yh, yf, hf, hb, z_tm, z_tm, z_tm, z_tm, x, mods3, wb, wo, nw.reshape(1, D_BR))


def _router_kernel(x_ref, w_ref, sh_ref, sc_ref, rw_ref, rb_ref, xn_ref, comb_ref):
    x = x_ref[...]
    y = x * lax.rsqrt(jnp.mean(x * x, axis=-1, keepdims=True) + EPS) * w_ref[...]
    t = y * (1.0 + sc_ref[...]) + sh_ref[...]
    xn_ref[...] = t.astype(BF16)
    logits = _dot_f32ish(t, rw_ref[...]) + rb_ref[...]
    col = lax.broadcasted_iota(jnp.int32, logits.shape, 1)
    big = jnp.int32(1 << 20)
    ninf = -jnp.inf
    is_g = col < MOE_GROUPS
    gl = jnp.where(is_g, logits, ninf)
    gmax = jnp.max(gl, axis=-1, keepdims=True)
    g_sel = jnp.min(jnp.where(is_g & (gl == gmax), col, big), axis=-1, keepdims=True)
    p_top = 1.0 / jnp.sum(jnp.where(is_g, jnp.exp(gl - gmax), 0.0), axis=-1, keepdims=True)
    lo = MOE_GROUPS + g_sel * MOE_PER_GROUP
    in_grp = (col >= lo) & (col < lo + MOE_PER_GROUP)
    e1v = jnp.where(in_grp, logits, ninf)
    top1 = jnp.max(e1v, axis=-1, keepdims=True)
    idx1 = jnp.min(jnp.where(in_grp & (e1v == top1), col, big), axis=-1, keepdims=True)
    e2v = jnp.where(col == idx1, ninf, e1v)
    top2 = jnp.max(e2v, axis=-1, keepdims=True)
    idx2 = jnp.min(jnp.where(in_grp & (col != idx1) & (e2v == top2), col, big), axis=-1, keepdims=True)
    ex = jnp.exp(top2 - top1)
    s1 = 1.0 / (1.0 + ex)
    comb_ref[...] = jnp.where(col == idx1, p_top * s1, 0.0) + jnp.where(col == idx2, p_top * (ex * s1), 0.0)


def _router(x, w, mods3, rw, rb, *, seg, tm):
    nt, d = x.shape
    return pl.pallas_call(
        _router_kernel,
        grid=(nt // tm,),
        in_specs=[pl.BlockSpec((tm, d), lambda i: (i, 0)),
                  pl.BlockSpec((1, d), lambda i: (0, 0)),
                  pl.BlockSpec((None, 1, d), lambda i: (seg(i), 0, 3)),
                  pl.BlockSpec((None, 1, d), lambda i: (seg(i), 0, 4)),
                  pl.BlockSpec((d, LANES), lambda i: (0, 0)),
                  pl.BlockSpec((1, LANES), lambda i: (0, 0))],
        out_specs=[pl.BlockSpec((tm, d), lambda i: (i, 0)), pl.BlockSpec((tm, LANES), lambda i: (i, 0))],
        out_shape=[jax.ShapeDtypeStruct((nt, d), BF16), jax.ShapeDtypeStruct((nt, LANES), F32)],
        compiler_params=_params(("arbitrary",)),
        name="norm_router",
    )(x, w.reshape(1, d), mods3, mods3, rw, rb)


def _moe_kernel(xn_ref, comb_ref, wg_ref, wu_ref, wd_ref, x_ref, gate_ref, o_ref, acc_ref):
    e = pl.program_id(1)

    @pl.when(e == 0)
    def _():
        acc_ref[...] = jnp.zeros(acc_ref.shape, F32)

    xn = xn_ref[...]
    comb = comb_ref[...]
    col = lax.broadcasted_iota(jnp.int32, comb.shape, 1)
    cw = jnp.sum(jnp.where(col == e + MOE_GROUPS, comb, 0.0), axis=-1, keepdims=True)
    hg = jnp.dot(xn, wg_ref[...], preferred_element_type=F32)
    hu = jnp.dot(xn, wu_ref[...], preferred_element_type=F32)
    a = (hg * jax.nn.sigmoid(hg)) * hu * cw
    acc_ref[...] += jnp.dot(a.astype(BF16), wd_ref[...], preferred_element_type=F32)

    @pl.when(e == MOE_EXPERTS - 1)
    def _():
        o_ref[...] = x_ref[...] + gate_ref[...] * acc_ref[...]


def _moe(xn, comb, wg, wu, wd, x, mods3, *, seg, tm):
    nt, d = x.shape
    return pl.pallas_call(
        _moe_kernel,
        grid=(nt // tm, MOE_EXPERTS),
        in_specs=[pl.BlockSpec((tm, d), lambda i, e: (i, 0)),
                  pl.BlockSpec((tm, LANES), lambda i, e: (i, 0)),
                  pl.BlockSpec((None, d, EXPERT_HID), lambda i, e: (e, 0, 0)),
                  pl.BlockSpec((None, d, EXPERT_HID), lambda i, e: (e, 0, 0)),
                  pl.BlockSpec((None, EXPERT_HID, d), lambda i, e: (e, 0, 0)),
                  pl.BlockSpec((tm, d), lambda i, e: (i, 0)),
                  pl.BlockSpec((None, 1, d), lambda i, e: (seg(i), 0, 5))],
        out_specs=pl.BlockSpec((tm, d), lambda i, e: (i, 0)),
        out_shape=jax.ShapeDtypeStruct((nt, d), F32),
        scratch_shapes=[pltpu.VMEM((tm, d), F32)],
        compiler_params=_params(("arbitrary", "arbitrary")),
        name="moe_experts",
    )(xn, comb, wg, wu, wd, x, mods3)


def _final_norm_kernel(x_ref, w_ref, o_ref):
    x = x_ref[...]
    o_ref[...] = x * lax.rsqrt(jnp.mean(x * x, axis=-1, keepdims=True) + EPS) * w_ref[...]


def _final_norm(x, w, n_rows, tm):
    d = x.shape[1]
    return pl.pallas_call(
        _final_norm_kernel,
        grid=(n_rows // tm,),
        in_specs=[pl.BlockSpec((tm, d), lambda i: (i, 0)), pl.BlockSpec((1, d), lambda i: (0, 0))],
        out_specs=pl.BlockSpec((tm, d), lambda i: (i, 0)),
        out_shape=jax.ShapeDtypeStruct((n_rows, d), F32),
        compiler_params=_params(("arbitrary",)),
        name="final_norm",
    )(x, w.reshape(1, d))


def _slab_to_tm(y_s):
    ns, c, _ = y_s.shape
    return jnp.transpose(y_s, (0, 2, 1)).reshape(ns * LANES, c)


def kernel(x, c, ctx, c_ctx, ada_w, ada_b, norm1_w, norm2_w, w_in, b_in, hy_conv_w, hy_conv_b, hy_f_w1, hy_f_b1, hy_f_w2, hy_f_b2, hy_f_w3, hy_f_freq, hy_decay, hy_skip, ml_conv_w, ml_conv_b, ml_norm_w, w_branch, w_out, moe_rg_w, moe_rg_b, moe_re_w, moe_re_b, moe_w_gate, moe_w_up, moe_w_down, norm_f_w):
    nb, seq, d = x.shape
    lc = ctx.shape[1]
    depth = ada_w.shape[0]
    assert d == D_MODEL and seq % (GRID_W * 2) == 0 and lc % LANES == 0 and nb + 1 <= 8
    rows = seq // GRID_W
    a_lat = seq // LANES
    a_ctx = lc // LANES
    n_lat = nb * seq
    nt = n_lat + nb * lc
    tm = 256
    assert seq % tm == 0 and (nb * lc) % tm == 0
    seg = _seg_fn(n_lat // tm, seq // tm, nb)
    tm_moe = 512 if (seq % 512 == 0 and (nb * lc) % 512 == 0) else tm
    seg_moe = _seg_fn(n_lat // tm_moe, seq // tm_moe, nb)

    xs = jnp.concatenate([x.reshape(n_lat, d), ctx.reshape(nb * lc, d)], axis=0)
    cvec = jnp.zeros((8, d), F32).at[:nb].set(c).at[nb].set(c_ctx)
    mods = _mods(cvec, ada_w, ada_b)

    o_fn, o_ml, o_mlg, o_gate = 3 * D_BR, 4 * D_BR, 8 * D_BR, 8 * D_BR + 4 * ML_HEADS
    pad_g = LANES - 4 * ML_HEADS

    for l in range(depth):
        lp = {"hy_f_w1": hy_f_w1[l], "hy_f_b1": hy_f_b1[l], "hy_f_w2": hy_f_w2[l], "hy_f_b2": hy_f_b2[l],
              "hy_f_w3": hy_f_w3[l], "hy_f_freq": hy_f_freq[l], "hy_decay": hy_decay[l]}
        mods3 = mods[l].reshape(8, 1, 6 * d)
        wl, bl = w_in[l], b_in[l]
        w_cm = jnp.concatenate([wl[:, :o_fn], wl[:, o_ml:o_ml + 2 * D_BR], wl[:, o_fn:o_ml]], axis=1)
        b_cm = jnp.concatenate([bl[:o_fn], bl[o_ml:o_ml + 2 * D_BR], bl[o_fn:o_ml]])
        w_tm = jnp.concatenate([wl[:, o_ml + 2 * D_BR:o_mlg], wl[:, o_gate:], wl[:, o_mlg:o_gate],
                                jnp.zeros((d, pad_g), F32)], axis=1)
        b_tm = jnp.concatenate([bl[o_ml + 2 * D_BR:o_mlg], bl[o_gate:], bl[o_mlg:o_gate], jnp.zeros((pad_g,), F32)])
        c_hy, c_qk, c_fn = 0, 3 * D_BR, 5 * D_BR
        g_col = (2 * D_BR + 3 * d) // LANES

        xn = _norm_mod(xs, norm1_w[l], mods3, 0, 1, seg, tm)
        z_tm = _mm_tm(xn, w_tm.astype(BF16), b_tm)
        z_s = _mm_slab(xn, w_cm.T.astype(BF16), b_cm)

        conv_w = jnp.concatenate([hy_conv_w[l].reshape(9, 3 * D_BR), ml_conv_w[l].reshape(9, 2 * D_BR)], axis=1)
        conv_b = jnp.concatenate([hy_conv_b[l], ml_conv_b[l]])
        u_lat = _conv(z_s, conv_w, conv_b, rows=rows, width=GRID_W, n_batch=nb, slab0=0,
                      chan_lo=0, chan_n=5 * D_BR)
        u_ctx = _conv(z_s, conv_w, conv_b, rows=1, width=lc, n_batch=nb, slab0=nb * a_lat,
                      chan_lo=0, chan_n=5 * D_BR)

        kf_lat = _filter_spectrum(_hyena_taps(seq, 2 * a_lat, lp), 2 * a_lat).reshape(HY_ORDER, D_BR, 2 * a_lat, 2 * LANES)
        yh_lat = _hyena(u_lat, kf_lat, hy_skip[l], a_in=a_lat, na=2 * a_lat, n_batch=nb)
        yh_ctx = _hyena_short(u_ctx, _hyena_taps(lc, 2 * a_ctx, lp), hy_skip[l], a_n=a_ctx, n_batch=nb)
        yh_tm = jnp.concatenate([_slab_to_tm(yh_lat), yh_ctx.reshape(nb * lc, D_BR)], axis=0)

        pq = _fn_mix(z_s, a_n=a_lat, n_batch=nb, chan_lo=c_fn)
        yk = _fn_seq(pq, a_n=a_lat, n_batch=nb)
        yf_lat = jnp.transpose(yk, (0, 3, 2, 1)).reshape(n_lat, D_BR)
        yf_ctx = _fn_small(z_s, a_n=a_ctx, n_batch=nb, slab0=nb * a_lat, chan_lo=c_fn).reshape(nb * lc, D_BR)
        yf_tm = jnp.concatenate([yf_lat, yf_ctx], axis=0)

        qk_tm = _slab_to_tm(jnp.concatenate([u_lat[:, c_qk:], u_ctx[:, c_qk:]], axis=0))
        h_f, h_b = _mlstm(qk_tm, z_tm, n_batch=nb, lat_chunks=a_lat, ctx_chunks=a_ctx, v_col=0, g_col=g_col)

        xs = _merge(yh_tm, yf_tm, h_f, h_b, z_tm, xs, mods3, w_branch[l].astype(BF16), w_out[l].astype(BF16),
                    ml_norm_w[l], seg=seg, tm=tm, o_col=1, gate_col0=1)

        rw = jnp.concatenate([moe_rg_w[l], moe_re_w[l], jnp.zeros((d, LANES - MOE_GROUPS - MOE_EXPERTS), F32)], axis=1)
        rb = jnp.concatenate([moe_rg_b[l], moe_re_b[l], jnp.zeros((LANES - MOE_GROUPS - MOE_EXPERTS,), F32)]).reshape(1, LANES)
        xn2, comb = _router(xs, norm2_w[l], mods3, rw, rb, seg=seg, tm=tm)
        xs = _moe(xn2, comb, moe_w_gate[l].astype(BF16), moe_w_up[l].astype(BF16), moe_w_down[l].astype(BF16),
                  xs, mods3, seg=seg_moe, tm=tm_moe)

    return _final_norm(xs, norm_f_w, n_lat, tm).reshape(nb, seq, d)
```

```python
import functools
import math

import numpy as np
import jax
import jax.numpy as jnp
from jax import lax
from jax.experimental import pallas as pl
from jax.experimental.pallas import tpu as pltpu

F32 = jnp.float32
BF16 = jnp.bfloat16

D_MODEL = 1024
D_BR = 512
GRID_W = 64
LANES = 128
CB = 8
HY_ORDER = 2
HY_BANDS = 16
FN_GROUPS = 4
ML_HEADS = 4
MOE_GROUPS = 4
MOE_PER_GROUP = 4
MOE_EXPERTS = 16
EXPERT_HID = 256
EPS = 1e-6
VMEM_LIMIT = 56 * 1024 * 1024


def _params(sem):
    return pltpu.CompilerParams(dimension_semantics=sem, vmem_limit_bytes=VMEM_LIMIT)


def _bdot(a, b):
    return jnp.dot(a.astype(BF16), b.astype(BF16), preferred_element_type=F32)


def _split3(x):
    hi = x.astype(BF16)
    r1 = x - hi.astype(F32)
    mid = r1.astype(BF16)
    lo = (r1 - mid.astype(F32)).astype(BF16)
    return hi, mid, lo


def _dot_f32ish(x, w):
    xh, xm, xl = _split3(x)
    wh, wm, wl = _split3(w)
    d = lambda a, b: jnp.dot(a, b, preferred_element_type=F32)
    return (d(xh, wh) + (d(xh, wm) + d(xm, wh))) + (d(xm, wm) + d(xh, wl) + d(xl, wh))


def _swap_halves(x):
    return jnp.concatenate([x[..., LANES:], x[..., :LANES]], axis=-1)


def _mods_kernel(c_ref, w_ref, b_ref, o_ref):
    c = c_ref[...]
    s = c * jax.nn.sigmoid(c)
    o_ref[...] = _dot_f32ish(s, w_ref[...]) + b_ref[...]


def _mods(cvec, ada_w, ada_b):
    depth, d, n6 = ada_w.shape
    tn = 1536
    return pl.pallas_call(
        _mods_kernel,
        grid=(depth, n6 // tn),
        in_specs=[pl.BlockSpec((8, d), lambda l, j: (0, 0)),
                  pl.BlockSpec((None, d, tn), lambda l, j: (l, 0, j)),
                  pl.BlockSpec((None, 1, tn), lambda l, j: (l, 0, j))],
        out_specs=pl.BlockSpec((None, 8, tn), lambda l, j: (l, 0, j)),
        out_shape=jax.ShapeDtypeStruct((depth, 8, n6), F32),
        compiler_params=_params(("arbitrary", "arbitrary")),
        name="adaln_mods",
    )(cvec, ada_w, ada_b.reshape(depth, 1, n6))


def _norm_mod_kernel(x_ref, w_ref, sh_ref, sc_ref, o_ref):
    x = x_ref[...]
    y = x * lax.rsqrt(jnp.mean(x * x, axis=-1, keepdims=True) + EPS) * w_ref[...]
    o_ref[...] = (y * (1.0 + sc_ref[...]) + sh_ref[...]).astype(o_ref.dtype)


def _seg_fn(n_lat_tiles, tiles_per_batch, n_batch):
    def seg(i):
        return jnp.where(i < n_lat_tiles, i // tiles_per_batch, n_batch)
    return seg


def _norm_mod(x, w, mods3, col_shift, col_scale, seg, tm):
    nt, d = x.shape
    return pl.pallas_call(
        _norm_mod_kernel,
        grid=(nt // tm,),
        in_specs=[pl.BlockSpec((tm, d), lambda i: (i, 0)),
                  pl.BlockSpec((1, d), lambda i: (0, 0)),
                  pl.BlockSpec((None, 1, d), lambda i: (seg(i), 0, col_shift)),
                  pl.BlockSpec((None, 1, d), lambda i: (seg(i), 0, col_scale))],
        out_specs=pl.BlockSpec((tm, d), lambda i: (i, 0)),
        out_shape=jax.ShapeDtypeStruct((nt, d), BF16),
        compiler_params=_params(("arbitrary",)),
        name="norm_mod",
    )(x, w.reshape(1, d), mods3, mods3)


def _mm_tm_kernel(x_ref, w_ref, b_ref, o_ref):
    o_ref[...] = jnp.dot(x_ref[...], w_ref[...], preferred_element_type=F32) + b_ref[...]


def _pick(n, cands):
    for c in cands:
        if n % c == 0:
            return c
    raise ValueError(f"no tile for {n} in {cands}")


def _mm_tm(xn, w, b):
    nt, k = xn.shape
    n = w.shape[1]
    tm = _pick(nt, (1056, 1024, 768, 512, 256))
    tn = _pick(n, (1408, 1024, 512, 384, 256, 128))
    return pl.pallas_call(
        _mm_tm_kernel,
        grid=(n // tn, nt // tm),
        in_specs=[pl.BlockSpec((tm, k), lambda j, i: (i, 0)),
                  pl.BlockSpec((k, tn), lambda j, i: (0, j)),
                  pl.BlockSpec((1, tn), lambda j, i: (0, j))],
        out_specs=pl.BlockSpec((tm, tn), lambda j, i: (i, j)),
        out_shape=jax.ShapeDtypeStruct((nt, n), F32),
        compiler_params=_params(("arbitrary", "arbitrary")),
        name="inproj_token_major",
    )(xn, w, b.reshape(1, n))


def _mm_slab_kernel(w_ref, x_ref, b_ref, o_ref, *, slabs):
    w = w_ref[...]
    b = b_ref[...]
    step = 2 if slabs % 2 == 0 else 1
    for s in range(0, slabs, step):
        xs = x_ref[s * LANES:(s + step) * LANES, :]
        y = lax.dot_general(w, xs, (((1,), (1,)), ((), ())), preferred_element_type=F32) + b
        for i in range(step):
            o_ref[s + i] = y[:, i * LANES:(i + 1) * LANES]


def _mm_slab(xn, wt, b):
    nt, k = xn.shape
    c = wt.shape[0]
    ns = nt // LANES
    ts = _pick(ns, (12, 11, 8, 6, 4, 3, 2, 1))
    tc = _pick(c, (512, 256, 128))
    return pl.pallas_call(
        functools.partial(_mm_slab_kernel, slabs=ts),
        grid=(c // tc, ns // ts),
        in_specs=[pl.BlockSpec((tc, k), lambda j, i: (j, 0)),
                  pl.BlockSpec((ts * LANES, k), lambda j, i: (i, 0)),
                  pl.BlockSpec((tc, 1), lambda j, i: (j, 0))],
        out_specs=pl.BlockSpec((ts, tc, LANES), lambda j, i: (i, j, 0)),
        out_shape=jax.ShapeDtypeStruct((ns, c, LANES), F32),
        compiler_params=_params(("arbitrary", "arbitrary")),
        name="inproj_slab",
    )(wt, xn, b.reshape(c, 1))


def _conv_taps(rows, width):
    taps = []
    for dr in (-1, 0, 1):
        if rows == 1 and dr != 0:
            continue
        for dw in (-1, 0, 1):
            taps.append((dr, dw))
    return taps


def _silu(x):
    return x * jax.nn.sigmoid(x)


def _conv_kernel(x_ref, w_ref, b_ref, o_ref, *, taps, width, n_slabs, silu):
    ct = x_ref.shape[1]
    lane = lax.broadcasted_iota(jnp.int32, (ct, LANES), 1)
    bias = jnp.zeros((ct, LANES), F32) + b_ref[...]
    planes = []
    for t, (dr, dw) in enumerate(taps):
        w = w_ref[t]
        if width < LANES and dw != 0:
            col = lane % width + dw
            w = jnp.where((col >= 0) & (col < width), w, 0.0)
        planes.append((dr * width + dw, w))

    def body(a, carry):
        x0 = x_ref[a]
        xm = jnp.where(a > 0, x_ref[jnp.maximum(a - 1, 0)], 0.0)
        xp = jnp.where(a < n_slabs - 1, x_ref[jnp.minimum(a + 1, n_slabs - 1)], 0.0)
        acc = bias
        for delta, w in planes:
            if delta == 0:
                src = x0
            elif delta > 0:
                src = pltpu.roll(jnp.where(lane >= delta, x0, xp), LANES - delta, 1)
            else:
                src = pltpu.roll(jnp.where(lane < LANES + delta, x0, xm), -delta, 1)
            acc = acc + src * w
        o_ref[a] = _silu(acc) if silu else acc
        return carry

    lax.fori_loop(0, n_slabs, body, 0, unroll=2 if n_slabs % 2 == 0 else 1)


def _conv_grid_kernel(x_ref, w_ref, b_ref, o_ref, ym_ref, yp_ref, *, width, n_slabs, silu):
    ct = x_ref.shape[1]
    lane = lax.broadcasted_iota(jnp.int32, (ct, LANES), 1)
    col = lane % width
    wl = [jnp.where(col >= 1, w_ref[3 * i], 0.0) for i in range(3)]
    wc = [w_ref[3 * i + 1] for i in range(3)]
    wr = [jnp.where(col < width - 1, w_ref[3 * i + 2], 0.0) for i in range(3)]
    bias = jnp.zeros((ct, LANES), F32) + b_ref[...]

    def row_sums(a, carry):
        x0 = x_ref[a]
        xl = pltpu.roll(x0, 1, 1)
        xr = pltpu.roll(x0, LANES - 1, 1)
        ym_ref[a] = wl[0] * xl + wc[0] * x0 + wr[0] * xr
        o_ref[a] = bias + wl[1] * xl + wc[1] * x0 + wr[1] * xr
        yp_ref[a] = wl[2] * xl + wc[2] * x0 + wr[2] * xr
        return carry

    def combine(a, carry):
        up = jnp.where(a > 0, ym_ref[jnp.maximum(a - 1, 0)], 0.0)
        dn = jnp.where(a < n_slabs - 1, yp_ref[jnp.minimum(a + 1, n_slabs - 1)], 0.0)
        from_up = jnp.where(lane < LANES - width, ym_ref[a], up)
        from_dn = jnp.where(lane >= width, yp_ref[a], dn)
        if 2 * width == LANES:
            y = o_ref[a] + pltpu.roll(from_up + from_dn, width, 1)
        else:
            y = o_ref[a] + pltpu.roll(from_up, width, 1) + pltpu.roll(from_dn, LANES - width, 1)
        o_ref[a] = _silu(y) if silu else y
        return carry

    unroll = 4 if n_slabs % 4 == 0 else 1
    lax.fori_loop(0, n_slabs, row_sums, 0, unroll=unroll)
    lax.fori_loop(0, n_slabs, combine, 0, unroll=unroll)


def _conv(z_s, w9, bias, *, rows, width, n_batch, slab0, chan_lo, chan_n, silu):
    seq = rows * width
    a_n = seq // LANES
    taps = tuple(_conv_taps(rows, width))
    assert all(abs(dr * width + dw) < LANES for dr, dw in taps)
    assert LANES % width == 0 or (rows == 1 and width % LANES == 0)
    tap_ids = [(dr + 1) * 3 + (dw + 1) for dr, dw in taps]
    w_t = jnp.broadcast_to(w9[jnp.array(tap_ids)][:, :, None], (len(taps), chan_n, LANES))
    ct = 64
    assert chan_lo % ct == 0 and chan_n % ct == 0 and slab0 % a_n == 0
    nt_ = len(taps)
    if rows > 1:
        assert LANES % width == 0 and nt_ == 9
        body = functools.partial(_conv_grid_kernel, width=width, n_slabs=a_n, silu=silu)
        scratch = [pltpu.VMEM((a_n, ct, LANES), F32), pltpu.VMEM((a_n, ct, LANES), F32)]
    else:
        body = functools.partial(_conv_kernel, taps=taps, width=width, n_slabs=a_n, silu=silu)
        scratch = []
    return pl.pallas_call(
        body,
        scratch_shapes=scratch,
        grid=(n_batch, chan_n // ct),
        in_specs=[pl.BlockSpec((a_n, ct, LANES), lambda b, j: (slab0 // a_n + b, chan_lo // ct + j, 0)),
                  pl.BlockSpec((nt_, ct, LANES), lambda b, j: (0, j, 0)),
                  pl.BlockSpec((ct, 1), lambda b, j: (j, 0))],
        out_specs=pl.BlockSpec((a_n, ct, LANES), lambda b, j: (b, j, 0)),
        out_shape=jax.ShapeDtypeStruct((n_batch * a_n, chan_n, LANES), F32),
        compiler_params=_params(("arbitrary", "arbitrary")),
        name=f"dwconv_{rows}x{width}",
    )(z_s, w_t, bias.reshape(chan_n, 1))


def _dft_consts(a_in, na):
    n = na * LANES
    k = np.arange(na)[:, None]
    a = np.arange(a_in)[None, :]
    ang = 2 * np.pi * (k * a % na) / na
    fa = np.concatenate([np.cos(ang), -np.sin(ang)], axis=0)
    r = np.arange(LANES)
    ang_t = 2 * np.pi * (np.arange(na)[:, None] * r[None, :] % n) / n
    tr, ti = np.cos(ang_t), -np.sin(ang_t)
    ta = np.concatenate([tr, tr], axis=1)
    tb = np.concatenate([-ti, ti], axis=1)
    ang2 = 2 * np.pi * (r[:, None] * r[None, :] % LANES) / LANES
    c2, s2 = np.cos(ang2), np.sin(ang2)
    g2 = np.block([[c2, -s2], [s2, c2]])
    g2i = np.block([[c2, s2], [-s2, c2]])
    ang_i = 2 * np.pi * (np.arange(a_in)[:, None] * np.arange(na)[None, :] % na) / na
    ci, si = np.cos(ang_i) / n, -np.sin(ang_i) / n
    f = lambda v, dt: jnp.asarray(v, dtype=dt)
    return dict(fa=f(fa, F32), ta=f(ta, F32), tb=f(tb, F32), g2=f(g2, F32), g2i=f(g2i, F32),
                ci=f(ci, F32), si=f(si, F32))


def _fwd_slab_stage(m, fa, ta, tb, na):
    pp = jnp.dot(fa, m.astype(BF16), preferred_element_type=F32)
    p = jnp.concatenate([pp[:na], pp[na:]], axis=1)
    return p * ta + _swap_halves(p) * tb


def _cmul(x, kf):
    kr, ki = kf[..., :LANES], kf[..., LANES:]
    ka = jnp.concatenate([kr, kr], axis=-1)
    kb = jnp.concatenate([-ki, ki], axis=-1)
    return x * ka + _swap_halves(x) * kb


def _chan_load(ref, c):
    n, cb, _ = ref.shape
    return ref.reshape(n * cb, LANES)[pl.ds(c, n, stride=cb), :]


def _chan_store(ref, c, val):
    ref[:, c, :] = val


def _dot_f32ish_k(w, h):
    wh, wm, _ = _split3(w)
    hh, hm, _ = _split3(h)
    lhs = jnp.concatenate([wh, wh, wm], axis=1)
    rhs = jnp.concatenate([hh, hm, hh], axis=0)
    return jnp.dot(lhs, rhs, preferred_element_type=F32)


def _taps_kernel(bands_ref, w1t_ref, w1c_ref, w1s_ref, b1_ref, w2_ref, b2_ref, fq_ref, w3_ref, dec_ref, o_ref,
                 *, seq, a_seq, na, spb):
    step = pl.program_id(0)
    lane = lax.broadcasted_iota(jnp.int32, (1, LANES), 1)
    fq = fq_ref[...]
    n_total = na * LANES
    for i in range(spb):
        a = step * spb + i
        is_f = a < a_seq
        is_b = a >= na - a_seq
        live = jnp.logical_or(is_f, is_b)

        @pl.when(live)
        def _():
            n = a * LANES + lane
            pos = jnp.where(is_f, n, n_total - n)
            valid = (n > jnp.where(is_f, -1, n_total - seq)) & (n < jnp.where(is_f, seq, n_total))
            t = pos.astype(F32) / seq
            ang = ((2 * math.pi) * t) * bands_ref[...]
            pre = (w1t_ref[...] * t + _dot_f32ish(w1c_ref[...], jnp.cos(ang))
                   + _dot_f32ish(w1s_ref[...], jnp.sin(ang)))
            h = jnp.sin(fq * (pre + b1_ref[...]))
            h = jnp.sin(fq * (_dot_f32ish(w2_ref[...], h) + b2_ref[...]))
            d = jnp.where(is_f, 0, 1)
            k = _dot_f32ish_k(w3_ref[d], h) * jnp.exp(-t * jnp.abs(dec_ref[d]))
            o_ref[i] = jnp.where(valid, k, 0.0)

        @pl.when(jnp.logical_not(live))
        def _():
            o_ref[i] = jnp.zeros(o_ref.shape[1:], F32)


def _hyena_taps(seq, na, lp):
    nc = HY_ORDER * D_BR
    hid = lp["hy_f_w2"].shape[0]
    w1 = lp["hy_f_w1"]
    col = lambda v: v.reshape(-1, 1)
    bands = col(jnp.linspace(1e-4, HY_BANDS - 1, HY_BANDS, dtype=F32))
    w3 = jnp.transpose(lp["hy_f_w3"].T.reshape(HY_ORDER, 2, D_BR, hid), (1, 0, 2, 3)).reshape(2, nc, hid)
    dec = jnp.broadcast_to(jnp.transpose(lp["hy_decay"], (1, 0, 2)).reshape(2, nc, 1), (2, nc, LANES))
    spb = min(8, na)
    assert na % spb == 0
    args = (bands, col(w1[0]), w1[1:1 + HY_BANDS].T, w1[1 + HY_BANDS:].T, col(lp["hy_f_b1"]), lp["hy_f_w2"].T,
            col(lp["hy_f_b2"]), col(lp["hy_f_freq"]), w3, dec)
    full = lambda v: pl.BlockSpec(v.shape, lambda s: (0,) * v.ndim)
    return pl.pallas_call(
        functools.partial(_taps_kernel, seq=seq, a_seq=seq // LANES, na=na, spb=spb),
        grid=(na // spb,),
        in_specs=[full(v) for v in args],
        out_specs=pl.BlockSpec((spb, nc, LANES), lambda s: (s, 0, 0)),
        out_shape=jax.ShapeDtypeStruct((na, nc, LANES), F32),
        compiler_params=_params(("arbitrary",)),
        name=f"hyena_filter_taps_{na}",
    )(*args)


def _hyena_kernel(v_ref, x1_ref, x2_ref, k0_ref, k1_ref, skip_ref, fa_ref, faf_ref, ta_ref, tb_ref, g2_ref, g2i_ref,
                  ci_ref, si_ref, o_ref, p_buf, z_buf, kf_buf, *, a_in, na):
    fa, ta, tb = fa_ref[...].astype(BF16), ta_ref[...], tb_ref[...]
    ci, si = ci_ref[...].astype(BF16), si_ref[...].astype(BF16)

    @pl.when(pl.program_id(1) == 0)
    def _():
        faf = faf_ref[...].astype(BF16)
        for order, k_ref in enumerate((k0_ref, k1_ref)):
            scales = []
            for c in range(CB):
                m = _chan_load(k_ref, c)
                ss = jnp.sum(jnp.sum(m * m, axis=1, keepdims=True), axis=0, keepdims=True)
                scales.append(lax.rsqrt(ss + EPS))
                p_buf[c] = _fwd_slab_stage(m, faf, ta, tb, na)
            x = _bdot(p_buf[...].reshape(CB * na, 2 * LANES), g2_ref[...]).reshape(CB, na, 2 * LANES)
            for c in range(CB):
                kf_buf[order, c] = x[c] * scales[c]

    def spectral(order):
        x = _bdot(p_buf[...].reshape(CB * na, 2 * LANES), g2_ref[...])
        y = _cmul(x, kf_buf[order].reshape(CB * na, 2 * LANES))
        bm = _bdot(y, g2i_ref[...]).reshape(CB, na, 2 * LANES)
        p_buf[...] = bm * ta - _swap_halves(bm) * tb

    def conv_out(c):
        bb = p_buf[c]
        return (jnp.dot(ci, bb[:, :LANES].astype(BF16), preferred_element_type=F32)
                + jnp.dot(si, bb[:, LANES:].astype(BF16), preferred_element_type=F32))

    for c in range(CB):
        p_buf[c] = _fwd_slab_stage(_chan_load(v_ref, c), fa, ta, tb, na)
    spectral(0)
    for c in range(CB):
        z_buf[c] = _chan_load(x1_ref, c) * (conv_out(c) + _chan_load(v_ref, c) * skip_ref[0, c])
    for c in range(CB):
        p_buf[c] = _fwd_slab_stage(z_buf[c], fa, ta, tb, na)
    spectral(1)
    for c in range(CB):
        _chan_store(o_ref, c, _chan_load(x2_ref, c) * (conv_out(c) + z_buf[c] * skip_ref[1, c]))


def _hyena(u_s, taps_s, skip, *, a_in, na, n_batch):
    cs = _dft_consts(a_in, na)
    faf = _dft_consts(na, na)["fa"]
    nblk = D_BR // CB
    const = lambda shp: pl.BlockSpec(shp, lambda j, b: (0,) * len(shp))
    skip_b = jnp.broadcast_to(skip[:, :, None, None], (HY_ORDER, D_BR, 1, LANES))
    return pl.pallas_call(
        functools.partial(_hyena_kernel, a_in=a_in, na=na),
        grid=(nblk, n_batch),
        in_specs=[pl.BlockSpec((a_in, CB, LANES), lambda j, b: (b, j, 0)),
                  pl.BlockSpec((a_in, CB, LANES), lambda j, b: (b, nblk + j, 0)),
                  pl.BlockSpec((a_in, CB, LANES), lambda j, b: (b, 2 * nblk + j, 0)),
                  pl.BlockSpec((na, CB, LANES), lambda j, b: (0, j, 0)),
                  pl.BlockSpec((na, CB, LANES), lambda j, b: (0, nblk + j, 0)),
                  pl.BlockSpec((HY_ORDER, CB, 1, LANES), lambda j, b: (0, j, 0, 0)),
                  const((2 * na, a_in)), const((2 * na, na)), const((na, 2 * LANES)), const((na, 2 * LANES)),
                  const((2 * LANES, 2 * LANES)), const((2 * LANES, 2 * LANES)),
                  const((a_in, na)), const((a_in, na))],
        out_specs=pl.BlockSpec((a_in, CB, LANES), lambda j, b: (b, j, 0)),
        out_shape=jax.ShapeDtypeStruct((n_batch * a_in, D_BR, LANES), F32),
        scratch_shapes=[pltpu.VMEM((CB, na, 2 * LANES), F32), pltpu.VMEM((CB, a_in, LANES), F32),
                        pltpu.VMEM((HY_ORDER, CB, na, 2 * LANES), F32)],
        compiler_params=_params(("arbitrary", "arbitrary")),
        name=f"hyena_longconv_{a_in}",
    )(u_s, u_s, u_s, taps_s, taps_s, skip_b, cs["fa"], faf, cs["ta"], cs["tb"], cs["g2"], cs["g2i"],
      cs["ci"], cs["si"])


def _slabs_to_rows(ref, n):
    return jnp.concatenate([ref[a].T for a in range(n)], axis=0)


def _hyena_short_kernel(v_ref, x1_ref, x2_ref, k0_ref, k1_ref, skip_ref, f_ref, g_ref, o_ref, *, a_n, a_k):
    seq, nf = a_n * LANES, a_k * LANES
    ff = f_ref[...].astype(BF16)
    gi = g_ref[...].astype(BF16)
    v, x1, x2 = _slabs_to_rows(v_ref, a_n), _slabs_to_rows(x1_ref, a_n), _slabs_to_rows(x2_ref, a_n)

    def longconv(u, k_ref):
        k = _slabs_to_rows(k_ref, a_k)
        s = lax.rsqrt(jnp.sum(k * k, axis=0, keepdims=True) + EPS)
        kf = jnp.dot(ff, k.astype(BF16), preferred_element_type=F32) * s
        x = jnp.dot(ff[:, :seq], u.astype(BF16), preferred_element_type=F32)
        xr, xi, kr, ki = x[:nf], x[nf:], kf[:nf], kf[nf:]
        y = jnp.concatenate([xr * kr - xi * ki, xr * ki + xi * kr], axis=0)
        return jnp.dot(gi, y.astype(BF16), preferred_element_type=F32)

    z = x1 * (longconv(v, k0_ref) + v * skip_ref[0])
    o_ref[...] = x2 * (longconv(z, k1_ref) + z * skip_ref[1])


def _hyena_short(u_s, taps_s, skip, *, a_n, n_batch):
    a_k = 2 * a_n
    seq, nf = a_n * LANES, a_k * LANES
    k = np.arange(nf)
    ang = 2 * np.pi * (k[:, None] * k[None, :] % nf) / nf
    f = np.concatenate([np.cos(ang), -np.sin(ang)], axis=0)
    g = np.concatenate([np.cos(ang[:seq]), -np.sin(ang[:seq])], axis=1) / nf
    nblk = D_BR // LANES
    const = lambda shp: pl.BlockSpec(shp, lambda b, j: (0,) * len(shp))
    return pl.pallas_call(
        functools.partial(_hyena_short_kernel, a_n=a_n, a_k=a_k),
        grid=(n_batch, nblk),
        in_specs=[pl.BlockSpec((a_n, LANES, LANES), lambda b, j: (b, j, 0)),
                  pl.BlockSpec((a_n, LANES, LANES), lambda b, j: (b, nblk + j, 0)),
                  pl.BlockSpec((a_n, LANES, LANES), lambda b, j: (b, 2 * nblk + j, 0)),
                  pl.BlockSpec((a_k, LANES, LANES), lambda b, j: (0, j, 0)),
                  pl.BlockSpec((a_k, LANES, LANES), lambda b, j: (0, nblk + j, 0)),
                  pl.BlockSpec((HY_ORDER, 1, LANES), lambda b, j: (0, 0, j)),
                  const((2 * nf, nf)), const((seq, 2 * nf))],
        out_specs=pl.BlockSpec((None, seq, LANES), lambda b, j: (b, 0, j)),
        out_shape=jax.ShapeDtypeStruct((n_batch, seq, D_BR), F32),
        compiler_params=_params(("arbitrary", "arbitrary")),
        name="hyena_short",
    )(u_s, u_s, u_s, taps_s, taps_s, skip.reshape(HY_ORDER, 1, D_BR), jnp.asarray(f, F32), jnp.asarray(g, F32))


def _chan_dft_mats():
    r = np.arange(LANES)
    ang = 2 * np.pi * (r[:, None] * r[None, :] % LANES) / LANES
    return np.cos(ang), np.sin(ang)


def _fn_mix_kernel(u_ref, cs_ref, o_ref, *, n_slabs):
    w = cs_ref[...].astype(BF16)

    def body(a, carry):
        o_ref[a] = jnp.dot(w, u_ref[a].astype(BF16), preferred_element_type=F32)
        return carry

    lax.fori_loop(0, n_slabs, body, 0)


def _fn_mix(z_s, *, a_n, n_batch, chan_lo):
    c, s = _chan_dft_mats()
    w = jnp.asarray(np.concatenate([c, s], axis=0), dtype=F32)
    g0 = chan_lo // LANES
    return pl.pallas_call(
        functools.partial(_fn_mix_kernel, n_slabs=a_n),
        grid=(n_batch, FN_GROUPS),
        in_specs=[pl.BlockSpec((a_n, LANES, LANES), lambda b, g: (b, g0 + g, 0)),
                  pl.BlockSpec((2 * LANES, LANES), lambda b, g: (0, 0))],
        out_specs=pl.BlockSpec((a_n, 2 * LANES, LANES), lambda b, g: (b, g, 0)),
        out_shape=jax.ShapeDtypeStruct((n_batch * a_n, 2 * D_BR, LANES), F32),
        compiler_params=_params(("arbitrary", "arbitrary")),
        name="fnet_channel_dft",
    )(z_s, w)


def _fn_seq_kernel(p_ref, q_ref, fa_ref, tr_ref, ti_ref, g_ref, o_ref, a_buf, *, a_n, scale):
    fa, tr, ti = fa_ref[...].astype(BF16), tr_ref[...], ti_ref[...]
    for c in range(CB):
        r1 = jnp.dot(fa, _chan_load(p_ref, c).astype(BF16), preferred_element_type=F32)
        r2 = jnp.dot(fa, _chan_load(q_ref, c).astype(BF16), preferred_element_type=F32)
        ar = r1[:a_n] - r2[a_n:]
        ai = -(r2[:a_n] + r1[a_n:])
        a_buf[c] = jnp.concatenate([ar * tr - ai * ti, ar * ti + ai * tr], axis=1)
    y = _bdot(a_buf[...].reshape(CB * a_n, 2 * LANES), g_ref[...]) * scale
    o_ref[...] = y.reshape(CB, a_n, LANES)


def _fn_seq(pq, *, a_n, n_batch):
    seq = a_n * LANES
    k = np.arange(a_n)
    ang = 2 * np.pi * (k[:, None] * k[None, :] % a_n) / a_n
    fa = np.concatenate([np.cos(ang), np.sin(ang)], axis=0)
    r = np.arange(LANES)
    ang_t = 2 * np.pi * (k[:, None] * r[None, :] % seq) / seq
    c2, s2 = _chan_dft_mats()
    g = np.concatenate([c2, s2], axis=0)
    nblk = LANES // CB
    const = lambda shp: pl.BlockSpec(shp, lambda b, j: (0,) * len(shp))

    def chan_blk(j, off):
        return (j // nblk) * (2 * nblk) + off * nblk + j % nblk

    return pl.pallas_call(
        functools.partial(_fn_seq_kernel, a_n=a_n, scale=1.0 / math.sqrt(seq * LANES)),
        grid=(n_batch, D_BR // CB),
        in_specs=[pl.BlockSpec((a_n, CB, LANES), lambda b, j: (b, chan_blk(j, 0), 0)),
                  pl.BlockSpec((a_n, CB, LANES), lambda b, j: (b, chan_blk(j, 1), 0)),
                  const((2 * a_n, a_n)), const((a_n, LANES)), const((a_n, LANES)), const((2 * LANES, LANES))],
        out_specs=pl.BlockSpec((None, CB, a_n, LANES), lambda b, j: (b, j, 0, 0)),
        out_shape=jax.ShapeDtypeStruct((n_batch, D_BR, a_n, LANES), F32),
        scratch_shapes=[pltpu.VMEM((CB, a_n, 2 * LANES), F32)],
        compiler_params=_params(("arbitrary", "arbitrary")),
        name="fnet_sequence_dft",
    )(pq, pq, jnp.asarray(fa, F32), jnp.asarray(np.cos(ang_t), F32), jnp.asarray(-np.sin(ang_t), F32),
      jnp.asarray(g, F32))


def _fn_small_kernel(u_ref, cw_ref, sw_ref, cl_ref, sl_ref, o_ref, *, a_n, scale):
    u = jnp.concatenate([u_ref[a].T for a in range(a_n)], axis=0)
    p = _bdot(u, cw_ref[...])
    q = _bdot(u, sw_ref[...])
    o_ref[...] = (_bdot(cl_ref[...], p) - _bdot(sl_ref[...], q)) * scale


def _fn_small(z_s, *, a_n, n_batch, slab0, chan_lo):
    seq = a_n * LANES
    cw, sw = _chan_dft_mats()
    n = np.arange(seq)
    ang = 2 * np.pi * (n[:, None] * n[None, :] % seq) / seq
    const = lambda shp: pl.BlockSpec(shp, lambda b, g: (0,) * len(shp))
    g0 = chan_lo // LANES
    return pl.pallas_call(
        functools.partial(_fn_small_kernel, a_n=a_n, scale=1.0 / math.sqrt(seq * LANES)),
        grid=(n_batch, FN_GROUPS),
        in_specs=[pl.BlockSpec((a_n, LANES, LANES), lambda b, g: (slab0 // a_n + b, g0 + g, 0)),
                  const((LANES, LANES)), const((LANES, LANES)), const((seq, seq)), const((seq, seq))],
        out_specs=pl.BlockSpec((None, seq, LANES), lambda b, g: (b, 0, g)),
        out_shape=jax.ShapeDtypeStruct((n_batch, seq, D_BR), F32),
        compiler_params=_params(("arbitrary", "arbitrary")),
        name="fnet_short",
    )(z_s, jnp.asarray(cw, F32), jnp.asarray(sw, F32), jnp.asarray(np.cos(ang), F32), jnp.asarray(np.sin(ang), F32))


def _log_sigmoid(x):
    return jnp.minimum(x, 0.0) - jnp.log(1.0 + jnp.exp(-jnp.abs(x)))


def _exact_tri_dot(tri, x, tri_on_left):
    h, m, l = _split3(x)
    if tri_on_left:
        d = lambda p: jnp.dot(tri, p, preferred_element_type=F32)
    else:
        d = lambda p: jnp.dot(p, tri, preferred_element_type=F32)
    return d(h) + d(m) + d(l)


def _mlstm_dir(q_ref, k_ref, v_ref, g_ref, h_ref, c_st, n_st, m_st, *, d, reverse):
    t = LANES
    hd = LANES
    row = lax.broadcasted_iota(jnp.int32, (t, t), 0)
    col = lax.broadcasted_iota(jnp.int32, (t, t), 1)
    lower = (col <= row)
    tri = jnp.where(lower, 1.0, 0.0).astype(BF16)
    tri_t = jnp.where(col >= row, 1.0, 0.0).astype(BF16)
    mask = (col >= row) if reverse else lower
    g = g_ref[...]
    gt = g.T
    lf_c = _log_sigmoid(g)
    lf_r = lf_c.T
    if reverse:
        b_c = _exact_tri_dot(tri_t, lf_c, True)
        b_r = _exact_tri_dot(tri, lf_r, False)
    else:
        b_c = _exact_tri_dot(tri, lf_c, True)
        b_r = _exact_tri_dot(tri_t, lf_r, False)
    i_off = 2 * ML_HEADS * d
    f_off = i_off + ML_HEADS
    for h in range(ML_HEADS):
        sl = slice(h * hd, (h + 1) * hd)
        q = q_ref[:, sl] * (hd ** -0.5)
        k = k_ref[:, sl]
        v = v_ref[:, sl]
        bc = b_c[:, f_off + h:f_off + h + 1]
        br = b_r[f_off + h:f_off + h + 1, :]
        ic = g[:, i_off + h:i_off + h + 1]
        ir = gt[i_off + h:i_off + h + 1, :]
        m_prev = m_st[h][:, :1]
        ct = c_st[h]
        n_prev = n_st[h]
        dm = jnp.where(mask, bc - br + ir, -jnp.inf)
        inter = bc + m_prev
        m_row = jnp.maximum(inter, jnp.max(dm, axis=-1, keepdims=True))
        w_intra = jnp.exp(dm - m_row)
        w_inter = jnp.exp(inter - m_row)
        qb, kb, vb = q.astype(BF16), k.astype(BF16), v.astype(BF16)
        s = lax.dot_general(qb, kb, (((1,), (1,)), ((), ())), preferred_element_type=F32) * w_intra
        num = jnp.dot(s.astype(BF16), vb, preferred_element_type=F32) + w_inter * jnp.dot(qb, ct.astype(BF16), preferred_element_type=F32)
        den = jnp.sum(s, axis=-1, keepdims=True) + w_inter * jnp.sum(q * n_prev, axis=-1, keepdims=True)
        den = jnp.maximum(jnp.abs(den), jnp.exp(-m_row))
        h_ref[:, sl] = num / den
        if reverse:
            b_tot_c, b_tot_r = bc[:1, :], br[:, :1]
        else:
            b_tot_c, b_tot_r = bc[t - 1:, :], br[:, t - 1:]
        a_c = b_tot_c - bc + ic
        a_r = b_tot_r - br + ir
        m_new = jnp.maximum(b_tot_c + m_prev, jnp.max(a_r, axis=-1, keepdims=True))
        sc = jnp.exp(a_c - m_new)
        decay = jnp.exp(b_tot_c + m_prev - m_new)
        ks = k * sc
        c_st[h] = decay * ct + lax.dot_general(ks.astype(BF16), vb, (((0,), (0,)), ((), ())), preferred_element_type=F32)
        n_st[h] = decay * n_prev + jnp.sum(ks, axis=0, keepdims=True)
        m_st[h] = jnp.broadcast_to(m_new, (1, LANES))


def _mlstm_kernel(qf, kf, vf, gf, qb, kb, vb, gb, hf_ref, hb_ref, c_st, n_st, m_st):
    @pl.when(pl.program_id(1) == 0)
    def _():
        c_st[...] = jnp.zeros(c_st.shape, F32)
        n_st[...] = jnp.zeros(n_st.shape, F32)
        m_st[...] = jnp.zeros(m_st.shape, F32)

    _mlstm_dir(qf, kf, vf, gf, hf_ref, c_st.at[0], n_st.at[0], m_st.at[0], d=0, reverse=False)
    _mlstm_dir(qb, kb, vb, gb, hb_ref, c_st.at[1], n_st.at[1], m_st.at[1], d=1, reverse=True)


def _mlstm(qk_tm, z_tm, *, n_batch, lat_chunks, ctx_chunks, v_col, g_col):
    nt = qk_tm.shape[0]
    n_lat = n_batch * lat_chunks
    nsteps = ctx_chunks + lat_chunks

    def glob(b, p):
        return jnp.where(p < ctx_chunks, n_lat + b * ctx_chunks + p, b * lat_chunks + p - ctx_chunks)

    def fwd(b, j):
        return glob(b, j)

    def bwd(b, j):
        p = jnp.where(j < ctx_chunks, ctx_chunks - 1 - j, ctx_chunks + lat_chunks - 1 - (j - ctx_chunks))
        return glob(b, p)

    def specs(ix):
        return [pl.BlockSpec((LANES, D_BR), lambda b, j: (ix(b, j), 0)),
                pl.BlockSpec((LANES, D_BR), lambda b, j: (ix(b, j), 1)),
                pl.BlockSpec((LANES, D_BR), lambda b, j: (ix(b, j), v_col)),
                pl.BlockSpec((LANES, LANES), lambda b, j: (ix(b, j), g_col))]

    out_sd = jax.ShapeDtypeStruct((nt, D_BR), F32)
    return pl.pallas_call(
        _mlstm_kernel,
        grid=(n_batch, nsteps),
        in_specs=specs(fwd) + specs(bwd),
        out_specs=[pl.BlockSpec((LANES, D_BR), lambda b, j: (fwd(b, j), 0)),
                   pl.BlockSpec((LANES, D_BR), lambda b, j: (bwd(b, j), 0))],
        out_shape=[out_sd, out_sd],
        scratch_shapes=[pltpu.VMEM((2, ML_HEADS, LANES, LANES), F32),
                        pltpu.VMEM((2, ML_HEADS, 1, LANES), F32),
                        pltpu.VMEM((2, ML_HEADS, 1, LANES), F32)],
        compiler_params=_params(("arbitrary", "arbitrary")),
        name="mlstm_bidir",
    )(qk_tm, qk_tm, z_tm, z_tm, qk_tm, qk_tm, z_tm, z_tm)


def _rms_mod(x, w, shift, scale):
    y = x * lax.rsqrt(jnp.mean(x * x, axis=-1, keepdims=True) + EPS) * w
    return y * (1.0 + scale) + shift


def _route(t, rw, rb):
    logits = _dot_f32ish(t, rw) + rb
    col = lax.broadcasted_iota(jnp.int32, logits.shape, 1)
    big = jnp.int32(1 << 20)
    ninf = -jnp.inf
    is_g = col < MOE_GROUPS
    gl = jnp.where(is_g, logits, ninf)
    gmax = jnp.max(gl, axis=-1, keepdims=True)
    g_sel = jnp.min(jnp.where(is_g & (gl == gmax), col, big), axis=-1, keepdims=True)
    p_top = 1.0 / jnp.sum(jnp.where(is_g, jnp.exp(gl - gmax), 0.0), axis=-1, keepdims=True)
    lo = MOE_GROUPS + g_sel * MOE_PER_GROUP
    in_grp = (col >= lo) & (col < lo + MOE_PER_GROUP)
    e1v = jnp.where(in_grp, logits, ninf)
    top1 = jnp.max(e1v, axis=-1, keepdims=True)
    idx1 = jnp.min(jnp.where(in_grp & (e1v == top1), col, big), axis=-1, keepdims=True)
    e2v = jnp.where(col == idx1, ninf, e1v)
    top2 = jnp.max(e2v, axis=-1, keepdims=True)
    idx2 = jnp.min(jnp.where(in_grp & (col != idx1) & (e2v == top2), col, big), axis=-1, keepdims=True)
    ex = jnp.exp(top2 - top1)
    s1 = 1.0 / (1.0 + ex)
    return jnp.where(col == idx1, p_top * s1, 0.0) + jnp.where(col == idx2, p_top * (ex * s1), 0.0)


def _merge_kernel(yh_ref, yf_ref, hf_ref, hb_ref, o_ref, g0_ref, g1_ref, g2_ref, x_ref, gate_ref,
                  wb_ref, wo_ref, nw_ref, n2_ref, sh_ref, sc_ref, rw_ref, rb_ref, out_ref, xn_ref, comb_ref):
    hd = LANES
    h = hf_ref[...] + hb_ref[...]
    parts = []
    for i in range(ML_HEADS):
        hh = h[:, i * hd:(i + 1) * hd]
        parts.append(hh * lax.rsqrt(jnp.mean(hh * hh, axis=-1, keepdims=True) + EPS))
    y_ml = jax.nn.sigmoid(o_ref[...]) * (jnp.concatenate(parts, axis=1) * nw_ref[...])
    acc = jax.nn.sigmoid(g0_ref[...]) * _bdot(yh_ref[...], wb_ref[0])
    acc = acc + jax.nn.sigmoid(g1_ref[...]) * _bdot(yf_ref[...], wb_ref[1])
    acc = acc + jax.nn.sigmoid(g2_ref[...]) * _bdot(y_ml, wb_ref[2])
    x_new = x_ref[...] + gate_ref[...] * _bdot(acc, wo_ref[...])
    out_ref[...] = x_new
    t = _rms_mod(x_new, n2_ref[...], sh_ref[...], sc_ref[...])
    xn_ref[...] = t.astype(BF16)
    comb_ref[...] = _route(t, rw_ref[...], rb_ref[...])


def _merge(yh, yf, hf, hb, z_tm, x, mods3, wb, wo, nw, n2w, rw, rb, *, seg, tm, o_col, gate_col0):
    nt, d = x.shape
    tok = lambda w, cidx: pl.BlockSpec((tm, w), lambda i: (i, cidx))
    mod = lambda k: pl.BlockSpec((None, 1, d), lambda i: (seg(i), 0, k))
    return pl.pallas_call(
        _merge_kernel,
        grid=(nt // tm,),
        in_specs=[tok(D_BR, 0), tok(D_BR, 0), tok(D_BR, 0), tok(D_BR, 0),
                  tok(D_BR, o_col),
                  tok(d, gate_col0), tok(d, gate_col0 + 1), tok(d, gate_col0 + 2),
                  tok(d, 0),
                  mod(2),
                  pl.BlockSpec((3, D_BR, d), lambda i: (0, 0, 0)),
                  pl.BlockSpec((d, d), lambda i: (0, 0)),
                  pl.BlockSpec((1, D_BR), lambda i: (0, 0)),
                  pl.BlockSpec((1, d), lambda i: (0, 0)),
                  mod(3), mod(4),
                  pl.BlockSpec((d, LANES), lambda i: (0, 0)),
                  pl.BlockSpec((1, LANES), lambda i: (0, 0))],
        out_specs=[tok(d, 0), tok(d, 0), tok(LANES, 0)],
        out_shape=[jax.ShapeDtypeStruct((nt, d), F32), jax.ShapeDtypeStruct((nt, d), BF16),
                   jax.ShapeDtypeStruct((nt, LANES), F32)],
        compiler_params=_params(("arbitrary",)),
        name="merge_branches_router",
    )(yh, yf, hf, hb, z_tm, z_tm, z_tm, z_tm, x, mods3, wb, wo, nw.reshape(1, D_BR), n2w.reshape(1, d),
      mods3, mods3, rw, rb)


def _moe_kernel(xn_ref, comb_ref, wg_ref, wu_ref, wd_ref, x_ref, gate_ref, nw_ref, sh_ref, sc_ref, *out_and_scratch,
                final, n_keep):
    acc_ref = out_and_scratch[-1]
    e = pl.program_id(1)

    @pl.when(e == 0)
    def _():
        acc_ref[...] = jnp.zeros(acc_ref.shape, F32)

    xn = xn_ref[...]
    comb = comb_ref[...]
    col = lax.broadcasted_iota(jnp.int32, comb.shape, 1)
    cw = jnp.sum(jnp.where(col == e + MOE_GROUPS, comb, 0.0), axis=-1, keepdims=True)
    hg = jnp.dot(xn, wg_ref[...], preferred_element_type=F32)
    hu = jnp.dot(xn, wu_ref[...], preferred_element_type=F32)
    a = (hg * jax.nn.sigmoid(hg)) * hu * cw
    acc_ref[...] += jnp.dot(a.astype(BF16), wd_ref[...], preferred_element_type=F32)

    if final:
        y_ref, = out_and_scratch[:-1]

        @pl.when((e == MOE_EXPERTS - 1) & (pl.program_id(0) < n_keep))
        def _():
            x_new = x_ref[...] + gate_ref[...] * acc_ref[...]
            y_ref[...] = x_new * lax.rsqrt(jnp.mean(x_new * x_new, axis=-1, keepdims=True) + EPS) * nw_ref[...]
    else:
        o_ref, xn_next_ref = out_and_scratch[:-1]

        @pl.when(e == MOE_EXPERTS - 1)
        def _():
            x_new = x_ref[...] + gate_ref[...] * acc_ref[...]
            o_ref[...] = x_new
            xn_next_ref[...] = _rms_mod(x_new, nw_ref[...], sh_ref[...], sc_ref[...]).astype(BF16)


def _moe(xn, comb, wg, wu, wd, x, mods3, post_w, post_mods3, *, seg, tm, final, n_keep_rows):
    nt, d = x.shape
    n_keep = n_keep_rows // tm
    tok = pl.BlockSpec((tm, d), lambda i, e: (i, 0))
    if final:
        out_specs = [pl.BlockSpec((tm, d), lambda i, e: (jnp.minimum(i, n_keep - 1), 0))]
        out_shape = [jax.ShapeDtypeStruct((n_keep_rows, d), F32)]
    else:
        out_specs = [tok, tok]
        out_shape = [jax.ShapeDtypeStruct((nt, d), F32), jax.ShapeDtypeStruct((nt, d), BF16)]
    mod = lambda k: pl.BlockSpec((None, 1, d), lambda i, e: (seg(i), 0, k))
    return pl.pallas_call(
        functools.partial(_moe_kernel, final=final, n_keep=n_keep),
        grid=(nt // tm, MOE_EXPERTS),
        in_specs=[tok,
                  pl.BlockSpec((tm, LANES), lambda i, e: (i, 0)),
                  pl.BlockSpec((None, d, EXPERT_HID), lambda i, e: (e, 0, 0)),
                  pl.BlockSpec((None, d, EXPERT_HID), lambda i, e: (e, 0, 0)),
                  pl.BlockSpec((None, EXPERT_HID, d), lambda i, e: (e, 0, 0)),
                  tok,
                  mod(5),
                  pl.BlockSpec((1, d), lambda i, e: (0, 0)),
                  mod(0), mod(1)],
        out_specs=out_specs,
        out_shape=out_shape,
        scratch_shapes=[pltpu.VMEM((tm, d), F32)],
        compiler_params=_params(("arbitrary", "arbitrary")),
        name="moe_experts_final" if final else "moe_experts",
    )(xn, comb, wg, wu, wd, x, mods3, post_w.reshape(1, d), post_mods3, post_mods3)


def _slab_to_tm(y_s):
    ns, c, _ = y_s.shape
    return jnp.transpose(y_s, (0, 2, 1)).reshape(ns * LANES, c)


def kernel(x, c, ctx, c_ctx, ada_w, ada_b, norm1_w, norm2_w, w_in, b_in, hy_conv_w, hy_conv_b, hy_f_w1, hy_f_b1, hy_f_w2, hy_f_b2, hy_f_w3, hy_f_freq, hy_decay, hy_skip, ml_conv_w, ml_conv_b, ml_norm_w, w_branch, w_out, moe_rg_w, moe_rg_b, moe_re_w, moe_re_b, moe_w_gate, moe_w_up, moe_w_down, norm_f_w):
    nb, seq, d = x.shape
    lc = ctx.shape[1]
    depth = ada_w.shape[0]
    assert d == D_MODEL and seq % (GRID_W * 2) == 0 and lc % LANES == 0 and nb + 1 <= 8
    rows = seq // GRID_W
    a_lat = seq // LANES
    a_ctx = lc // LANES
    n_lat = nb * seq
    nt = n_lat + nb * lc
    tm = 256
    assert seq % tm == 0 and (nb * lc) % tm == 0
    seg = _seg_fn(n_lat // tm, seq // tm, nb)
    tm_moe = 512 if (seq % 512 == 0 and (nb * lc) % 512 == 0) else tm
    seg_moe = _seg_fn(n_lat // tm_moe, seq // tm_moe, nb)

    xs = jnp.concatenate([x.reshape(n_lat, d), ctx.reshape(nb * lc, d)], axis=0)
    cvec = jnp.zeros((8, d), F32).at[:nb].set(c).at[nb].set(c_ctx)
    mods = _mods(cvec, ada_w, ada_b)

    o_fn, o_ml, o_mlg, o_gate = 3 * D_BR, 4 * D_BR, 8 * D_BR, 8 * D_BR + 4 * ML_HEADS
    pad_g = LANES - 4 * ML_HEADS

    for l in range(depth):
        lp = {"hy_f_w1": hy_f_w1[l], "hy_f_b1": hy_f_b1[l], "hy_f_w2": hy_f_w2[l], "hy_f_b2": hy_f_b2[l],
              "hy_f_w3": hy_f_w3[l], "hy_f_freq": hy_f_freq[l], "hy_decay": hy_decay[l]}
        mods3 = mods[l].reshape(8, 1, 6 * d)
        wl, bl = w_in[l], b_in[l]
        w_cm = jnp.concatenate([wl[:, :o_fn], wl[:, o_ml:o_ml + 2 * D_BR], wl[:, o_fn:o_ml]], axis=1)
        b_cm = jnp.concatenate([bl[:o_fn], bl[o_ml:o_ml + 2 * D_BR], bl[o_fn:o_ml]])
        w_tm = jnp.concatenate([wl[:, o_ml + 2 * D_BR:o_mlg], wl[:, o_gate:], wl[:, o_mlg:o_gate],
                                jnp.zeros((d, pad_g), F32)], axis=1)
        b_tm = jnp.concatenate([bl[o_ml + 2 * D_BR:o_mlg], bl[o_gate:], bl[o_mlg:o_gate], jnp.zeros((pad_g,), F32)])
        c_hy, c_qk, c_fn = 0, 3 * D_BR, 5 * D_BR
        g_col = (2 * D_BR + 3 * d) // LANES

        if l == 0:
            xn = _norm_mod(xs, norm1_w[l], mods3, 0, 1, seg, tm)
        z_tm = _mm_tm(xn, w_tm.astype(BF16), b_tm)
        z_s = _mm_slab(xn, w_cm.T.astype(BF16), b_cm)

        lat = dict(rows=rows, width=GRID_W, n_batch=nb, slab0=0)
        strip = dict(rows=1, width=lc, n_batch=nb, slab0=nb * a_lat)
        hy_w, hy_b = hy_conv_w[l].reshape(9, 3 * D_BR), hy_conv_b[l]
        ml_w, ml_b = ml_conv_w[l].reshape(9, 2 * D_BR), ml_conv_b[l]
        u_lat = _conv(z_s, hy_w, hy_b, chan_lo=c_hy, chan_n=3 * D_BR, silu=False, **lat)
        u_ctx = _conv(z_s, hy_w, hy_b, chan_lo=c_hy, chan_n=3 * D_BR, silu=False, **strip)
        qk_lat = _conv(z_s, ml_w, ml_b, chan_lo=c_qk, chan_n=2 * D_BR, silu=True, **lat)
        qk_ctx = _conv(z_s, ml_w, ml_b, chan_lo=c_qk, chan_n=2 * D_BR, silu=True, **strip)

        yh_lat = _hyena(u_lat, _hyena_taps(seq, 2 * a_lat, lp), hy_skip[l], a_in=a_lat, na=2 * a_lat, n_batch=nb)
        yh_ctx = _hyena_short(u_ctx, _hyena_taps(lc, 2 * a_ctx, lp), hy_skip[l], a_n=a_ctx, n_batch=nb)
        yh_tm = jnp.concatenate([_slab_to_tm(yh_lat), yh_ctx.reshape(nb * lc, D_BR)], axis=0)

        pq = _fn_mix(z_s, a_n=a_lat, n_batch=nb, chan_lo=c_fn)
        yk = _fn_seq(pq, a_n=a_lat, n_batch=nb)
        yf_lat = jnp.transpose(yk, (0, 3, 2, 1)).reshape(n_lat, D_BR)
        yf_ctx = _fn_small(z_s, a_n=a_ctx, n_batch=nb, slab0=nb * a_lat, chan_lo=c_fn).reshape(nb * lc, D_BR)
        yf_tm = jnp.concatenate([yf_lat, yf_ctx], axis=0)

        qk_tm = _slab_to_tm(jnp.concatenate([qk_lat, qk_ctx], axis=0))
        h_f, h_b = _mlstm(qk_tm, z_tm, n_batch=nb, lat_chunks=a_lat, ctx_chunks=a_ctx, v_col=0, g_col=g_col)

        rw = jnp.concatenate([moe_rg_w[l], moe_re_w[l], jnp.zeros((d, LANES - MOE_GROUPS - MOE_EXPERTS), F32)], axis=1)
        rb = jnp.concatenate([moe_rg_b[l], moe_re_b[l], jnp.zeros((LANES - MOE_GROUPS - MOE_EXPERTS,), F32)]).reshape(1, LANES)
        xs, xn2, comb = _merge(yh_tm, yf_tm, h_f, h_b, z_tm, xs, mods3, w_branch[l].astype(BF16),
                               w_out[l].astype(BF16), ml_norm_w[l], norm2_w[l], rw, rb,
                               seg=seg, tm=tm, o_col=1, gate_col0=1)
        experts = (moe_w_gate[l].astype(BF16), moe_w_up[l].astype(BF16), moe_w_down[l].astype(BF16))
        if l + 1 < depth:
            xs, xn = _moe(xn2, comb, *experts, xs, mods3, norm1_w[l + 1], mods[l + 1].reshape(8, 1, 6 * d),
                          seg=seg_moe, tm=tm_moe, final=False, n_keep_rows=nt)
        else:
            out, = _moe(xn2, comb, *experts, xs, mods3, norm_f_w, mods3,
                        seg=seg_moe, tm=tm_moe, final=True, n_keep_rows=n_lat)

    return out.reshape(nb, seq, d)
```

```python
import functools
import math

import numpy as np
import jax
import jax.numpy as jnp
from jax import lax
from jax.experimental import pallas as pl
from jax.experimental.pallas import tpu as pltpu

F32 = jnp.float32
BF16 = jnp.bfloat16

D_MODEL = 1024
D_BR = 512
GRID_W = 64
LANES = 128
CB = 8
HY_ORDER = 2
HY_BANDS = 16
FN_GROUPS = 4
ML_HEADS = 4
MOE_GROUPS = 4
MOE_PER_GROUP = 4
MOE_EXPERTS = 16
EXPERT_HID = 256
EPS = 1e-6
VMEM_LIMIT = 56 * 1024 * 1024


def _params(sem):
    return pltpu.CompilerParams(dimension_semantics=sem, vmem_limit_bytes=VMEM_LIMIT)


def _bdot(a, b):
    return jnp.dot(a.astype(BF16), b.astype(BF16), preferred_element_type=F32)


def _split3(x):
    hi = x.astype(BF16)
    r1 = x - hi.astype(F32)
    mid = r1.astype(BF16)
    lo = (r1 - mid.astype(F32)).astype(BF16)
    return hi, mid, lo


def _dot_f32ish(x, w):
    xh, xm, xl = _split3(x)
    wh, wm, wl = _split3(w)
    d = lambda a, b: jnp.dot(a, b, preferred_element_type=F32)
    return (d(xh, wh) + (d(xh, wm) + d(xm, wh))) + (d(xm, wm) + d(xh, wl) + d(xl, wh))


def _swap_halves(x):
    return jnp.concatenate([x[..., LANES:], x[..., :LANES]], axis=-1)


def _mods_kernel(c_ref, w_ref, b_ref, o_ref):
    c = c_ref[...]
    s = c * jax.nn.sigmoid(c)
    o_ref[...] = _dot_f32ish(s, w_ref[...]) + b_ref[...]


def _mods(cvec, ada_w, ada_b):
    depth, d, n6 = ada_w.shape
    tn = 1536
    return pl.pallas_call(
        _mods_kernel,
        grid=(depth, n6 // tn),
        in_specs=[pl.BlockSpec((8, d), lambda l, j: (0, 0)),
                  pl.BlockSpec((None, d, tn), lambda l, j: (l, 0, j)),
                  pl.BlockSpec((None, 1, tn), lambda l, j: (l, 0, j))],
        out_specs=pl.BlockSpec((None, 8, tn), lambda l, j: (l, 0, j)),
        out_shape=jax.ShapeDtypeStruct((depth, 8, n6), F32),
        compiler_params=_params(("arbitrary", "arbitrary")),
        name="adaln_mods",
    )(cvec, ada_w, ada_b.reshape(depth, 1, n6))


def _norm_mod_kernel(x_ref, w_ref, sh_ref, sc_ref, o_ref):
    x = x_ref[...]
    y = x * lax.rsqrt(jnp.mean(x * x, axis=-1, keepdims=True) + EPS) * w_ref[...]
    o_ref[...] = (y * (1.0 + sc_ref[...]) + sh_ref[...]).astype(o_ref.dtype)


def _norm_mod(x, w, mods3, col_shift, col_scale, seg, tm):
    nt, d = x.shape
    return pl.pallas_call(
        _norm_mod_kernel,
        grid=(nt // tm,),
        in_specs=[pl.BlockSpec((tm, d), lambda i: (i, 0)),
                  pl.BlockSpec((1, d), lambda i: (0, 0)),
                  pl.BlockSpec((None, 1, d), lambda i: (seg(i), 0, col_shift)),
                  pl.BlockSpec((None, 1, d), lambda i: (seg(i), 0, col_scale))],
        out_specs=pl.BlockSpec((tm, d), lambda i: (i, 0)),
        out_shape=jax.ShapeDtypeStruct((nt, d), BF16),
        compiler_params=_params(("arbitrary",)),
        name="norm_mod",
    )(x, w.reshape(1, d), mods3, mods3)


def _mm_tm_kernel(x_ref, w_ref, b_ref, o_ref):
    o_ref[...] = jnp.dot(x_ref[...], w_ref[...], preferred_element_type=F32) + b_ref[...]


def _pick(n, cands):
    for c in cands:
        if n % c == 0:
            return c
    raise ValueError(f"no tile for {n} in {cands}")


def _mm_tm(xn, w, b):
    nt, k = xn.shape
    n = w.shape[1]
    tm = _pick(nt, (1056, 1024, 768, 512, 256))
    tn = _pick(n, (1408, 1024, 512, 384, 256, 128))
    return pl.pallas_call(
        _mm_tm_kernel,
        grid=(n // tn, nt // tm),
        in_specs=[pl.BlockSpec((tm, k), lambda j, i: (i, 0)),
                  pl.BlockSpec((k, tn), lambda j, i: (0, j)),
                  pl.BlockSpec((1, tn), lambda j, i: (0, j))],
        out_specs=pl.BlockSpec((tm, tn), lambda j, i: (i, j)),
        out_shape=jax.ShapeDtypeStruct((nt, n), F32),
        compiler_params=_params(("arbitrary", "arbitrary")),
        name="inproj_token_major",
    )(xn, w, b.reshape(1, n))


def _mm_slab_kernel(w_ref, x_ref, b_ref, o_ref, *, slabs):
    w = w_ref[...]
    b = b_ref[...]
    step = 2 if slabs % 2 == 0 else 1
    for s in range(0, slabs, step):
        xs = x_ref[s * LANES:(s + step) * LANES, :]
        y = lax.dot_general(w, xs, (((1,), (1,)), ((), ())), preferred_element_type=F32) + b
        for i in range(step):
            o_ref[s + i] = y[:, i * LANES:(i + 1) * LANES]


def _mm_slab(xn, wt, b):
    nt, k = xn.shape
    c = wt.shape[0]
    ns = nt // LANES
    ts = _pick(ns, (12, 11, 8, 6, 4, 3, 2, 1))
    tc = _pick(c, (512, 256, 128))
    return pl.pallas_call(
        functools.partial(_mm_slab_kernel, slabs=ts),
        grid=(c // tc, ns // ts),
        in_specs=[pl.BlockSpec((tc, k), lambda j, i: (j, 0)),
                  pl.BlockSpec((ts * LANES, k), lambda j, i: (i, 0)),
                  pl.BlockSpec((tc, 1), lambda j, i: (j, 0))],
        out_specs=pl.BlockSpec((ts, tc, LANES), lambda j, i: (i, j, 0)),
        out_shape=jax.ShapeDtypeStruct((ns, c, LANES), F32),
        compiler_params=_params(("arbitrary", "arbitrary")),
        name="inproj_slab",
    )(wt, xn, b.reshape(c, 1))


def _conv_taps(rows, width):
    taps = []
    for dr in (-1, 0, 1):
        if rows == 1 and dr != 0:
            continue
        for dw in (-1, 0, 1):
            taps.append((dr, dw))
    return taps


def _silu(x):
    return x * jax.nn.sigmoid(x)


def _conv_kernel(x_ref, w_ref, b_ref, o_ref, *, taps, width, n_slabs, silu):
    ct = x_ref.shape[1]
    lane = lax.broadcasted_iota(jnp.int32, (ct, LANES), 1)
    bias = jnp.zeros((ct, LANES), F32) + b_ref[...]
    planes = []
    for t, (dr, dw) in enumerate(taps):
        w = w_ref[t]
        if width < LANES and dw != 0:
            col = lane % width + dw
            w = jnp.where((col >= 0) & (col < width), w, 0.0)
        planes.append((dr * width + dw, w))

    def body(a, carry):
        x0 = x_ref[a]
        xm = jnp.where(a > 0, x_ref[jnp.maximum(a - 1, 0)], 0.0)
        xp = jnp.where(a < n_slabs - 1, x_ref[jnp.minimum(a + 1, n_slabs - 1)], 0.0)
        acc = bias
        for delta, w in planes:
            if delta == 0:
                src = x0
            elif delta > 0:
                src = pltpu.roll(jnp.where(lane >= delta, x0, xp), LANES - delta, 1)
            else:
                src = pltpu.roll(jnp.where(lane < LANES + delta, x0, xm), -delta, 1)
            acc = acc + src * w
        o_ref[a] = _silu(acc) if silu else acc
        return carry

    lax.fori_loop(0, n_slabs, body, 0, unroll=2 if n_slabs % 2 == 0 else 1)


def _conv_grid_kernel(x_ref, w_ref, b_ref, o_ref, ym_ref, yp_ref, *, width, n_slabs, silu):
    ct = x_ref.shape[1]
    lane = lax.broadcasted_iota(jnp.int32, (ct, LANES), 1)
    col = lane % width
    wl = [jnp.where(col >= 1, w_ref[3 * i], 0.0) for i in range(3)]
    wc = [w_ref[3 * i + 1] for i in range(3)]
    wr = [jnp.where(col < width - 1, w_ref[3 * i + 2], 0.0) for i in range(3)]
    bias = jnp.zeros((ct, LANES), F32) + b_ref[...]

    def row_sums(a, carry):
        x0 = x_ref[a]
        xl = pltpu.roll(x0, 1, 1)
        xr = pltpu.roll(x0, LANES - 1, 1)
        ym_ref[a] = wl[0] * xl + wc[0] * x0 + wr[0] * xr
        o_ref[a] = bias + wl[1] * xl + wc[1] * x0 + wr[1] * xr
        yp_ref[a] = wl[2] * xl + wc[2] * x0 + wr[2] * xr
        return carry

    def combine(a, carry):
        up = jnp.where(a > 0, ym_ref[jnp.maximum(a - 1, 0)], 0.0)
        dn = jnp.where(a < n_slabs - 1, yp_ref[jnp.minimum(a + 1, n_slabs - 1)], 0.0)
        from_up = jnp.where(lane < LANES - width, ym_ref[a], up)
        from_dn = jnp.where(lane >= width, yp_ref[a], dn)
        if 2 * width == LANES:
            y = o_ref[a] + pltpu.roll(from_up + from_dn, width, 1)
        else:
            y = o_ref[a] + pltpu.roll(from_up, width, 1) + pltpu.roll(from_dn, LANES - width, 1)
        o_ref[a] = _silu(y) if silu else y
        return carry

    unroll = 4 if n_slabs % 4 == 0 else 1
    lax.fori_loop(0, n_slabs, row_sums, 0, unroll=unroll)
    lax.fori_loop(0, n_slabs, combine, 0, unroll=unroll)


def _conv(z_s, w9, bias, *, rows, width, n_batch, slab0, chan_lo, chan_n, silu):
    seq = rows * width
    a_n = seq // LANES
    taps = tuple(_conv_taps(rows, width))
    assert all(abs(dr * width + dw) < LANES for dr, dw in taps)
    assert LANES % width == 0 or (rows == 1 and width % LANES == 0)
    tap_ids = [(dr + 1) * 3 + (dw + 1) for dr, dw in taps]
    w_t = jnp.broadcast_to(w9[jnp.array(tap_ids)][:, :, None], (len(taps), chan_n, LANES))
    ct = 64 if rows > 1 else 256
    assert chan_lo % ct == 0 and chan_n % ct == 0 and slab0 % a_n == 0
    nt_ = len(taps)
    if rows > 1:
        assert LANES % width == 0 and nt_ == 9
        body = functools.partial(_conv_grid_kernel, width=width, n_slabs=a_n, silu=silu)
        scratch = [pltpu.VMEM((a_n, ct, LANES), F32), pltpu.VMEM((a_n, ct, LANES), F32)]
    else:
        body = functools.partial(_conv_kernel, taps=taps, width=width, n_slabs=a_n, silu=silu)
        scratch = []
    return pl.pallas_call(
        body,
        scratch_shapes=scratch,
        grid=(n_batch, chan_n // ct),
        in_specs=[pl.BlockSpec((a_n, ct, LANES), lambda b, j: (slab0 // a_n + b, chan_lo // ct + j, 0)),
                  pl.BlockSpec((nt_, ct, LANES), lambda b, j: (0, j, 0)),
                  pl.BlockSpec((ct, 1), lambda b, j: (j, 0))],
        out_specs=pl.BlockSpec((a_n, ct, LANES), lambda b, j: (b, j, 0)),
        out_shape=jax.ShapeDtypeStruct((n_batch * a_n, chan_n, LANES), F32),
        compiler_params=_params(("arbitrary", "arbitrary")),
        name=f"dwconv_{rows}x{width}",
    )(z_s, w_t, bias.reshape(chan_n, 1))


def _dft_consts(a_in, na):
    n = na * LANES
    k = np.arange(na)[:, None]
    a = np.arange(a_in)[None, :]
    ang = 2 * np.pi * (k * a % na) / na
    fa = np.concatenate([np.cos(ang), -np.sin(ang)], axis=0)
    r = np.arange(LANES)
    ang_t = 2 * np.pi * (np.arange(na)[:, None] * r[None, :] % n) / n
    tr, ti = np.cos(ang_t), -np.sin(ang_t)
    ta = np.concatenate([tr, tr], axis=1)
    tb = np.concatenate([-ti, ti], axis=1)
    ang2 = 2 * np.pi * (r[:, None] * r[None, :] % LANES) / LANES
    c2, s2 = np.cos(ang2), np.sin(ang2)
    g2 = np.block([[c2, -s2], [s2, c2]])
    g2i = np.block([[c2, s2], [-s2, c2]])
    ang_i = 2 * np.pi * (np.arange(a_in)[:, None] * np.arange(na)[None, :] % na) / na
    ci, si = np.cos(ang_i) / n, -np.sin(ang_i) / n
    f = lambda v, dt: jnp.asarray(v, dtype=dt)
    return dict(fa=f(fa, F32), ta=f(ta, F32), tb=f(tb, F32), g2=f(g2, F32), g2i=f(g2i, F32),
                ci=f(ci, F32), si=f(si, F32))


def _fwd_slab_stage(m, fa, ta, tb, na):
    pp = jnp.dot(fa, m.astype(BF16), preferred_element_type=F32)
    p = jnp.concatenate([pp[:na], pp[na:]], axis=1)
    return p * ta + _swap_halves(p) * tb


def _cmul(x, kf):
    kr, ki = kf[..., :LANES], kf[..., LANES:]
    ka = jnp.concatenate([kr, kr], axis=-1)
    kb = jnp.concatenate([-ki, ki], axis=-1)
    return x * ka + _swap_halves(x) * kb


def _chan_load(ref, c):
    n, cb, _ = ref.shape
    return ref.reshape(n * cb, LANES)[pl.ds(c, n, stride=cb), :]


def _chan_store(ref, c, val):
    ref[:, c, :] = val


def _dot_f32ish_k(w, h):
    wh, wm, _ = _split3(w)
    hh, hm, _ = _split3(h)
    lhs = jnp.concatenate([wh, wh, wm], axis=1)
    rhs = jnp.concatenate([hh, hm, hh], axis=0)
    return jnp.dot(lhs, rhs, preferred_element_type=F32)


def _taps_kernel(bands_ref, w1t_ref, w1c_ref, w1s_ref, b1_ref, w2_ref, b2_ref, fq_ref, w3_ref, dec_ref, o_ref,
                 *, seq, a_seq, na, spb):
    step = pl.program_id(0)
    lane = lax.broadcasted_iota(jnp.int32, (1, LANES), 1)
    fq = fq_ref[...]
    n_total = na * LANES
    for i in range(spb):
        a = step * spb + i
        is_f = a < a_seq
        is_b = a >= na - a_seq
        live = jnp.logical_or(is_f, is_b)

        @pl.when(live)
        def _():
            n = a * LANES + lane
            pos = jnp.where(is_f, n, n_total - n)
            valid = (n > jnp.where(is_f, -1, n_total - seq)) & (n < jnp.where(is_f, seq, n_total))
            t = pos.astype(F32) / seq
            ang = ((2 * math.pi) * t) * bands_ref[...]
            pre = (w1t_ref[...] * t + _dot_f32ish(w1c_ref[...], jnp.cos(ang))
                   + _dot_f32ish(w1s_ref[...], jnp.sin(ang)))
            h = jnp.sin(fq * (pre + b1_ref[...]))
            h = jnp.sin(fq * (_dot_f32ish(w2_ref[...], h) + b2_ref[...]))
            d = jnp.where(is_f, 0, 1)
            k = _dot_f32ish_k(w3_ref[d], h) * jnp.exp(-t * jnp.abs(dec_ref[d]))
            o_ref[i] = jnp.where(valid, k, 0.0)

        @pl.when(jnp.logical_not(live))
        def _():
            o_ref[i] = jnp.zeros(o_ref.shape[1:], F32)


def _hyena_taps(seq, na, lp):
    nc = HY_ORDER * D_BR
    hid = lp["hy_f_w2"].shape[0]
    w1 = lp["hy_f_w1"]
    col = lambda v: v.reshape(-1, 1)
    bands = col(jnp.linspace(1e-4, HY_BANDS - 1, HY_BANDS, dtype=F32))
    w3 = jnp.transpose(lp["hy_f_w3"].T.reshape(HY_ORDER, 2, D_BR, hid), (1, 0, 2, 3)).reshape(2, nc, hid)
    dec = jnp.broadcast_to(jnp.transpose(lp["hy_decay"], (1, 0, 2)).reshape(2, nc, 1), (2, nc, LANES))
    spb = min(8, na)
    assert na % spb == 0
    args = (bands, col(w1[0]), w1[1:1 + HY_BANDS].T, w1[1 + HY_BANDS:].T, col(lp["hy_f_b1"]), lp["hy_f_w2"].T,
            col(lp["hy_f_b2"]), col(lp["hy_f_freq"]), w3, dec)
    full = lambda v: pl.BlockSpec(v.shape, lambda s: (0,) * v.ndim)
    return pl.pallas_call(
        functools.partial(_taps_kernel, seq=seq, a_seq=seq // LANES, na=na, spb=spb),
        grid=(na // spb,),
        in_specs=[full(v) for v in args],
        out_specs=pl.BlockSpec((spb, nc, LANES), lambda s: (s, 0, 0)),
        out_shape=jax.ShapeDtypeStruct((na, nc, LANES), F32),
        compiler_params=_params(("arbitrary",)),
        name=f"hyena_filter_taps_{na}",
    )(*args)


def _hyena_kernel(v_ref, x1_ref, x2_ref, k0_ref, k1_ref, skip_ref, fa_ref, faf_ref, ta_ref, tb_ref, g2_ref, g2i_ref,
                  ci_ref, si_ref, o_ref, p_buf, z_buf, kf_buf, *, a_in, na):
    fa, ta, tb = fa_ref[...].astype(BF16), ta_ref[...], tb_ref[...]
    ci, si = ci_ref[...].astype(BF16), si_ref[...].astype(BF16)

    @pl.when(pl.program_id(1) == 0)
    def _():
        faf = faf_ref[...].astype(BF16)
        for order, k_ref in enumerate((k0_ref, k1_ref)):
            scales = []
            for c in range(CB):
                m = _chan_load(k_ref, c)
                ss = jnp.sum(jnp.sum(m * m, axis=1, keepdims=True), axis=0, keepdims=True)
                scales.append(lax.rsqrt(ss + EPS))
                p_buf[c] = _fwd_slab_stage(m, faf, ta, tb, na)
            x = _bdot(p_buf[...].reshape(CB * na, 2 * LANES), g2_ref[...]).reshape(CB, na, 2 * LANES)
            for c in range(CB):
                kf_buf[order, c] = x[c] * scales[c]

    def spectral(order):
        x = _bdot(p_buf[...].reshape(CB * na, 2 * LANES), g2_ref[...])
        y = _cmul(x, kf_buf[order].reshape(CB * na, 2 * LANES))
        bm = _bdot(y, g2i_ref[...]).reshape(CB, na, 2 * LANES)
        p_buf[...] = bm * ta - _swap_halves(bm) * tb

    def conv_out(c):
        bb = p_buf[c]
        return (jnp.dot(ci, bb[:, :LANES].astype(BF16), preferred_element_type=F32)
                + jnp.dot(si, bb[:, LANES:].astype(BF16), preferred_element_type=F32))

    for c in range(CB):
        p_buf[c] = _fwd_slab_stage(_chan_load(v_ref, c), fa, ta, tb, na)
    spectral(0)
    for c in range(CB):
        z_buf[c] = _chan_load(x1_ref, c) * (conv_out(c) + _chan_load(v_ref, c) * skip_ref[0, c])
    for c in range(CB):
        p_buf[c] = _fwd_slab_stage(z_buf[c], fa, ta, tb, na)
    spectral(1)
    for c in range(CB):
        _chan_store(o_ref, c, _chan_load(x2_ref, c) * (conv_out(c) + z_buf[c] * skip_ref[1, c]))


def _hyena(u_s, taps_s, skip, *, a_in, na, n_batch):
    cs = _dft_consts(a_in, na)
    faf = _dft_consts(na, na)["fa"]
    nblk = D_BR // CB
    const = lambda shp: pl.BlockSpec(shp, lambda j, b: (0,) * len(shp))
    skip_b = jnp.broadcast_to(skip[:, :, None, None], (HY_ORDER, D_BR, 1, LANES))
    return pl.pallas_call(
        functools.partial(_hyena_kernel, a_in=a_in, na=na),
        grid=(nblk, n_batch),
        in_specs=[pl.BlockSpec((a_in, CB, LANES), lambda j, b: (b, j, 0)),
                  pl.BlockSpec((a_in, CB, LANES), lambda j, b: (b, nblk + j, 0)),
                  pl.BlockSpec((a_in, CB, LANES), lambda j, b: (b, 2 * nblk + j, 0)),
                  pl.BlockSpec((na, CB, LANES), lambda j, b: (0, j, 0)),
                  pl.BlockSpec((na, CB, LANES), lambda j, b: (0, nblk + j, 0)),
                  pl.BlockSpec((HY_ORDER, CB, 1, LANES), lambda j, b: (0, j, 0, 0)),
                  const((2 * na, a_in)), const((2 * na, na)), const((na, 2 * LANES)), const((na, 2 * LANES)),
                  const((2 * LANES, 2 * LANES)), const((2 * LANES, 2 * LANES)),
                  const((a_in, na)), const((a_in, na))],
        out_specs=pl.BlockSpec((a_in, CB, LANES), lambda j, b: (b, j, 0)),
        out_shape=jax.ShapeDtypeStruct((n_batch * a_in, D_BR, LANES), F32),
        scratch_shapes=[pltpu.VMEM((CB, na, 2 * LANES), F32), pltpu.VMEM((CB, a_in, LANES), F32),
                        pltpu.VMEM((HY_ORDER, CB, na, 2 * LANES), F32)],
        compiler_params=_params(("arbitrary", "arbitrary")),
        name=f"hyena_longconv_{a_in}",
    )(u_s, u_s, u_s, taps_s, taps_s, skip_b, cs["fa"], faf, cs["ta"], cs["tb"], cs["g2"], cs["g2i"],
      cs["ci"], cs["si"])


def _slabs_to_rows(ref, n):
    return jnp.concatenate([ref[a].T for a in range(n)], axis=0)


def _hyena_short_kernel(v_ref, x1_ref, x2_ref, k0_ref, k1_ref, skip_ref, f_ref, g_ref, o_ref, *, a_n, a_k):
    seq, nf = a_n * LANES, a_k * LANES
    ff = f_ref[...].astype(BF16)
    gi = g_ref[...].astype(BF16)
    v, x1, x2 = _slabs_to_rows(v_ref, a_n), _slabs_to_rows(x1_ref, a_n), _slabs_to_rows(x2_ref, a_n)

    def longconv(u, k_ref):
        k = _slabs_to_rows(k_ref, a_k)
        s = lax.rsqrt(jnp.sum(k * k, axis=0, keepdims=True) + EPS)
        kf = jnp.dot(ff, k.astype(BF16), preferred_element_type=F32) * s
        x = jnp.dot(ff[:, :seq], u.astype(BF16), preferred_element_type=F32)
        xr, xi, kr, ki = x[:nf], x[nf:], kf[:nf], kf[nf:]
        y = jnp.concatenate([xr * kr - xi * ki, xr * ki + xi * kr], axis=0)
        return jnp.dot(gi, y.astype(BF16), preferred_element_type=F32)

    z = x1 * (longconv(v, k0_ref) + v * skip_ref[0])
    o_ref[...] = x2 * (longconv(z, k1_ref) + z * skip_ref[1])


def _hyena_short(u_s, taps_s, skip, *, a_n, n_batch):
    a_k = 2 * a_n
    seq, nf = a_n * LANES, a_k * LANES
    k = np.arange(nf)
    ang = 2 * np.pi * (k[:, None] * k[None, :] % nf) / nf
    f = np.concatenate([np.cos(ang), -np.sin(ang)], axis=0)
    g = np.concatenate([np.cos(ang[:seq]), -np.sin(ang[:seq])], axis=1) / nf
    nblk = D_BR // LANES
    const = lambda shp: pl.BlockSpec(shp, lambda b, j: (0,) * len(shp))
    return pl.pallas_call(
        functools.partial(_hyena_short_kernel, a_n=a_n, a_k=a_k),
        grid=(n_batch, nblk),
        in_specs=[pl.BlockSpec((a_n, LANES, LANES), lambda b, j: (b, j, 0)),
                  pl.BlockSpec((a_n, LANES, LANES), lambda b, j: (b, nblk + j, 0)),
                  pl.BlockSpec((a_n, LANES, LANES), lambda b, j: (b, 2 * nblk + j, 0)),
                  pl.BlockSpec((a_k, LANES, LANES), lambda b, j: (0, j, 0)),
                  pl.BlockSpec((a_k, LANES, LANES), lambda b, j: (0, nblk + j, 0)),
                  pl.BlockSpec((HY_ORDER, 1, LANES), lambda b, j: (0, 0, j)),
                  const((2 * nf, nf)), const((seq, 2 * nf))],
        out_specs=pl.BlockSpec((None, seq, LANES), lambda b, j: (b, 0, j)),
        out_shape=jax.ShapeDtypeStruct((n_batch, seq, D_BR), F32),
        compiler_params=_params(("arbitrary", "arbitrary")),
        name="hyena_short",
    )(u_s, u_s, u_s, taps_s, taps_s, skip.reshape(HY_ORDER, 1, D_BR), jnp.asarray(f, F32), jnp.asarray(g, F32))


def _chan_dft_mats():
    r = np.arange(LANES)
    ang = 2 * np.pi * (r[:, None] * r[None, :] % LANES) / LANES
    return np.cos(ang), np.sin(ang)


def _fn_mix_kernel(u_ref, cs_ref, o_ref, *, n_slabs):
    w = cs_ref[...].astype(BF16)

    def body(a, carry):
        o_ref[a] = jnp.dot(w, u_ref[a].astype(BF16), preferred_element_type=F32)
        return carry

    lax.fori_loop(0, n_slabs, body, 0)


def _fn_mix(z_s, *, a_n, n_batch, chan_lo):
    c, s = _chan_dft_mats()
    w = jnp.asarray(np.concatenate([c, s], axis=0), dtype=F32)
    g0 = chan_lo // LANES
    return pl.pallas_call(
        functools.partial(_fn_mix_kernel, n_slabs=a_n),
        grid=(n_batch, FN_GROUPS),
        in_specs=[pl.BlockSpec((a_n, LANES, LANES), lambda b, g: (b, g0 + g, 0)),
                  pl.BlockSpec((2 * LANES, LANES), lambda b, g: (0, 0))],
        out_specs=pl.BlockSpec((a_n, 2 * LANES, LANES), lambda b, g: (b, g, 0)),
        out_shape=jax.ShapeDtypeStruct((n_batch * a_n, 2 * D_BR, LANES), F32),
        compiler_params=_params(("arbitrary", "arbitrary")),
        name="fnet_channel_dft",
    )(z_s, w)


def _fn_seq_kernel(p_ref, q_ref, fa_ref, tr_ref, ti_ref, g_ref, o_ref, a_buf, *, a_n, scale):
    fa, tr, ti = fa_ref[...].astype(BF16), tr_ref[...], ti_ref[...]
    for c in range(CB):
        r1 = jnp.dot(fa, _chan_load(p_ref, c).astype(BF16), preferred_element_type=F32)
        r2 = jnp.dot(fa, _chan_load(q_ref, c).astype(BF16), preferred_element_type=F32)
        ar = r1[:a_n] - r2[a_n:]
        ai = -(r2[:a_n] + r1[a_n:])
        a_buf[c] = jnp.concatenate([ar * tr - ai * ti, ar * ti + ai * tr], axis=1)
    y = _bdot(a_buf[...].reshape(CB * a_n, 2 * LANES), g_ref[...]) * scale
    o_ref[...] = y.reshape(CB, a_n, LANES)


def _fn_seq(pq, *, a_n, n_batch):
    seq = a_n * LANES
    k = np.arange(a_n)
    ang = 2 * np.pi * (k[:, None] * k[None, :] % a_n) / a_n
    fa = np.concatenate([np.cos(ang), np.sin(ang)], axis=0)
    r = np.arange(LANES)
    ang_t = 2 * np.pi * (k[:, None] * r[None, :] % seq) / seq
    c2, s2 = _chan_dft_mats()
    g = np.concatenate([c2, s2], axis=0)
    nblk = LANES // CB
    const = lambda shp: pl.BlockSpec(shp, lambda b, j: (0,) * len(shp))

    def chan_blk(j, off):
        return (j // nblk) * (2 * nblk) + off * nblk + j % nblk

    return pl.pallas_call(
        functools.partial(_fn_seq_kernel, a_n=a_n, scale=1.0 / math.sqrt(seq * LANES)),
        grid=(n_batch, D_BR // CB),
        in_specs=[pl.BlockSpec((a_n, CB, LANES), lambda b, j: (b, chan_blk(j, 0), 0)),
                  pl.BlockSpec((a_n, CB, LANES), lambda b, j: (b, chan_blk(j, 1), 0)),
                  const((2 * a_n, a_n)), const((a_n, LANES)), const((a_n, LANES)), const((2 * LANES, LANES))],
        out_specs=pl.BlockSpec((None, CB, a_n, LANES), lambda b, j: (b, j, 0, 0)),
        out_shape=jax.ShapeDtypeStruct((n_batch, D_BR, a_n, LANES), F32),
        scratch_shapes=[pltpu.VMEM((CB, a_n, 2 * LANES), F32)],
        compiler_params=_params(("arbitrary", "arbitrary")),
        name="fnet_sequence_dft",
    )(pq, pq, jnp.asarray(fa, F32), jnp.asarray(np.cos(ang_t), F32), jnp.asarray(-np.sin(ang_t), F32),
      jnp.asarray(g, F32))


def _fn_small_kernel(u_ref, cw_ref, sw_ref, cl_ref, sl_ref, o_ref, *, a_n, scale):
    u = jnp.concatenate([u_ref[a].T for a in range(a_n)], axis=0)
    p = _bdot(u, cw_ref[...])
    q = _bdot(u, sw_ref[...])
    o_ref[...] = (_bdot(cl_ref[...], p) - _bdot(sl_ref[...], q)) * scale


def _fn_small(z_s, *, a_n, n_batch, slab0, chan_lo):
    seq = a_n * LANES
    cw, sw = _chan_dft_mats()
    n = np.arange(seq)
    ang = 2 * np.pi * (n[:, None] * n[None, :] % seq) / seq
    const = lambda shp: pl.BlockSpec(shp, lambda b, g: (0,) * len(shp))
    g0 = chan_lo // LANES
    return pl.pallas_call(
        functools.partial(_fn_small_kernel, a_n=a_n, scale=1.0 / math.sqrt(seq * LANES)),
        grid=(n_batch, FN_GROUPS),
        in_specs=[pl.BlockSpec((a_n, LANES, LANES), lambda b, g: (slab0 // a_n + b, g0 + g, 0)),
                  const((LANES, LANES)), const((LANES, LANES)), const((seq, seq)), const((seq, seq))],
        out_specs=pl.BlockSpec((None, seq, LANES), lambda b, g: (b, 0, g)),
        out_shape=jax.ShapeDtypeStruct((n_batch, seq, D_BR), F32),
        compiler_params=_params(("arbitrary", "arbitrary")),
        name="fnet_short",
    )(z_s, jnp.asarray(cw, F32), jnp.asarray(sw, F32), jnp.asarray(np.cos(ang), F32), jnp.asarray(np.sin(ang), F32))


def _log_sigmoid(x):
    return jnp.minimum(x, 0.0) - jnp.log(1.0 + jnp.exp(-jnp.abs(x)))


def _exact_tri_dot(tri, x, tri_on_left):
    h, m, l = _split3(x)
    if tri_on_left:
        d = lambda p: jnp.dot(tri, p, preferred_element_type=F32)
    else:
        d = lambda p: jnp.dot(p, tri, preferred_element_type=F32)
    return d(h) + d(m) + d(l)


def _mlstm_dir(q_all, k_all, v_all, g, c_st, n_st, m_st, *, d, reverse):
    t = LANES
    hd = LANES
    row = lax.broadcasted_iota(jnp.int32, (t, t), 0)
    col = lax.broadcasted_iota(jnp.int32, (t, t), 1)
    lower = (col <= row)
    tri = jnp.where(lower, 1.0, 0.0).astype(BF16)
    tri_t = jnp.where(col >= row, 1.0, 0.0).astype(BF16)
    mask = (col >= row) if reverse else lower
    gt = g.T
    lf_c = _log_sigmoid(g)
    lf_r = lf_c.T
    if reverse:
        b_c = _exact_tri_dot(tri_t, lf_c, True)
        b_r = _exact_tri_dot(tri, lf_r, False)
    else:
        b_c = _exact_tri_dot(tri, lf_c, True)
        b_r = _exact_tri_dot(tri_t, lf_r, False)
    i_off = 2 * ML_HEADS * d
    f_off = i_off + ML_HEADS
    outs = []
    for h in range(ML_HEADS):
        sl = slice(h * hd, (h + 1) * hd)
        q = q_all[:, sl] * (hd ** -0.5)
        k = k_all[:, sl]
        v = v_all[:, sl]
        bc = b_c[:, f_off + h:f_off + h + 1]
        br = b_r[f_off + h:f_off + h + 1, :]
        ic = g[:, i_off + h:i_off + h + 1]
        ir = gt[i_off + h:i_off + h + 1, :]
        m_prev = m_st[h][:, :1]
        ct = c_st[h]
        n_prev = n_st[h]
        dm = jnp.where(mask, bc - br + ir, -jnp.inf)
        inter = bc + m_prev
        m_row = jnp.maximum(inter, jnp.max(dm, axis=-1, keepdims=True))
        w_intra = jnp.exp(dm - m_row)
        w_inter = jnp.exp(inter - m_row)
        qb, kb, vb = q.astype(BF16), k.astype(BF16), v.astype(BF16)
        s = lax.dot_general(qb, kb, (((1,), (1,)), ((), ())), preferred_element_type=F32) * w_intra
        num = jnp.dot(s.astype(BF16), vb, preferred_element_type=F32) + w_inter * jnp.dot(qb, ct.astype(BF16), preferred_element_type=F32)
        den = jnp.sum(s, axis=-1, keepdims=True) + w_inter * jnp.sum(q * n_prev, axis=-1, keepdims=True)
        den = jnp.maximum(jnp.abs(den), jnp.exp(-m_row))
        outs.append(num / den)
        if reverse:
            b_tot_c, b_tot_r = bc[:1, :], br[:, :1]
        else:
            b_tot_c, b_tot_r = bc[t - 1:, :], br[:, t - 1:]
        a_c = b_tot_c - bc + ic
        a_r = b_tot_r - br + ir
        m_new = jnp.maximum(b_tot_c + m_prev, jnp.max(a_r, axis=-1, keepdims=True))
        sc = jnp.exp(a_c - m_new)
        decay = jnp.exp(b_tot_c + m_prev - m_new)
        ks = k * sc
        c_st[h] = decay * ct + lax.dot_general(ks.astype(BF16), vb, (((0,), (0,)), ((), ())), preferred_element_type=F32)
        n_st[h] = decay * n_prev + jnp.sum(ks, axis=0, keepdims=True)
        m_st[h] = jnp.broadcast_to(m_new, (1, LANES))
    return jnp.concatenate(outs, axis=1)


def _mlstm_kernel(*refs, n_batch, ctx_chunks):
    lat_f, lat_b, ctx_f, ctx_b = refs[0:4], refs[4:8], refs[8:12], refs[12:16]
    hf_lat, hb_lat, hf_ctx, hb_ctx, c_st, n_st, m_st = refs[16:]
    j = pl.program_id(0)
    is_ctx = j < ctx_chunks

    @pl.when(j == 0)
    def _():
        c_st[...] = jnp.zeros(c_st.shape, F32)
        n_st[...] = jnp.zeros(n_st.shape, F32)
        m_st[...] = jnp.zeros(m_st.shape, F32)

    for b in range(n_batch):
        pick = lambda c_refs, l_refs: [jnp.where(is_ctx, c[b], l[b]) for c, l in zip(c_refs, l_refs)]
        hf = _mlstm_dir(*pick(ctx_f, lat_f), c_st.at[0, b], n_st.at[0, b], m_st.at[0, b], d=0, reverse=False)
        hb = _mlstm_dir(*pick(ctx_b, lat_b), c_st.at[1, b], n_st.at[1, b], m_st.at[1, b], d=1, reverse=True)

        @pl.when(is_ctx)
        def _():
            hf_ctx[b] = hf
            hb_ctx[b] = hb

        @pl.when(jnp.logical_not(is_ctx))
        def _():
            hf_lat[b] = hf
            hb_lat[b] = hb


def _mlstm(qk_lat, z_lat, qk_ctx, z_ctx, *, n_batch, v_col, g_col):
    lat_len, ctx_len = qk_lat.shape[0] // n_batch, qk_ctx.shape[0] // n_batch
    nlc, ncc = lat_len // LANES, ctx_len // LANES
    r3 = lambda a: a.reshape(n_batch, a.shape[0] // n_batch, a.shape[1])
    lf = lambda j: jnp.maximum(j - ncc, 0)
    lb = lambda j: jnp.where(j < ncc, nlc - 1, nlc - 1 - (j - ncc))
    cf = lambda j: jnp.minimum(j, ncc - 1)
    cb = lambda j: jnp.where(j < ncc, ncc - 1 - j, 0)

    def specs(ix):
        blk = lambda w, cidx: pl.BlockSpec((n_batch, LANES, w), lambda j: (0, ix(j), cidx))
        return [blk(D_BR, 0), blk(D_BR, 1), blk(D_BR, v_col), blk(LANES, g_col)]

    out = lambda ix: pl.BlockSpec((n_batch, LANES, D_BR), lambda j: (0, ix(j), 0))
    sd = lambda n: jax.ShapeDtypeStruct((n_batch, n, D_BR), F32)
    ql, zl, qc, zc = r3(qk_lat), r3(z_lat), r3(qk_ctx), r3(z_ctx)
    hf_lat, hb_lat, hf_ctx, hb_ctx = pl.pallas_call(
        functools.partial(_mlstm_kernel, n_batch=n_batch, ctx_chunks=ncc),
        grid=(ncc + nlc,),
        in_specs=specs(lf) + specs(lb) + specs(cf) + specs(cb),
        out_specs=[out(lf), out(lb), out(cf), out(cb)],
        out_shape=[sd(lat_len), sd(lat_len), sd(ctx_len), sd(ctx_len)],
        scratch_shapes=[pltpu.VMEM((2, n_batch, ML_HEADS, LANES, LANES), F32),
                        pltpu.VMEM((2, n_batch, ML_HEADS, 1, LANES), F32),
                        pltpu.VMEM((2, n_batch, ML_HEADS, 1, LANES), F32)],
        compiler_params=_params(("arbitrary",)),
        name="mlstm_bidir",
    )(ql, ql, zl, zl, ql, ql, zl, zl, qc, qc, zc, zc, qc, qc, zc, zc)
    flat = lambda a: a.reshape(a.shape[0] * a.shape[1], D_BR)
    return (flat(hf_lat), flat(hb_lat)), (flat(hf_ctx), flat(hb_ctx))


def _rms_mod(x, w, shift, scale):
    y = x * lax.rsqrt(jnp.mean(x * x, axis=-1, keepdims=True) + EPS) * w
    return y * (1.0 + scale) + shift


def _route(t, rw, rb):
    logits = _dot_f32ish(t, rw) + rb
    col = lax.broadcasted_iota(jnp.int32, logits.shape, 1)
    big = jnp.int32(1 << 20)
    ninf = -jnp.inf
    is_g = col < MOE_GROUPS
    gl = jnp.where(is_g, logits, ninf)
    gmax = jnp.max(gl, axis=-1, keepdims=True)
    g_sel = jnp.min(jnp.where(is_g & (gl == gmax), col, big), axis=-1, keepdims=True)
    p_top = 1.0 / jnp.sum(jnp.where(is_g, jnp.exp(gl - gmax), 0.0), axis=-1, keepdims=True)
    lo = MOE_GROUPS + g_sel * MOE_PER_GROUP
    in_grp = (col >= lo) & (col < lo + MOE_PER_GROUP)
    e1v = jnp.where(in_grp, logits, ninf)
    top1 = jnp.max(e1v, axis=-1, keepdims=True)
    idx1 = jnp.min(jnp.where(in_grp & (e1v == top1), col, big), axis=-1, keepdims=True)
    e2v = jnp.where(col == idx1, ninf, e1v)
    top2 = jnp.max(e2v, axis=-1, keepdims=True)
    idx2 = jnp.min(jnp.where(in_grp & (col != idx1) & (e2v == top2), col, big), axis=-1, keepdims=True)
    ex = jnp.exp(top2 - top1)
    s1 = 1.0 / (1.0 + ex)
    return jnp.where(col == idx1, p_top * s1, 0.0) + jnp.where(col == idx2, p_top * (ex * s1), 0.0)


def _merge_kernel(yh_ref, yf_ref, hf_ref, hb_ref, o_ref, g0_ref, g1_ref, g2_ref, x_ref, gate_ref,
                  wb_ref, wo_ref, nw_ref, n2_ref, sh_ref, sc_ref, rw_ref, rb_ref, out_ref, xn_ref, comb_ref):
    hd = LANES
    h = hf_ref[...] + hb_ref[...]
    parts = []
    for i in range(ML_HEADS):
        hh = h[:, i * hd:(i + 1) * hd]
        parts.append(hh * lax.rsqrt(jnp.mean(hh * hh, axis=-1, keepdims=True) + EPS))
    y_ml = jax.nn.sigmoid(o_ref[...]) * (jnp.concatenate(parts, axis=1) * nw_ref[...])
    acc = jax.nn.sigmoid(g0_ref[...]) * _bdot(yh_ref[...], wb_ref[0])
    acc = acc + jax.nn.sigmoid(g1_ref[...]) * _bdot(yf_ref[...], wb_ref[1])
    acc = acc + jax.nn.sigmoid(g2_ref[...]) * _bdot(y_ml, wb_ref[2])
    x_new = x_ref[...] + gate_ref[...] * _bdot(acc, wo_ref[...])
    out_ref[...] = x_new
    t = _rms_mod(x_new, n2_ref[...], sh_ref[...], sc_ref[...])
    xn_ref[...] = t.astype(BF16)
    comb_ref[...] = _route(t, rw_ref[...], rb_ref[...])


def _merge(yh, yf, hf, hb, z_tm, x, mods3, wb, wo, nw, n2w, rw, rb, *, seg, tm, o_col, gate_col0):
    nt, d = x.shape
    tok = lambda w, cidx: pl.BlockSpec((tm, w), lambda i: (i, cidx))
    mod = lambda k: pl.BlockSpec((None, 1, d), lambda i: (seg(i), 0, k))
    return pl.pallas_call(
        _merge_kernel,
        grid=(nt // tm,),
        in_specs=[tok(D_BR, 0), tok(D_BR, 0), tok(D_BR, 0), tok(D_BR, 0),
                  tok(D_BR, o_col),
                  tok(d, gate_col0), tok(d, gate_col0 + 1), tok(d, gate_col0 + 2),
                  tok(d, 0),
                  mod(2),
                  pl.BlockSpec((3, D_BR, d), lambda i: (0, 0, 0)),
                  pl.BlockSpec((d, d), lambda i: (0, 0)),
                  pl.BlockSpec((1, D_BR), lambda i: (0, 0)),
                  pl.BlockSpec((1, d), lambda i: (0, 0)),
                  mod(3), mod(4),
                  pl.BlockSpec((d, LANES), lambda i: (0, 0)),
                  pl.BlockSpec((1, LANES), lambda i: (0, 0))],
        out_specs=[tok(d, 0), tok(d, 0), tok(LANES, 0)],
        out_shape=[jax.ShapeDtypeStruct((nt, d), F32), jax.ShapeDtypeStruct((nt, d), BF16),
                   jax.ShapeDtypeStruct((nt, LANES), F32)],
        compiler_params=_params(("arbitrary",)),
        name="merge_branches_router",
    )(yh, yf, hf, hb, z_tm, z_tm, z_tm, z_tm, x, mods3, wb, wo, nw.reshape(1, D_BR), n2w.reshape(1, d),
      mods3, mods3, rw, rb)


def _moe_kernel(xn_ref, comb_ref, wg_ref, wu_ref, wd_ref, x_ref, gate_ref, nw_ref, sh_ref, sc_ref, *out_and_scratch,
                final, n_keep):
    acc_ref = out_and_scratch[-1]
    e = pl.program_id(1)

    @pl.when(e == 0)
    def _():
        acc_ref[...] = jnp.zeros(acc_ref.shape, F32)

    xn = xn_ref[...]
    comb = comb_ref[...]
    col = lax.broadcasted_iota(jnp.int32, comb.shape, 1)
    cw = jnp.sum(jnp.where(col == e + MOE_GROUPS, comb, 0.0), axis=-1, keepdims=True)
    hg = jnp.dot(xn, wg_ref[...], preferred_element_type=F32)
    hu = jnp.dot(xn, wu_ref[...], preferred_element_type=F32)
    a = (hg * jax.nn.sigmoid(hg)) * hu * cw
    acc_ref[...] += jnp.dot(a.astype(BF16), wd_ref[...], preferred_element_type=F32)

    if final:
        y_ref, = out_and_scratch[:-1]

        @pl.when((e == MOE_EXPERTS - 1) & (pl.program_id(0) < n_keep))
        def _():
            x_new = x_ref[...] + gate_ref[...] * acc_ref[...]
            y_ref[...] = x_new * lax.rsqrt(jnp.mean(x_new * x_new, axis=-1, keepdims=True) + EPS) * nw_ref[...]
    else:
        o_ref, xn_next_ref = out_and_scratch[:-1]

        @pl.when(e == MOE_EXPERTS - 1)
        def _():
            x_new = x_ref[...] + gate_ref[...] * acc_ref[...]
            o_ref[...] = x_new
            xn_next_ref[...] = _rms_mod(x_new, nw_ref[...], sh_ref[...], sc_ref[...]).astype(BF16)


def _moe(xn, comb, wg, wu, wd, x, mods3, post_w, post_mods3, *, seg, tm, final, n_keep_rows):
    nt, d = x.shape
    n_keep = n_keep_rows // tm
    tok = pl.BlockSpec((tm, d), lambda i, e: (i, 0))
    if final:
        out_specs = [pl.BlockSpec((tm, d), lambda i, e: (jnp.minimum(i, n_keep - 1), 0))]
        out_shape = [jax.ShapeDtypeStruct((n_keep_rows, d), F32)]
    else:
        out_specs = [tok, tok]
        out_shape = [jax.ShapeDtypeStruct((nt, d), F32), jax.ShapeDtypeStruct((nt, d), BF16)]
    mod = lambda k: pl.BlockSpec((None, 1, d), lambda i, e: (seg(i), 0, k))
    return pl.pallas_call(
        functools.partial(_moe_kernel, final=final, n_keep=n_keep),
        grid=(nt // tm, MOE_EXPERTS),
        in_specs=[tok,
                  pl.BlockSpec((tm, LANES), lambda i, e: (i, 0)),
                  pl.BlockSpec((None, d, EXPERT_HID), lambda i, e: (e, 0, 0)),
                  pl.BlockSpec((None, d, EXPERT_HID), lambda i, e: (e, 0, 0)),
                  pl.BlockSpec((None, EXPERT_HID, d), lambda i, e: (e, 0, 0)),
                  tok,
                  mod(5),
                  pl.BlockSpec((1, d), lambda i, e: (0, 0)),
                  mod(0), mod(1)],
        out_specs=out_specs,
        out_shape=out_shape,
        scratch_shapes=[pltpu.VMEM((tm, d), F32)],
        compiler_params=_params(("arbitrary", "arbitrary")),
        name="moe_experts_final" if final else "moe_experts",
    )(xn, comb, wg, wu, wd, x, mods3, post_w.reshape(1, d), post_mods3, post_mods3)


def _slab_to_tm(y_s):
    ns, c, _ = y_s.shape
    return jnp.transpose(y_s, (0, 2, 1)).reshape(ns * LANES, c)


def kernel(x, c, ctx, c_ctx, ada_w, ada_b, norm1_w, norm2_w, w_in, b_in, hy_conv_w, hy_conv_b, hy_f_w1, hy_f_b1, hy_f_w2, hy_f_b2, hy_f_w3, hy_f_freq, hy_decay, hy_skip, ml_conv_w, ml_conv_b, ml_norm_w, w_branch, w_out, moe_rg_w, moe_rg_b, moe_re_w, moe_re_b, moe_w_gate, moe_w_up, moe_w_down, norm_f_w):
    nb, seq, d = x.shape
    lc = ctx.shape[1]
    depth = ada_w.shape[0]
    assert d == D_MODEL and seq % (GRID_W * 2) == 0 and lc % LANES == 0 and nb + 1 <= 8
    rows = seq // GRID_W
    a_lat = seq // LANES
    a_ctx = lc // LANES
    n_lat, n_ctx = nb * seq, nb * lc
    tm = 256
    tm_moe = {"lat": _pick(seq, (1024, 512, 256)), "ctx": _pick(n_ctx, (512, 256))}
    assert seq % tm == 0 and n_ctx % tm == 0
    seg_of = lambda s, t: (lambda i: i // (seq // t)) if s == "lat" else (lambda i: nb)
    streams = ("lat", "ctx")
    xs = {"lat": x.reshape(n_lat, d), "ctx": ctx.reshape(n_ctx, d)}
    xn = {}
    cvec = jnp.zeros((8, d), F32).at[:nb].set(c).at[nb].set(c_ctx)
    mods = _mods(cvec, ada_w, ada_b)

    o_fn, o_ml, o_mlg, o_gate = 3 * D_BR, 4 * D_BR, 8 * D_BR, 8 * D_BR + 4 * ML_HEADS
    pad_g = LANES - 4 * ML_HEADS

    for l in range(depth):
        lp = {"hy_f_w1": hy_f_w1[l], "hy_f_b1": hy_f_b1[l], "hy_f_w2": hy_f_w2[l], "hy_f_b2": hy_f_b2[l],
              "hy_f_w3": hy_f_w3[l], "hy_f_freq": hy_f_freq[l], "hy_decay": hy_decay[l]}
        mods3 = mods[l].reshape(8, 1, 6 * d)
        wl, bl = w_in[l], b_in[l]
        w_cm = jnp.concatenate([wl[:, :o_fn], wl[:, o_ml:o_ml + 2 * D_BR], wl[:, o_fn:o_ml]], axis=1)
        b_cm = jnp.concatenate([bl[:o_fn], bl[o_ml:o_ml + 2 * D_BR], bl[o_fn:o_ml]])
        w_tm = jnp.concatenate([wl[:, o_ml + 2 * D_BR:o_mlg], wl[:, o_gate:], wl[:, o_mlg:o_gate],
                                jnp.zeros((d, pad_g), F32)], axis=1)
        b_tm = jnp.concatenate([bl[o_ml + 2 * D_BR:o_mlg], bl[o_gate:], bl[o_mlg:o_gate], jnp.zeros((pad_g,), F32)])
        c_hy, c_qk, c_fn = 0, 3 * D_BR, 5 * D_BR
        g_col = (2 * D_BR + 3 * d) // LANES

        last = l + 1 == depth
        live = ("lat",) if last else streams
        if l == 0:
            xn = {s: _norm_mod(xs[s], norm1_w[l], mods3, 0, 1, seg_of(s, tm), tm) for s in streams}
        w_tm_b, w_cm_t = w_tm.astype(BF16), w_cm.T.astype(BF16)
        z_tm = {s: _mm_tm(xn[s], w_tm_b, b_tm) for s in streams}
        z_s = {s: _mm_slab(xn[s], w_cm_t, b_cm) for s in streams}

        grid_kw = {"lat": dict(rows=rows, width=GRID_W), "ctx": dict(rows=1, width=lc)}
        hy_w, hy_b = hy_conv_w[l].reshape(9, 3 * D_BR), hy_conv_b[l]
        ml_w, ml_b = ml_conv_w[l].reshape(9, 2 * D_BR), ml_conv_b[l]
        conv = lambda s, w, b, lo, n, act: _conv(z_s[s], w, b, chan_lo=lo, chan_n=n, silu=act, n_batch=nb,
                                                 slab0=0, **grid_kw[s])
        u = {s: conv(s, hy_w, hy_b, c_hy, 3 * D_BR, False) for s in live}
        qk = {s: _slab_to_tm(conv(s, ml_w, ml_b, c_qk, 2 * D_BR, True)) for s in streams}

        (hf_lat, hb_lat), (hf_ctx, hb_ctx) = _mlstm(qk["lat"], z_tm["lat"], qk["ctx"], z_tm["ctx"],
                                                    n_batch=nb, v_col=0, g_col=g_col)
        h_f, h_b = {"lat": hf_lat, "ctx": hf_ctx}, {"lat": hb_lat, "ctx": hb_ctx}

        yh, yf = {}, {}
        yh["lat"] = _slab_to_tm(_hyena(u["lat"], _hyena_taps(seq, 2 * a_lat, lp), hy_skip[l],
                                       a_in=a_lat, na=2 * a_lat, n_batch=nb))
        pq = _fn_mix(z_s["lat"], a_n=a_lat, n_batch=nb, chan_lo=c_fn)
        yk = _fn_seq(pq, a_n=a_lat, n_batch=nb)
        yf["lat"] = jnp.transpose(yk, (0, 3, 2, 1)).reshape(n_lat, D_BR)
        if not last:
            yh["ctx"] = _hyena_short(u["ctx"], _hyena_taps(lc, 2 * a_ctx, lp), hy_skip[l],
                                     a_n=a_ctx, n_batch=nb).reshape(n_ctx, D_BR)
            yf["ctx"] = _fn_small(z_s["ctx"], a_n=a_ctx, n_batch=nb, slab0=0, chan_lo=c_fn).reshape(n_ctx, D_BR)

        rw = jnp.concatenate([moe_rg_w[l], moe_re_w[l], jnp.zeros((d, LANES - MOE_GROUPS - MOE_EXPERTS), F32)], axis=1)
        rb = jnp.concatenate([moe_rg_b[l], moe_re_b[l], jnp.zeros((LANES - MOE_GROUPS - MOE_EXPERTS,), F32)]).reshape(1, LANES)
        wb, wo = w_branch[l].astype(BF16), w_out[l].astype(BF16)
        experts = (moe_w_gate[l].astype(BF16), moe_w_up[l].astype(BF16), moe_w_down[l].astype(BF16))
        for s in live:
            xs[s], xn2, comb = _merge(yh[s], yf[s], h_f[s], h_b[s], z_tm[s], xs[s], mods3, wb, wo, ml_norm_w[l],
                                      norm2_w[l], rw, rb, seg=seg_of(s, tm), tm=tm, o_col=1, gate_col0=1)
            moe_kw = dict(seg=seg_of(s, tm_moe[s]), tm=tm_moe[s], n_keep_rows=xs[s].shape[0])
            if last:
                out, = _moe(xn2, comb, *experts, xs[s], mods3, norm_f_w, mods3, final=True, **moe_kw)
            else:
                xs[s], xn[s] = _moe(xn2, comb, *experts, xs[s], mods3, norm1_w[l + 1],
                                    mods[l + 1].reshape(8, 1, 6 * d), final=False, **moe_kw)

    return out.reshape(nb, seq, d)
```

```python
import functools
import math

import numpy as np
import jax
import jax.numpy as jnp
from jax import lax
from jax.experimental import pallas as pl
from jax.experimental.pallas import tpu as pltpu

F32 = jnp.float32
BF16 = jnp.bfloat16

D_MODEL = 1024
D_BR = 512
GRID_W = 64
LANES = 128
CB = 8
HY_ORDER = 2
HY_BANDS = 16
FN_GROUPS = 4
ML_HEADS = 4
MOE_GROUPS = 4
MOE_PER_GROUP = 4
MOE_EXPERTS = 16
EXPERT_HID = 256
EPS = 1e-6
VMEM_LIMIT = 56 * 1024 * 1024


def _params(sem):
    return pltpu.CompilerParams(dimension_semantics=sem, vmem_limit_bytes=VMEM_LIMIT)


def _bdot(a, b):
    return jnp.dot(a.astype(BF16), b.astype(BF16), preferred_element_type=F32)


def _split3(x):
    hi = x.astype(BF16)
    r1 = x - hi.astype(F32)
    mid = r1.astype(BF16)
    lo = (r1 - mid.astype(F32)).astype(BF16)
    return hi, mid, lo


def _dot_f32ish(x, w):
    xh, xm, xl = _split3(x)
    wh, wm, wl = _split3(w)
    d = lambda a, b: jnp.dot(a, b, preferred_element_type=F32)
    return (d(xh, wh) + (d(xh, wm) + d(xm, wh))) + (d(xm, wm) + d(xh, wl) + d(xl, wh))


def _swap_halves(x):
    return jnp.concatenate([x[..., LANES:], x[..., :LANES]], axis=-1)


def _mods_kernel(c_ref, w_ref, b_ref, o_ref):
    c = c_ref[...]
    s = c * jax.nn.sigmoid(c)
    o_ref[...] = _dot_f32ish(s, w_ref[...]) + b_ref[...]


def _mods(cvec, ada_w, ada_b):
    depth, d, n6 = ada_w.shape
    tn = 1536
    return pl.pallas_call(
        _mods_kernel,
        grid=(depth, n6 // tn),
        in_specs=[pl.BlockSpec((8, d), lambda l, j: (0, 0)),
                  pl.BlockSpec((None, d, tn), lambda l, j: (l, 0, j)),
                  pl.BlockSpec((None, 1, tn), lambda l, j: (l, 0, j))],
        out_specs=pl.BlockSpec((None, 8, tn), lambda l, j: (l, 0, j)),
        out_shape=jax.ShapeDtypeStruct((depth, 8, n6), F32),
        compiler_params=_params(("arbitrary", "arbitrary")),
        name="adaln_mods",
    )(cvec, ada_w, ada_b.reshape(depth, 1, n6))


def _norm_mod_kernel(x_ref, w_ref, sh_ref, sc_ref, o_ref):
    x = x_ref[...]
    y = x * lax.rsqrt(jnp.mean(x * x, axis=-1, keepdims=True) + EPS) * w_ref[...]
    o_ref[...] = (y * (1.0 + sc_ref[...]) + sh_ref[...]).astype(o_ref.dtype)


def _norm_mod(x, w, mods3, col_shift, col_scale, seg, tm):
    nt, d = x.shape
    return pl.pallas_call(
        _norm_mod_kernel,
        grid=(nt // tm,),
        in_specs=[pl.BlockSpec((tm, d), lambda i: (i, 0)),
                  pl.BlockSpec((1, d), lambda i: (0, 0)),
                  pl.BlockSpec((None, 1, d), lambda i: (seg(i), 0, col_shift)),
                  pl.BlockSpec((None, 1, d), lambda i: (seg(i), 0, col_scale))],
        out_specs=pl.BlockSpec((tm, d), lambda i: (i, 0)),
        out_shape=jax.ShapeDtypeStruct((nt, d), BF16),
        compiler_params=_params(("arbitrary",)),
        name="norm_mod",
    )(x, w.reshape(1, d), mods3, mods3)


def _mm_tm_kernel(x_ref, w_ref, b_ref, o_ref):
    o_ref[...] = jnp.dot(x_ref[...], w_ref[...], preferred_element_type=F32) + b_ref[...]


def _pick(n, cands):
    for c in cands:
        if n % c == 0:
            return c
    raise ValueError(f"no tile for {n} in {cands}")


def _mm_tm(xn, w, b):
    nt, k = xn.shape
    n = w.shape[1]
    tm = _pick(nt, (1056, 1024, 768, 512, 256))
    tn = _pick(n, (1408, 1024, 512, 384, 256, 128))
    return pl.pallas_call(
        _mm_tm_kernel,
        grid=(n // tn, nt // tm),
        in_specs=[pl.BlockSpec((tm, k), lambda j, i: (i, 0)),
                  pl.BlockSpec((k, tn), lambda j, i: (0, j)),
                  pl.BlockSpec((1, tn), lambda j, i: (0, j))],
        out_specs=pl.BlockSpec((tm, tn), lambda j, i: (i, j)),
        out_shape=jax.ShapeDtypeStruct((nt, n), F32),
        compiler_params=_params(("arbitrary", "arbitrary")),
        name="inproj_token_major",
    )(xn, w, b.reshape(1, n))


def _mm_slab_kernel(w_ref, x_ref, b_ref, o_ref, *, slabs):
    w = w_ref[...]
    b = b_ref[...]
    step = 2 if slabs % 2 == 0 else 1
    for s in range(0, slabs, step):
        xs = x_ref[s * LANES:(s + step) * LANES, :]
        y = lax.dot_general(w, xs, (((1,), (1,)), ((), ())), preferred_element_type=F32) + b
        for i in range(step):
            o_ref[s + i] = y[:, i * LANES:(i + 1) * LANES]


def _mm_slab(xn, wt, b):
    nt, k = xn.shape
    c = wt.shape[0]
    ns = nt // LANES
    ts = _pick(ns, (12, 11, 8, 6, 4, 3, 2, 1))
    tc = _pick(c, (512, 256, 128))
    return pl.pallas_call(
        functools.partial(_mm_slab_kernel, slabs=ts),
        grid=(c // tc, ns // ts),
        in_specs=[pl.BlockSpec((tc, k), lambda j, i: (j, 0)),
                  pl.BlockSpec((ts * LANES, k), lambda j, i: (i, 0)),
                  pl.BlockSpec((tc, 1), lambda j, i: (j, 0))],
        out_specs=pl.BlockSpec((ts, tc, LANES), lambda j, i: (i, j, 0)),
        out_shape=jax.ShapeDtypeStruct((ns, c, LANES), F32),
        compiler_params=_params(("arbitrary", "arbitrary")),
        name="inproj_slab",
    )(wt, xn, b.reshape(c, 1))


def _conv_taps(rows, width):
    taps = []
    for dr in (-1, 0, 1):
        if rows == 1 and dr != 0:
            continue
        for dw in (-1, 0, 1):
            taps.append((dr, dw))
    return taps


def _silu(x):
    return x * jax.nn.sigmoid(x)


def _conv_kernel(x_ref, w_ref, b_ref, o_ref, *, taps, width, n_slabs, silu):
    ct = x_ref.shape[1]
    lane = lax.broadcasted_iota(jnp.int32, (ct, LANES), 1)
    bias = jnp.zeros((ct, LANES), F32) + b_ref[...]
    planes = []
    for t, (dr, dw) in enumerate(taps):
        w = w_ref[t]
        if width < LANES and dw != 0:
            col = lane % width + dw
            w = jnp.where((col >= 0) & (col < width), w, 0.0)
        planes.append((dr * width + dw, w))

    def body(a, carry):
        x0 = x_ref[a]
        xm = jnp.where(a > 0, x_ref[jnp.maximum(a - 1, 0)], 0.0)
        xp = jnp.where(a < n_slabs - 1, x_ref[jnp.minimum(a + 1, n_slabs - 1)], 0.0)
        acc = bias
        for delta, w in planes:
            if delta == 0:
                src = x0
            elif delta > 0:
                src = pltpu.roll(jnp.where(lane >= delta, x0, xp), LANES - delta, 1)
            else:
                src = pltpu.roll(jnp.where(lane < LANES + delta, x0, xm), -delta, 1)
            acc = acc + src * w
        o_ref[a] = _silu(acc) if silu else acc
        return carry

    lax.fori_loop(0, n_slabs, body, 0, unroll=2 if n_slabs % 2 == 0 else 1)


def _conv_grid_kernel(x_ref, w_ref, b_ref, o_ref, ym_ref, yp_ref, *, width, n_slabs, silu):
    ct = x_ref.shape[1]
    lane = lax.broadcasted_iota(jnp.int32, (ct, LANES), 1)
    col = lane % width
    wl = [jnp.where(col >= 1, w_ref[3 * i], 0.0) for i in range(3)]
    wc = [w_ref[3 * i + 1] for i in range(3)]
    wr = [jnp.where(col < width - 1, w_ref[3 * i + 2], 0.0) for i in range(3)]
    bias = jnp.zeros((ct, LANES), F32) + b_ref[...]

    def row_sums(a, carry):
        x0 = x_ref[a]
        xl = pltpu.roll(x0, 1, 1)
        xr = pltpu.roll(x0, LANES - 1, 1)
        ym_ref[a] = wl[0] * xl + wc[0] * x0 + wr[0] * xr
        o_ref[a] = bias + wl[1] * xl + wc[1] * x0 + wr[1] * xr
        yp_ref[a] = wl[2] * xl + wc[2] * x0 + wr[2] * xr
        return carry

    def combine(a, carry):
        up = jnp.where(a > 0, ym_ref[jnp.maximum(a - 1, 0)], 0.0)
        dn = jnp.where(a < n_slabs - 1, yp_ref[jnp.minimum(a + 1, n_slabs - 1)], 0.0)
        from_up = jnp.where(lane < LANES - width, ym_ref[a], up)
        from_dn = jnp.where(lane >= width, yp_ref[a], dn)
        if 2 * width == LANES:
            y = o_ref[a] + pltpu.roll(from_up + from_dn, width, 1)
        else:
            y = o_ref[a] + pltpu.roll(from_up, width, 1) + pltpu.roll(from_dn, LANES - width, 1)
        o_ref[a] = _silu(y) if silu else y
        return carry

    unroll = 4 if n_slabs % 4 == 0 else 1
    lax.fori_loop(0, n_slabs, row_sums, 0, unroll=unroll)
    lax.fori_loop(0, n_slabs, combine, 0, unroll=unroll)


def _conv(z_s, w9, bias, *, rows, width, n_batch, slab0, chan_lo, chan_n, silu):
    seq = rows * width
    a_n = seq // LANES
    taps = tuple(_conv_taps(rows, width))
    assert all(abs(dr * width + dw) < LANES for dr, dw in taps)
    assert LANES % width == 0 or (rows == 1 and width % LANES == 0)
    tap_ids = [(dr + 1) * 3 + (dw + 1) for dr, dw in taps]
    w_t = jnp.broadcast_to(w9[jnp.array(tap_ids)][:, :, None], (len(taps), chan_n, LANES))
    ct = 64 if rows > 1 else 256
    assert chan_lo % ct == 0 and chan_n % ct == 0 and slab0 % a_n == 0
    nt_ = len(taps)
    if rows > 1:
        assert LANES % width == 0 and nt_ == 9
        body = functools.partial(_conv_grid_kernel, width=width, n_slabs=a_n, silu=silu)
        scratch = [pltpu.VMEM((a_n, ct, LANES), F32), pltpu.VMEM((a_n, ct, LANES), F32)]
    else:
        body = functools.partial(_conv_kernel, taps=taps, width=width, n_slabs=a_n, silu=silu)
        scratch = []
    return pl.pallas_call(
        body,
        scratch_shapes=scratch,
        grid=(n_batch, chan_n // ct),
        in_specs=[pl.BlockSpec((a_n, ct, LANES), lambda b, j: (slab0 // a_n + b, chan_lo // ct + j, 0)),
                  pl.BlockSpec((nt_, ct, LANES), lambda b, j: (0, j, 0)),
                  pl.BlockSpec((ct, 1), lambda b, j: (j, 0))],
        out_specs=pl.BlockSpec((a_n, ct, LANES), lambda b, j: (b, j, 0)),
        out_shape=jax.ShapeDtypeStruct((n_batch * a_n, chan_n, LANES), F32),
        compiler_params=_params(("arbitrary", "arbitrary")),
        name=f"dwconv_{rows}x{width}",
    )(z_s, w_t, bias.reshape(chan_n, 1))


def _dft_consts(a_in, na):
    n = na * LANES
    k = np.arange(na)[:, None]
    a = np.arange(a_in)[None, :]
    ang = 2 * np.pi * (k * a % na) / na
    fa = np.concatenate([np.cos(ang), -np.sin(ang)], axis=0)
    r = np.arange(LANES)
    ang_t = 2 * np.pi * (np.arange(na)[:, None] * r[None, :] % n) / n
    tr, ti = np.cos(ang_t), -np.sin(ang_t)
    ta = np.concatenate([tr, tr], axis=1)
    tb = np.concatenate([-ti, ti], axis=1)
    ang2 = 2 * np.pi * (r[:, None] * r[None, :] % LANES) / LANES
    c2, s2 = np.cos(ang2), np.sin(ang2)
    g2 = np.block([[c2, -s2], [s2, c2]])
    g2i = np.block([[c2, s2], [-s2, c2]])
    ang_i = 2 * np.pi * (np.arange(a_in)[:, None] * np.arange(na)[None, :] % na) / na
    ci, si = np.cos(ang_i) / n, -np.sin(ang_i) / n
    f = lambda v, dt: jnp.asarray(v, dtype=dt)
    return dict(fa=f(fa, F32), ta=f(ta, F32), tb=f(tb, F32), g2=f(g2, F32), g2i=f(g2i, F32),
                ci=f(ci, F32), si=f(si, F32))


def _fwd_slab_stage(m, fa, ta, tb, na):
    pp = jnp.dot(fa, m.astype(BF16), preferred_element_type=F32)
    p = jnp.concatenate([pp[:na], pp[na:]], axis=1)
    return p * ta + _swap_halves(p) * tb


def _cmul(x, kf):
    kr, ki = kf[..., :LANES], kf[..., LANES:]
    ka = jnp.concatenate([kr, kr], axis=-1)
    kb = jnp.concatenate([-ki, ki], axis=-1)
    return x * ka + _swap_halves(x) * kb


def _chan_load(ref, c):
    n, cb, _ = ref.shape
    return ref.reshape(n * cb, LANES)[pl.ds(c, n, stride=cb), :]


def _chan_store(ref, c, val):
    ref[:, c, :] = val


def _dot_f32ish_k(w, h):
    wh, wm, _ = _split3(w)
    hh, hm, _ = _split3(h)
    lhs = jnp.concatenate([wh, wh, wm], axis=1)
    rhs = jnp.concatenate([hh, hm, hh], axis=0)
    return jnp.dot(lhs, rhs, preferred_element_type=F32)


def _taps_kernel(bands_ref, w1t_ref, w1c_ref, w1s_ref, b1_ref, w2_ref, b2_ref, fq_ref, w3_ref, dec_ref, o_ref,
                 *, seq, a_seq, na, spb):
    step = pl.program_id(0)
    lane = lax.broadcasted_iota(jnp.int32, (1, LANES), 1)
    fq = fq_ref[...]
    n_total = na * LANES
    for i in range(spb):
        a = step * spb + i
        is_f = a < a_seq
        is_b = a >= na - a_seq
        live = jnp.logical_or(is_f, is_b)

        @pl.when(live)
        def _():
            n = a * LANES + lane
            pos = jnp.where(is_f, n, n_total - n)
            valid = (n > jnp.where(is_f, -1, n_total - seq)) & (n < jnp.where(is_f, seq, n_total))
            t = pos.astype(F32) / seq
            ang = ((2 * math.pi) * t) * bands_ref[...]
            pre = (w1t_ref[...] * t + _dot_f32ish(w1c_ref[...], jnp.cos(ang))
                   + _dot_f32ish(w1s_ref[...], jnp.sin(ang)))
            h = jnp.sin(fq * (pre + b1_ref[...]))
            h = jnp.sin(fq * (_dot_f32ish(w2_ref[...], h) + b2_ref[...]))
            d = jnp.where(is_f, 0, 1)
            k = _dot_f32ish_k(w3_ref[d], h) * jnp.exp(-t * jnp.abs(dec_ref[d]))
            o_ref[i] = jnp.where(valid, k, 0.0)

        @pl.when(jnp.logical_not(live))
        def _():
            o_ref[i] = jnp.zeros(o_ref.shape[1:], F32)


def _hyena_taps(seq, na, lp):
    nc = HY_ORDER * D_BR
    hid = lp["hy_f_w2"].shape[0]
    w1 = lp["hy_f_w1"]
    col = lambda v: v.reshape(-1, 1)
    bands = col(jnp.linspace(1e-4, HY_BANDS - 1, HY_BANDS, dtype=F32))
    w3 = jnp.transpose(lp["hy_f_w3"].T.reshape(HY_ORDER, 2, D_BR, hid), (1, 0, 2, 3)).reshape(2, nc, hid)
    dec = jnp.broadcast_to(jnp.transpose(lp["hy_decay"], (1, 0, 2)).reshape(2, nc, 1), (2, nc, LANES))
    spb = min(8, na)
    assert na % spb == 0
    args = (bands, col(w1[0]), w1[1:1 + HY_BANDS].T, w1[1 + HY_BANDS:].T, col(lp["hy_f_b1"]), lp["hy_f_w2"].T,
            col(lp["hy_f_b2"]), col(lp["hy_f_freq"]), w3, dec)
    full = lambda v: pl.BlockSpec(v.shape, lambda s: (0,) * v.ndim)
    return pl.pallas_call(
        functools.partial(_taps_kernel, seq=seq, a_seq=seq // LANES, na=na, spb=spb),
        grid=(na // spb,),
        in_specs=[full(v) for v in args],
        out_specs=pl.BlockSpec((spb, nc, LANES), lambda s: (s, 0, 0)),
        out_shape=jax.ShapeDtypeStruct((na, nc, LANES), F32),
        compiler_params=_params(("arbitrary",)),
        name=f"hyena_filter_taps_{na}",
    )(*args)


def _hyena_kernel(v_ref, x1_ref, x2_ref, k0_ref, k1_ref, skip_ref, fa_ref, faf_ref, ta_ref, tb_ref, g2_ref, g2i_ref,
                  ci_ref, si_ref, o_ref, p_buf, z_buf, kf_buf, *, a_in, na):
    fa, ta, tb = fa_ref[...].astype(BF16), ta_ref[...], tb_ref[...]
    ci, si = ci_ref[...].astype(BF16), si_ref[...].astype(BF16)

    @pl.when(pl.program_id(1) == 0)
    def _():
        faf = faf_ref[...].astype(BF16)
        for order, k_ref in enumerate((k0_ref, k1_ref)):
            scales = []
            for c in range(CB):
                m = _chan_load(k_ref, c)
                ss = jnp.sum(jnp.sum(m * m, axis=1, keepdims=True), axis=0, keepdims=True)
                scales.append(lax.rsqrt(ss + EPS))
                p_buf[c] = _fwd_slab_stage(m, faf, ta, tb, na)
            x = _bdot(p_buf[...].reshape(CB * na, 2 * LANES), g2_ref[...]).reshape(CB, na, 2 * LANES)
            for c in range(CB):
                kf_buf[order, c] = x[c] * scales[c]

    def spectral(order):
        x = _bdot(p_buf[...].reshape(CB * na, 2 * LANES), g2_ref[...])
        y = _cmul(x, kf_buf[order].reshape(CB * na, 2 * LANES))
        bm = _bdot(y, g2i_ref[...]).reshape(CB, na, 2 * LANES)
        p_buf[...] = bm * ta - _swap_halves(bm) * tb

    def conv_out(c):
        bb = p_buf[c]
        return (jnp.dot(ci, bb[:, :LANES].astype(BF16), preferred_element_type=F32)
                + jnp.dot(si, bb[:, LANES:].astype(BF16), preferred_element_type=F32))

    for c in range(CB):
        p_buf[c] = _fwd_slab_stage(_chan_load(v_ref, c), fa, ta, tb, na)
    spectral(0)
    for c in range(CB):
        z_buf[c] = _chan_load(x1_ref, c) * (conv_out(c) + _chan_load(v_ref, c) * skip_ref[0, c])
    for c in range(CB):
        p_buf[c] = _fwd_slab_stage(z_buf[c], fa, ta, tb, na)
    spectral(1)
    for c in range(CB):
        _chan_store(o_ref, c, _chan_load(x2_ref, c) * (conv_out(c) + z_buf[c] * skip_ref[1, c]))


def _hyena(u_s, taps_s, skip, *, a_in, na, n_batch):
    cs = _dft_consts(a_in, na)
    faf = _dft_consts(na, na)["fa"]
    nblk = D_BR // CB
    const = lambda shp: pl.BlockSpec(shp, lambda j, b: (0,) * len(shp))
    skip_b = jnp.broadcast_to(skip[:, :, None, None], (HY_ORDER, D_BR, 1, LANES))
    return pl.pallas_call(
        functools.partial(_hyena_kernel, a_in=a_in, na=na),
        grid=(nblk, n_batch),
        in_specs=[pl.BlockSpec((a_in, CB, LANES), lambda j, b: (b, j, 0)),
                  pl.BlockSpec((a_in, CB, LANES), lambda j, b: (b, nblk + j, 0)),
                  pl.BlockSpec((a_in, CB, LANES), lambda j, b: (b, 2 * nblk + j, 0)),
                  pl.BlockSpec((na, CB, LANES), lambda j, b: (0, j, 0)),
                  pl.BlockSpec((na, CB, LANES), lambda j, b: (0, nblk + j, 0)),
                  pl.BlockSpec((HY_ORDER, CB, 1, LANES), lambda j, b: (0, j, 0, 0)),
                  const((2 * na, a_in)), const((2 * na, na)), const((na, 2 * LANES)), const((na, 2 * LANES)),
                  const((2 * LANES, 2 * LANES)), const((2 * LANES, 2 * LANES)),
                  const((a_in, na)), const((a_in, na))],
        out_specs=pl.BlockSpec((a_in, CB, LANES), lambda j, b: (b, j, 0)),
        out_shape=jax.ShapeDtypeStruct((n_batch * a_in, D_BR, LANES), F32),
        scratch_shapes=[pltpu.VMEM((CB, na, 2 * LANES), F32), pltpu.VMEM((CB, a_in, LANES), F32),
                        pltpu.VMEM((HY_ORDER, CB, na, 2 * LANES), F32)],
        compiler_params=_params(("arbitrary", "arbitrary")),
        name=f"hyena_longconv_{a_in}",
    )(u_s, u_s, u_s, taps_s, taps_s, skip_b, cs["fa"], faf, cs["ta"], cs["tb"], cs["g2"], cs["g2i"],
      cs["ci"], cs["si"])


def _slabs_to_rows(ref, n):
    return jnp.concatenate([ref[a].T for a in range(n)], axis=0)


def _hyena_short_kernel(v_ref, x1_ref, x2_ref, k0_ref, k1_ref, skip_ref, f_ref, g_ref, o_ref, *, a_n, a_k):
    seq, nf = a_n * LANES, a_k * LANES
    ff = f_ref[...].astype(BF16)
    gi = g_ref[...].astype(BF16)
    v, x1, x2 = _slabs_to_rows(v_ref, a_n), _slabs_to_rows(x1_ref, a_n), _slabs_to_rows(x2_ref, a_n)

    def longconv(u, k_ref):
        k = _slabs_to_rows(k_ref, a_k)
        s = lax.rsqrt(jnp.sum(k * k, axis=0, keepdims=True) + EPS)
        kf = jnp.dot(ff, k.astype(BF16), preferred_element_type=F32) * s
        x = jnp.dot(ff[:, :seq], u.astype(BF16), preferred_element_type=F32)
        xr, xi, kr, ki = x[:nf], x[nf:], kf[:nf], kf[nf:]
        y = jnp.concatenate([xr * kr - xi * ki, xr * ki + xi * kr], axis=0)
        return jnp.dot(gi, y.astype(BF16), preferred_element_type=F32)

    z = x1 * (longconv(v, k0_ref) + v * skip_ref[0])
    o_ref[...] = x2 * (longconv(z, k1_ref) + z * skip_ref[1])


def _hyena_short(u_s, taps_s, skip, *, a_n, n_batch):
    a_k = 2 * a_n
    seq, nf = a_n * LANES, a_k * LANES
    k = np.arange(nf)
    ang = 2 * np.pi * (k[:, None] * k[None, :] % nf) / nf
    f = np.concatenate([np.cos(ang), -np.sin(ang)], axis=0)
    g = np.concatenate([np.cos(ang[:seq]), -np.sin(ang[:seq])], axis=1) / nf
    nblk = D_BR // LANES
    const = lambda shp: pl.BlockSpec(shp, lambda b, j: (0,) * len(shp))
    return pl.pallas_call(
        functools.partial(_hyena_short_kernel, a_n=a_n, a_k=a_k),
        grid=(n_batch, nblk),
        in_specs=[pl.BlockSpec((a_n, LANES, LANES), lambda b, j: (b, j, 0)),
                  pl.BlockSpec((a_n, LANES, LANES), lambda b, j: (b, nblk + j, 0)),
                  pl.BlockSpec((a_n, LANES, LANES), lambda b, j: (b, 2 * nblk + j, 0)),
                  pl.BlockSpec((a_k, LANES, LANES), lambda b, j: (0, j, 0)),
                  pl.BlockSpec((a_k, LANES, LANES), lambda b, j: (0, nblk + j, 0)),
                  pl.BlockSpec((HY_ORDER, 1, LANES), lambda b, j: (0, 0, j)),
                  const((2 * nf, nf)), const((seq, 2 * nf))],
        out_specs=pl.BlockSpec((None, seq, LANES), lambda b, j: (b, 0, j)),
        out_shape=jax.ShapeDtypeStruct((n_batch, seq, D_BR), F32),
        compiler_params=_params(("arbitrary", "arbitrary")),
        name="hyena_short",
    )(u_s, u_s, u_s, taps_s, taps_s, skip.reshape(HY_ORDER, 1, D_BR), jnp.asarray(f, F32), jnp.asarray(g, F32))


def _chan_dft_mats():
    r = np.arange(LANES)
    ang = 2 * np.pi * (r[:, None] * r[None, :] % LANES) / LANES
    return np.cos(ang), np.sin(ang)


def _fn_mix_kernel(u_ref, cs_ref, o_ref, *, n_slabs):
    w = cs_ref[...].astype(BF16)

    def body(a, carry):
        o_ref[a] = jnp.dot(w, u_ref[a].astype(BF16), preferred_element_type=F32)
        return carry

    lax.fori_loop(0, n_slabs, body, 0)


def _fn_mix(z_s, *, a_n, n_batch, chan_lo):
    c, s = _chan_dft_mats()
    w = jnp.asarray(np.concatenate([c, s], axis=0), dtype=F32)
    g0 = chan_lo // LANES
    return pl.pallas_call(
        functools.partial(_fn_mix_kernel, n_slabs=a_n),
        grid=(n_batch, FN_GROUPS),
        in_specs=[pl.BlockSpec((a_n, LANES, LANES), lambda b, g: (b, g0 + g, 0)),
                  pl.BlockSpec((2 * LANES, LANES), lambda b, g: (0, 0))],
        out_specs=pl.BlockSpec((a_n, 2 * LANES, LANES), lambda b, g: (b, g, 0)),
        out_shape=jax.ShapeDtypeStruct((n_batch * a_n, 2 * D_BR, LANES), F32),
        compiler_params=_params(("arbitrary", "arbitrary")),
        name="fnet_channel_dft",
    )(z_s, w)


def _fn_seq_kernel(p_ref, q_ref, fa_ref, tr_ref, ti_ref, g_ref, o_ref, a_buf, *, a_n, scale):
    fa, tr, ti = fa_ref[...].astype(BF16), tr_ref[...], ti_ref[...]
    for c in range(CB):
        r1 = jnp.dot(fa, _chan_load(p_ref, c).astype(BF16), preferred_element_type=F32)
        r2 = jnp.dot(fa, _chan_load(q_ref, c).astype(BF16), preferred_element_type=F32)
        ar = r1[:a_n] - r2[a_n:]
        ai = -(r2[:a_n] + r1[a_n:])
        a_buf[c] = jnp.concatenate([ar * tr - ai * ti, ar * ti + ai * tr], axis=1)
    y = _bdot(a_buf[...].reshape(CB * a_n, 2 * LANES), g_ref[...]) * scale
    o_ref[...] = y.reshape(CB, a_n, LANES)


def _fn_seq(pq, *, a_n, n_batch):
    seq = a_n * LANES
    k = np.arange(a_n)
    ang = 2 * np.pi * (k[:, None] * k[None, :] % a_n) / a_n
    fa = np.concatenate([np.cos(ang), np.sin(ang)], axis=0)
    r = np.arange(LANES)
    ang_t = 2 * np.pi * (k[:, None] * r[None, :] % seq) / seq
    c2, s2 = _chan_dft_mats()
    g = np.concatenate([c2, s2], axis=0)
    nblk = LANES // CB
    const = lambda shp: pl.BlockSpec(shp, lambda b, j: (0,) * len(shp))

    def chan_blk(j, off):
        return (j // nblk) * (2 * nblk) + off * nblk + j % nblk

    return pl.pallas_call(
        functools.partial(_fn_seq_kernel, a_n=a_n, scale=1.0 / math.sqrt(seq * LANES)),
        grid=(n_batch, D_BR // CB),
        in_specs=[pl.BlockSpec((a_n, CB, LANES), lambda b, j: (b, chan_blk(j, 0), 0)),
                  pl.BlockSpec((a_n, CB, LANES), lambda b, j: (b, chan_blk(j, 1), 0)),
                  const((2 * a_n, a_n)), const((a_n, LANES)), const((a_n, LANES)), const((2 * LANES, LANES))],
        out_specs=pl.BlockSpec((None, CB, a_n, LANES), lambda b, j: (b, j, 0, 0)),
        out_shape=jax.ShapeDtypeStruct((n_batch, D_BR, a_n, LANES), F32),
        scratch_shapes=[pltpu.VMEM((CB, a_n, 2 * LANES), F32)],
        compiler_params=_params(("arbitrary", "arbitrary")),
        name="fnet_sequence_dft",
    )(pq, pq, jnp.asarray(fa, F32), jnp.asarray(np.cos(ang_t), F32), jnp.asarray(-np.sin(ang_t), F32),
      jnp.asarray(g, F32))


def _fn_small_kernel(u_ref, cw_ref, sw_ref, cl_ref, sl_ref, o_ref, *, a_n, scale):
    u = jnp.concatenate([u_ref[a].T for a in range(a_n)], axis=0)
    p = _bdot(u, cw_ref[...])
    q = _bdot(u, sw_ref[...])
    o_ref[...] = (_bdot(cl_ref[...], p) - _bdot(sl_ref[...], q)) * scale


def _fn_small(z_s, *, a_n, n_batch, slab0, chan_lo):
    seq = a_n * LANES
    cw, sw = _chan_dft_mats()
    n = np.arange(seq)
    ang = 2 * np.pi * (n[:, None] * n[None, :] % seq) / seq
    const = lambda shp: pl.BlockSpec(shp, lambda b, g: (0,) * len(shp))
    g0 = chan_lo // LANES
    return pl.pallas_call(
        functools.partial(_fn_small_kernel, a_n=a_n, scale=1.0 / math.sqrt(seq * LANES)),
        grid=(n_batch, FN_GROUPS),
        in_specs=[pl.BlockSpec((a_n, LANES, LANES), lambda b, g: (slab0 // a_n + b, g0 + g, 0)),
                  const((LANES, LANES)), const((LANES, LANES)), const((seq, seq)), const((seq, seq))],
        out_specs=pl.BlockSpec((None, seq, LANES), lambda b, g: (b, 0, g)),
        out_shape=jax.ShapeDtypeStruct((n_batch, seq, D_BR), F32),
        compiler_params=_params(("arbitrary", "arbitrary")),
        name="fnet_short",
    )(z_s, jnp.asarray(cw, F32), jnp.asarray(sw, F32), jnp.asarray(np.cos(ang), F32), jnp.asarray(np.sin(ang), F32))


def _log_sigmoid(x):
    return jnp.minimum(x, 0.0) - jnp.log(1.0 + jnp.exp(-jnp.abs(x)))


def _exact_tri_dot(tri, x, tri_on_left):
    h, m, l = _split3(x)
    if tri_on_left:
        d = lambda p: jnp.dot(tri, p, preferred_element_type=F32)
    else:
        d = lambda p: jnp.dot(p, tri, preferred_element_type=F32)
    return d(h) + d(m) + d(l)


def _mlstm_step(inputs, c_st, n_st, m_st):
    t = hd = LANES
    n_dir, n_batch = len(inputs), len(inputs[0])
    n_grp = n_dir * n_batch * ML_HEADS
    row = lax.broadcasted_iota(jnp.int32, (t, t), 0)
    col = lax.broadcasted_iota(jnp.int32, (t, t), 1)
    tri = jnp.where(col <= row, 1.0, 0.0).astype(BF16)
    tri_t = jnp.where(col >= row, 1.0, 0.0).astype(BF16)
    qs, ks, vs, bcs, brs, ics, irs = [], [], [], [], [], [], []
    for d in range(n_dir):
        i_off = 2 * ML_HEADS * d
        f_off = i_off + ML_HEADS
        for b in range(n_batch):
            q_all, k_all, v_all, g = inputs[d][b]
            gt = g.T
            lf_c = _log_sigmoid(g)
            lf_r = lf_c.T
            if d == 1:
                b_c = _exact_tri_dot(tri_t, lf_c, True)
                b_r = _exact_tri_dot(tri, lf_r, False)
            else:
                b_c = _exact_tri_dot(tri, lf_c, True)
                b_r = _exact_tri_dot(tri_t, lf_r, False)
            for h in range(ML_HEADS):
                sl = slice(h * hd, (h + 1) * hd)
                qs.append(q_all[:, sl])
                ks.append(k_all[:, sl])
                vs.append(v_all[:, sl])
                bcs.append(b_c[:, f_off + h:f_off + h + 1])
                brs.append(b_r[f_off + h:f_off + h + 1, :])
                ics.append(g[:, i_off + h:i_off + h + 1])
                irs.append(gt[i_off + h:i_off + h + 1, :])
    q = jnp.stack(qs) * (hd ** -0.5)
    k, v = jnp.stack(ks), jnp.stack(vs)
    bc, br, ic, ir = jnp.stack(bcs), jnp.stack(brs), jnp.stack(ics), jnp.stack(irs)
    m_prev = m_st[...][:, :, :1]
    ct = c_st[...]
    n_prev = n_st[...]

    shp = (n_grp, t, t)
    grp = lax.broadcasted_iota(jnp.int32, shp, 0)
    r3, c3 = lax.broadcasted_iota(jnp.int32, shp, 1), lax.broadcasted_iota(jnp.int32, shp, 2)
    back = grp >= (n_grp // n_dir)
    mask = (back & (c3 >= r3)) | (jnp.logical_not(back) & (c3 <= r3))
    bdot = lambda a, b_, ca, cb: lax.dot_general(a.astype(BF16), b_.astype(BF16), (((ca,), (cb,)), ((0,), (0,))),
                                                 preferred_element_type=F32)
    dm = jnp.where(mask, bc - br + ir, -jnp.inf)
    inter = bc + m_prev
    m_row = jnp.maximum(inter, jnp.max(dm, axis=-1, keepdims=True))
    w_intra = jnp.exp(dm - m_row)
    w_inter = jnp.exp(inter - m_row)
    s = bdot(q, k, 2, 2) * w_intra
    num = bdot(s, v, 2, 1) + w_inter * bdot(q, ct, 2, 1)
    den = jnp.sum(s, axis=-1, keepdims=True) + w_inter * jnp.sum(q * n_prev, axis=-1, keepdims=True)
    den = jnp.maximum(jnp.abs(den), jnp.exp(-m_row))
    h_all = num / den

    is_back = lax.broadcasted_iota(jnp.int32, (n_grp, 1, 1), 0) >= (n_grp // n_dir)
    b_tot_c = jnp.where(is_back, bc[:, :1, :], bc[:, t - 1:, :])
    b_tot_r = jnp.where(is_back, br[:, :, :1], br[:, :, t - 1:])
    a_c = b_tot_c - bc + ic
    a_r = b_tot_r - br + ir
    m_new = jnp.maximum(b_tot_c + m_prev, jnp.max(a_r, axis=-1, keepdims=True))
    sc = jnp.exp(a_c - m_new)
    decay = jnp.exp(b_tot_c + m_prev - m_new)
    k_sc = k * sc
    c_st[...] = decay * ct + bdot(k_sc, v, 1, 1)
    n_st[...] = decay * n_prev + jnp.sum(k_sc, axis=1, keepdims=True)
    m_st[...] = jnp.broadcast_to(m_new, (n_grp, 1, LANES))

    out = []
    for d in range(n_dir):
        out.append([jnp.concatenate([h_all[(d * n_batch + b) * ML_HEADS + h] for h in range(ML_HEADS)], axis=1)
                    for b in range(n_batch)])
    return out


def _mlstm_kernel(*refs, n_batch, ctx_chunks):
    lat_f, lat_b, ctx_f, ctx_b = refs[0:4], refs[4:8], refs[8:12], refs[12:16]
    hf_lat, hb_lat, hf_ctx, hb_ctx, c_st, n_st, m_st = refs[16:]
    j = pl.program_id(0)
    is_ctx = j < ctx_chunks

    @pl.when(j == 0)
    def _():
        c_st[...] = jnp.zeros(c_st.shape, F32)
        n_st[...] = jnp.zeros(n_st.shape, F32)
        m_st[...] = jnp.zeros(m_st.shape, F32)

    pick = lambda c_refs, l_refs, b: tuple(jnp.where(is_ctx, c[b], l[b]) for c, l in zip(c_refs, l_refs))
    inputs = [[pick(ctx_f, lat_f, b) for b in range(n_batch)], [pick(ctx_b, lat_b, b) for b in range(n_batch)]]
    hf, hb = _mlstm_step(inputs, c_st, n_st, m_st)

    @pl.when(is_ctx)
    def _():
        for b in range(n_batch):
            hf_ctx[b] = hf[b]
            hb_ctx[b] = hb[b]

    @pl.when(jnp.logical_not(is_ctx))
    def _():
        for b in range(n_batch):
            hf_lat[b] = hf[b]
            hb_lat[b] = hb[b]


def _mlstm(qk_lat, z_lat, qk_ctx, z_ctx, *, n_batch, v_col, g_col):
    lat_len, ctx_len = qk_lat.shape[0] // n_batch, qk_ctx.shape[0] // n_batch
    nlc, ncc = lat_len // LANES, ctx_len // LANES
    r3 = lambda a: a.reshape(n_batch, a.shape[0] // n_batch, a.shape[1])
    lf = lambda j: jnp.maximum(j - ncc, 0)
    lb = lambda j: jnp.where(j < ncc, nlc - 1, nlc - 1 - (j - ncc))
    cf = lambda j: jnp.minimum(j, ncc - 1)
    cb = lambda j: jnp.where(j < ncc, ncc - 1 - j, 0)

    def specs(ix):
        blk = lambda w, cidx: pl.BlockSpec((n_batch, LANES, w), lambda j: (0, ix(j), cidx))
        return [blk(D_BR, 0), blk(D_BR, 1), blk(D_BR, v_col), blk(LANES, g_col)]

    out = lambda ix: pl.BlockSpec((n_batch, LANES, D_BR), lambda j: (0, ix(j), 0))
    sd = lambda n: jax.ShapeDtypeStruct((n_batch, n, D_BR), F32)
    ql, zl, qc, zc = r3(qk_lat), r3(z_lat), r3(qk_ctx), r3(z_ctx)
    hf_lat, hb_lat, hf_ctx, hb_ctx = pl.pallas_call(
        functools.partial(_mlstm_kernel, n_batch=n_batch, ctx_chunks=ncc),
        grid=(ncc + nlc,),
        in_specs=specs(lf) + specs(lb) + specs(cf) + specs(cb),
        out_specs=[out(lf), out(lb), out(cf), out(cb)],
        out_shape=[sd(lat_len), sd(lat_len), sd(ctx_len), sd(ctx_len)],
        scratch_shapes=[pltpu.VMEM((2 * n_batch * ML_HEADS, LANES, LANES), F32),
                        pltpu.VMEM((2 * n_batch * ML_HEADS, 1, LANES), F32),
                        pltpu.VMEM((2 * n_batch * ML_HEADS, 1, LANES), F32)],
        compiler_params=_params(("arbitrary",)),
        name="mlstm_bidir",
    )(ql, ql, zl, zl, ql, ql, zl, zl, qc, qc, zc, zc, qc, qc, zc, zc)
    flat = lambda a: a.reshape(a.shape[0] * a.shape[1], D_BR)
    return (flat(hf_lat), flat(hb_lat)), (flat(hf_ctx), flat(hb_ctx))


def _rms_mod(x, w, shift, scale):
    y = x * lax.rsqrt(jnp.mean(x * x, axis=-1, keepdims=True) + EPS) * w
    return y * (1.0 + scale) + shift


def _route(t, rw, rb):
    logits = _dot_f32ish(t, rw) + rb
    col = lax.broadcasted_iota(jnp.int32, logits.shape, 1)
    big = jnp.int32(1 << 20)
    ninf = -jnp.inf
    is_g = col < MOE_GROUPS
    gl = jnp.where(is_g, logits, ninf)
    gmax = jnp.max(gl, axis=-1, keepdims=True)
    g_sel = jnp.min(jnp.where(is_g & (gl == gmax), col, big), axis=-1, keepdims=True)
    p_top = 1.0 / jnp.sum(jnp.where(is_g, jnp.exp(gl - gmax), 0.0), axis=-1, keepdims=True)
    lo = MOE_GROUPS + g_sel * MOE_PER_GROUP
    in_grp = (col >= lo) & (col < lo + MOE_PER_GROUP)
    e1v = jnp.where(in_grp, logits, ninf)
    top1 = jnp.max(e1v, axis=-1, keepdims=True)
    idx1 = jnp.min(jnp.where(in_grp & (e1v == top1), col, big), axis=-1, keepdims=True)
    e2v = jnp.where(col == idx1, ninf, e1v)
    top2 = jnp.max(e2v, axis=-1, keepdims=True)
    idx2 = jnp.min(jnp.where(in_grp & (col != idx1) & (e2v == top2), col, big), axis=-1, keepdims=True)
    ex = jnp.exp(top2 - top1)
    s1 = 1.0 / (1.0 + ex)
    return jnp.where(col == idx1, p_top * s1, 0.0) + jnp.where(col == idx2, p_top * (ex * s1), 0.0)


def _merge_kernel(yh_ref, yf_ref, hf_ref, hb_ref, o_ref, g0_ref, g1_ref, g2_ref, x_ref, gate_ref,
                  wb_ref, wo_ref, nw_ref, n2_ref, sh_ref, sc_ref, rw_ref, rb_ref, out_ref, xn_ref, comb_ref):
    hd = LANES
    h = hf_ref[...] + hb_ref[...]
    parts = []
    for i in range(ML_HEADS):
        hh = h[:, i * hd:(i + 1) * hd]
        parts.append(hh * lax.rsqrt(jnp.mean(hh * hh, axis=-1, keepdims=True) + EPS))
    y_ml = jax.nn.sigmoid(o_ref[...]) * (jnp.concatenate(parts, axis=1) * nw_ref[...])
    acc = jax.nn.sigmoid(g0_ref[...]) * _bdot(yh_ref[...], wb_ref[0])
    acc = acc + jax.nn.sigmoid(g1_ref[...]) * _bdot(yf_ref[...], wb_ref[1])
    acc = acc + jax.nn.sigmoid(g2_ref[...]) * _bdot(y_ml, wb_ref[2])
    x_new = x_ref[...] + gate_ref[...] * _bdot(acc, wo_ref[...])
    out_ref[...] = x_new
    t = _rms_mod(x_new, n2_ref[...], sh_ref[...], sc_ref[...])
    xn_ref[...] = t.astype(BF16)
    comb_ref[...] = _route(t, rw_ref[...], rb_ref[...])


def _merge(yh, yf, hf, hb, z_tm, x, mods3, wb, wo, nw, n2w, rw, rb, *, seg, tm, o_col, gate_col0):
    nt, d = x.shape
    tok = lambda w, cidx: pl.BlockSpec((tm, w), lambda i: (i, cidx))
    mod = lambda k: pl.BlockSpec((None, 1, d), lambda i: (seg(i), 0, k))
    return pl.pallas_call(
        _merge_kernel,
        grid=(nt // tm,),
        in_specs=[tok(D_BR, 0), tok(D_BR, 0), tok(D_BR, 0), tok(D_BR, 0),
                  tok(D_BR, o_col),
                  tok(d, gate_col0), tok(d, gate_col0 + 1), tok(d, gate_col0 + 2),
                  tok(d, 0),
                  mod(2),
                  pl.BlockSpec((3, D_BR, d), lambda i: (0, 0, 0)),
                  pl.BlockSpec((d, d), lambda i: (0, 0)),
                  pl.BlockSpec((1, D_BR), lambda i: (0, 0)),
                  pl.BlockSpec((1, d), lambda i: (0, 0)),
                  mod(3), mod(4),
                  pl.BlockSpec((d, LANES), lambda i: (0, 0)),
                  pl.BlockSpec((1, LANES), lambda i: (0, 0))],
        out_specs=[tok(d, 0), tok(d, 0), tok(LANES, 0)],
        out_shape=[jax.ShapeDtypeStruct((nt, d), F32), jax.ShapeDtypeStruct((nt, d), BF16),
                   jax.ShapeDtypeStruct((nt, LANES), F32)],
        compiler_params=_params(("arbitrary",)),
        name="merge_branches_router",
    )(yh, yf, hf, hb, z_tm, z_tm, z_tm, z_tm, x, mods3, wb, wo, nw.reshape(1, D_BR), n2w.reshape(1, d),
      mods3, mods3, rw, rb)


def _moe_kernel(xn_ref, comb_ref, wg_ref, wu_ref, wd_ref, x_ref, gate_ref, nw_ref, sh_ref, sc_ref, *out_and_scratch,
                final, n_keep):
    acc_ref = out_and_scratch[-1]
    e = pl.program_id(1)

    @pl.when(e == 0)
    def _():
        acc_ref[...] = jnp.zeros(acc_ref.shape, F32)

    xn = xn_ref[...]
    comb = comb_ref[...]
    col = lax.broadcasted_iota(jnp.int32, comb.shape, 1)
    cw = jnp.sum(jnp.where(col == e + MOE_GROUPS, comb, 0.0), axis=-1, keepdims=True)
    hg = jnp.dot(xn, wg_ref[...], preferred_element_type=F32)
    hu = jnp.dot(xn, wu_ref[...], preferred_element_type=F32)
    a = (hg * jax.nn.sigmoid(hg)) * hu * cw
    acc_ref[...] += jnp.dot(a.astype(BF16), wd_ref[...], preferred_element_type=F32)

    if final:
        y_ref, = out_and_scratch[:-1]

        @pl.when((e == MOE_EXPERTS - 1) & (pl.program_id(0) < n_keep))
        def _():
            x_new = x_ref[...] + gate_ref[...] * acc_ref[...]
            y_ref[...] = x_new * lax.rsqrt(jnp.mean(x_new * x_new, axis=-1, keepdims=True) + EPS) * nw_ref[...]
    else:
        o_ref, xn_next_ref = out_and_scratch[:-1]

        @pl.when(e == MOE_EXPERTS - 1)
        def _():
            x_new = x_ref[...] + gate_ref[...] * acc_ref[...]
            o_ref[...] = x_new
            xn_next_ref[...] = _rms_mod(x_new, nw_ref[...], sh_ref[...], sc_ref[...]).astype(BF16)


def _moe(xn, comb, wg, wu, wd, x, mods3, post_w, post_mods3, *, seg, tm, final, n_keep_rows):
    nt, d = x.shape
    n_keep = n_keep_rows // tm
    tok = pl.BlockSpec((tm, d), lambda i, e: (i, 0))
    if final:
        out_specs = [pl.BlockSpec((tm, d), lambda i, e: (jnp.minimum(i, n_keep - 1), 0))]
        out_shape = [jax.ShapeDtypeStruct((n_keep_rows, d), F32)]
    else:
        out_specs = [tok, tok]
        out_shape = [jax.ShapeDtypeStruct((nt, d), F32), jax.ShapeDtypeStruct((nt, d), BF16)]
    mod = lambda k: pl.BlockSpec((None, 1, d), lambda i, e: (seg(i), 0, k))
    return pl.pallas_call(
        functools.partial(_moe_kernel, final=final, n_keep=n_keep),
        grid=(nt // tm, MOE_EXPERTS),
        in_specs=[tok,
                  pl.BlockSpec((tm, LANES), lambda i, e: (i, 0)),
                  pl.BlockSpec((None, d, EXPERT_HID), lambda i, e: (e, 0, 0)),
                  pl.BlockSpec((None, d, EXPERT_HID), lambda i, e: (e, 0, 0)),
                  pl.BlockSpec((None, EXPERT_HID, d), lambda i, e: (e, 0, 0)),
                  tok,
                  mod(5),
                  pl.BlockSpec((1, d), lambda i, e: (0, 0)),
                  mod(0), mod(1)],
        out_specs=out_specs,
        out_shape=out_shape,
        scratch_shapes=[pltpu.VMEM((tm, d), F32)],
        compiler_params=_params(("arbitrary", "arbitrary")),
        name="moe_experts_final" if final else "moe_experts",
    )(xn, comb, wg, wu, wd, x, mods3, post_w.reshape(1, d), post_mods3, post_mods3)


def _slab_to_tm(y_s):
    ns, c, _ = y_s.shape
    return jnp.transpose(y_s, (0, 2, 1)).reshape(ns * LANES, c)


def kernel(x, c, ctx, c_ctx, ada_w, ada_b, norm1_w, norm2_w, w_in, b_in, hy_conv_w, hy_conv_b, hy_f_w1, hy_f_b1, hy_f_w2, hy_f_b2, hy_f_w3, hy_f_freq, hy_decay, hy_skip, ml_conv_w, ml_conv_b, ml_norm_w, w_branch, w_out, moe_rg_w, moe_rg_b, moe_re_w, moe_re_b, moe_w_gate, moe_w_up, moe_w_down, norm_f_w):
    nb, seq, d = x.shape
    lc = ctx.shape[1]
    depth = ada_w.shape[0]
    assert d == D_MODEL and seq % (GRID_W * 2) == 0 and lc % LANES == 0 and nb + 1 <= 8
    rows = seq // GRID_W
    a_lat = seq // LANES
    a_ctx = lc // LANES
    n_lat, n_ctx = nb * seq, nb * lc
    tm = 256
    tm_moe = {"lat": _pick(seq, (1024, 512, 256)), "ctx": _pick(n_ctx, (512, 256))}
    assert seq % tm == 0 and n_ctx % tm == 0
    seg_of = lambda s, t: (lambda i: i // (seq // t)) if s == "lat" else (lambda i: nb)
    streams = ("lat", "ctx")
    xs = {"lat": x.reshape(n_lat, d), "ctx": ctx.reshape(n_ctx, d)}
    xn = {}
    cvec = jnp.zeros((8, d), F32).at[:nb].set(c).at[nb].set(c_ctx)
    mods = _mods(cvec, ada_w, ada_b)

    o_fn, o_ml, o_mlg, o_gate = 3 * D_BR, 4 * D_BR, 8 * D_BR, 8 * D_BR + 4 * ML_HEADS
    pad_g = LANES - 4 * ML_HEADS

    for l in range(depth):
        lp = {"hy_f_w1": hy_f_w1[l], "hy_f_b1": hy_f_b1[l], "hy_f_w2": hy_f_w2[l], "hy_f_b2": hy_f_b2[l],
              "hy_f_w3": hy_f_w3[l], "hy_f_freq": hy_f_freq[l], "hy_decay": hy_decay[l]}
        mods3 = mods[l].reshape(8, 1, 6 * d)
        wl, bl = w_in[l], b_in[l]
        w_cm = jnp.concatenate([wl[:, :o_fn], wl[:, o_ml:o_ml + 2 * D_BR], wl[:, o_fn:o_ml]], axis=1)
        b_cm = jnp.concatenate([bl[:o_fn], bl[o_ml:o_ml + 2 * D_BR], bl[o_fn:o_ml]])
        w_tm = jnp.concatenate([wl[:, o_ml + 2 * D_BR:o_mlg], wl[:, o_gate:], wl[:, o_mlg:o_gate],
                                jnp.zeros((d, pad_g), F32)], axis=1)
        b_tm = jnp.concatenate([bl[o_ml + 2 * D_BR:o_mlg], bl[o_gate:], bl[o_mlg:o_gate], jnp.zeros((pad_g,), F32)])
        c_hy, c_qk, c_fn = 0, 3 * D_BR, 5 * D_BR
        g_col = (2 * D_BR + 3 * d) // LANES

        last = l + 1 == depth
        live = ("lat",) if last else streams
        if l == 0:
            xn = {s: _norm_mod(xs[s], norm1_w[l], mods3, 0, 1, seg_of(s, tm), tm) for s in streams}
        w_tm_b, w_cm_t = w_tm.astype(BF16), w_cm.T.astype(BF16)
        z_tm = {s: _mm_tm(xn[s], w_tm_b, b_tm) for s in streams}
        z_s = {s: _mm_slab(xn[s], w_cm_t, b_cm) for s in streams}

        grid_kw = {"lat": dict(rows=rows, width=GRID_W), "ctx": dict(rows=1, width=lc)}
        hy_w, hy_b = hy_conv_w[l].reshape(9, 3 * D_BR), hy_conv_b[l]
        ml_w, ml_b = ml_conv_w[l].reshape(9, 2 * D_BR), ml_conv_b[l]
        conv = lambda s, w, b, lo, n, act: _conv(z_s[s], w, b, chan_lo=lo, chan_n=n, silu=act, n_batch=nb,
                                                 slab0=0, **grid_kw[s])
        u = {s: conv(s, hy_w, hy_b, c_hy, 3 * D_BR, False) for s in live}
        qk = {s: _slab_to_tm(conv(s, ml_w, ml_b, c_qk, 2 * D_BR, True)) for s in streams}

        (hf_lat, hb_lat), (hf_ctx, hb_ctx) = _mlstm(qk["lat"], z_tm["lat"], qk["ctx"], z_tm["ctx"],
                                                    n_batch=nb, v_col=0, g_col=g_col)
        h_f, h_b = {"lat": hf_lat, "ctx": hf_ctx}, {"lat": hb_lat, "ctx": hb_ctx}

        yh, yf = {}, {}
        yh["lat"] = _slab_to_tm(_hyena(u["lat"], _hyena_taps(seq, 2 * a_lat, lp), hy_skip[l],
                                       a_in=a_lat, na=2 * a_lat, n_batch=nb))
        pq = _fn_mix(z_s["lat"], a_n=a_lat, n_batch=nb, chan_lo=c_fn)
        yk = _fn_seq(pq, a_n=a_lat, n_batch=nb)
        yf["lat"] = jnp.transpose(yk, (0, 3, 2, 1)).reshape(n_lat, D_BR)
        if not last:
            yh["ctx"] = _hyena_short(u["ctx"], _hyena_taps(lc, 2 * a_ctx, lp), hy_skip[l],
                                     a_n=a_ctx, n_batch=nb).reshape(n_ctx, D_BR)
            yf["ctx"] = _fn_small(z_s["ctx"], a_n=a_ctx, n_batch=nb, slab0=0, chan_lo=c_fn).reshape(n_ctx, D_BR)

        rw = jnp.concatenate([moe_rg_w[l], moe_re_w[l], jnp.zeros((d, LANES - MOE_GROUPS - MOE_EXPERTS), F32)], axis=1)
        rb = jnp.concatenate([moe_rg_b[l], moe_re_b[l], jnp.zeros((LANES - MOE_GROUPS - MOE_EXPERTS,), F32)]).reshape(1, LANES)
        wb, wo = w_branch[l].astype(BF16), w_out[l].astype(BF16)
        experts = (moe_w_gate[l].astype(BF16), moe_w_up[l].astype(BF16), moe_w_down[l].astype(BF16))
        for s in live:
            xs[s], xn2, comb = _merge(yh[s], yf[s], h_f[s], h_b[s], z_tm[s], xs[s], mods3, wb, wo, ml_norm_w[l],
                                      norm2_w[l], rw, rb, seg=seg_of(s, tm), tm=tm, o_col=1, gate_col0=1)
            moe_kw = dict(seg=seg_of(s, tm_moe[s]), tm=tm_moe[s], n_keep_rows=xs[s].shape[0])
            if last:
                out, = _moe(xn2, comb, *experts, xs[s], mods3, norm_f_w, mods3, final=True, **moe_kw)
            else:
                xs[s], xn[s] = _moe(xn2, comb, *experts, xs[s], mods3, norm1_w[l + 1],
                                    mods[l + 1].reshape(8, 1, 6 * d), final=False, **moe_kw)

    return out.reshape(nb, seq, d)
```

```python
import functools
import math

import numpy as np
import jax
import jax.numpy as jnp
from jax import lax
from jax.experimental import pallas as pl
from jax.experimental.pallas import tpu as pltpu

F32 = jnp.float32
BF16 = jnp.bfloat16

D_MODEL = 1024
D_BR = 512
GRID_W = 64
LANES = 128
CB = 8
HY_ORDER = 2
HY_BANDS = 16
FN_GROUPS = 4
ML_HEADS = 4
MOE_GROUPS = 4
MOE_PER_GROUP = 4
MOE_EXPERTS = 16
EXPERT_HID = 256
EPS = 1e-6
VMEM_LIMIT = 56 * 1024 * 1024


def _params(sem):
    return pltpu.CompilerParams(dimension_semantics=sem, vmem_limit_bytes=VMEM_LIMIT)


def _bdot(a, b):
    return jnp.dot(a.astype(BF16), b.astype(BF16), preferred_element_type=F32)


def _split3(x):
    hi = x.astype(BF16)
    r1 = x - hi.astype(F32)
    mid = r1.astype(BF16)
    lo = (r1 - mid.astype(F32)).astype(BF16)
    return hi, mid, lo


def _dot_f32ish(x, w):
    xh, xm, xl = _split3(x)
    wh, wm, wl = _split3(w)
    d = lambda a, b: jnp.dot(a, b, preferred_element_type=F32)
    return (d(xh, wh) + (d(xh, wm) + d(xm, wh))) + (d(xm, wm) + d(xh, wl) + d(xl, wh))


def _swap_halves(x):
    return jnp.concatenate([x[..., LANES:], x[..., :LANES]], axis=-1)


def _mods_kernel(c_ref, w_ref, b_ref, o_ref):
    c = c_ref[...]
    s = c * jax.nn.sigmoid(c)
    o_ref[...] = _dot_f32ish(s, w_ref[...]) + b_ref[...]


def _mods(cvec, ada_w, ada_b):
    depth, d, n6 = ada_w.shape
    tn = 1536
    return pl.pallas_call(
        _mods_kernel,
        grid=(depth, n6 // tn),
        in_specs=[pl.BlockSpec((8, d), lambda l, j: (0, 0)),
                  pl.BlockSpec((None, d, tn), lambda l, j: (l, 0, j)),
                  pl.BlockSpec((None, 1, tn), lambda l, j: (l, 0, j))],
        out_specs=pl.BlockSpec((None, 8, tn), lambda l, j: (l, 0, j)),
        out_shape=jax.ShapeDtypeStruct((depth, 8, n6), F32),
        compiler_params=_params(("arbitrary", "arbitrary")),
        name="adaln_mods",
    )(cvec, ada_w, ada_b.reshape(depth, 1, n6))


def _norm_mod_kernel(x_ref, w_ref, sh_ref, sc_ref, o_ref):
    x = x_ref[...]
    y = x * lax.rsqrt(jnp.mean(x * x, axis=-1, keepdims=True) + EPS) * w_ref[...]
    o_ref[...] = (y * (1.0 + sc_ref[...]) + sh_ref[...]).astype(o_ref.dtype)


def _norm_mod(x, w, mods3, col_shift, col_scale, seg, tm):
    nt, d = x.shape
    return pl.pallas_call(
        _norm_mod_kernel,
        grid=(nt // tm,),
        in_specs=[pl.BlockSpec((tm, d), lambda i: (i, 0)),
                  pl.BlockSpec((1, d), lambda i: (0, 0)),
                  pl.BlockSpec((None, 1, d), lambda i: (seg(i), 0, col_shift)),
                  pl.BlockSpec((None, 1, d), lambda i: (seg(i), 0, col_scale))],
        out_specs=pl.BlockSpec((tm, d), lambda i: (i, 0)),
        out_shape=jax.ShapeDtypeStruct((nt, d), BF16),
        compiler_params=_params(("arbitrary",)),
        name="norm_mod",
    )(x, w.reshape(1, d), mods3, mods3)


def _mm_tm_kernel(x_ref, w_ref, b_ref, o_ref, *, gate):
    y = jnp.dot(x_ref[...], w_ref[...], preferred_element_type=F32) + b_ref[...]
    o_ref[...] = (jax.nn.sigmoid(y) if gate else y).astype(o_ref.dtype)


def _pick(n, cands):
    for c in cands:
        if n % c == 0:
            return c
    raise ValueError(f"no tile for {n} in {cands}")


def _mm_tm(xn, w, b, *, gate):
    nt, k = xn.shape
    n = w.shape[1]
    tm = _pick(nt, (1056, 1024, 768, 512, 256))
    tn = _pick(n, (1536, 1408, 1152, 1024, 512, 384, 256, 128))
    return pl.pallas_call(
        functools.partial(_mm_tm_kernel, gate=gate),
        grid=(n // tn, nt // tm),
        in_specs=[pl.BlockSpec((tm, k), lambda j, i: (i, 0)),
                  pl.BlockSpec((k, tn), lambda j, i: (0, j)),
                  pl.BlockSpec((1, tn), lambda j, i: (0, j))],
        out_specs=pl.BlockSpec((tm, tn), lambda j, i: (i, j)),
        out_shape=jax.ShapeDtypeStruct((nt, n), BF16 if gate else F32),
        compiler_params=_params(("arbitrary", "arbitrary")),
        name="inproj_gates" if gate else "inproj_token_major",
    )(xn, w, b.reshape(1, n))


def _mm_slab_kernel(w_ref, x_ref, b_ref, o_ref, *, slabs):
    w = w_ref[...]
    b = b_ref[...]
    step = 2 if slabs % 2 == 0 else 1
    for s in range(0, slabs, step):
        xs = x_ref[s * LANES:(s + step) * LANES, :]
        y = lax.dot_general(w, xs, (((1,), (1,)), ((), ())), preferred_element_type=F32) + b
        for i in range(step):
            o_ref[s + i] = y[:, i * LANES:(i + 1) * LANES]


def _mm_slab(xn, wt, b):
    nt, k = xn.shape
    c = wt.shape[0]
    ns = nt // LANES
    ts = _pick(ns, (12, 11, 8, 6, 4, 3, 2, 1))
    tc = _pick(c, (512, 256, 128))
    return pl.pallas_call(
        functools.partial(_mm_slab_kernel, slabs=ts),
        grid=(c // tc, ns // ts),
        in_specs=[pl.BlockSpec((tc, k), lambda j, i: (j, 0)),
                  pl.BlockSpec((ts * LANES, k), lambda j, i: (i, 0)),
                  pl.BlockSpec((tc, 1), lambda j, i: (j, 0))],
        out_specs=pl.BlockSpec((ts, tc, LANES), lambda j, i: (i, j, 0)),
        out_shape=jax.ShapeDtypeStruct((ns, c, LANES), F32),
        compiler_params=_params(("arbitrary", "arbitrary")),
        name="inproj_slab",
    )(wt, xn, b.reshape(c, 1))


def _conv_taps(rows, width):
    taps = []
    for dr in (-1, 0, 1):
        if rows == 1 and dr != 0:
            continue
        for dw in (-1, 0, 1):
            taps.append((dr, dw))
    return taps


def _silu(x):
    return x * jax.nn.sigmoid(x)


def _conv_kernel(x_ref, w_ref, b_ref, o_ref, *, taps, width, n_slabs, silu):
    ct = x_ref.shape[1]
    lane = lax.broadcasted_iota(jnp.int32, (ct, LANES), 1)
    bias = jnp.zeros((ct, LANES), F32) + b_ref[...]
    planes = []
    for t, (dr, dw) in enumerate(taps):
        w = w_ref[t]
        if width < LANES and dw != 0:
            col = lane % width + dw
            w = jnp.where((col >= 0) & (col < width), w, 0.0)
        planes.append((dr * width + dw, w))

    def body(a, carry):
        x0 = x_ref[a]
        xm = jnp.where(a > 0, x_ref[jnp.maximum(a - 1, 0)], 0.0)
        xp = jnp.where(a < n_slabs - 1, x_ref[jnp.minimum(a + 1, n_slabs - 1)], 0.0)
        acc = bias
        for delta, w in planes:
            if delta == 0:
                src = x0
            elif delta > 0:
                src = pltpu.roll(jnp.where(lane >= delta, x0, xp), LANES - delta, 1)
            else:
                src = pltpu.roll(jnp.where(lane < LANES + delta, x0, xm), -delta, 1)
            acc = acc + src * w
        o_ref[a] = _silu(acc) if silu else acc
        return carry

    lax.fori_loop(0, n_slabs, body, 0, unroll=2 if n_slabs % 2 == 0 else 1)


def _conv_grid_kernel(x_ref, w_ref, b_ref, o_ref, ym_ref, yp_ref, *, width, n_slabs, silu):
    ct = x_ref.shape[1]
    lane = lax.broadcasted_iota(jnp.int32, (ct, LANES), 1)
    col = lane % width
    bias = jnp.zeros((ct, LANES), F32) + b_ref[...]

    def row_sums(a, carry):
        x0 = x_ref[a]
        xl = jnp.where(col >= 1, pltpu.roll(x0, 1, 1), 0.0)
        xr = jnp.where(col < width - 1, pltpu.roll(x0, LANES - 1, 1), 0.0)
        ym_ref[a] = w_ref[0] * xl + w_ref[1] * x0 + w_ref[2] * xr
        o_ref[a] = bias + w_ref[3] * xl + w_ref[4] * x0 + w_ref[5] * xr
        yp_ref[a] = w_ref[6] * xl + w_ref[7] * x0 + w_ref[8] * xr
        return carry

    def combine(a, carry):
        up = jnp.where(a > 0, ym_ref[jnp.maximum(a - 1, 0)], 0.0)
        dn = jnp.where(a < n_slabs - 1, yp_ref[jnp.minimum(a + 1, n_slabs - 1)], 0.0)
        from_up = jnp.where(lane < LANES - width, ym_ref[a], up)
        from_dn = jnp.where(lane >= width, yp_ref[a], dn)
        if 2 * width == LANES:
            y = o_ref[a] + pltpu.roll(from_up + from_dn, width, 1)
        else:
            y = o_ref[a] + pltpu.roll(from_up, width, 1) + pltpu.roll(from_dn, LANES - width, 1)
        o_ref[a] = _silu(y) if silu else y
        return carry

    unroll = 8 if n_slabs % 8 == 0 else 1
    lax.fori_loop(0, n_slabs, row_sums, 0, unroll=unroll)
    lax.fori_loop(0, n_slabs, combine, 0, unroll=unroll)


def _conv(z_s, w9, bias, *, rows, width, n_batch, slab0, chan_lo, chan_n, silu):
    seq = rows * width
    a_n = seq // LANES
    taps = tuple(_conv_taps(rows, width))
    assert all(abs(dr * width + dw) < LANES for dr, dw in taps)
    assert LANES % width == 0 or (rows == 1 and width % LANES == 0)
    tap_ids = [(dr + 1) * 3 + (dw + 1) for dr, dw in taps]
    w_t = jnp.broadcast_to(w9[jnp.array(tap_ids)][:, :, None], (len(taps), chan_n, LANES))
    ct = 64 if rows > 1 else 256
    assert chan_lo % ct == 0 and chan_n % ct == 0 and slab0 % a_n == 0
    nt_ = len(taps)
    if rows > 1:
        assert LANES % width == 0 and nt_ == 9
        body = functools.partial(_conv_grid_kernel, width=width, n_slabs=a_n, silu=silu)
        scratch = [pltpu.VMEM((a_n, ct, LANES), F32), pltpu.VMEM((a_n, ct, LANES), F32)]
    else:
        body = functools.partial(_conv_kernel, taps=taps, width=width, n_slabs=a_n, silu=silu)
        scratch = []
    return pl.pallas_call(
        body,
        scratch_shapes=scratch,
        grid=(n_batch, chan_n // ct),
        in_specs=[pl.BlockSpec((a_n, ct, LANES), lambda b, j: (slab0 // a_n + b, chan_lo // ct + j, 0)),
                  pl.BlockSpec((nt_, ct, LANES), lambda b, j: (0, j, 0)),
                  pl.BlockSpec((ct, 1), lambda b, j: (j, 0))],
        out_specs=pl.BlockSpec((a_n, ct, LANES), lambda b, j: (b, j, 0)),
        out_shape=jax.ShapeDtypeStruct((n_batch * a_n, chan_n, LANES), F32),
        compiler_params=_params(("arbitrary", "arbitrary")),
        name=f"dwconv_{rows}x{width}",
    )(z_s, w_t, bias.reshape(chan_n, 1))


def _dft_consts(a_in, na):
    n = na * LANES
    k = np.arange(na)[:, None]
    a = np.arange(a_in)[None, :]
    ang = 2 * np.pi * (k * a % na) / na
    fa = np.concatenate([np.cos(ang), -np.sin(ang)], axis=0)
    r = np.arange(LANES)
    ang_t = 2 * np.pi * (np.arange(na)[:, None] * r[None, :] % n) / n
    tr, ti = np.cos(ang_t), -np.sin(ang_t)
    ta = np.concatenate([tr, tr], axis=1)
    tb = np.concatenate([-ti, ti], axis=1)
    ang2 = 2 * np.pi * (r[:, None] * r[None, :] % LANES) / LANES
    c2, s2 = np.cos(ang2), np.sin(ang2)
    g2 = np.block([[c2, -s2], [s2, c2]])
    g2i = np.block([[c2, s2], [-s2, c2]])
    ang_i = 2 * np.pi * (np.arange(a_in)[:, None] * np.arange(na)[None, :] % na) / na
    ci, si = np.cos(ang_i) / n, -np.sin(ang_i) / n
    f = lambda v, dt: jnp.asarray(v, dtype=dt)
    return dict(fa=f(fa, F32), ta=f(ta, F32), tb=f(tb, F32), g2=f(g2, F32), g2i=f(g2i, F32),
                ci=f(ci, F32), si=f(si, F32))


def _fwd_slab_stage(m, fa, ta, tb, na):
    pp = jnp.dot(fa, m.astype(BF16), preferred_element_type=F32)
    p = jnp.concatenate([pp[:na], pp[na:]], axis=1)
    return p * ta + _swap_halves(p) * tb


def _cmul(x, kf):
    kr, ki = kf[..., :LANES], kf[..., LANES:]
    ka = jnp.concatenate([kr, kr], axis=-1)
    kb = jnp.concatenate([-ki, ki], axis=-1)
    return x * ka + _swap_halves(x) * kb


def _chan_load(ref, c):
    n, cb, _ = ref.shape
    return ref.reshape(n * cb, LANES)[pl.ds(c, n, stride=cb), :]


def _chan_store(ref, c, val):
    ref[:, c, :] = val


def _dot_f32ish_k(w, h):
    wh, wm, _ = _split3(w)
    hh, hm, _ = _split3(h)
    lhs = jnp.concatenate([wh, wh, wm], axis=1)
    rhs = jnp.concatenate([hh, hm, hh], axis=0)
    return jnp.dot(lhs, rhs, preferred_element_type=F32)


def _taps_kernel(bands_ref, w1t_ref, w1c_ref, w1s_ref, b1_ref, w2_ref, b2_ref, fq_ref, w3_ref, dec_ref, o_ref,
                 *, seq, a_seq, na, spb):
    step = pl.program_id(0)
    lane = lax.broadcasted_iota(jnp.int32, (1, LANES), 1)
    fq = fq_ref[...]
    n_total = na * LANES
    for i in range(spb):
        a = step * spb + i
        is_f = a < a_seq
        is_b = a >= na - a_seq
        live = jnp.logical_or(is_f, is_b)

        @pl.when(live)
        def _():
            n = a * LANES + lane
            pos = jnp.where(is_f, n, n_total - n)
            valid = (n > jnp.where(is_f, -1, n_total - seq)) & (n < jnp.where(is_f, seq, n_total))
            t = pos.astype(F32) / seq
            ang = ((2 * math.pi) * t) * bands_ref[...]
            pre = (w1t_ref[...] * t + _dot_f32ish(w1c_ref[...], jnp.cos(ang))
                   + _dot_f32ish(w1s_ref[...], jnp.sin(ang)))
            h = jnp.sin(fq * (pre + b1_ref[...]))
            h = jnp.sin(fq * (_dot_f32ish(w2_ref[...], h) + b2_ref[...]))
            d = jnp.where(is_f, 0, 1)
            k = _dot_f32ish_k(w3_ref[d], h) * jnp.exp(-t * jnp.abs(dec_ref[d]))
            o_ref[i] = jnp.where(valid, k, 0.0)

        @pl.when(jnp.logical_not(live))
        def _():
            o_ref[i] = jnp.zeros(o_ref.shape[1:], F32)


def _hyena_taps(seq, na, lp):
    nc = HY_ORDER * D_BR
    hid = lp["hy_f_w2"].shape[0]
    w1 = lp["hy_f_w1"]
    col = lambda v: v.reshape(-1, 1)
    bands = col(jnp.linspace(1e-4, HY_BANDS - 1, HY_BANDS, dtype=F32))
    w3 = jnp.transpose(lp["hy_f_w3"].T.reshape(HY_ORDER, 2, D_BR, hid), (1, 0, 2, 3)).reshape(2, nc, hid)
    dec = jnp.broadcast_to(jnp.transpose(lp["hy_decay"], (1, 0, 2)).reshape(2, nc, 1), (2, nc, LANES))
    spb = min(8, na)
    assert na % spb == 0
    args = (bands, col(w1[0]), w1[1:1 + HY_BANDS].T, w1[1 + HY_BANDS:].T, col(lp["hy_f_b1"]), lp["hy_f_w2"].T,
            col(lp["hy_f_b2"]), col(lp["hy_f_freq"]), w3, dec)
    full = lambda v: pl.BlockSpec(v.shape, lambda s: (0,) * v.ndim)
    return pl.pallas_call(
        functools.partial(_taps_kernel, seq=seq, a_seq=seq // LANES, na=na, spb=spb),
        grid=(na // spb,),
        in_specs=[full(v) for v in args],
        out_specs=pl.BlockSpec((spb, nc, LANES), lambda s: (s, 0, 0)),
        out_shape=jax.ShapeDtypeStruct((na, nc, LANES), F32),
        compiler_params=_params(("arbitrary",)),
        name=f"hyena_filter_taps_{na}",
    )(*args)


def _hyena_kernel(v_ref, x1_ref, x2_ref, k0_ref, k1_ref, skip_ref, fa_ref, faf_ref, ta_ref, tb_ref, g2_ref, g2i_ref,
                  ci_ref, si_ref, o_ref, p_buf, z_buf, kf_buf, *, a_in, na):
    fa, ta, tb = fa_ref[...].astype(BF16), ta_ref[...], tb_ref[...]
    ci, si = ci_ref[...].astype(BF16), si_ref[...].astype(BF16)

    @pl.when(pl.program_id(1) == 0)
    def _():
        faf = faf_ref[...].astype(BF16)
        for order, k_ref in enumerate((k0_ref, k1_ref)):
            scales = []
            for c in range(CB):
                m = _chan_load(k_ref, c)
                ss = jnp.sum(jnp.sum(m * m, axis=1, keepdims=True), axis=0, keepdims=True)
                scales.append(lax.rsqrt(ss + EPS))
                p_buf[c] = _fwd_slab_stage(m, faf, ta, tb, na)
            x = _bdot(p_buf[...].reshape(CB * na, 2 * LANES), g2_ref[...]).reshape(CB, na, 2 * LANES)
            for c in range(CB):
                kf_buf[order, c] = x[c] * scales[c]

    def spectral(order):
        x = _bdot(p_buf[...].reshape(CB * na, 2 * LANES), g2_ref[...])
        y = _cmul(x, kf_buf[order].reshape(CB * na, 2 * LANES))
        bm = _bdot(y, g2i_ref[...]).reshape(CB, na, 2 * LANES)
        p_buf[...] = bm * ta - _swap_halves(bm) * tb

    def conv_out(c):
        bb = p_buf[c]
        return (jnp.dot(ci, bb[:, :LANES].astype(BF16), preferred_element_type=F32)
                + jnp.dot(si, bb[:, LANES:].astype(BF16), preferred_element_type=F32))

    for c in range(CB):
        p_buf[c] = _fwd_slab_stage(_chan_load(v_ref, c), fa, ta, tb, na)
    spectral(0)
    for c in range(CB):
        z_buf[c] = _chan_load(x1_ref, c) * (conv_out(c) + _chan_load(v_ref, c) * skip_ref[0, c])
    for c in range(CB):
        p_buf[c] = _fwd_slab_stage(z_buf[c], fa, ta, tb, na)
    spectral(1)
    for c in range(CB):
        _chan_store(o_ref, c, _chan_load(x2_ref, c) * (conv_out(c) + z_buf[c] * skip_ref[1, c]))


def _hyena(u_s, taps_s, skip, *, a_in, na, n_batch):
    cs = _dft_consts(a_in, na)
    faf = _dft_consts(na, na)["fa"]
    nblk = D_BR // CB
    const = lambda shp: pl.BlockSpec(shp, lambda j, b: (0,) * len(shp))
    skip_b = jnp.broadcast_to(skip[:, :, None, None], (HY_ORDER, D_BR, 1, LANES))
    return pl.pallas_call(
        functools.partial(_hyena_kernel, a_in=a_in, na=na),
        grid=(nblk, n_batch),
        in_specs=[pl.BlockSpec((a_in, CB, LANES), lambda j, b: (b, j, 0)),
                  pl.BlockSpec((a_in, CB, LANES), lambda j, b: (b, nblk + j, 0)),
                  pl.BlockSpec((a_in, CB, LANES), lambda j, b: (b, 2 * nblk + j, 0)),
                  pl.BlockSpec((na, CB, LANES), lambda j, b: (0, j, 0)),
                  pl.BlockSpec((na, CB, LANES), lambda j, b: (0, nblk + j, 0)),
                  pl.BlockSpec((HY_ORDER, CB, 1, LANES), lambda j, b: (0, j, 0, 0)),
                  const((2 * na, a_in)), const((2 * na, na)), const((na, 2 * LANES)), const((na, 2 * LANES)),
                  const((2 * LANES, 2 * LANES)), const((2 * LANES, 2 * LANES)),
                  const((a_in, na)), const((a_in, na))],
        out_specs=pl.BlockSpec((a_in, CB, LANES), lambda j, b: (b, j, 0)),
        out_shape=jax.ShapeDtypeStruct((n_batch * a_in, D_BR, LANES), F32),
        scratch_shapes=[pltpu.VMEM((CB, na, 2 * LANES), F32), pltpu.VMEM((CB, a_in, LANES), F32),
                        pltpu.VMEM((HY_ORDER, CB, na, 2 * LANES), F32)],
        compiler_params=_params(("arbitrary", "arbitrary")),
        name=f"hyena_longconv_{a_in}",
    )(u_s, u_s, u_s, taps_s, taps_s, skip_b, cs["fa"], faf, cs["ta"], cs["tb"], cs["g2"], cs["g2i"],
      cs["ci"], cs["si"])


def _slabs_to_rows(ref, n):
    return jnp.concatenate([ref[a].T for a in range(n)], axis=0)


def _hyena_short_kernel(v_ref, x1_ref, x2_ref, k0_ref, k1_ref, skip_ref, f_ref, g_ref, o_ref, *, a_n, a_k):
    seq, nf = a_n * LANES, a_k * LANES
    ff = f_ref[...].astype(BF16)
    gi = g_ref[...].astype(BF16)
    v, x1, x2 = _slabs_to_rows(v_ref, a_n), _slabs_to_rows(x1_ref, a_n), _slabs_to_rows(x2_ref, a_n)

    def longconv(u, k_ref):
        k = _slabs_to_rows(k_ref, a_k)
        s = lax.rsqrt(jnp.sum(k * k, axis=0, keepdims=True) + EPS)
        kf = jnp.dot(ff, k.astype(BF16), preferred_element_type=F32) * s
        x = jnp.dot(ff[:, :seq], u.astype(BF16), preferred_element_type=F32)
        xr, xi, kr, ki = x[:nf], x[nf:], kf[:nf], kf[nf:]
        y = jnp.concatenate([xr * kr - xi * ki, xr * ki + xi * kr], axis=0)
        return jnp.dot(gi, y.astype(BF16), preferred_element_type=F32)

    z = x1 * (longconv(v, k0_ref) + v * skip_ref[0])
    o_ref[...] = x2 * (longconv(z, k1_ref) + z * skip_ref[1])


def _hyena_short(u_s, taps_s, skip, *, a_n, n_batch):
    a_k = 2 * a_n
    seq, nf = a_n * LANES, a_k * LANES
    k = np.arange(nf)
    ang = 2 * np.pi * (k[:, None] * k[None, :] % nf) / nf
    f = np.concatenate([np.cos(ang), -np.sin(ang)], axis=0)
    g = np.concatenate([np.cos(ang[:seq]), -np.sin(ang[:seq])], axis=1) / nf
    nblk = D_BR // LANES
    const = lambda shp: pl.BlockSpec(shp, lambda b, j: (0,) * len(shp))
    return pl.pallas_call(
        functools.partial(_hyena_short_kernel, a_n=a_n, a_k=a_k),
        grid=(n_batch, nblk),
        in_specs=[pl.BlockSpec((a_n, LANES, LANES), lambda b, j: (b, j, 0)),
                  pl.BlockSpec((a_n, LANES, LANES), lambda b, j: (b, nblk + j, 0)),
                  pl.BlockSpec((a_n, LANES, LANES), lambda b, j: (b, 2 * nblk + j, 0)),
                  pl.BlockSpec((a_k, LANES, LANES), lambda b, j: (0, j, 0)),
                  pl.BlockSpec((a_k, LANES, LANES), lambda b, j: (0, nblk + j, 0)),
                  pl.BlockSpec((HY_ORDER, 1, LANES), lambda b, j: (0, 0, j)),
                  const((2 * nf, nf)), const((seq, 2 * nf))],
        out_specs=pl.BlockSpec((None, seq, LANES), lambda b, j: (b, 0, j)),
        out_shape=jax.ShapeDtypeStruct((n_batch, seq, D_BR), F32),
        compiler_params=_params(("arbitrary", "arbitrary")),
        name="hyena_short",
    )(u_s, u_s, u_s, taps_s, taps_s, skip.reshape(HY_ORDER, 1, D_BR), jnp.asarray(f, F32), jnp.asarray(g, F32))


def _chan_dft_mats():
    r = np.arange(LANES)
    ang = 2 * np.pi * (r[:, None] * r[None, :] % LANES) / LANES
    return np.cos(ang), np.sin(ang)


def _fn_mix_kernel(u_ref, cs_ref, o_ref, *, n_slabs):
    w = cs_ref[...].astype(BF16)

    def body(a, carry):
        o_ref[a] = jnp.dot(w, u_ref[a].astype(BF16), preferred_element_type=F32)
        return carry

    lax.fori_loop(0, n_slabs, body, 0)


def _fn_mix(z_s, *, a_n, n_batch, chan_lo):
    c, s = _chan_dft_mats()
    w = jnp.asarray(np.concatenate([c, s], axis=0), dtype=F32)
    g0 = chan_lo // LANES
    return pl.pallas_call(
        functools.partial(_fn_mix_kernel, n_slabs=a_n),
        grid=(n_batch, FN_GROUPS),
        in_specs=[pl.BlockSpec((a_n, LANES, LANES), lambda b, g: (b, g0 + g, 0)),
                  pl.BlockSpec((2 * LANES, LANES), lambda b, g: (0, 0))],
        out_specs=pl.BlockSpec((a_n, 2 * LANES, LANES), lambda b, g: (b, g, 0)),
        out_shape=jax.ShapeDtypeStruct((n_batch * a_n, 2 * D_BR, LANES), F32),
        compiler_params=_params(("arbitrary", "arbitrary")),
        name="fnet_channel_dft",
    )(z_s, w)


def _fn_seq_kernel(p_ref, q_ref, fa_ref, tr_ref, ti_ref, g_ref, o_ref, a_buf, *, a_n, scale):
    fa, tr, ti = fa_ref[...].astype(BF16), tr_ref[...], ti_ref[...]
    for c in range(CB):
        r1 = jnp.dot(fa, _chan_load(p_ref, c).astype(BF16), preferred_element_type=F32)
        r2 = jnp.dot(fa, _chan_load(q_ref, c).astype(BF16), preferred_element_type=F32)
        ar = r1[:a_n] - r2[a_n:]
        ai = -(r2[:a_n] + r1[a_n:])
        a_buf[c] = jnp.concatenate([ar * tr - ai * ti, ar * ti + ai * tr], axis=1)
    y = _bdot(a_buf[...].reshape(CB * a_n, 2 * LANES), g_ref[...]) * scale
    o_ref[...] = y.reshape(CB, a_n, LANES)


def _fn_seq(pq, *, a_n, n_batch):
    seq = a_n * LANES
    k = np.arange(a_n)
    ang = 2 * np.pi * (k[:, None] * k[None, :] % a_n) / a_n
    fa = np.concatenate([np.cos(ang), np.sin(ang)], axis=0)
    r = np.arange(LANES)
    ang_t = 2 * np.pi * (k[:, None] * r[None, :] % seq) / seq
    c2, s2 = _chan_dft_mats()
    g = np.concatenate([c2, s2], axis=0)
    nblk = LANES // CB
    const = lambda shp: pl.BlockSpec(shp, lambda b, j: (0,) * len(shp))

    def chan_blk(j, off):
        return (j // nblk) * (2 * nblk) + off * nblk + j % nblk

    return pl.pallas_call(
        functools.partial(_fn_seq_kernel, a_n=a_n, scale=1.0 / math.sqrt(seq * LANES)),
        grid=(n_batch, D_BR // CB),
        in_specs=[pl.BlockSpec((a_n, CB, LANES), lambda b, j: (b, chan_blk(j, 0), 0)),
                  pl.BlockSpec((a_n, CB, LANES), lambda b, j: (b, chan_blk(j, 1), 0)),
                  const((2 * a_n, a_n)), const((a_n, LANES)), const((a_n, LANES)), const((2 * LANES, LANES))],
        out_specs=pl.BlockSpec((None, CB, a_n, LANES), lambda b, j: (b, j, 0, 0)),
        out_shape=jax.ShapeDtypeStruct((n_batch, D_BR, a_n, LANES), F32),
        scratch_shapes=[pltpu.VMEM((CB, a_n, 2 * LANES), F32)],
        compiler_params=_params(("arbitrary", "arbitrary")),
        name="fnet_sequence_dft",
    )(pq, pq, jnp.asarray(fa, F32), jnp.asarray(np.cos(ang_t), F32), jnp.asarray(-np.sin(ang_t), F32),
      jnp.asarray(g, F32))


def _fn_small_kernel(u_ref, cw_ref, sw_ref, cl_ref, sl_ref, o_ref, *, a_n, scale):
    u = jnp.concatenate([u_ref[a].T for a in range(a_n)], axis=0)
    p = _bdot(u, cw_ref[...])
    q = _bdot(u, sw_ref[...])
    o_ref[...] = (_bdot(cl_ref[...], p) - _bdot(sl_ref[...], q)) * scale


def _fn_small(z_s, *, a_n, n_batch, slab0, chan_lo):
    seq = a_n * LANES
    cw, sw = _chan_dft_mats()
    n = np.arange(seq)
    ang = 2 * np.pi * (n[:, None] * n[None, :] % seq) / seq
    const = lambda shp: pl.BlockSpec(shp, lambda b, g: (0,) * len(shp))
    g0 = chan_lo // LANES
    return pl.pallas_call(
        functools.partial(_fn_small_kernel, a_n=a_n, scale=1.0 / math.sqrt(seq * LANES)),
        grid=(n_batch, FN_GROUPS),
        in_specs=[pl.BlockSpec((a_n, LANES, LANES), lambda b, g: (slab0 // a_n + b, g0 + g, 0)),
                  const((LANES, LANES)), const((LANES, LANES)), const((seq, seq)), const((seq, seq))],
        out_specs=pl.BlockSpec((None, seq, LANES), lambda b, g: (b, 0, g)),
        out_shape=jax.ShapeDtypeStruct((n_batch, seq, D_BR), F32),
        compiler_params=_params(("arbitrary", "arbitrary")),
        name="fnet_short",
    )(z_s, jnp.asarray(cw, F32), jnp.asarray(sw, F32), jnp.asarray(np.cos(ang), F32), jnp.asarray(np.sin(ang), F32))


def _log_sigmoid(x):
    return jnp.minimum(x, 0.0) - jnp.log(1.0 + jnp.exp(-jnp.abs(x)))


def _exact_tri_dot(tri, x, tri_on_left):
    h, m, l = _split3(x)
    if tri_on_left:
        d = lambda p: jnp.dot(tri, p, preferred_element_type=F32)
    else:
        d = lambda p: jnp.dot(p, tri, preferred_element_type=F32)
    return d(h) + d(m) + d(l)


def _mlstm_step(inputs, c_st, n_st, m_st):
    t = hd = LANES
    n_dir, n_batch = len(inputs), len(inputs[0])
    n_grp = n_dir * n_batch * ML_HEADS
    row = lax.broadcasted_iota(jnp.int32, (t, t), 0)
    col = lax.broadcasted_iota(jnp.int32, (t, t), 1)
    tri = jnp.where(col <= row, 1.0, 0.0).astype(BF16)
    tri_t = jnp.where(col >= row, 1.0, 0.0).astype(BF16)
    qs, ks, vs, bcs, brs, ics, irs = [], [], [], [], [], [], []
    for d in range(n_dir):
        i_off = 2 * ML_HEADS * d
        f_off = i_off + ML_HEADS
        for b in range(n_batch):
            q_all, k_all, v_all, g = inputs[d][b]
            gt = g.T
            lf_c = _log_sigmoid(g)
            lf_r = lf_c.T
            if d == 1:
                b_c = _exact_tri_dot(tri_t, lf_c, True)
                b_r = _exact_tri_dot(tri, lf_r, False)
            else:
                b_c = _exact_tri_dot(tri, lf_c, True)
                b_r = _exact_tri_dot(tri_t, lf_r, False)
            for h in range(ML_HEADS):
                sl = slice(h * hd, (h + 1) * hd)
                qs.append(q_all[:, sl])
                ks.append(k_all[:, sl])
                vs.append(v_all[:, sl])
                bcs.append(b_c[:, f_off + h:f_off + h + 1])
                brs.append(b_r[f_off + h:f_off + h + 1, :])
                ics.append(g[:, i_off + h:i_off + h + 1])
                irs.append(gt[i_off + h:i_off + h + 1, :])
    q = jnp.stack(qs) * (hd ** -0.5)
    k, v = jnp.stack(ks), jnp.stack(vs)
    bc, br, ic, ir = jnp.stack(bcs), jnp.stack(brs), jnp.stack(ics), jnp.stack(irs)
    m_prev = m_st[...][:, :, :1]
    ct = c_st[...]
    n_prev = n_st[...]

    shp = (n_grp, t, t)
    grp = lax.broadcasted_iota(jnp.int32, shp, 0)
    r3, c3 = lax.broadcasted_iota(jnp.int32, shp, 1), lax.broadcasted_iota(jnp.int32, shp, 2)
    back = grp >= (n_grp // n_dir)
    mask = (back & (c3 >= r3)) | (jnp.logical_not(back) & (c3 <= r3))
    bdot = lambda a, b_, ca, cb: lax.dot_general(a.astype(BF16), b_.astype(BF16), (((ca,), (cb,)), ((0,), (0,))),
                                                 preferred_element_type=F32)
    dm = jnp.where(mask, bc - br + ir, -jnp.inf)
    inter = bc + m_prev
    m_row = jnp.maximum(inter, jnp.max(dm, axis=-1, keepdims=True))
    w_intra = jnp.exp(dm - m_row)
    w_inter = jnp.exp(inter - m_row)
    s = bdot(q, k, 2, 2) * w_intra
    num = bdot(s, v, 2, 1) + w_inter * bdot(q, ct, 2, 1)
    den = jnp.sum(s, axis=-1, keepdims=True) + w_inter * jnp.sum(q * n_prev, axis=-1, keepdims=True)
    den = jnp.maximum(jnp.abs(den), jnp.exp(-m_row))
    h_all = num / den

    is_back = lax.broadcasted_iota(jnp.int32, (n_grp, 1, 1), 0) >= (n_grp // n_dir)
    b_tot_c = jnp.where(is_back, bc[:, :1, :], bc[:, t - 1:, :])
    b_tot_r = jnp.where(is_back, br[:, :, :1], br[:, :, t - 1:])
    a_c = b_tot_c - bc + ic
    a_r = b_tot_r - br + ir
    m_new = jnp.maximum(b_tot_c + m_prev, jnp.max(a_r, axis=-1, keepdims=True))
    sc = jnp.exp(a_c - m_new)
    decay = jnp.exp(b_tot_c + m_prev - m_new)
    k_sc = k * sc
    c_st[...] = decay * ct + bdot(k_sc, v, 1, 1)
    n_st[...] = decay * n_prev + jnp.sum(k_sc, axis=1, keepdims=True)
    m_st[...] = jnp.broadcast_to(m_new, (n_grp, 1, LANES))

    out = []
    for d in range(n_dir):
        out.append([jnp.concatenate([h_all[(d * n_batch + b) * ML_HEADS + h] for h in range(ML_HEADS)], axis=1)
                    for b in range(n_batch)])
    return out


def _mlstm_kernel(*refs, n_batch, ctx_chunks):
    lat_f, lat_b, ctx_f, ctx_b = refs[0:4], refs[4:8], refs[8:12], refs[12:16]
    hf_lat, hb_lat, hf_ctx, hb_ctx, c_st, n_st, m_st = refs[16:]
    j = pl.program_id(0)
    is_ctx = j < ctx_chunks

    @pl.when(j == 0)
    def _():
        c_st[...] = jnp.zeros(c_st.shape, F32)
        n_st[...] = jnp.zeros(n_st.shape, F32)
        m_st[...] = jnp.zeros(m_st.shape, F32)

    pick = lambda c_refs, l_refs, b: tuple(jnp.where(is_ctx, c[b], l[b]) for c, l in zip(c_refs, l_refs))
    inputs = [[pick(ctx_f, lat_f, b) for b in range(n_batch)], [pick(ctx_b, lat_b, b) for b in range(n_batch)]]
    hf, hb = _mlstm_step(inputs, c_st, n_st, m_st)

    @pl.when(is_ctx)
    def _():
        for b in range(n_batch):
            hf_ctx[b] = hf[b]
            hb_ctx[b] = hb[b]

    @pl.when(jnp.logical_not(is_ctx))
    def _():
        for b in range(n_batch):
            hf_lat[b] = hf[b]
            hb_lat[b] = hb[b]


def _mlstm(qk_lat, z_lat, qk_ctx, z_ctx, *, n_batch, v_col, g_col):
    lat_len, ctx_len = qk_lat.shape[0] // n_batch, qk_ctx.shape[0] // n_batch
    nlc, ncc = lat_len // LANES, ctx_len // LANES
    r3 = lambda a: a.reshape(n_batch, a.shape[0] // n_batch, a.shape[1])
    lf = lambda j: jnp.maximum(j - ncc, 0)
    lb = lambda j: jnp.where(j < ncc, nlc - 1, nlc - 1 - (j - ncc))
    cf = lambda j: jnp.minimum(j, ncc - 1)
    cb = lambda j: jnp.where(j < ncc, ncc - 1 - j, 0)

    def specs(ix):
        blk = lambda w, cidx: pl.BlockSpec((n_batch, LANES, w), lambda j: (0, ix(j), cidx))
        return [blk(D_BR, 0), blk(D_BR, 1), blk(D_BR, v_col), blk(LANES, g_col)]

    out = lambda ix: pl.BlockSpec((n_batch, LANES, D_BR), lambda j: (0, ix(j), 0))
    sd = lambda n: jax.ShapeDtypeStruct((n_batch, n, D_BR), F32)
    ql, zl, qc, zc = r3(qk_lat), r3(z_lat), r3(qk_ctx), r3(z_ctx)
    hf_lat, hb_lat, hf_ctx, hb_ctx = pl.pallas_call(
        functools.partial(_mlstm_kernel, n_batch=n_batch, ctx_chunks=ncc),
        grid=(ncc + nlc,),
        in_specs=specs(lf) + specs(lb) + specs(cf) + specs(cb),
        out_specs=[out(lf), out(lb), out(cf), out(cb)],
        out_shape=[sd(lat_len), sd(lat_len), sd(ctx_len), sd(ctx_len)],
        scratch_shapes=[pltpu.VMEM((2 * n_batch * ML_HEADS, LANES, LANES), F32),
                        pltpu.VMEM((2 * n_batch * ML_HEADS, 1, LANES), F32),
                        pltpu.VMEM((2 * n_batch * ML_HEADS, 1, LANES), F32)],
        compiler_params=_params(("arbitrary",)),
        name="mlstm_bidir",
    )(ql, ql, zl, zl, ql, ql, zl, zl, qc, qc, zc, zc, qc, qc, zc, zc)
    flat = lambda a: a.reshape(a.shape[0] * a.shape[1], D_BR)
    return (flat(hf_lat), flat(hb_lat)), (flat(hf_ctx), flat(hb_ctx))


def _rms_mod(x, w, shift, scale):
    y = x * lax.rsqrt(jnp.mean(x * x, axis=-1, keepdims=True) + EPS) * w
    return y * (1.0 + scale) + shift


def _route(t, rw, rb):
    logits = _dot_f32ish(t, rw) + rb
    col = lax.broadcasted_iota(jnp.int32, logits.shape, 1)
    big = jnp.int32(1 << 20)
    ninf = -jnp.inf
    is_g = col < MOE_GROUPS
    gl = jnp.where(is_g, logits, ninf)
    gmax = jnp.max(gl, axis=-1, keepdims=True)
    g_sel = jnp.min(jnp.where(is_g & (gl == gmax), col, big), axis=-1, keepdims=True)
    p_top = 1.0 / jnp.sum(jnp.where(is_g, jnp.exp(gl - gmax), 0.0), axis=-1, keepdims=True)
    lo = MOE_GROUPS + g_sel * MOE_PER_GROUP
    in_grp = (col >= lo) & (col < lo + MOE_PER_GROUP)
    e1v = jnp.where(in_grp, logits, ninf)
    top1 = jnp.max(e1v, axis=-1, keepdims=True)
    idx1 = jnp.min(jnp.where(in_grp & (e1v == top1), col, big), axis=-1, keepdims=True)
    e2v = jnp.where(col == idx1, ninf, e1v)
    top2 = jnp.max(e2v, axis=-1, keepdims=True)
    idx2 = jnp.min(jnp.where(in_grp & (col != idx1) & (e2v == top2), col, big), axis=-1, keepdims=True)
    ex = jnp.exp(top2 - top1)
    s1 = 1.0 / (1.0 + ex)
    return jnp.where(col == idx1, p_top * s1, 0.0) + jnp.where(col == idx2, p_top * (ex * s1), 0.0)


def _merge_kernel(yh_ref, yf_ref, hf_ref, hb_ref, o_ref, g0_ref, g1_ref, g2_ref, x_ref, gate_ref,
                  wb_ref, wo_ref, nw_ref, n2_ref, sh_ref, sc_ref, rw_ref, rb_ref, out_ref, xn_ref, comb_ref):
    hd = LANES
    h = hf_ref[...] + hb_ref[...]
    parts = []
    for i in range(ML_HEADS):
        hh = h[:, i * hd:(i + 1) * hd]
        parts.append(hh * lax.rsqrt(jnp.mean(hh * hh, axis=-1, keepdims=True) + EPS))
    y_ml = jax.nn.sigmoid(o_ref[...]) * (jnp.concatenate(parts, axis=1) * nw_ref[...])
    acc = g0_ref[...].astype(F32) * _bdot(yh_ref[...], wb_ref[0])
    acc = acc + g1_ref[...].astype(F32) * _bdot(yf_ref[...], wb_ref[1])
    acc = acc + g2_ref[...].astype(F32) * _bdot(y_ml, wb_ref[2])
    x_new = x_ref[...] + gate_ref[...] * _bdot(acc, wo_ref[...])
    out_ref[...] = x_new
    t = _rms_mod(x_new, n2_ref[...], sh_ref[...], sc_ref[...])
    xn_ref[...] = t.astype(BF16)
    comb_ref[...] = _route(t, rw_ref[...], rb_ref[...])


def _merge(yh, yf, hf, hb, z_tm, gates, x, mods3, wb, wo, nw, n2w, rw, rb, *, seg, tm, o_col):
    nt, d = x.shape
    tok = lambda w, cidx: pl.BlockSpec((tm, w), lambda i: (i, cidx))
    mod = lambda k: pl.BlockSpec((None, 1, d), lambda i: (seg(i), 0, k))
    return pl.pallas_call(
        _merge_kernel,
        grid=(nt // tm,),
        in_specs=[tok(D_BR, 0), tok(D_BR, 0), tok(D_BR, 0), tok(D_BR, 0),
                  tok(D_BR, o_col),
                  tok(d, 0), tok(d, 1), tok(d, 2),
                  tok(d, 0),
                  mod(2),
                  pl.BlockSpec((3, D_BR, d), lambda i: (0, 0, 0)),
                  pl.BlockSpec((d, d), lambda i: (0, 0)),
                  pl.BlockSpec((1, D_BR), lambda i: (0, 0)),
                  pl.BlockSpec((1, d), lambda i: (0, 0)),
                  mod(3), mod(4),
                  pl.BlockSpec((d, LANES), lambda i: (0, 0)),
                  pl.BlockSpec((1, LANES), lambda i: (0, 0))],
        out_specs=[tok(d, 0), tok(d, 0), tok(LANES, 0)],
        out_shape=[jax.ShapeDtypeStruct((nt, d), F32), jax.ShapeDtypeStruct((nt, d), BF16),
                   jax.ShapeDtypeStruct((nt, LANES), F32)],
        compiler_params=_params(("arbitrary",)),
        name="merge_branches_router",
    )(yh, yf, hf, hb, z_tm, gates, gates, gates, x, mods3, wb, wo, nw.reshape(1, D_BR), n2w.reshape(1, d),
      mods3, mods3, rw, rb)


def _moe_kernel(xn_ref, comb_ref, wg_ref, wu_ref, wd_ref, x_ref, gate_ref, nw_ref, sh_ref, sc_ref, *out_and_scratch,
                final, n_keep):
    acc_ref = out_and_scratch[-1]
    e = pl.program_id(1)

    @pl.when(e == 0)
    def _():
        acc_ref[...] = jnp.zeros(acc_ref.shape, F32)

    xn = xn_ref[...]
    comb = comb_ref[...]
    col = lax.broadcasted_iota(jnp.int32, comb.shape, 1)
    cw = jnp.sum(jnp.where(col == e + MOE_GROUPS, comb, 0.0), axis=-1, keepdims=True)
    hg = jnp.dot(xn, wg_ref[...], preferred_element_type=F32)
    hu = jnp.dot(xn, wu_ref[...], preferred_element_type=F32)
    a = (hg * jax.nn.sigmoid(hg)) * hu * cw
    acc_ref[...] += jnp.dot(a.astype(BF16), wd_ref[...], preferred_element_type=F32)

    if final:
        y_ref, = out_and_scratch[:-1]

        @pl.when((e == MOE_EXPERTS - 1) & (pl.program_id(0) < n_keep))
        def _():
            x_new = x_ref[...] + gate_ref[...] * acc_ref[...]
            y_ref[...] = x_new * lax.rsqrt(jnp.mean(x_new * x_new, axis=-1, keepdims=True) + EPS) * nw_ref[...]
    else:
        o_ref, xn_next_ref = out_and_scratch[:-1]

        @pl.when(e == MOE_EXPERTS - 1)
        def _():
            x_new = x_ref[...] + gate_ref[...] * acc_ref[...]
            o_ref[...] = x_new
            xn_next_ref[...] = _rms_mod(x_new, nw_ref[...], sh_ref[...], sc_ref[...]).astype(BF16)


def _moe(xn, comb, wg, wu, wd, x, mods3, post_w, post_mods3, *, seg, tm, final, n_keep_rows):
    nt, d = x.shape
    n_keep = n_keep_rows // tm
    tok = pl.BlockSpec((tm, d), lambda i, e: (i, 0))
    if final:
        out_specs = [pl.BlockSpec((tm, d), lambda i, e: (jnp.minimum(i, n_keep - 1), 0))]
        out_shape = [jax.ShapeDtypeStruct((n_keep_rows, d), F32)]
    else:
        out_specs = [tok, tok]
        out_shape = [jax.ShapeDtypeStruct((nt, d), F32), jax.ShapeDtypeStruct((nt, d), BF16)]
    mod = lambda k: pl.BlockSpec((None, 1, d), lambda i, e: (seg(i), 0, k))
    return pl.pallas_call(
        functools.partial(_moe_kernel, final=final, n_keep=n_keep),
        grid=(nt // tm, MOE_EXPERTS),
        in_specs=[tok,
                  pl.BlockSpec((tm, LANES), lambda i, e: (i, 0)),
                  pl.BlockSpec((None, d, EXPERT_HID), lambda i, e: (e, 0, 0)),
                  pl.BlockSpec((None, d, EXPERT_HID), lambda i, e: (e, 0, 0)),
                  pl.BlockSpec((None, EXPERT_HID, d), lambda i, e: (e, 0, 0)),
                  tok,
                  mod(5),
                  pl.BlockSpec((1, d), lambda i, e: (0, 0)),
                  mod(0), mod(1)],
        out_specs=out_specs,
        out_shape=out_shape,
        scratch_shapes=[pltpu.VMEM((tm, d), F32)],
        compiler_params=_params(("arbitrary", "arbitrary")),
        name="moe_experts_final" if final else "moe_experts",
    )(xn, comb, wg, wu, wd, x, mods3, post_w.reshape(1, d), post_mods3, post_mods3)


def _slab_to_tm(y_s):
    ns, c, _ = y_s.shape
    return jnp.transpose(y_s, (0, 2, 1)).reshape(ns * LANES, c)


def kernel(x, c, ctx, c_ctx, ada_w, ada_b, norm1_w, norm2_w, w_in, b_in, hy_conv_w, hy_conv_b, hy_f_w1, hy_f_b1, hy_f_w2, hy_f_b2, hy_f_w3, hy_f_freq, hy_decay, hy_skip, ml_conv_w, ml_conv_b, ml_norm_w, w_branch, w_out, moe_rg_w, moe_rg_b, moe_re_w, moe_re_b, moe_w_gate, moe_w_up, moe_w_down, norm_f_w):
    nb, seq, d = x.shape
    lc = ctx.shape[1]
    depth = ada_w.shape[0]
    assert d == D_MODEL and seq % (GRID_W * 2) == 0 and lc % LANES == 0 and nb + 1 <= 8
    rows = seq // GRID_W
    a_lat = seq // LANES
    a_ctx = lc // LANES
    n_lat, n_ctx = nb * seq, nb * lc
    tm = 256
    tm_moe = {"lat": _pick(seq, (1024, 512, 256)), "ctx": _pick(n_ctx, (512, 256))}
    tm_mrg = {"lat": _pick(seq, (512, 256)), "ctx": _pick(n_ctx, (512, 256))}
    assert seq % tm == 0 and n_ctx % tm == 0
    seg_of = lambda s, t: (lambda i: i // (seq // t)) if s == "lat" else (lambda i: nb)
    streams = ("lat", "ctx")
    xs = {"lat": x.reshape(n_lat, d), "ctx": ctx.reshape(n_ctx, d)}
    xn = {}
    cvec = jnp.zeros((8, d), F32).at[:nb].set(c).at[nb].set(c_ctx)
    mods = _mods(cvec, ada_w, ada_b)

    o_fn, o_ml, o_mlg, o_gate = 3 * D_BR, 4 * D_BR, 8 * D_BR, 8 * D_BR + 4 * ML_HEADS
    pad_g = LANES - 4 * ML_HEADS

    for l in range(depth):
        lp = {"hy_f_w1": hy_f_w1[l], "hy_f_b1": hy_f_b1[l], "hy_f_w2": hy_f_w2[l], "hy_f_b2": hy_f_b2[l],
              "hy_f_w3": hy_f_w3[l], "hy_f_freq": hy_f_freq[l], "hy_decay": hy_decay[l]}
        mods3 = mods[l].reshape(8, 1, 6 * d)
        wl, bl = w_in[l], b_in[l]
        w_cm = jnp.concatenate([wl[:, :o_fn], wl[:, o_ml:o_ml + 2 * D_BR], wl[:, o_fn:o_ml]], axis=1)
        b_cm = jnp.concatenate([bl[:o_fn], bl[o_ml:o_ml + 2 * D_BR], bl[o_fn:o_ml]])
        w_tm = jnp.concatenate([wl[:, o_ml + 2 * D_BR:o_mlg], wl[:, o_mlg:o_gate], jnp.zeros((d, pad_g), F32)], axis=1)
        b_tm = jnp.concatenate([bl[o_ml + 2 * D_BR:o_mlg], bl[o_mlg:o_gate], jnp.zeros((pad_g,), F32)])
        c_hy, c_qk, c_fn = 0, 3 * D_BR, 5 * D_BR
        g_col = (2 * D_BR) // LANES

        last = l + 1 == depth
        live = ("lat",) if last else streams
        if l == 0:
            xn = {s: _norm_mod(xs[s], norm1_w[l], mods3, 0, 1, seg_of(s, tm), tm) for s in streams}
        w_tm_b, w_cm_t = w_tm.astype(BF16), w_cm.T.astype(BF16)
        z_tm = {s: _mm_tm(xn[s], w_tm_b, b_tm, gate=False) for s in streams}
        w_gate_b = wl[:, o_gate:].astype(BF16)
        gates = {s: _mm_tm(xn[s], w_gate_b, bl[o_gate:], gate=True) for s in live}
        z_s = {s: _mm_slab(xn[s], w_cm_t, b_cm) for s in streams}

        grid_kw = {"lat": dict(rows=rows, width=GRID_W), "ctx": dict(rows=1, width=lc)}
        hy_w, hy_b = hy_conv_w[l].reshape(9, 3 * D_BR), hy_conv_b[l]
        ml_w, ml_b = ml_conv_w[l].reshape(9, 2 * D_BR), ml_conv_b[l]
        conv = lambda s, w, b, lo, n, act: _conv(z_s[s], w, b, chan_lo=lo, chan_n=n, silu=act, n_batch=nb,
                                                 slab0=0, **grid_kw[s])
        u = {s: conv(s, hy_w, hy_b, c_hy, 3 * D_BR, False) for s in live}
        qk = {s: _slab_to_tm(conv(s, ml_w, ml_b, c_qk, 2 * D_BR, True)) for s in streams}

        (hf_lat, hb_lat), (hf_ctx, hb_ctx) = _mlstm(qk["lat"], z_tm["lat"], qk["ctx"], z_tm["ctx"],
                                                    n_batch=nb, v_col=0, g_col=g_col)
        h_f, h_b = {"lat": hf_lat, "ctx": hf_ctx}, {"lat": hb_lat, "ctx": hb_ctx}

        yh, yf = {}, {}
        yh["lat"] = _slab_to_tm(_hyena(u["lat"], _hyena_taps(seq, 2 * a_lat, lp), hy_skip[l],
                                       a_in=a_lat, na=2 * a_lat, n_batch=nb))
        pq = _fn_mix(z_s["lat"], a_n=a_lat, n_batch=nb, chan_lo=c_fn)
        yk = _fn_seq(pq, a_n=a_lat, n_batch=nb)
        yf["lat"] = jnp.transpose(yk, (0, 3, 2, 1)).reshape(n_lat, D_BR)
        if not last:
            yh["ctx"] = _hyena_short(u["ctx"], _hyena_taps(lc, 2 * a_ctx, lp), hy_skip[l],
                                     a_n=a_ctx, n_batch=nb).reshape(n_ctx, D_BR)
            yf["ctx"] = _fn_small(z_s["ctx"], a_n=a_ctx, n_batch=nb, slab0=0, chan_lo=c_fn).reshape(n_ctx, D_BR)

        rw = jnp.concatenate([moe_rg_w[l], moe_re_w[l], jnp.zeros((d, LANES - MOE_GROUPS - MOE_EXPERTS), F32)], axis=1)
        rb = jnp.concatenate([moe_rg_b[l], moe_re_b[l], jnp.zeros((LANES - MOE_GROUPS - MOE_EXPERTS,), F32)]).reshape(1, LANES)
        wb, wo = w_branch[l].astype(BF16), w_out[l].astype(BF16)
        experts = (moe_w_gate[l].astype(BF16), moe_w_up[l].astype(BF16), moe_w_down[l].astype(BF16))
        for s in live:
            xs[s], xn2, comb = _merge(yh[s], yf[s], h_f[s], h_b[s], z_tm[s], gates[s], xs[s], mods3, wb, wo,
                                      ml_norm_w[l], norm2_w[l], rw, rb, seg=seg_of(s, tm_mrg[s]), tm=tm_mrg[s],
                                      o_col=1)
            moe_kw = dict(seg=seg_of(s, tm_moe[s]), tm=tm_moe[s], n_keep_rows=xs[s].shape[0])
            if last:
                out, = _moe(xn2, comb, *experts, xs[s], mods3, norm_f_w, mods3, final=True, **moe_kw)
            else:
                xs[s], xn[s] = _moe(xn2, comb, *experts, xs[s], mods3, norm1_w[l + 1],
                                    mods[l + 1].reshape(8, 1, 6 * d), final=False, **moe_kw)

    return out.reshape(nb, seq, d)
```

```python
import functools
import math

import numpy as np
import jax
import jax.numpy as jnp
from jax import lax
from jax.experimental import pallas as pl
from jax.experimental.pallas import tpu as pltpu

F32 = jnp.float32
BF16 = jnp.bfloat16

D_MODEL = 1024
D_BR = 512
GRID_W = 64
LANES = 128
CB = 8
HY_ORDER = 2
HY_BANDS = 16
FN_GROUPS = 4
ML_HEADS = 4
MOE_GROUPS = 4
MOE_PER_GROUP = 4
MOE_EXPERTS = 16
EXPERT_HID = 256
EPS = 1e-6
VMEM_LIMIT = 56 * 1024 * 1024


def _params(sem):
    return pltpu.CompilerParams(dimension_semantics=sem, vmem_limit_bytes=VMEM_LIMIT)


def _bdot(a, b):
    return jnp.dot(a.astype(BF16), b.astype(BF16), preferred_element_type=F32)


def _split3(x):
    hi = x.astype(BF16)
    r1 = x - hi.astype(F32)
    mid = r1.astype(BF16)
    lo = (r1 - mid.astype(F32)).astype(BF16)
    return hi, mid, lo


def _dot_f32ish(x, w):
    xh, xm, xl = _split3(x)
    wh, wm, wl = _split3(w)
    d = lambda a, b: jnp.dot(a, b, preferred_element_type=F32)
    return (d(xh, wh) + (d(xh, wm) + d(xm, wh))) + (d(xm, wm) + d(xh, wl) + d(xl, wh))


def _swap_halves(x):
    return jnp.concatenate([x[..., LANES:], x[..., :LANES]], axis=-1)


def _mods_kernel(c_ref, w_ref, b_ref, o_ref):
    c = c_ref[...]
    s = c * jax.nn.sigmoid(c)
    o_ref[...] = _dot_f32ish(s, w_ref[...]) + b_ref[...]


def _mods(cvec, ada_w, ada_b):
    depth, d, n6 = ada_w.shape
    tn = 1536
    return pl.pallas_call(
        _mods_kernel,
        grid=(depth, n6 // tn),
        in_specs=[pl.BlockSpec((8, d), lambda l, j: (0, 0)),
                  pl.BlockSpec((None, d, tn), lambda l, j: (l, 0, j)),
                  pl.BlockSpec((None, 1, tn), lambda l, j: (l, 0, j))],
        out_specs=pl.BlockSpec((None, 8, tn), lambda l, j: (l, 0, j)),
        out_shape=jax.ShapeDtypeStruct((depth, 8, n6), F32),
        compiler_params=_params(("arbitrary", "arbitrary")),
        name="adaln_mods",
    )(cvec, ada_w, ada_b.reshape(depth, 1, n6))


def _norm_mod_kernel(x_ref, w_ref, sh_ref, sc_ref, o_ref):
    x = x_ref[...]
    y = x * lax.rsqrt(jnp.mean(x * x, axis=-1, keepdims=True) + EPS) * w_ref[...]
    o_ref[...] = (y * (1.0 + sc_ref[...]) + sh_ref[...]).astype(o_ref.dtype)


def _norm_mod(x, w, mods3, col_shift, col_scale, seg, tm):
    nt, d = x.shape
    return pl.pallas_call(
        _norm_mod_kernel,
        grid=(nt // tm,),
        in_specs=[pl.BlockSpec((tm, d), lambda i: (i, 0)),
                  pl.BlockSpec((1, d), lambda i: (0, 0)),
                  pl.BlockSpec((None, 1, d), lambda i: (seg(i), 0, col_shift)),
                  pl.BlockSpec((None, 1, d), lambda i: (seg(i), 0, col_scale))],
        out_specs=pl.BlockSpec((tm, d), lambda i: (i, 0)),
        out_shape=jax.ShapeDtypeStruct((nt, d), BF16),
        compiler_params=_params(("arbitrary",)),
        name="norm_mod",
    )(x, w.reshape(1, d), mods3, mods3)


def _mm_tm_kernel(x_ref, w_ref, b_ref, o_ref, *, gate):
    y = jnp.dot(x_ref[...], w_ref[...], preferred_element_type=F32) + b_ref[...]
    o_ref[...] = (jax.nn.sigmoid(y) if gate else y).astype(o_ref.dtype)


def _pick(n, cands):
    for c in cands:
        if n % c == 0:
            return c
    raise ValueError(f"no tile for {n} in {cands}")


def _mm_tm(xn, w, b, *, gate):
    nt, k = xn.shape
    n = w.shape[1]
    tm = _pick(nt, (1056, 1024, 768, 512, 256))
    tn = _pick(n, (1536, 1408, 1152, 1024, 512, 384, 256, 128))
    return pl.pallas_call(
        functools.partial(_mm_tm_kernel, gate=gate),
        grid=(n // tn, nt // tm),
        in_specs=[pl.BlockSpec((tm, k), lambda j, i: (i, 0)),
                  pl.BlockSpec((k, tn), lambda j, i: (0, j)),
                  pl.BlockSpec((1, tn), lambda j, i: (0, j))],
        out_specs=pl.BlockSpec((tm, tn), lambda j, i: (i, j)),
        out_shape=jax.ShapeDtypeStruct((nt, n), BF16 if gate else F32),
        compiler_params=_params(("arbitrary", "arbitrary")),
        name="inproj_gates" if gate else "inproj_token_major",
    )(xn, w, b.reshape(1, n))


def _mm_slab_kernel(w_ref, x_ref, b_ref, o_ref, *, slabs):
    w = w_ref[...]
    b = b_ref[...]
    step = 2 if slabs % 2 == 0 else 1
    for s in range(0, slabs, step):
        xs = x_ref[s * LANES:(s + step) * LANES, :]
        y = lax.dot_general(w, xs, (((1,), (1,)), ((), ())), preferred_element_type=F32) + b
        for i in range(step):
            o_ref[s + i] = y[:, i * LANES:(i + 1) * LANES]


def _mm_slab(xn, wt, b):
    nt, k = xn.shape
    c = wt.shape[0]
    ns = nt // LANES
    ts = _pick(ns, (12, 11, 8, 6, 4, 3, 2, 1))
    tc = _pick(c, (1024, 512, 256, 128))
    return pl.pallas_call(
        functools.partial(_mm_slab_kernel, slabs=ts),
        grid=(c // tc, ns // ts),
        in_specs=[pl.BlockSpec((tc, k), lambda j, i: (j, 0)),
                  pl.BlockSpec((ts * LANES, k), lambda j, i: (i, 0)),
                  pl.BlockSpec((tc, 1), lambda j, i: (j, 0))],
        out_specs=pl.BlockSpec((ts, tc, LANES), lambda j, i: (i, j, 0)),
        out_shape=jax.ShapeDtypeStruct((ns, c, LANES), F32),
        compiler_params=_params(("arbitrary", "arbitrary")),
        name="inproj_slab",
    )(wt, xn, b.reshape(c, 1))


def _conv_taps(rows, width):
    taps = []
    for dr in (-1, 0, 1):
        if rows == 1 and dr != 0:
            continue
        for dw in (-1, 0, 1):
            taps.append((dr, dw))
    return taps


def _silu(x):
    return x * jax.nn.sigmoid(x)


def _conv_kernel(x_ref, w_ref, b_ref, o_ref, *, taps, width, n_slabs, silu):
    ct = x_ref.shape[1]
    lane = lax.broadcasted_iota(jnp.int32, (ct, LANES), 1)
    bias = jnp.zeros((ct, LANES), F32) + b_ref[...]
    planes = []
    for t, (dr, dw) in enumerate(taps):
        w = w_ref[t]
        if width < LANES and dw != 0:
            col = lane % width + dw
            w = jnp.where((col >= 0) & (col < width), w, 0.0)
        planes.append((dr * width + dw, w))

    def body(a, carry):
        x0 = x_ref[a]
        xm = jnp.where(a > 0, x_ref[jnp.maximum(a - 1, 0)], 0.0)
        xp = jnp.where(a < n_slabs - 1, x_ref[jnp.minimum(a + 1, n_slabs - 1)], 0.0)
        acc = bias
        for delta, w in planes:
            if delta == 0:
                src = x0
            elif delta > 0:
                src = pltpu.roll(jnp.where(lane >= delta, x0, xp), LANES - delta, 1)
            else:
                src = pltpu.roll(jnp.where(lane < LANES + delta, x0, xm), -delta, 1)
            acc = acc + src * w
        o_ref[a] = _silu(acc) if silu else acc
        return carry

    lax.fori_loop(0, n_slabs, body, 0, unroll=2 if n_slabs % 2 == 0 else 1)


def _conv_grid_kernel(x_ref, w_ref, b_ref, o_ref, ym_ref, yp_ref, *, width, n_slabs, silu):
    ct = x_ref.shape[1]
    lane = lax.broadcasted_iota(jnp.int32, (ct, LANES), 1)
    col = lane % width
    bias = jnp.zeros((ct, LANES), F32) + b_ref[...]

    def row_sums(a, carry):
        x0 = x_ref[a]
        xl = jnp.where(col >= 1, pltpu.roll(x0, 1, 1), 0.0)
        xr = jnp.where(col < width - 1, pltpu.roll(x0, LANES - 1, 1), 0.0)
        ym_ref[a] = w_ref[0] * xl + w_ref[1] * x0 + w_ref[2] * xr
        o_ref[a] = bias + w_ref[3] * xl + w_ref[4] * x0 + w_ref[5] * xr
        yp_ref[a] = w_ref[6] * xl + w_ref[7] * x0 + w_ref[8] * xr
        return carry

    def combine(a, carry):
        up = jnp.where(a > 0, ym_ref[jnp.maximum(a - 1, 0)], 0.0)
        dn = jnp.where(a < n_slabs - 1, yp_ref[jnp.minimum(a + 1, n_slabs - 1)], 0.0)
        from_up = jnp.where(lane < LANES - width, ym_ref[a], up)
        from_dn = jnp.where(lane >= width, yp_ref[a], dn)
        if 2 * width == LANES:
            y = o_ref[a] + pltpu.roll(from_up + from_dn, width, 1)
        else:
            y = o_ref[a] + pltpu.roll(from_up, width, 1) + pltpu.roll(from_dn, LANES - width, 1)
        o_ref[a] = _silu(y) if silu else y
        return carry

    unroll = 8 if n_slabs % 8 == 0 else 1
    lax.fori_loop(0, n_slabs, row_sums, 0, unroll=unroll)
    lax.fori_loop(0, n_slabs, combine, 0, unroll=unroll)


def _conv(z_s, w9, bias, *, rows, width, n_batch, slab0, chan_lo, chan_n, silu):
    seq = rows * width
    a_n = seq // LANES
    taps = tuple(_conv_taps(rows, width))
    assert all(abs(dr * width + dw) < LANES for dr, dw in taps)
    assert LANES % width == 0 or (rows == 1 and width % LANES == 0)
    tap_ids = [(dr + 1) * 3 + (dw + 1) for dr, dw in taps]
    w_t = jnp.broadcast_to(w9[jnp.array(tap_ids)][:, :, None], (len(taps), chan_n, LANES))
    ct = 64 if rows > 1 else 256
    assert chan_lo % ct == 0 and chan_n % ct == 0 and slab0 % a_n == 0
    nt_ = len(taps)
    if rows > 1:
        assert LANES % width == 0 and nt_ == 9
        body = functools.partial(_conv_grid_kernel, width=width, n_slabs=a_n, silu=silu)
        scratch = [pltpu.VMEM((a_n, ct, LANES), F32), pltpu.VMEM((a_n, ct, LANES), F32)]
    else:
        body = functools.partial(_conv_kernel, taps=taps, width=width, n_slabs=a_n, silu=silu)
        scratch = []
    return pl.pallas_call(
        body,
        scratch_shapes=scratch,
        grid=(n_batch, chan_n // ct),
        in_specs=[pl.BlockSpec((a_n, ct, LANES), lambda b, j: (slab0 // a_n + b, chan_lo // ct + j, 0)),
                  pl.BlockSpec((nt_, ct, LANES), lambda b, j: (0, j, 0)),
                  pl.BlockSpec((ct, 1), lambda b, j: (j, 0))],
        out_specs=pl.BlockSpec((a_n, ct, LANES), lambda b, j: (b, j, 0)),
        out_shape=jax.ShapeDtypeStruct((n_batch * a_n, chan_n, LANES), F32),
        compiler_params=_params(("arbitrary", "arbitrary")),
        name=f"dwconv_{rows}x{width}",
    )(z_s, w_t, bias.reshape(chan_n, 1))


def _dft_consts(a_in, na):
    n = na * LANES
    k = np.arange(na)[:, None]
    a = np.arange(a_in)[None, :]
    ang = 2 * np.pi * (k * a % na) / na
    fa = np.concatenate([np.cos(ang), -np.sin(ang)], axis=0)
    r = np.arange(LANES)
    ang_t = 2 * np.pi * (np.arange(na)[:, None] * r[None, :] % n) / n
    tr, ti = np.cos(ang_t), -np.sin(ang_t)
    ta = np.concatenate([tr, tr], axis=1)
    tb = np.concatenate([-ti, ti], axis=1)
    ang2 = 2 * np.pi * (r[:, None] * r[None, :] % LANES) / LANES
    c2, s2 = np.cos(ang2), np.sin(ang2)
    g2 = np.block([[c2, -s2], [s2, c2]])
    g2i = np.block([[c2, s2], [-s2, c2]])
    ang_i = 2 * np.pi * (np.arange(a_in)[:, None] * np.arange(na)[None, :] % na) / na
    ci, si = np.cos(ang_i) / n, -np.sin(ang_i) / n
    f = lambda v, dt: jnp.asarray(v, dtype=dt)
    return dict(fa=f(fa, F32), ta=f(ta, F32), tb=f(tb, F32), g2=f(g2, F32), g2i=f(g2i, F32),
                ci=f(ci, F32), si=f(si, F32))


def _fwd_slab_stage(m, fa, ta, tb, na):
    pp = jnp.dot(fa, m.astype(BF16), preferred_element_type=F32)
    p = jnp.concatenate([pp[:na], pp[na:]], axis=1)
    return p * ta + _swap_halves(p) * tb


def _cmul(x, kf):
    kr, ki = kf[..., :LANES], kf[..., LANES:]
    ka = jnp.concatenate([kr, kr], axis=-1)
    kb = jnp.concatenate([-ki, ki], axis=-1)
    return x * ka + _swap_halves(x) * kb


def _chan_load(ref, c):
    n, cb, _ = ref.shape
    return ref.reshape(n * cb, LANES)[pl.ds(c, n, stride=cb), :]


def _chan_store(ref, c, val):
    ref[:, c, :] = val


def _dot_f32ish_k(w, h):
    wh, wm, _ = _split3(w)
    hh, hm, _ = _split3(h)
    lhs = jnp.concatenate([wh, wh, wm], axis=1)
    rhs = jnp.concatenate([hh, hm, hh], axis=0)
    return jnp.dot(lhs, rhs, preferred_element_type=F32)


def _taps_kernel(bands_ref, w1t_ref, w1c_ref, w1s_ref, b1_ref, w2_ref, b2_ref, fq_ref, w3_ref, dec_ref, o_ref,
                 *, seq, a_seq, na, spb):
    step = pl.program_id(0)
    lane = lax.broadcasted_iota(jnp.int32, (1, LANES), 1)
    fq = fq_ref[...]
    n_total = na * LANES
    for i in range(spb):
        a = step * spb + i
        is_f = a < a_seq
        is_b = a >= na - a_seq
        live = jnp.logical_or(is_f, is_b)

        @pl.when(live)
        def _():
            n = a * LANES + lane
            pos = jnp.where(is_f, n, n_total - n)
            valid = (n > jnp.where(is_f, -1, n_total - seq)) & (n < jnp.where(is_f, seq, n_total))
            t = pos.astype(F32) / seq
            ang = ((2 * math.pi) * t) * bands_ref[...]
            pre = (w1t_ref[...] * t + _dot_f32ish(w1c_ref[...], jnp.cos(ang))
                   + _dot_f32ish(w1s_ref[...], jnp.sin(ang)))
            h = jnp.sin(fq * (pre + b1_ref[...]))
            h = jnp.sin(fq * (_dot_f32ish(w2_ref[...], h) + b2_ref[...]))
            d = jnp.where(is_f, 0, 1)
            k = _dot_f32ish_k(w3_ref[d], h) * jnp.exp(-t * jnp.abs(dec_ref[d]))
            o_ref[i] = jnp.where(valid, k, 0.0)

        @pl.when(jnp.logical_not(live))
        def _():
            o_ref[i] = jnp.zeros(o_ref.shape[1:], F32)


def _hyena_taps(seq, na, lp):
    nc = HY_ORDER * D_BR
    hid = lp["hy_f_w2"].shape[0]
    w1 = lp["hy_f_w1"]
    col = lambda v: v.reshape(-1, 1)
    bands = col(jnp.linspace(1e-4, HY_BANDS - 1, HY_BANDS, dtype=F32))
    w3 = jnp.transpose(lp["hy_f_w3"].T.reshape(HY_ORDER, 2, D_BR, hid), (1, 0, 2, 3)).reshape(2, nc, hid)
    dec = jnp.broadcast_to(jnp.transpose(lp["hy_decay"], (1, 0, 2)).reshape(2, nc, 1), (2, nc, LANES))
    spb = min(8, na)
    assert na % spb == 0
    args = (bands, col(w1[0]), w1[1:1 + HY_BANDS].T, w1[1 + HY_BANDS:].T, col(lp["hy_f_b1"]), lp["hy_f_w2"].T,
            col(lp["hy_f_b2"]), col(lp["hy_f_freq"]), w3, dec)
    full = lambda v: pl.BlockSpec(v.shape, lambda s: (0,) * v.ndim)
    return pl.pallas_call(
        functools.partial(_taps_kernel, seq=seq, a_seq=seq // LANES, na=na, spb=spb),
        grid=(na // spb,),
        in_specs=[full(v) for v in args],
        out_specs=pl.BlockSpec((spb, nc, LANES), lambda s: (s, 0, 0)),
        out_shape=jax.ShapeDtypeStruct((na, nc, LANES), F32),
        compiler_params=_params(("arbitrary",)),
        name=f"hyena_filter_taps_{na}",
    )(*args)


def _hyena_kernel(v_ref, x1_ref, x2_ref, k0_ref, k1_ref, skip_ref, fa_ref, faf_ref, ta_ref, tb_ref, g2_ref, g2i_ref,
                  ci_ref, si_ref, o_ref, p_buf, z_buf, kf_buf, *, a_in, na):
    fa, ta, tb = fa_ref[...].astype(BF16), ta_ref[...], tb_ref[...]
    ci, si = ci_ref[...].astype(BF16), si_ref[...].astype(BF16)

    @pl.when(pl.program_id(1) == 0)
    def _():
        faf = faf_ref[...].astype(BF16)
        for order, k_ref in enumerate((k0_ref, k1_ref)):
            scales = []
            for c in range(CB):
                m = _chan_load(k_ref, c)
                ss = jnp.sum(jnp.sum(m * m, axis=1, keepdims=True), axis=0, keepdims=True)
                scales.append(lax.rsqrt(ss + EPS))
                p_buf[c] = _fwd_slab_stage(m, faf, ta, tb, na)
            x = _bdot(p_buf[...].reshape(CB * na, 2 * LANES), g2_ref[...]).reshape(CB, na, 2 * LANES)
            for c in range(CB):
                kf_buf[order, c] = x[c] * scales[c]

    def spectral(order):
        x = _bdot(p_buf[...].reshape(CB * na, 2 * LANES), g2_ref[...])
        y = _cmul(x, kf_buf[order].reshape(CB * na, 2 * LANES))
        bm = _bdot(y, g2i_ref[...]).reshape(CB, na, 2 * LANES)
        p_buf[...] = bm * ta - _swap_halves(bm) * tb

    def conv_out(c):
        bb = p_buf[c]
        return (jnp.dot(ci, bb[:, :LANES].astype(BF16), preferred_element_type=F32)
                + jnp.dot(si, bb[:, LANES:].astype(BF16), preferred_element_type=F32))

    for c in range(CB):
        p_buf[c] = _fwd_slab_stage(_chan_load(v_ref, c), fa, ta, tb, na)
    spectral(0)
    for c in range(CB):
        z_buf[c] = _chan_load(x1_ref, c) * (conv_out(c) + _chan_load(v_ref, c) * skip_ref[0, c])
    for c in range(CB):
        p_buf[c] = _fwd_slab_stage(z_buf[c], fa, ta, tb, na)
    spectral(1)
    for c in range(CB):
        _chan_store(o_ref, c, _chan_load(x2_ref, c) * (conv_out(c) + z_buf[c] * skip_ref[1, c]))


def _hyena(u_s, taps_s, skip, *, a_in, na, n_batch):
    cs = _dft_consts(a_in, na)
    faf = _dft_consts(na, na)["fa"]
    nblk = D_BR // CB
    const = lambda shp: pl.BlockSpec(shp, lambda j, b: (0,) * len(shp))
    skip_b = jnp.broadcast_to(skip[:, :, None, None], (HY_ORDER, D_BR, 1, LANES))
    return pl.pallas_call(
        functools.partial(_hyena_kernel, a_in=a_in, na=na),
        grid=(nblk, n_batch),
        in_specs=[pl.BlockSpec((a_in, CB, LANES), lambda j, b: (b, j, 0)),
                  pl.BlockSpec((a_in, CB, LANES), lambda j, b: (b, nblk + j, 0)),
                  pl.BlockSpec((a_in, CB, LANES), lambda j, b: (b, 2 * nblk + j, 0)),
                  pl.BlockSpec((na, CB, LANES), lambda j, b: (0, j, 0)),
                  pl.BlockSpec((na, CB, LANES), lambda j, b: (0, nblk + j, 0)),
                  pl.BlockSpec((HY_ORDER, CB, 1, LANES), lambda j, b: (0, j, 0, 0)),
                  const((2 * na, a_in)), const((2 * na, na)), const((na, 2 * LANES)), const((na, 2 * LANES)),
                  const((2 * LANES, 2 * LANES)), const((2 * LANES, 2 * LANES)),
                  const((a_in, na)), const((a_in, na))],
        out_specs=pl.BlockSpec((a_in, CB, LANES), lambda j, b: (b, j, 0)),
        out_shape=jax.ShapeDtypeStruct((n_batch * a_in, D_BR, LANES), F32),
        scratch_shapes=[pltpu.VMEM((CB, na, 2 * LANES), F32), pltpu.VMEM((CB, a_in, LANES), F32),
                        pltpu.VMEM((HY_ORDER, CB, na, 2 * LANES), F32)],
        compiler_params=_params(("arbitrary", "arbitrary")),
        name=f"hyena_longconv_{a_in}",
    )(u_s, u_s, u_s, taps_s, taps_s, skip_b, cs["fa"], faf, cs["ta"], cs["tb"], cs["g2"], cs["g2i"],
      cs["ci"], cs["si"])


def _slabs_to_rows(ref, n):
    return jnp.concatenate([ref[a].T for a in range(n)], axis=0)


def _hyena_short_kernel(v_ref, x1_ref, x2_ref, k0_ref, k1_ref, skip_ref, f_ref, g_ref, o_ref, *, a_n, a_k):
    seq, nf = a_n * LANES, a_k * LANES
    ff = f_ref[...].astype(BF16)
    gi = g_ref[...].astype(BF16)
    v, x1, x2 = _slabs_to_rows(v_ref, a_n), _slabs_to_rows(x1_ref, a_n), _slabs_to_rows(x2_ref, a_n)

    def longconv(u, k_ref):
        k = _slabs_to_rows(k_ref, a_k)
        s = lax.rsqrt(jnp.sum(k * k, axis=0, keepdims=True) + EPS)
        kf = jnp.dot(ff, k.astype(BF16), preferred_element_type=F32) * s
        x = jnp.dot(ff[:, :seq], u.astype(BF16), preferred_element_type=F32)
        xr, xi, kr, ki = x[:nf], x[nf:], kf[:nf], kf[nf:]
        y = jnp.concatenate([xr * kr - xi * ki, xr * ki + xi * kr], axis=0)
        return jnp.dot(gi, y.astype(BF16), preferred_element_type=F32)

    z = x1 * (longconv(v, k0_ref) + v * skip_ref[0])
    o_ref[...] = x2 * (longconv(z, k1_ref) + z * skip_ref[1])


def _hyena_short(u_s, taps_s, skip, *, a_n, n_batch):
    a_k = 2 * a_n
    seq, nf = a_n * LANES, a_k * LANES
    k = np.arange(nf)
    ang = 2 * np.pi * (k[:, None] * k[None, :] % nf) / nf
    f = np.concatenate([np.cos(ang), -np.sin(ang)], axis=0)
    g = np.concatenate([np.cos(ang[:seq]), -np.sin(ang[:seq])], axis=1) / nf
    nblk = D_BR // LANES
    const = lambda shp: pl.BlockSpec(shp, lambda b, j: (0,) * len(shp))
    return pl.pallas_call(
        functools.partial(_hyena_short_kernel, a_n=a_n, a_k=a_k),
        grid=(n_batch, nblk),
        in_specs=[pl.BlockSpec((a_n, LANES, LANES), lambda b, j: (b, j, 0)),
                  pl.BlockSpec((a_n, LANES, LANES), lambda b, j: (b, nblk + j, 0)),
                  pl.BlockSpec((a_n, LANES, LANES), lambda b, j: (b, 2 * nblk + j, 0)),
                  pl.BlockSpec((a_k, LANES, LANES), lambda b, j: (0, j, 0)),
                  pl.BlockSpec((a_k, LANES, LANES), lambda b, j: (0, nblk + j, 0)),
                  pl.BlockSpec((HY_ORDER, 1, LANES), lambda b, j: (0, 0, j)),
                  const((2 * nf, nf)), const((seq, 2 * nf))],
        out_specs=pl.BlockSpec((None, seq, LANES), lambda b, j: (b, 0, j)),
        out_shape=jax.ShapeDtypeStruct((n_batch, seq, D_BR), F32),
        compiler_params=_params(("arbitrary", "arbitrary")),
        name="hyena_short",
    )(u_s, u_s, u_s, taps_s, taps_s, skip.reshape(HY_ORDER, 1, D_BR), jnp.asarray(f, F32), jnp.asarray(g, F32))


def _chan_dft_mats():
    r = np.arange(LANES)
    ang = 2 * np.pi * (r[:, None] * r[None, :] % LANES) / LANES
    return np.cos(ang), np.sin(ang)


def _fn_mix_kernel(u_ref, cs_ref, o_ref, *, n_slabs):
    w = cs_ref[...].astype(BF16)

    def body(a, carry):
        o_ref[a] = jnp.dot(w, u_ref[a].astype(BF16), preferred_element_type=F32)
        return carry

    lax.fori_loop(0, n_slabs, body, 0)


def _fn_mix(z_s, *, a_n, n_batch, chan_lo):
    c, s = _chan_dft_mats()
    w = jnp.asarray(np.concatenate([c, s], axis=0), dtype=F32)
    g0 = chan_lo // LANES
    return pl.pallas_call(
        functools.partial(_fn_mix_kernel, n_slabs=a_n),
        grid=(n_batch, FN_GROUPS),
        in_specs=[pl.BlockSpec((a_n, LANES, LANES), lambda b, g: (b, g0 + g, 0)),
                  pl.BlockSpec((2 * LANES, LANES), lambda b, g: (0, 0))],
        out_specs=pl.BlockSpec((a_n, 2 * LANES, LANES), lambda b, g: (b, g, 0)),
        out_shape=jax.ShapeDtypeStruct((n_batch * a_n, 2 * D_BR, LANES), F32),
        compiler_params=_params(("arbitrary", "arbitrary")),
        name="fnet_channel_dft",
    )(z_s, w)


def _fn_seq_kernel(p_ref, q_ref, fa_ref, tr_ref, ti_ref, g_ref, o_ref, a_buf, *, a_n, scale):
    fa, tr, ti = fa_ref[...].astype(BF16), tr_ref[...], ti_ref[...]
    for c in range(CB):
        r1 = jnp.dot(fa, _chan_load(p_ref, c).astype(BF16), preferred_element_type=F32)
        r2 = jnp.dot(fa, _chan_load(q_ref, c).astype(BF16), preferred_element_type=F32)
        ar = r1[:a_n] - r2[a_n:]
        ai = -(r2[:a_n] + r1[a_n:])
        a_buf[c] = jnp.concatenate([ar * tr - ai * ti, ar * ti + ai * tr], axis=1)
    y = _bdot(a_buf[...].reshape(CB * a_n, 2 * LANES), g_ref[...]) * scale
    o_ref[...] = y.reshape(CB, a_n, LANES)


def _fn_seq(pq, *, a_n, n_batch):
    seq = a_n * LANES
    k = np.arange(a_n)
    ang = 2 * np.pi * (k[:, None] * k[None, :] % a_n) / a_n
    fa = np.concatenate([np.cos(ang), np.sin(ang)], axis=0)
    r = np.arange(LANES)
    ang_t = 2 * np.pi * (k[:, None] * r[None, :] % seq) / seq
    c2, s2 = _chan_dft_mats()
    g = np.concatenate([c2, s2], axis=0)
    nblk = LANES // CB
    const = lambda shp: pl.BlockSpec(shp, lambda b, j: (0,) * len(shp))

    def chan_blk(j, off):
        return (j // nblk) * (2 * nblk) + off * nblk + j % nblk

    return pl.pallas_call(
        functools.partial(_fn_seq_kernel, a_n=a_n, scale=1.0 / math.sqrt(seq * LANES)),
        grid=(n_batch, D_BR // CB),
        in_specs=[pl.BlockSpec((a_n, CB, LANES), lambda b, j: (b, chan_blk(j, 0), 0)),
                  pl.BlockSpec((a_n, CB, LANES), lambda b, j: (b, chan_blk(j, 1), 0)),
                  const((2 * a_n, a_n)), const((a_n, LANES)), const((a_n, LANES)), const((2 * LANES, LANES))],
        out_specs=pl.BlockSpec((None, CB, a_n, LANES), lambda b, j: (b, j, 0, 0)),
        out_shape=jax.ShapeDtypeStruct((n_batch, D_BR, a_n, LANES), F32),
        scratch_shapes=[pltpu.VMEM((CB, a_n, 2 * LANES), F32)],
        compiler_params=_params(("arbitrary", "arbitrary")),
        name="fnet_sequence_dft",
    )(pq, pq, jnp.asarray(fa, F32), jnp.asarray(np.cos(ang_t), F32), jnp.asarray(-np.sin(ang_t), F32),
      jnp.asarray(g, F32))


def _fn_small_kernel(u_ref, cw_ref, sw_ref, cl_ref, sl_ref, o_ref, *, a_n, scale):
    u = jnp.concatenate([u_ref[a].T for a in range(a_n)], axis=0)
    p = _bdot(u, cw_ref[...])
    q = _bdot(u, sw_ref[...])
    o_ref[...] = (_bdot(cl_ref[...], p) - _bdot(sl_ref[...], q)) * scale


def _fn_small(z_s, *, a_n, n_batch, slab0, chan_lo):
    seq = a_n * LANES
    cw, sw = _chan_dft_mats()
    n = np.arange(seq)
    ang = 2 * np.pi * (n[:, None] * n[None, :] % seq) / seq
    const = lambda shp: pl.BlockSpec(shp, lambda b, g: (0,) * len(shp))
    g0 = chan_lo // LANES
    return pl.pallas_call(
        functools.partial(_fn_small_kernel, a_n=a_n, scale=1.0 / math.sqrt(seq * LANES)),
        grid=(n_batch, FN_GROUPS),
        in_specs=[pl.BlockSpec((a_n, LANES, LANES), lambda b, g: (slab0 // a_n + b, g0 + g, 0)),
                  const((LANES, LANES)), const((LANES, LANES)), const((seq, seq)), const((seq, seq))],
        out_specs=pl.BlockSpec((None, seq, LANES), lambda b, g: (b, 0, g)),
        out_shape=jax.ShapeDtypeStruct((n_batch, seq, D_BR), F32),
        compiler_params=_params(("arbitrary", "arbitrary")),
        name="fnet_short",
    )(z_s, jnp.asarray(cw, F32), jnp.asarray(sw, F32), jnp.asarray(np.cos(ang), F32), jnp.asarray(np.sin(ang), F32))


def _log_sigmoid(x):
    return jnp.minimum(x, 0.0) - jnp.log(1.0 + jnp.exp(-jnp.abs(x)))


def _exact_tri_dot(tri, x, tri_on_left):
    h, m, l = _split3(x)
    if tri_on_left:
        d = lambda p: jnp.dot(tri, p, preferred_element_type=F32)
    else:
        d = lambda p: jnp.dot(p, tri, preferred_element_type=F32)
    return d(h) + d(m) + d(l)


def _mlstm_step(inputs, c_st, n_st, m_st):
    t = hd = LANES
    n_dir, n_batch = len(inputs), len(inputs[0])
    n_grp = n_dir * n_batch * ML_HEADS
    row = lax.broadcasted_iota(jnp.int32, (t, t), 0)
    col = lax.broadcasted_iota(jnp.int32, (t, t), 1)
    tri = jnp.where(col <= row, 1.0, 0.0).astype(BF16)
    tri_t = jnp.where(col >= row, 1.0, 0.0).astype(BF16)
    qs, ks, vs, bcs, brs, ics, irs = [], [], [], [], [], [], []
    for d in range(n_dir):
        i_off = 2 * ML_HEADS * d
        f_off = i_off + ML_HEADS
        for b in range(n_batch):
            q_all, k_all, v_all, g = inputs[d][b]
            gt = g.T
            lf_c = _log_sigmoid(g)
            lf_r = lf_c.T
            if d == 1:
                b_c = _exact_tri_dot(tri_t, lf_c, True)
                b_r = _exact_tri_dot(tri, lf_r, False)
            else:
                b_c = _exact_tri_dot(tri, lf_c, True)
                b_r = _exact_tri_dot(tri_t, lf_r, False)
            for h in range(ML_HEADS):
                sl = slice(h * hd, (h + 1) * hd)
                qs.append(q_all[:, sl])
                ks.append(k_all[:, sl])
                vs.append(v_all[:, sl])
                bcs.append(b_c[:, f_off + h:f_off + h + 1])
                brs.append(b_r[f_off + h:f_off + h + 1, :])
                ics.append(g[:, i_off + h:i_off + h + 1])
                irs.append(gt[i_off + h:i_off + h + 1, :])
    q = jnp.stack(qs) * (hd ** -0.5)
    k, v = jnp.stack(ks), jnp.stack(vs)
    bc, br, ic, ir = jnp.stack(bcs), jnp.stack(brs), jnp.stack(ics), jnp.stack(irs)
    m_prev = m_st[...][:, :, :1]
    ct = c_st[...]
    n_prev = n_st[...]

    shp = (n_grp, t, t)
    grp = lax.broadcasted_iota(jnp.int32, shp, 0)
    r3, c3 = lax.broadcasted_iota(jnp.int32, shp, 1), lax.broadcasted_iota(jnp.int32, shp, 2)
    back = grp >= (n_grp // n_dir)
    mask = (back & (c3 >= r3)) | (jnp.logical_not(back) & (c3 <= r3))
    bdot = lambda a, b_, ca, cb: lax.dot_general(a.astype(BF16), b_.astype(BF16), (((ca,), (cb,)), ((0,), (0,))),
                                                 preferred_element_type=F32)
    dm = jnp.where(mask, bc - br + ir, -jnp.inf)
    inter = bc + m_prev
    m_row = jnp.maximum(inter, jnp.max(dm, axis=-1, keepdims=True))
    w_intra = jnp.exp(dm - m_row)
    w_inter = jnp.exp(inter - m_row)
    s = bdot(q, k, 2, 2) * w_intra
    num = bdot(s, v, 2, 1) + w_inter * bdot(q, ct, 2, 1)
    den = jnp.sum(s, axis=-1, keepdims=True) + w_inter * jnp.sum(q * n_prev, axis=-1, keepdims=True)
    den = jnp.maximum(jnp.abs(den), jnp.exp(-m_row))
    h_all = num / den

    is_back = lax.broadcasted_iota(jnp.int32, (n_grp, 1, 1), 0) >= (n_grp // n_dir)
    b_tot_c = jnp.where(is_back, bc[:, :1, :], bc[:, t - 1:, :])
    b_tot_r = jnp.where(is_back, br[:, :, :1], br[:, :, t - 1:])
    a_c = b_tot_c - bc + ic
    a_r = b_tot_r - br + ir
    m_new = jnp.maximum(b_tot_c + m_prev, jnp.max(a_r, axis=-1, keepdims=True))
    sc = jnp.exp(a_c - m_new)
    decay = jnp.exp(b_tot_c + m_prev - m_new)
    k_sc = k * sc
    c_st[...] = decay * ct + bdot(k_sc, v, 1, 1)
    n_st[...] = decay * n_prev + jnp.sum(k_sc, axis=1, keepdims=True)
    m_st[...] = jnp.broadcast_to(m_new, (n_grp, 1, LANES))

    out = []
    for d in range(n_dir):
        out.append([jnp.concatenate([h_all[(d * n_batch + b) * ML_HEADS + h] for h in range(ML_HEADS)], axis=1)
                    for b in range(n_batch)])
    return out


def _mlstm_kernel(*refs, n_batch, ctx_chunks):
    lat_f, lat_b, ctx_f, ctx_b = refs[0:4], refs[4:8], refs[8:12], refs[12:16]
    hf_lat, hb_lat, hf_ctx, hb_ctx, c_st, n_st, m_st = refs[16:]
    j = pl.program_id(0)
    is_ctx = j < ctx_chunks

    @pl.when(j == 0)
    def _():
        c_st[...] = jnp.zeros(c_st.shape, F32)
        n_st[...] = jnp.zeros(n_st.shape, F32)
        m_st[...] = jnp.zeros(m_st.shape, F32)

    pick = lambda c_refs, l_refs, b: tuple(jnp.where(is_ctx, c[b], l[b]) for c, l in zip(c_refs, l_refs))
    inputs = [[pick(ctx_f, lat_f, b) for b in range(n_batch)], [pick(ctx_b, lat_b, b) for b in range(n_batch)]]
    hf, hb = _mlstm_step(inputs, c_st, n_st, m_st)

    @pl.when(is_ctx)
    def _():
        for b in range(n_batch):
            hf_ctx[b] = hf[b]
            hb_ctx[b] = hb[b]

    @pl.when(jnp.logical_not(is_ctx))
    def _():
        for b in range(n_batch):
            hf_lat[b] = hf[b]
            hb_lat[b] = hb[b]


def _mlstm(qk_lat, z_lat, qk_ctx, z_ctx, *, n_batch, v_col, g_col):
    lat_len, ctx_len = qk_lat.shape[0] // n_batch, qk_ctx.shape[0] // n_batch
    nlc, ncc = lat_len // LANES, ctx_len // LANES
    r3 = lambda a: a.reshape(n_batch, a.shape[0] // n_batch, a.shape[1])
    lf = lambda j: jnp.maximum(j - ncc, 0)
    lb = lambda j: jnp.where(j < ncc, nlc - 1, nlc - 1 - (j - ncc))
    cf = lambda j: jnp.minimum(j, ncc - 1)
    cb = lambda j: jnp.where(j < ncc, ncc - 1 - j, 0)

    def specs(ix):
        blk = lambda w, cidx: pl.BlockSpec((n_batch, LANES, w), lambda j: (0, ix(j), cidx))
        return [blk(D_BR, 0), blk(D_BR, 1), blk(D_BR, v_col), blk(LANES, g_col)]

    out = lambda ix: pl.BlockSpec((n_batch, LANES, D_BR), lambda j: (0, ix(j), 0))
    sd = lambda n: jax.ShapeDtypeStruct((n_batch, n, D_BR), F32)
    ql, zl, qc, zc = r3(qk_lat), r3(z_lat), r3(qk_ctx), r3(z_ctx)
    hf_lat, hb_lat, hf_ctx, hb_ctx = pl.pallas_call(
        functools.partial(_mlstm_kernel, n_batch=n_batch, ctx_chunks=ncc),
        grid=(ncc + nlc,),
        in_specs=specs(lf) + specs(lb) + specs(cf) + specs(cb),
        out_specs=[out(lf), out(lb), out(cf), out(cb)],
        out_shape=[sd(lat_len), sd(lat_len), sd(ctx_len), sd(ctx_len)],
        scratch_shapes=[pltpu.VMEM((2 * n_batch * ML_HEADS, LANES, LANES), F32),
                        pltpu.VMEM((2 * n_batch * ML_HEADS, 1, LANES), F32),
                        pltpu.VMEM((2 * n_batch * ML_HEADS, 1, LANES), F32)],
        compiler_params=_params(("arbitrary",)),
        name="mlstm_bidir",
    )(ql, ql, zl, zl, ql, ql, zl, zl, qc, qc, zc, zc, qc, qc, zc, zc)
    flat = lambda a: a.reshape(a.shape[0] * a.shape[1], D_BR)
    return (flat(hf_lat), flat(hb_lat)), (flat(hf_ctx), flat(hb_ctx))


def _rms_mod(x, w, shift, scale):
    y = x * lax.rsqrt(jnp.mean(x * x, axis=-1, keepdims=True) + EPS) * w
    return y * (1.0 + scale) + shift


def _route(t, rw, rb):
    logits = _dot_f32ish(t, rw) + rb
    col = lax.broadcasted_iota(jnp.int32, logits.shape, 1)
    big = jnp.int32(1 << 20)
    ninf = -jnp.inf
    is_g = col < MOE_GROUPS
    gl = jnp.where(is_g, logits, ninf)
    gmax = jnp.max(gl, axis=-1, keepdims=True)
    g_sel = jnp.min(jnp.where(is_g & (gl == gmax), col, big), axis=-1, keepdims=True)
    p_top = 1.0 / jnp.sum(jnp.where(is_g, jnp.exp(gl - gmax), 0.0), axis=-1, keepdims=True)
    lo = MOE_GROUPS + g_sel * MOE_PER_GROUP
    in_grp = (col >= lo) & (col < lo + MOE_PER_GROUP)
    e1v = jnp.where(in_grp, logits, ninf)
    top1 = jnp.max(e1v, axis=-1, keepdims=True)
    idx1 = jnp.min(jnp.where(in_grp & (e1v == top1), col, big), axis=-1, keepdims=True)
    e2v = jnp.where(col == idx1, ninf, e1v)
    top2 = jnp.max(e2v, axis=-1, keepdims=True)
    idx2 = jnp.min(jnp.where(in_grp & (col != idx1) & (e2v == top2), col, big), axis=-1, keepdims=True)
    ex = jnp.exp(top2 - top1)
    s1 = 1.0 / (1.0 + ex)
    return jnp.where(col == idx1, p_top * s1, 0.0) + jnp.where(col == idx2, p_top * (ex * s1), 0.0)


def _merge_kernel(yh_ref, yf_ref, hf_ref, hb_ref, o_ref, g0_ref, g1_ref, g2_ref, x_ref, gate_ref,
                  wb_ref, wo_ref, nw_ref, n2_ref, sh_ref, sc_ref, rw_ref, rb_ref, out_ref, xn_ref, comb_ref):
    hd = LANES
    h = hf_ref[...] + hb_ref[...]
    parts = []
    for i in range(ML_HEADS):
        hh = h[:, i * hd:(i + 1) * hd]
        parts.append(hh * lax.rsqrt(jnp.mean(hh * hh, axis=-1, keepdims=True) + EPS))
    y_ml = jax.nn.sigmoid(o_ref[...]) * (jnp.concatenate(parts, axis=1) * nw_ref[...])
    acc = g0_ref[...].astype(F32) * _bdot(yh_ref[...], wb_ref[0])
    acc = acc + g1_ref[...].astype(F32) * _bdot(yf_ref[...], wb_ref[1])
    acc = acc + g2_ref[...].astype(F32) * _bdot(y_ml, wb_ref[2])
    x_new = x_ref[...] + gate_ref[...] * _bdot(acc, wo_ref[...])
    out_ref[...] = x_new
    t = _rms_mod(x_new, n2_ref[...], sh_ref[...], sc_ref[...])
    xn_ref[...] = t.astype(BF16)
    comb_ref[...] = _route(t, rw_ref[...], rb_ref[...])


def _merge(yh, yf, hf, hb, z_tm, gates, x, mods3, wb, wo, nw, n2w, rw, rb, *, seg, tm, o_col):
    nt, d = x.shape
    tok = lambda w, cidx: pl.BlockSpec((tm, w), lambda i: (i, cidx))
    mod = lambda k: pl.BlockSpec((None, 1, d), lambda i: (seg(i), 0, k))
    return pl.pallas_call(
        _merge_kernel,
        grid=(nt // tm,),
        in_specs=[tok(D_BR, 0), tok(D_BR, 0), tok(D_BR, 0), tok(D_BR, 0),
                  tok(D_BR, o_col),
                  tok(d, 0), tok(d, 1), tok(d, 2),
                  tok(d, 0),
                  mod(2),
                  pl.BlockSpec((3, D_BR, d), lambda i: (0, 0, 0)),
                  pl.BlockSpec((d, d), lambda i: (0, 0)),
                  pl.BlockSpec((1, D_BR), lambda i: (0, 0)),
                  pl.BlockSpec((1, d), lambda i: (0, 0)),
                  mod(3), mod(4),
                  pl.BlockSpec((d, LANES), lambda i: (0, 0)),
                  pl.BlockSpec((1, LANES), lambda i: (0, 0))],
        out_specs=[tok(d, 0), tok(d, 0), tok(LANES, 0)],
        out_shape=[jax.ShapeDtypeStruct((nt, d), F32), jax.ShapeDtypeStruct((nt, d), BF16),
                   jax.ShapeDtypeStruct((nt, LANES), F32)],
        compiler_params=_params(("arbitrary",)),
        name="merge_branches_router",
    )(yh, yf, hf, hb, z_tm, gates, gates, gates, x, mods3, wb, wo, nw.reshape(1, D_BR), n2w.reshape(1, d),
      mods3, mods3, rw, rb)


def _moe_kernel(xn_ref, comb_ref, wg_ref, wu_ref, wd_ref, x_ref, gate_ref, nw_ref, sh_ref, sc_ref, *out_and_scratch,
                final, n_keep):
    acc_ref = out_and_scratch[-1]
    e = pl.program_id(1)

    @pl.when(e == 0)
    def _():
        acc_ref[...] = jnp.zeros(acc_ref.shape, F32)

    xn = xn_ref[...]
    comb = comb_ref[...]
    col = lax.broadcasted_iota(jnp.int32, comb.shape, 1)
    acts = []
    for i in range(MOE_PER_GROUP):
        cw = jnp.sum(jnp.where(col == e * MOE_PER_GROUP + i + MOE_GROUPS, comb, 0.0), axis=-1, keepdims=True)
        hg = jnp.dot(xn, wg_ref[i], preferred_element_type=F32)
        hu = jnp.dot(xn, wu_ref[i], preferred_element_type=F32)
        acts.append(((hg * jax.nn.sigmoid(hg)) * hu * cw).astype(BF16))
    wd = wd_ref[...].reshape(MOE_PER_GROUP * EXPERT_HID, wd_ref.shape[-1])
    acc_ref[...] += jnp.dot(jnp.concatenate(acts, axis=1), wd, preferred_element_type=F32)

    n_steps = MOE_EXPERTS // MOE_PER_GROUP
    if final:
        y_ref, = out_and_scratch[:-1]

        @pl.when((e == n_steps - 1) & (pl.program_id(0) < n_keep))
        def _():
            x_new = x_ref[...] + gate_ref[...] * acc_ref[...]
            y_ref[...] = x_new * lax.rsqrt(jnp.mean(x_new * x_new, axis=-1, keepdims=True) + EPS) * nw_ref[...]
    else:
        o_ref, xn_next_ref = out_and_scratch[:-1]

        @pl.when(e == n_steps - 1)
        def _():
            x_new = x_ref[...] + gate_ref[...] * acc_ref[...]
            o_ref[...] = x_new
            xn_next_ref[...] = _rms_mod(x_new, nw_ref[...], sh_ref[...], sc_ref[...]).astype(BF16)


def _moe(xn, comb, wg, wu, wd, x, mods3, post_w, post_mods3, *, seg, tm, final, n_keep_rows):
    nt, d = x.shape
    n_keep = n_keep_rows // tm
    tok = pl.BlockSpec((tm, d), lambda i, e: (i, 0))
    if final:
        out_specs = [pl.BlockSpec((tm, d), lambda i, e: (jnp.minimum(i, n_keep - 1), 0))]
        out_shape = [jax.ShapeDtypeStruct((n_keep_rows, d), F32)]
    else:
        out_specs = [tok, tok]
        out_shape = [jax.ShapeDtypeStruct((nt, d), F32), jax.ShapeDtypeStruct((nt, d), BF16)]
    mod = lambda k: pl.BlockSpec((None, 1, d), lambda i, e: (seg(i), 0, k))
    return pl.pallas_call(
        functools.partial(_moe_kernel, final=final, n_keep=n_keep),
        grid=(nt // tm, MOE_EXPERTS // MOE_PER_GROUP),
        in_specs=[tok,
                  pl.BlockSpec((tm, LANES), lambda i, e: (i, 0)),
                  pl.BlockSpec((MOE_PER_GROUP, d, EXPERT_HID), lambda i, e: (e, 0, 0)),
                  pl.BlockSpec((MOE_PER_GROUP, d, EXPERT_HID), lambda i, e: (e, 0, 0)),
                  pl.BlockSpec((MOE_PER_GROUP, EXPERT_HID, d), lambda i, e: (e, 0, 0)),
                  tok,
                  mod(5),
                  pl.BlockSpec((1, d), lambda i, e: (0, 0)),
                  mod(0), mod(1)],
        out_specs=out_specs,
        out_shape=out_shape,
        scratch_shapes=[pltpu.VMEM((tm, d), F32)],
        compiler_params=_params(("arbitrary", "arbitrary")),
        name="moe_experts_final" if final else "moe_experts",
    )(xn, comb, wg, wu, wd, x, mods3, post_w.reshape(1, d), post_mods3, post_mods3)


def _slab_to_tm(y_s):
    ns, c, _ = y_s.shape
    return jnp.transpose(y_s, (0, 2, 1)).reshape(ns * LANES, c)


def kernel(x, c, ctx, c_ctx, ada_w, ada_b, norm1_w, norm2_w, w_in, b_in, hy_conv_w, hy_conv_b, hy_f_w1, hy_f_b1, hy_f_w2, hy_f_b2, hy_f_w3, hy_f_freq, hy_decay, hy_skip, ml_conv_w, ml_conv_b, ml_norm_w, w_branch, w_out, moe_rg_w, moe_rg_b, moe_re_w, moe_re_b, moe_w_gate, moe_w_up, moe_w_down, norm_f_w):
    nb, seq, d = x.shape
    lc = ctx.shape[1]
    depth = ada_w.shape[0]
    assert d == D_MODEL and seq % (GRID_W * 2) == 0 and lc % LANES == 0 and nb + 1 <= 8
    rows = seq // GRID_W
    a_lat = seq // LANES
    a_ctx = lc // LANES
    n_lat, n_ctx = nb * seq, nb * lc
    tm = 256
    tm_moe = {"lat": _pick(seq, (1024, 512, 256)), "ctx": _pick(n_ctx, (512, 256))}
    tm_mrg = {"lat": _pick(seq, (512, 256)), "ctx": _pick(n_ctx, (512, 256))}
    assert seq % tm == 0 and n_ctx % tm == 0
    seg_of = lambda s, t: (lambda i: i // (seq // t)) if s == "lat" else (lambda i: nb)
    streams = ("lat", "ctx")
    xs = {"lat": x.reshape(n_lat, d), "ctx": ctx.reshape(n_ctx, d)}
    xn = {}
    cvec = jnp.zeros((8, d), F32).at[:nb].set(c).at[nb].set(c_ctx)
    mods = _mods(cvec, ada_w, ada_b)

    o_fn, o_ml, o_mlg, o_gate = 3 * D_BR, 4 * D_BR, 8 * D_BR, 8 * D_BR + 4 * ML_HEADS
    pad_g = LANES - 4 * ML_HEADS

    for l in range(depth):
        lp = {"hy_f_w1": hy_f_w1[l], "hy_f_b1": hy_f_b1[l], "hy_f_w2": hy_f_w2[l], "hy_f_b2": hy_f_b2[l],
              "hy_f_w3": hy_f_w3[l], "hy_f_freq": hy_f_freq[l], "hy_decay": hy_decay[l]}
        mods3 = mods[l].reshape(8, 1, 6 * d)
        wl, bl = w_in[l], b_in[l]
        w_cm = jnp.concatenate([wl[:, :o_fn], wl[:, o_ml:o_ml + 2 * D_BR], wl[:, o_fn:o_ml]], axis=1)
        b_cm = jnp.concatenate([bl[:o_fn], bl[o_ml:o_ml + 2 * D_BR], bl[o_fn:o_ml]])
        w_tm = jnp.concatenate([wl[:, o_ml + 2 * D_BR:o_mlg], wl[:, o_mlg:o_gate], jnp.zeros((d, pad_g), F32)], axis=1)
        b_tm = jnp.concatenate([bl[o_ml + 2 * D_BR:o_mlg], bl[o_mlg:o_gate], jnp.zeros((pad_g,), F32)])
        c_hy, c_qk, c_fn = 0, 3 * D_BR, 5 * D_BR
        g_col = (2 * D_BR) // LANES

        last = l + 1 == depth
        live = ("lat",) if last else streams
        if l == 0:
            xn = {s: _norm_mod(xs[s], norm1_w[l], mods3, 0, 1, seg_of(s, tm), tm) for s in streams}
        w_tm_b, w_cm_t = w_tm.astype(BF16), w_cm.T.astype(BF16)
        z_tm = {s: _mm_tm(xn[s], w_tm_b, b_tm, gate=False) for s in streams}
        w_gate_b = wl[:, o_gate:].astype(BF16)
        gates = {s: _mm_tm(xn[s], w_gate_b, bl[o_gate:], gate=True) for s in live}
        z_s = {s: _mm_slab(xn[s], w_cm_t, b_cm) for s in streams}

        grid_kw = {"lat": dict(rows=rows, width=GRID_W), "ctx": dict(rows=1, width=lc)}
        hy_w, hy_b = hy_conv_w[l].reshape(9, 3 * D_BR), hy_conv_b[l]
        ml_w, ml_b = ml_conv_w[l].reshape(9, 2 * D_BR), ml_conv_b[l]
        conv = lambda s, w, b, lo, n, act: _conv(z_s[s], w, b, chan_lo=lo, chan_n=n, silu=act, n_batch=nb,
                                                 slab0=0, **grid_kw[s])
        u = {s: conv(s, hy_w, hy_b, c_hy, 3 * D_BR, False) for s in live}
        qk = {s: _slab_to_tm(conv(s, ml_w, ml_b, c_qk, 2 * D_BR, True)) for s in streams}

        (hf_lat, hb_lat), (hf_ctx, hb_ctx) = _mlstm(qk["lat"], z_tm["lat"], qk["ctx"], z_tm["ctx"],
                                                    n_batch=nb, v_col=0, g_col=g_col)
        h_f, h_b = {"lat": hf_lat, "ctx": hf_ctx}, {"lat": hb_lat, "ctx": hb_ctx}

        yh, yf = {}, {}
        yh["lat"] = _slab_to_tm(_hyena(u["lat"], _hyena_taps(seq, 2 * a_lat, lp), hy_skip[l],
                                       a_in=a_lat, na=2 * a_lat, n_batch=nb))
        pq = _fn_mix(z_s["lat"], a_n=a_lat, n_batch=nb, chan_lo=c_fn)
        yk = _fn_seq(pq, a_n=a_lat, n_batch=nb)
        yf["lat"] = jnp.transpose(yk, (0, 3, 2, 1)).reshape(n_lat, D_BR)
        if not last:
            yh["ctx"] = _hyena_short(u["ctx"], _hyena_taps(lc, 2 * a_ctx, lp), hy_skip[l],
                                     a_n=a_ctx, n_batch=nb).reshape(n_ctx, D_BR)
            yf["ctx"] = _fn_small(z_s["ctx"], a_n=a_ctx, n_batch=nb, slab0=0, chan_lo=c_fn).reshape(n_ctx, D_BR)

        rw = jnp.concatenate([moe_rg_w[l], moe_re_w[l], jnp.zeros((d, LANES - MOE_GROUPS - MOE_EXPERTS), F32)], axis=1)
        rb = jnp.concatenate([moe_rg_b[l], moe_re_b[l], jnp.zeros((LANES - MOE_GROUPS - MOE_EXPERTS,), F32)]).reshape(1, LANES)
        wb, wo = w_branch[l].astype(BF16), w_out[l].astype(BF16)
        experts = (moe_w_gate[l].astype(BF16), moe_w_up[l].astype(BF16), moe_w_down[l].astype(BF16))
        for s in live:
            xs[s], xn2, comb = _merge(yh[s], yf[s], h_f[s], h_b[s], z_tm[s], gates[s], xs[s], mods3, wb, wo,
                                      ml_norm_w[l], norm2_w[l], rw, rb, seg=seg_of(s, tm_mrg[s]), tm=tm_mrg[s],
                                      o_col=1)
            moe_kw = dict(seg=seg_of(s, tm_moe[s]), tm=tm_moe[s], n_keep_rows=xs[s].shape[0])
            if last:
                out, = _moe(xn2, comb, *experts, xs[s], mods3, norm_f_w, mods3, final=True, **moe_kw)
            else:
                xs[s], xn[s] = _moe(xn2, comb, *experts, xs[s], mods3, norm1_w[l + 1],
                                    mods[l + 1].reshape(8, 1, 6 * d), final=False, **moe_kw)

    return out.reshape(nb, seq, d)
```

```python
import functools
import math

import numpy as np
import jax
import jax.numpy as jnp
from jax import lax
from jax.experimental import pallas as pl
from jax.experimental.pallas import tpu as pltpu

F32 = jnp.float32
BF16 = jnp.bfloat16

D_MODEL = 1024
D_BR = 512
GRID_W = 64
LANES = 128
CB = 8
HY_ORDER = 2
HY_BANDS = 16
FN_GROUPS = 4
ML_HEADS = 4
MOE_GROUPS = 4
MOE_PER_GROUP = 4
MOE_EXPERTS = 16
EXPERT_HID = 256
EPS = 1e-6
VMEM_LIMIT = 56 * 1024 * 1024


def _params(sem):
    return pltpu.CompilerParams(dimension_semantics=sem, vmem_limit_bytes=VMEM_LIMIT)


def _bdot(a, b):
    return jnp.dot(a.astype(BF16), b.astype(BF16), preferred_element_type=F32)


def _split3(x):
    hi = x.astype(BF16)
    r1 = x - hi.astype(F32)
    mid = r1.astype(BF16)
    lo = (r1 - mid.astype(F32)).astype(BF16)
    return hi, mid, lo


def _dot_f32ish(x, w):
    xh, xm, xl = _split3(x)
    wh, wm, wl = _split3(w)
    d = lambda a, b: jnp.dot(a, b, preferred_element_type=F32)
    return (d(xh, wh) + (d(xh, wm) + d(xm, wh))) + (d(xm, wm) + d(xh, wl) + d(xl, wh))


def _dot_f32ish3(x, w):
    xh, xm, _ = _split3(x)
    wh, wm, _ = _split3(w)
    d = lambda a, b: jnp.dot(a, b, preferred_element_type=F32)
    return d(xh, wh) + (d(xh, wm) + d(xm, wh))


def _swap_halves(x):
    return jnp.concatenate([x[..., LANES:], x[..., :LANES]], axis=-1)


def _mods_kernel(c_ref, w_ref, b_ref, o_ref):
    c = c_ref[...]
    s = c * jax.nn.sigmoid(c)
    o_ref[...] = _dot_f32ish(s, w_ref[...]) + b_ref[...]


def _mods(cvec, ada_w, ada_b):
    depth, d, n6 = ada_w.shape
    tn = 1536
    return pl.pallas_call(
        _mods_kernel,
        grid=(depth, n6 // tn),
        in_specs=[pl.BlockSpec((8, d), lambda l, j: (0, 0)),
                  pl.BlockSpec((None, d, tn), lambda l, j: (l, 0, j)),
                  pl.BlockSpec((None, 1, tn), lambda l, j: (l, 0, j))],
        out_specs=pl.BlockSpec((None, 8, tn), lambda l, j: (l, 0, j)),
        out_shape=jax.ShapeDtypeStruct((depth, 8, n6), F32),
        compiler_params=_params(("arbitrary", "arbitrary")),
        name="adaln_mods",
    )(cvec, ada_w, ada_b.reshape(depth, 1, n6))


def _norm_mod_kernel(x_ref, w_ref, sh_ref, sc_ref, o_ref):
    x = x_ref[...]
    y = x * lax.rsqrt(jnp.mean(x * x, axis=-1, keepdims=True) + EPS) * w_ref[...]
    o_ref[...] = (y * (1.0 + sc_ref[...]) + sh_ref[...]).astype(o_ref.dtype)


def _norm_mod(x, w, mods3, col_shift, col_scale, seg, tm):
    nt, d = x.shape
    return pl.pallas_call(
        _norm_mod_kernel,
        grid=(nt // tm,),
        in_specs=[pl.BlockSpec((tm, d), lambda i: (i, 0)),
                  pl.BlockSpec((1, d), lambda i: (0, 0)),
                  pl.BlockSpec((None, 1, d), lambda i: (seg(i), 0, col_shift)),
                  pl.BlockSpec((None, 1, d), lambda i: (seg(i), 0, col_scale))],
        out_specs=pl.BlockSpec((tm, d), lambda i: (i, 0)),
        out_shape=jax.ShapeDtypeStruct((nt, d), BF16),
        compiler_params=_params(("arbitrary",)),
        name="norm_mod",
    )(x, w.reshape(1, d), mods3, mods3)


def _mm_tm_kernel(x_ref, w_ref, b_ref, o_ref, *, gate):
    y = jnp.dot(x_ref[...], w_ref[...], preferred_element_type=F32) + b_ref[...]
    o_ref[...] = (jax.nn.sigmoid(y) if gate else y).astype(o_ref.dtype)


def _pick(n, cands):
    for c in cands:
        if n % c == 0:
            return c
    raise ValueError(f"no tile for {n} in {cands}")


def _mm_tm(xn, w, b, *, gate):
    nt, k = xn.shape
    n = w.shape[1]
    tm = _pick(nt, (1056, 1024, 768, 512, 256))
    tn = _pick(n, (1536, 1408, 1152, 1024, 512, 384, 256, 128))
    return pl.pallas_call(
        functools.partial(_mm_tm_kernel, gate=gate),
        grid=(n // tn, nt // tm),
        in_specs=[pl.BlockSpec((tm, k), lambda j, i: (i, 0)),
                  pl.BlockSpec((k, tn), lambda j, i: (0, j)),
                  pl.BlockSpec((1, tn), lambda j, i: (0, j))],
        out_specs=pl.BlockSpec((tm, tn), lambda j, i: (i, j)),
        out_shape=jax.ShapeDtypeStruct((nt, n), BF16 if gate else F32),
        compiler_params=_params(("arbitrary", "arbitrary")),
        name="inproj_gates" if gate else "inproj_token_major",
    )(xn, w, b.reshape(1, n))


def _mm_slab_kernel(w_ref, x_ref, b_ref, o_ref, *, slabs):
    w = w_ref[...]
    b = b_ref[...]
    step = 2 if slabs % 2 == 0 else 1
    for s in range(0, slabs, step):
        xs = x_ref[s * LANES:(s + step) * LANES, :]
        y = lax.dot_general(w, xs, (((1,), (1,)), ((), ())), preferred_element_type=F32) + b
        for i in range(step):
            o_ref[s + i] = y[:, i * LANES:(i + 1) * LANES]


def _mm_slab(xn, wt, b):
    nt, k = xn.shape
    c = wt.shape[0]
    ns = nt // LANES
    ts = _pick(ns, (12, 11, 8, 6, 4, 3, 2, 1))
    tc = _pick(c, (1024, 512, 256, 128))
    return pl.pallas_call(
        functools.partial(_mm_slab_kernel, slabs=ts),
        grid=(c // tc, ns // ts),
        in_specs=[pl.BlockSpec((tc, k), lambda j, i: (j, 0)),
                  pl.BlockSpec((ts * LANES, k), lambda j, i: (i, 0)),
                  pl.BlockSpec((tc, 1), lambda j, i: (j, 0))],
        out_specs=pl.BlockSpec((ts, tc, LANES), lambda j, i: (i, j, 0)),
        out_shape=jax.ShapeDtypeStruct((ns, c, LANES), F32),
        compiler_params=_params(("arbitrary", "arbitrary")),
        name="inproj_slab",
    )(wt, xn, b.reshape(c, 1))


def _conv_taps(rows, width):
    taps = []
    for dr in (-1, 0, 1):
        if rows == 1 and dr != 0:
            continue
        for dw in (-1, 0, 1):
            taps.append((dr, dw))
    return taps


def _silu(x):
    return x * jax.nn.sigmoid(x)


def _conv_kernel(x_ref, w_ref, b_ref, o_ref, *, taps, width, n_slabs, silu):
    ct = x_ref.shape[1]
    lane = lax.broadcasted_iota(jnp.int32, (ct, LANES), 1)
    bias = jnp.zeros((ct, LANES), F32) + b_ref[...]
    planes = []
    for t, (dr, dw) in enumerate(taps):
        w = w_ref[t]
        if width < LANES and dw != 0:
            col = lane % width + dw
            w = jnp.where((col >= 0) & (col < width), w, 0.0)
        planes.append((dr * width + dw, w))

    def body(a, carry):
        x0 = x_ref[a]
        xm = jnp.where(a > 0, x_ref[jnp.maximum(a - 1, 0)], 0.0)
        xp = jnp.where(a < n_slabs - 1, x_ref[jnp.minimum(a + 1, n_slabs - 1)], 0.0)
        acc = bias
        for delta, w in planes:
            if delta == 0:
                src = x0
            elif delta > 0:
                src = pltpu.roll(jnp.where(lane >= delta, x0, xp), LANES - delta, 1)
            else:
                src = pltpu.roll(jnp.where(lane < LANES + delta, x0, xm), -delta, 1)
            acc = acc + src * w
        o_ref[a] = _silu(acc) if silu else acc
        return carry

    lax.fori_loop(0, n_slabs, body, 0, unroll=2 if n_slabs % 2 == 0 else 1)


def _conv_grid_kernel(x_ref, w_ref, b_ref, o_ref, ym_ref, yp_ref, *, width, n_slabs, silu):
    ct = x_ref.shape[1]
    lane = lax.broadcasted_iota(jnp.int32, (ct, LANES), 1)
    col = lane % width
    bias = jnp.zeros((ct, LANES), F32) + b_ref[...]

    def row_sums(a, carry):
        x0 = x_ref[a]
        xl = jnp.where(col >= 1, pltpu.roll(x0, 1, 1), 0.0)
        xr = jnp.where(col < width - 1, pltpu.roll(x0, LANES - 1, 1), 0.0)
        ym_ref[a] = w_ref[0] * xl + w_ref[1] * x0 + w_ref[2] * xr
        o_ref[a] = bias + w_ref[3] * xl + w_ref[4] * x0 + w_ref[5] * xr
        yp_ref[a] = w_ref[6] * xl + w_ref[7] * x0 + w_ref[8] * xr
        return carry

    def combine(a, carry):
        up = jnp.where(a > 0, ym_ref[jnp.maximum(a - 1, 0)], 0.0)
        dn = jnp.where(a < n_slabs - 1, yp_ref[jnp.minimum(a + 1, n_slabs - 1)], 0.0)
        from_up = jnp.where(lane < LANES - width, ym_ref[a], up)
        from_dn = jnp.where(lane >= width, yp_ref[a], dn)
        if 2 * width == LANES:
            y = o_ref[a] + pltpu.roll(from_up + from_dn, width, 1)
        else:
            y = o_ref[a] + pltpu.roll(from_up, width, 1) + pltpu.roll(from_dn, LANES - width, 1)
        o_ref[a] = _silu(y) if silu else y
        return carry

    unroll = 8 if n_slabs % 8 == 0 else 1
    lax.fori_loop(0, n_slabs, row_sums, 0, unroll=unroll)
    lax.fori_loop(0, n_slabs, combine, 0, unroll=unroll)


def _conv(z_s, w9, bias, *, rows, width, n_batch, slab0, chan_lo, chan_n, silu):
    seq = rows * width
    a_n = seq // LANES
    taps = tuple(_conv_taps(rows, width))
    assert all(abs(dr * width + dw) < LANES for dr, dw in taps)
    assert LANES % width == 0 or (rows == 1 and width % LANES == 0)
    tap_ids = [(dr + 1) * 3 + (dw + 1) for dr, dw in taps]
    w_t = jnp.broadcast_to(w9[jnp.array(tap_ids)][:, :, None], (len(taps), chan_n, LANES))
    ct = 64 if rows > 1 else 256
    assert chan_lo % ct == 0 and chan_n % ct == 0 and slab0 % a_n == 0
    nt_ = len(taps)
    if rows > 1:
        assert LANES % width == 0 and nt_ == 9
        body = functools.partial(_conv_grid_kernel, width=width, n_slabs=a_n, silu=silu)
        scratch = [pltpu.VMEM((a_n, ct, LANES), F32), pltpu.VMEM((a_n, ct, LANES), F32)]
    else:
        body = functools.partial(_conv_kernel, taps=taps, width=width, n_slabs=a_n, silu=silu)
        scratch = []
    return pl.pallas_call(
        body,
        scratch_shapes=scratch,
        grid=(n_batch, chan_n // ct),
        in_specs=[pl.BlockSpec((a_n, ct, LANES), lambda b, j: (slab0 // a_n + b, chan_lo // ct + j, 0)),
                  pl.BlockSpec((nt_, ct, LANES), lambda b, j: (0, j, 0)),
                  pl.BlockSpec((ct, 1), lambda b, j: (j, 0))],
        out_specs=pl.BlockSpec((a_n, ct, LANES), lambda b, j: (b, j, 0)),
        out_shape=jax.ShapeDtypeStruct((n_batch * a_n, chan_n, LANES), F32),
        compiler_params=_params(("arbitrary", "arbitrary")),
        name=f"dwconv_{rows}x{width}",
    )(z_s, w_t, bias.reshape(chan_n, 1))


def _dft_consts(a_in, na):
    n = na * LANES
    k = np.arange(na)[:, None]
    a = np.arange(a_in)[None, :]
    ang = 2 * np.pi * (k * a % na) / na
    fa = np.concatenate([np.cos(ang), -np.sin(ang)], axis=0)
    r = np.arange(LANES)
    ang_t = 2 * np.pi * (np.arange(na)[:, None] * r[None, :] % n) / n
    tr, ti = np.cos(ang_t), -np.sin(ang_t)
    ta = np.concatenate([tr, tr], axis=1)
    tb = np.concatenate([-ti, ti], axis=1)
    ang2 = 2 * np.pi * (r[:, None] * r[None, :] % LANES) / LANES
    c2, s2 = np.cos(ang2), np.sin(ang2)
    g2 = np.block([[c2, -s2], [s2, c2]])
    g2i = np.block([[c2, s2], [-s2, c2]])
    ang_i = 2 * np.pi * (np.arange(a_in)[:, None] * np.arange(na)[None, :] % na) / na
    ci, si = np.cos(ang_i) / n, -np.sin(ang_i) / n
    f = lambda v, dt: jnp.asarray(v, dtype=dt)
    return dict(fa=f(fa, F32), ta=f(ta, F32), tb=f(tb, F32), g2=f(g2, F32), g2i=f(g2i, F32),
                ci=f(ci, F32), si=f(si, F32))


def _fwd_slab_stage(m, fa, ta, tb, na):
    pp = jnp.dot(fa, m.astype(BF16), preferred_element_type=F32)
    p = jnp.concatenate([pp[:na], pp[na:]], axis=1)
    return p * ta + _swap_halves(p) * tb


def _cmul(x, kf):
    kr, ki = kf[..., :LANES], kf[..., LANES:]
    ka = jnp.concatenate([kr, kr], axis=-1)
    kb = jnp.concatenate([-ki, ki], axis=-1)
    return x * ka + _swap_halves(x) * kb


def _chan_load(ref, c):
    n, cb, _ = ref.shape
    return ref.reshape(n * cb, LANES)[pl.ds(c, n, stride=cb), :]


def _chan_store(ref, c, val):
    ref[:, c, :] = val


def _dot_f32ish_k(w, h):
    wh, wm, _ = _split3(w)
    hh, hm, _ = _split3(h)
    lhs = jnp.concatenate([wh, wh, wm], axis=1)
    rhs = jnp.concatenate([hh, hm, hh], axis=0)
    return jnp.dot(lhs, rhs, preferred_element_type=F32)


def _taps_kernel(bands_ref, w1t_ref, w1c_ref, w1s_ref, b1_ref, w2_ref, b2_ref, fq_ref, w3_ref, dec_ref, o_ref,
                 *, seq, a_seq, na, spb):
    step = pl.program_id(0)
    width = spb * LANES
    n_total = na * LANES
    is_f = step * spb < a_seq
    n = step * width + lax.broadcasted_iota(jnp.int32, (1, width), 1)
    pos = jnp.where(is_f, n, n_total - n)
    lo, hi = jnp.where(is_f, -1, n_total - seq), jnp.where(is_f, seq, n_total)
    t = pos.astype(F32) / seq
    fq = fq_ref[...]
    ang = ((2 * math.pi) * t) * bands_ref[...]
    pre = (w1t_ref[...] * t + _dot_f32ish(w1c_ref[...], jnp.cos(ang))
           + _dot_f32ish(w1s_ref[...], jnp.sin(ang)))
    h = jnp.sin(fq * (pre + b1_ref[...]))
    h = jnp.sin(fq * (_dot_f32ish(w2_ref[...], h) + b2_ref[...]))
    d = jnp.where(is_f, 0, 1)
    dec = jnp.abs(dec_ref[d])
    k = _dot_f32ish_k(w3_ref[d], h)
    for i in range(spb):
        sl = slice(i * LANES, (i + 1) * LANES)
        live = (n[:, sl] > lo) & (n[:, sl] < hi)
        o_ref[i] = jnp.where(live, k[:, sl] * jnp.exp(-t[:, sl] * dec), 0.0)


def _hyena_taps(seq, na, lp):
    nc = HY_ORDER * D_BR
    hid = lp["hy_f_w2"].shape[0]
    w1 = lp["hy_f_w1"]
    col = lambda v: v.reshape(-1, 1)
    bands = col(jnp.linspace(1e-4, HY_BANDS - 1, HY_BANDS, dtype=F32))
    w3 = jnp.transpose(lp["hy_f_w3"].T.reshape(HY_ORDER, 2, D_BR, hid), (1, 0, 2, 3)).reshape(2, nc, hid)
    dec = jnp.broadcast_to(jnp.transpose(lp["hy_decay"], (1, 0, 2)).reshape(2, nc, 1), (2, nc, LANES))
    a_seq = seq // LANES
    spb = min(8, a_seq)
    assert na == 2 * a_seq and a_seq % spb == 0
    args = (bands, col(w1[0]), w1[1:1 + HY_BANDS].T, w1[1 + HY_BANDS:].T, col(lp["hy_f_b1"]), lp["hy_f_w2"].T,
            col(lp["hy_f_b2"]), col(lp["hy_f_freq"]), w3, dec)
    full = lambda v: pl.BlockSpec(v.shape, lambda s: (0,) * v.ndim)
    return pl.pallas_call(
        functools.partial(_taps_kernel, seq=seq, a_seq=a_seq, na=na, spb=spb),
        grid=(na // spb,),
        in_specs=[full(v) for v in args],
        out_specs=pl.BlockSpec((spb, nc, LANES), lambda s: (s, 0, 0)),
        out_shape=jax.ShapeDtypeStruct((na, nc, LANES), F32),
        compiler_params=_params(("arbitrary",)),
        name=f"hyena_filter_taps_{na}",
    )(*args)


def _hyena_kernel(v_ref, x1_ref, x2_ref, k0_ref, k1_ref, skip_ref, fa_ref, faf_ref, ta_ref, tb_ref, g2_ref, g2i_ref,
                  ci_ref, si_ref, o_ref, p_buf, z_buf, kf_buf, *, a_in, na):
    fa, ta, tb = fa_ref[...].astype(BF16), ta_ref[...], tb_ref[...]
    ci, si = ci_ref[...].astype(BF16), si_ref[...].astype(BF16)

    @pl.when(pl.program_id(1) == 0)
    def _():
        faf = faf_ref[...].astype(BF16)
        for order, k_ref in enumerate((k0_ref, k1_ref)):
            scales = []
            for c in range(CB):
                m = _chan_load(k_ref, c)
                ss = jnp.sum(jnp.sum(m * m, axis=1, keepdims=True), axis=0, keepdims=True)
                scales.append(lax.rsqrt(ss + EPS))
                p_buf[c] = _fwd_slab_stage(m, faf, ta, tb, na)
            x = _bdot(p_buf[...].reshape(CB * na, 2 * LANES), g2_ref[...]).reshape(CB, na, 2 * LANES)
            for c in range(CB):
                kf_buf[order, c] = x[c] * scales[c]

    def spectral(order):
        x = _bdot(p_buf[...].reshape(CB * na, 2 * LANES), g2_ref[...])
        y = _cmul(x, kf_buf[order].reshape(CB * na, 2 * LANES))
        bm = _bdot(y, g2i_ref[...]).reshape(CB, na, 2 * LANES)
        p_buf[...] = bm * ta - _swap_halves(bm) * tb

    def conv_out(c):
        bb = p_buf[c]
        return (jnp.dot(ci, bb[:, :LANES].astype(BF16), preferred_element_type=F32)
                + jnp.dot(si, bb[:, LANES:].astype(BF16), preferred_element_type=F32))

    for c in range(CB):
        p_buf[c] = _fwd_slab_stage(_chan_load(v_ref, c), fa, ta, tb, na)
    spectral(0)
    for c in range(CB):
        z_buf[c] = _chan_load(x1_ref, c) * (conv_out(c) + _chan_load(v_ref, c) * skip_ref[0, c])
    for c in range(CB):
        p_buf[c] = _fwd_slab_stage(z_buf[c], fa, ta, tb, na)
    spectral(1)
    for c in range(CB):
        _chan_store(o_ref, c, _chan_load(x2_ref, c) * (conv_out(c) + z_buf[c] * skip_ref[1, c]))


def _hyena(u_s, taps_s, skip, *, a_in, na, n_batch):
    cs = _dft_consts(a_in, na)
    faf = _dft_consts(na, na)["fa"]
    nblk = D_BR // CB
    const = lambda shp: pl.BlockSpec(shp, lambda j, b: (0,) * len(shp))
    skip_b = jnp.broadcast_to(skip[:, :, None, None], (HY_ORDER, D_BR, 1, LANES))
    return pl.pallas_call(
        functools.partial(_hyena_kernel, a_in=a_in, na=na),
        grid=(nblk, n_batch),
        in_specs=[pl.BlockSpec((a_in, CB, LANES), lambda j, b: (b, j, 0)),
                  pl.BlockSpec((a_in, CB, LANES), lambda j, b: (b, nblk + j, 0)),
                  pl.BlockSpec((a_in, CB, LANES), lambda j, b: (b, 2 * nblk + j, 0)),
                  pl.BlockSpec((na, CB, LANES), lambda j, b: (0, j, 0)),
                  pl.BlockSpec((na, CB, LANES), lambda j, b: (0, nblk + j, 0)),
                  pl.BlockSpec((HY_ORDER, CB, 1, LANES), lambda j, b: (0, j, 0, 0)),
                  const((2 * na, a_in)), const((2 * na, na)), const((na, 2 * LANES)), const((na, 2 * LANES)),
                  const((2 * LANES, 2 * LANES)), const((2 * LANES, 2 * LANES)),
                  const((a_in, na)), const((a_in, na))],
        out_specs=pl.BlockSpec((a_in, CB, LANES), lambda j, b: (b, j, 0)),
        out_shape=jax.ShapeDtypeStruct((n_batch * a_in, D_BR, LANES), F32),
        scratch_shapes=[pltpu.VMEM((CB, na, 2 * LANES), F32), pltpu.VMEM((CB, a_in, LANES), F32),
                        pltpu.VMEM((HY_ORDER, CB, na, 2 * LANES), F32)],
        compiler_params=_params(("arbitrary", "arbitrary")),
        name=f"hyena_longconv_{a_in}",
    )(u_s, u_s, u_s, taps_s, taps_s, skip_b, cs["fa"], faf, cs["ta"], cs["tb"], cs["g2"], cs["g2i"],
      cs["ci"], cs["si"])


def _slabs_to_rows(ref, n):
    return jnp.concatenate([ref[a].T for a in range(n)], axis=0)


def _hyena_short_kernel(v_ref, x1_ref, x2_ref, k0_ref, k1_ref, skip_ref, f_ref, g_ref, o_ref, *, a_n, a_k):
    seq, nf = a_n * LANES, a_k * LANES
    ff = f_ref[...].astype(BF16)
    gi = g_ref[...].astype(BF16)
    v, x1, x2 = _slabs_to_rows(v_ref, a_n), _slabs_to_rows(x1_ref, a_n), _slabs_to_rows(x2_ref, a_n)

    def longconv(u, k_ref):
        k = _slabs_to_rows(k_ref, a_k)
        s = lax.rsqrt(jnp.sum(k * k, axis=0, keepdims=True) + EPS)
        kf = jnp.dot(ff, k.astype(BF16), preferred_element_type=F32) * s
        x = jnp.dot(ff[:, :seq], u.astype(BF16), preferred_element_type=F32)
        xr, xi, kr, ki = x[:nf], x[nf:], kf[:nf], kf[nf:]
        y = jnp.concatenate([xr * kr - xi * ki, xr * ki + xi * kr], axis=0)
        return jnp.dot(gi, y.astype(BF16), preferred_element_type=F32)

    z = x1 * (longconv(v, k0_ref) + v * skip_ref[0])
    o_ref[...] = x2 * (longconv(z, k1_ref) + z * skip_ref[1])


def _hyena_short(u_s, taps_s, skip, *, a_n, n_batch):
    a_k = 2 * a_n
    seq, nf = a_n * LANES, a_k * LANES
    k = np.arange(nf)
    ang = 2 * np.pi * (k[:, None] * k[None, :] % nf) / nf
    f = np.concatenate([np.cos(ang), -np.sin(ang)], axis=0)
    g = np.concatenate([np.cos(ang[:seq]), -np.sin(ang[:seq])], axis=1) / nf
    nblk = D_BR // LANES
    const = lambda shp: pl.BlockSpec(shp, lambda b, j: (0,) * len(shp))
    return pl.pallas_call(
        functools.partial(_hyena_short_kernel, a_n=a_n, a_k=a_k),
        grid=(n_batch, nblk),
        in_specs=[pl.BlockSpec((a_n, LANES, LANES), lambda b, j: (b, j, 0)),
                  pl.BlockSpec((a_n, LANES, LANES), lambda b, j: (b, nblk + j, 0)),
                  pl.BlockSpec((a_n, LANES, LANES), lambda b, j: (b, 2 * nblk + j, 0)),
                  pl.BlockSpec((a_k, LANES, LANES), lambda b, j: (0, j, 0)),
                  pl.BlockSpec((a_k, LANES, LANES), lambda b, j: (0, nblk + j, 0)),
                  pl.BlockSpec((HY_ORDER, 1, LANES), lambda b, j: (0, 0, j)),
                  const((2 * nf, nf)), const((seq, 2 * nf))],
        out_specs=pl.BlockSpec((None, seq, LANES), lambda b, j: (b, 0, j)),
        out_shape=jax.ShapeDtypeStruct((n_batch, seq, D_BR), F32),
        compiler_params=_params(("arbitrary", "arbitrary")),
        name="hyena_short",
    )(u_s, u_s, u_s, taps_s, taps_s, skip.reshape(HY_ORDER, 1, D_BR), jnp.asarray(f, F32), jnp.asarray(g, F32))


def _chan_dft_mats():
    r = np.arange(LANES)
    ang = 2 * np.pi * (r[:, None] * r[None, :] % LANES) / LANES
    return np.cos(ang), np.sin(ang)


def _fn_mix_kernel(u_ref, cs_ref, o_ref, *, n_slabs):
    w = cs_ref[...].astype(BF16)

    def body(a, carry):
        o_ref[a] = jnp.dot(w, u_ref[a].astype(BF16), preferred_element_type=F32)
        return carry

    lax.fori_loop(0, n_slabs, body, 0)


def _fn_mix(z_s, *, a_n, n_batch, chan_lo):
    c, s = _chan_dft_mats()
    w = jnp.asarray(np.concatenate([c, s], axis=0), dtype=F32)
    g0 = chan_lo // LANES
    return pl.pallas_call(
        functools.partial(_fn_mix_kernel, n_slabs=a_n),
        grid=(n_batch, FN_GROUPS),
        in_specs=[pl.BlockSpec((a_n, LANES, LANES), lambda b, g: (b, g0 + g, 0)),
                  pl.BlockSpec((2 * LANES, LANES), lambda b, g: (0, 0))],
        out_specs=pl.BlockSpec((a_n, 2 * LANES, LANES), lambda b, g: (b, g, 0)),
        out_shape=jax.ShapeDtypeStruct((n_batch * a_n, 2 * D_BR, LANES), F32),
        compiler_params=_params(("arbitrary", "arbitrary")),
        name="fnet_channel_dft",
    )(z_s, w)


def _fn_seq_kernel(p_ref, q_ref, fa_ref, tr_ref, ti_ref, g_ref, o_ref, a_buf, *, a_n, n_batch, scale):
    fa, tr, ti = fa_ref[...].astype(BF16), tr_ref[...], ti_ref[...]
    side = lambda m: jnp.concatenate([m[b * a_n:(b + 1) * a_n] for b in range(n_batch)], axis=1).astype(BF16)
    for c in range(CB):
        r1 = jnp.dot(fa, side(_chan_load(p_ref, c)), preferred_element_type=F32)
        r2 = jnp.dot(fa, side(_chan_load(q_ref, c)), preferred_element_type=F32)
        ar_all = r1[:a_n] - r2[a_n:]
        ai_all = -(r2[:a_n] + r1[a_n:])
        for b in range(n_batch):
            ar, ai = ar_all[:, b * LANES:(b + 1) * LANES], ai_all[:, b * LANES:(b + 1) * LANES]
            a_buf[b, c] = jnp.concatenate([ar * tr - ai * ti, ar * ti + ai * tr], axis=1)
    y = _bdot(a_buf[...].reshape(n_batch * CB * a_n, 2 * LANES), g_ref[...]) * scale
    o_ref[...] = y.reshape(n_batch, CB, a_n, LANES)


def _fn_seq(pq, *, a_n, n_batch):
    seq = a_n * LANES
    k = np.arange(a_n)
    ang = 2 * np.pi * (k[:, None] * k[None, :] % a_n) / a_n
    fa = np.concatenate([np.cos(ang), np.sin(ang)], axis=0)
    r = np.arange(LANES)
    ang_t = 2 * np.pi * (k[:, None] * r[None, :] % seq) / seq
    c2, s2 = _chan_dft_mats()
    g = np.concatenate([c2, s2], axis=0)
    nblk = LANES // CB
    const = lambda shp: pl.BlockSpec(shp, lambda j: (0,) * len(shp))

    def chan_blk(j, off):
        return (j // nblk) * (2 * nblk) + off * nblk + j % nblk

    return pl.pallas_call(
        functools.partial(_fn_seq_kernel, a_n=a_n, n_batch=n_batch, scale=1.0 / math.sqrt(seq * LANES)),
        grid=(D_BR // CB,),
        in_specs=[pl.BlockSpec((n_batch * a_n, CB, LANES), lambda j: (0, chan_blk(j, 0), 0)),
                  pl.BlockSpec((n_batch * a_n, CB, LANES), lambda j: (0, chan_blk(j, 1), 0)),
                  const((2 * a_n, a_n)), const((a_n, LANES)), const((a_n, LANES)), const((2 * LANES, LANES))],
        out_specs=pl.BlockSpec((n_batch, CB, a_n, LANES), lambda j: (0, j, 0, 0)),
        out_shape=jax.ShapeDtypeStruct((n_batch, D_BR, a_n, LANES), F32),
        scratch_shapes=[pltpu.VMEM((n_batch, CB, a_n, 2 * LANES), F32)],
        compiler_params=_params(("arbitrary",)),
        name="fnet_sequence_dft",
    )(pq, pq, jnp.asarray(fa, F32), jnp.asarray(np.cos(ang_t), F32), jnp.asarray(-np.sin(ang_t), F32),
      jnp.asarray(g, F32))


def _fn_small_kernel(u_ref, cw_ref, sw_ref, cl_ref, sl_ref, o_ref, *, a_n, scale):
    u = jnp.concatenate([u_ref[a].T for a in range(a_n)], axis=0)
    p = _bdot(u, cw_ref[...])
    q = _bdot(u, sw_ref[...])
    o_ref[...] = (_bdot(cl_ref[...], p) - _bdot(sl_ref[...], q)) * scale


def _fn_small(z_s, *, a_n, n_batch, slab0, chan_lo):
    seq = a_n * LANES
    cw, sw = _chan_dft_mats()
    n = np.arange(seq)
    ang = 2 * np.pi * (n[:, None] * n[None, :] % seq) / seq
    const = lambda shp: pl.BlockSpec(shp, lambda b, g: (0,) * len(shp))
    g0 = chan_lo // LANES
    return pl.pallas_call(
        functools.partial(_fn_small_kernel, a_n=a_n, scale=1.0 / math.sqrt(seq * LANES)),
        grid=(n_batch, FN_GROUPS),
        in_specs=[pl.BlockSpec((a_n, LANES, LANES), lambda b, g: (slab0 // a_n + b, g0 + g, 0)),
                  const((LANES, LANES)), const((LANES, LANES)), const((seq, seq)), const((seq, seq))],
        out_specs=pl.BlockSpec((None, seq, LANES), lambda b, g: (b, 0, g)),
        out_shape=jax.ShapeDtypeStruct((n_batch, seq, D_BR), F32),
        compiler_params=_params(("arbitrary", "arbitrary")),
        name="fnet_short",
    )(z_s, jnp.asarray(cw, F32), jnp.asarray(sw, F32), jnp.asarray(np.cos(ang), F32), jnp.asarray(np.sin(ang), F32))


def _log_sigmoid(x):
    return jnp.minimum(x, 0.0) - jnp.log(1.0 + jnp.exp(-jnp.abs(x)))


def _exact_tri_dot(tri, x, tri_on_left):
    h, m, l = _split3(x)
    if tri_on_left:
        d = lambda p: jnp.dot(tri, p, preferred_element_type=F32)
    else:
        d = lambda p: jnp.dot(p, tri, preferred_element_type=F32)
    return d(h) + d(m) + d(l)


def _mlstm_step(inputs, c_st, n_st, m_st):
    t = hd = LANES
    n_dir, n_batch = len(inputs), len(inputs[0])
    n_grp = n_dir * n_batch * ML_HEADS
    row = lax.broadcasted_iota(jnp.int32, (t, t), 0)
    col = lax.broadcasted_iota(jnp.int32, (t, t), 1)
    tri = jnp.where(col <= row, 1.0, 0.0).astype(BF16)
    tri_t = jnp.where(col >= row, 1.0, 0.0).astype(BF16)
    qs, ks, vs, bcs, brs, ics, irs = [], [], [], [], [], [], []
    for d in range(n_dir):
        i_off = 2 * ML_HEADS * d
        f_off = i_off + ML_HEADS
        for b in range(n_batch):
            q_all, k_all, v_all, g = inputs[d][b]
            gt = g.T
            lf_c = _log_sigmoid(g)
            lf_r = lf_c.T
            if d == 1:
                b_c = _exact_tri_dot(tri_t, lf_c, True)
                b_r = _exact_tri_dot(tri, lf_r, False)
            else:
                b_c = _exact_tri_dot(tri, lf_c, True)
                b_r = _exact_tri_dot(tri_t, lf_r, False)
            for h in range(ML_HEADS):
                sl = slice(h * hd, (h + 1) * hd)
                qs.append(q_all[:, sl])
                ks.append(k_all[:, sl])
                vs.append(v_all[:, sl])
                bcs.append(b_c[:, f_off + h:f_off + h + 1])
                brs.append(b_r[f_off + h:f_off + h + 1, :])
                ics.append(g[:, i_off + h:i_off + h + 1])
                irs.append(gt[i_off + h:i_off + h + 1, :])
    q = jnp.stack(qs) * (hd ** -0.5)
    k, v = jnp.stack(ks), jnp.stack(vs)
    bc, br, ic, ir = jnp.stack(bcs), jnp.stack(brs), jnp.stack(ics), jnp.stack(irs)
    m_prev = m_st[...][:, :, :1]
    ct = c_st[...]
    n_prev = n_st[...]

    shp = (n_grp, t, t)
    grp = lax.broadcasted_iota(jnp.int32, shp, 0)
    r3, c3 = lax.broadcasted_iota(jnp.int32, shp, 1), lax.broadcasted_iota(jnp.int32, shp, 2)
    back = grp >= (n_grp // n_dir)
    mask = (back & (c3 >= r3)) | (jnp.logical_not(back) & (c3 <= r3))
    bdot = lambda a, b_, ca, cb: lax.dot_general(a.astype(BF16), b_.astype(BF16), (((ca,), (cb,)), ((0,), (0,))),
                                                 preferred_element_type=F32)
    dm = jnp.where(mask, bc - br + ir, -jnp.inf)
    inter = bc + m_prev
    m_row = jnp.maximum(inter, jnp.max(dm, axis=-1, keepdims=True))
    w_intra = jnp.exp(dm - m_row)
    w_inter = jnp.exp(inter - m_row)
    s = bdot(q, k, 2, 2) * w_intra
    num = bdot(s, v, 2, 1) + w_inter * bdot(q, ct, 2, 1)
    den = jnp.sum(s, axis=-1, keepdims=True) + w_inter * jnp.sum(q * n_prev, axis=-1, keepdims=True)
    den = jnp.maximum(jnp.abs(den), jnp.exp(-m_row))
    h_all = num / den

    is_back = lax.broadcasted_iota(jnp.int32, (n_grp, 1, 1), 0) >= (n_grp // n_dir)
    b_tot_c = jnp.where(is_back, bc[:, :1, :], bc[:, t - 1:, :])
    b_tot_r = jnp.where(is_back, br[:, :, :1], br[:, :, t - 1:])
    a_c = b_tot_c - bc + ic
    a_r = b_tot_r - br + ir
    m_new = jnp.maximum(b_tot_c + m_prev, jnp.max(a_r, axis=-1, keepdims=True))
    sc = jnp.exp(a_c - m_new)
    decay = jnp.exp(b_tot_c + m_prev - m_new)
    k_sc = k * sc
    c_st[...] = decay * ct + bdot(k_sc, v, 1, 1)
    n_st[...] = decay * n_prev + jnp.sum(k_sc, axis=1, keepdims=True)
    m_st[...] = jnp.broadcast_to(m_new, (n_grp, 1, LANES))

    out = []
    for d in range(n_dir):
        out.append([jnp.concatenate([h_all[(d * n_batch + b) * ML_HEADS + h] for h in range(ML_HEADS)], axis=1)
                    for b in range(n_batch)])
    return out


def _mlstm_kernel(*refs, n_batch, ctx_chunks):
    lat_f, lat_b, ctx_f, ctx_b = refs[0:4], refs[4:8], refs[8:12], refs[12:16]
    hf_lat, hb_lat, hf_ctx, hb_ctx, c_st, n_st, m_st = refs[16:]
    j = pl.program_id(0)
    is_ctx = j < ctx_chunks

    @pl.when(j == 0)
    def _():
        c_st[...] = jnp.zeros(c_st.shape, F32)
        n_st[...] = jnp.zeros(n_st.shape, F32)
        m_st[...] = jnp.zeros(m_st.shape, F32)

    pick = lambda c_refs, l_refs, b: tuple(jnp.where(is_ctx, c[b], l[b]) for c, l in zip(c_refs, l_refs))
    inputs = [[pick(ctx_f, lat_f, b) for b in range(n_batch)], [pick(ctx_b, lat_b, b) for b in range(n_batch)]]
    hf, hb = _mlstm_step(inputs, c_st, n_st, m_st)

    @pl.when(is_ctx)
    def _():
        for b in range(n_batch):
            hf_ctx[b] = hf[b]
            hb_ctx[b] = hb[b]

    @pl.when(jnp.logical_not(is_ctx))
    def _():
        for b in range(n_batch):
            hf_lat[b] = hf[b]
            hb_lat[b] = hb[b]


def _mlstm(qk_lat, z_lat, qk_ctx, z_ctx, *, n_batch, v_col, g_col):
    lat_len, ctx_len = qk_lat.shape[0] // n_batch, qk_ctx.shape[0] // n_batch
    nlc, ncc = lat_len // LANES, ctx_len // LANES
    r3 = lambda a: a.reshape(n_batch, a.shape[0] // n_batch, a.shape[1])
    lf = lambda j: jnp.maximum(j - ncc, 0)
    lb = lambda j: jnp.where(j < ncc, nlc - 1, nlc - 1 - (j - ncc))
    cf = lambda j: jnp.minimum(j, ncc - 1)
    cb = lambda j: jnp.where(j < ncc, ncc - 1 - j, 0)

    def specs(ix):
        blk = lambda w, cidx: pl.BlockSpec((n_batch, LANES, w), lambda j: (0, ix(j), cidx))
        return [blk(D_BR, 0), blk(D_BR, 1), blk(D_BR, v_col), blk(LANES, g_col)]

    out = lambda ix: pl.BlockSpec((n_batch, LANES, D_BR), lambda j: (0, ix(j), 0))
    sd = lambda n: jax.ShapeDtypeStruct((n_batch, n, D_BR), F32)
    ql, zl, qc, zc = r3(qk_lat), r3(z_lat), r3(qk_ctx), r3(z_ctx)
    hf_lat, hb_lat, hf_ctx, hb_ctx = pl.pallas_call(
        functools.partial(_mlstm_kernel, n_batch=n_batch, ctx_chunks=ncc),
        grid=(ncc + nlc,),
        in_specs=specs(lf) + specs(lb) + specs(cf) + specs(cb),
        out_specs=[out(lf), out(lb), out(cf), out(cb)],
        out_shape=[sd(lat_len), sd(lat_len), sd(ctx_len), sd(ctx_len)],
        scratch_shapes=[pltpu.VMEM((2 * n_batch * ML_HEADS, LANES, LANES), F32),
                        pltpu.VMEM((2 * n_batch * ML_HEADS, 1, LANES), F32),
                        pltpu.VMEM((2 * n_batch * ML_HEADS, 1, LANES), F32)],
        compiler_params=_params(("arbitrary",)),
        name="mlstm_bidir",
    )(ql, ql, zl, zl, ql, ql, zl, zl, qc, qc, zc, zc, qc, qc, zc, zc)
    flat = lambda a: a.reshape(a.shape[0] * a.shape[1], D_BR)
    return (flat(hf_lat), flat(hb_lat)), (flat(hf_ctx), flat(hb_ctx))


def _rms_mod(x, w, shift, scale):
    y = x * lax.rsqrt(jnp.mean(x * x, axis=-1, keepdims=True) + EPS) * w
    return y * (1.0 + scale) + shift


def _route(t, rw, rb):
    logits = _dot_f32ish3(t, rw) + rb
    col = lax.broadcasted_iota(jnp.int32, logits.shape, 1)
    big = jnp.int32(1 << 20)
    ninf = -jnp.inf
    is_g = col < MOE_GROUPS
    gl = jnp.where(is_g, logits, ninf)
    gmax = jnp.max(gl, axis=-1, keepdims=True)
    g_sel = jnp.min(jnp.where(is_g & (gl == gmax), col, big), axis=-1, keepdims=True)
    p_top = 1.0 / jnp.sum(jnp.where(is_g, jnp.exp(gl - gmax), 0.0), axis=-1, keepdims=True)
    lo = MOE_GROUPS + g_sel * MOE_PER_GROUP
    in_grp = (col >= lo) & (col < lo + MOE_PER_GROUP)
    e1v = jnp.where(in_grp, logits, ninf)
    top1 = jnp.max(e1v, axis=-1, keepdims=True)
    idx1 = jnp.min(jnp.where(in_grp & (e1v == top1), col, big), axis=-1, keepdims=True)
    e2v = jnp.where(col == idx1, ninf, e1v)
    top2 = jnp.max(e2v, axis=-1, keepdims=True)
    idx2 = jnp.min(jnp.where(in_grp & (col != idx1) & (e2v == top2), col, big), axis=-1, keepdims=True)
    ex = jnp.exp(top2 - top1)
    s1 = 1.0 / (1.0 + ex)
    return jnp.where(col == idx1, p_top * s1, 0.0) + jnp.where(col == idx2, p_top * (ex * s1), 0.0)


def _merge_kernel(yh_ref, yf_ref, hf_ref, hb_ref, o_ref, g0_ref, g1_ref, g2_ref, x_ref, gate_ref,
                  wb_ref, wo_ref, nw_ref, n2_ref, sh_ref, sc_ref, rw_ref, rb_ref, out_ref, xn_ref, comb_ref):
    hd = LANES
    h = hf_ref[...] + hb_ref[...]
    parts = []
    for i in range(ML_HEADS):
        hh = h[:, i * hd:(i + 1) * hd]
        parts.append(hh * lax.rsqrt(jnp.mean(hh * hh, axis=-1, keepdims=True) + EPS))
    y_ml = jax.nn.sigmoid(o_ref[...]) * (jnp.concatenate(parts, axis=1) * nw_ref[...])
    acc = g0_ref[...].astype(F32) * _bdot(yh_ref[...], wb_ref[0])
    acc = acc + g1_ref[...].astype(F32) * _bdot(yf_ref[...], wb_ref[1])
    acc = acc + g2_ref[...].astype(F32) * _bdot(y_ml, wb_ref[2])
    x_new = x_ref[...] + gate_ref[...] * _bdot(acc, wo_ref[...])
    out_ref[...] = x_new
    t = _rms_mod(x_new, n2_ref[...], sh_ref[...], sc_ref[...])
    xn_ref[...] = t.astype(BF16)
    comb_ref[...] = _route(t, rw_ref[...], rb_ref[...])


def _merge(yh, yf, hf, hb, z_tm, gates, x, mods3, wb, wo, nw, n2w, rw, rb, *, seg, tm, o_col):
    nt, d = x.shape
    tok = lambda w, cidx: pl.BlockSpec((tm, w), lambda i: (i, cidx))
    mod = lambda k: pl.BlockSpec((None, 1, d), lambda i: (seg(i), 0, k))
    return pl.pallas_call(
        _merge_kernel,
        grid=(nt // tm,),
        in_specs=[tok(D_BR, 0), tok(D_BR, 0), tok(D_BR, 0), tok(D_BR, 0),
                  tok(D_BR, o_col),
                  tok(d, 0), tok(d, 1), tok(d, 2),
                  tok(d, 0),
                  mod(2),
                  pl.BlockSpec((3, D_BR, d), lambda i: (0, 0, 0)),
                  pl.BlockSpec((d, d), lambda i: (0, 0)),
                  pl.BlockSpec((1, D_BR), lambda i: (0, 0)),
                  pl.BlockSpec((1, d), lambda i: (0, 0)),
                  mod(3), mod(4),
                  pl.BlockSpec((d, LANES), lambda i: (0, 0)),
                  pl.BlockSpec((1, LANES), lambda i: (0, 0))],
        out_specs=[tok(d, 0), tok(d, 0), tok(LANES, 0)],
        out_shape=[jax.ShapeDtypeStruct((nt, d), F32), jax.ShapeDtypeStruct((nt, d), BF16),
                   jax.ShapeDtypeStruct((nt, LANES), F32)],
        compiler_params=_params(("arbitrary",)),
        name="merge_branches_router",
    )(yh, yf, hf, hb, z_tm, gates, gates, gates, x, mods3, wb, wo, nw.reshape(1, D_BR), n2w.reshape(1, d),
      mods3, mods3, rw, rb)


def _moe_kernel(xn_ref, comb_ref, wg_ref, wu_ref, wd_ref, x_ref, gate_ref, nw_ref, sh_ref, sc_ref, *out_and_scratch,
                final, n_keep):
    acc_ref = out_and_scratch[-1]
    e = pl.program_id(1)

    @pl.when(e == 0)
    def _():
        acc_ref[...] = jnp.zeros(acc_ref.shape, F32)

    xn = xn_ref[...]
    comb = comb_ref[...]
    col = lax.broadcasted_iota(jnp.int32, comb.shape, 1)
    acts = []
    for i in range(MOE_PER_GROUP):
        cw = jnp.sum(jnp.where(col == e * MOE_PER_GROUP + i + MOE_GROUPS, comb, 0.0), axis=-1, keepdims=True)
        hg = jnp.dot(xn, wg_ref[i], preferred_element_type=F32)
        hu = jnp.dot(xn, wu_ref[i], preferred_element_type=F32)
        acts.append(((hg * jax.nn.sigmoid(hg)) * hu * cw).astype(BF16))
    wd = wd_ref[...].reshape(MOE_PER_GROUP * EXPERT_HID, wd_ref.shape[-1])
    acc_ref[...] += jnp.dot(jnp.concatenate(acts, axis=1), wd, preferred_element_type=F32)

    n_steps = MOE_EXPERTS // MOE_PER_GROUP
    if final:
        y_ref, = out_and_scratch[:-1]

        @pl.when((e == n_steps - 1) & (pl.program_id(0) < n_keep))
        def _():
            x_new = x_ref[...] + gate_ref[...] * acc_ref[...]
            y_ref[...] = x_new * lax.rsqrt(jnp.mean(x_new * x_new, axis=-1, keepdims=True) + EPS) * nw_ref[...]
    else:
        o_ref, xn_next_ref = out_and_scratch[:-1]

        @pl.when(e == n_steps - 1)
        def _():
            x_new = x_ref[...] + gate_ref[...] * acc_ref[...]
            o_ref[...] = x_new
            xn_next_ref[...] = _rms_mod(x_new, nw_ref[...], sh_ref[...], sc_ref[...]).astype(BF16)


def _moe(xn, comb, wg, wu, wd, x, mods3, post_w, post_mods3, *, seg, tm, final, n_keep_rows):
    nt, d = x.shape
    n_keep = n_keep_rows // tm
    tok = pl.BlockSpec((tm, d), lambda i, e: (i, 0))
    if final:
        out_specs = [pl.BlockSpec((tm, d), lambda i, e: (jnp.minimum(i, n_keep - 1), 0))]
        out_shape = [jax.ShapeDtypeStruct((n_keep_rows, d), F32)]
    else:
        out_specs = [tok, tok]
        out_shape = [jax.ShapeDtypeStruct((nt, d), F32), jax.ShapeDtypeStruct((nt, d), BF16)]
    mod = lambda k: pl.BlockSpec((None, 1, d), lambda i, e: (seg(i), 0, k))
    return pl.pallas_call(
        functools.partial(_moe_kernel, final=final, n_keep=n_keep),
        grid=(nt // tm, MOE_EXPERTS // MOE_PER_GROUP),
        in_specs=[tok,
                  pl.BlockSpec((tm, LANES), lambda i, e: (i, 0)),
                  pl.BlockSpec((MOE_PER_GROUP, d, EXPERT_HID), lambda i, e: (e, 0, 0)),
                  pl.BlockSpec((MOE_PER_GROUP, d, EXPERT_HID), lambda i, e: (e, 0, 0)),
                  pl.BlockSpec((MOE_PER_GROUP, EXPERT_HID, d), lambda i, e: (e, 0, 0)),
                  tok,
                  mod(5),
                  pl.BlockSpec((1, d), lambda i, e: (0, 0)),
                  mod(0), mod(1)],
        out_specs=out_specs,
        out_shape=out_shape,
        scratch_shapes=[pltpu.VMEM((tm, d), F32)],
        compiler_params=_params(("arbitrary", "arbitrary")),
        name="moe_experts_final" if final else "moe_experts",
    )(xn, comb, wg, wu, wd, x, mods3, post_w.reshape(1, d), post_mods3, post_mods3)


def _slab_to_tm(y_s):
    ns, c, _ = y_s.shape
    return jnp.transpose(y_s, (0, 2, 1)).reshape(ns * LANES, c)


def kernel(x, c, ctx, c_ctx, ada_w, ada_b, norm1_w, norm2_w, w_in, b_in, hy_conv_w, hy_conv_b, hy_f_w1, hy_f_b1, hy_f_w2, hy_f_b2, hy_f_w3, hy_f_freq, hy_decay, hy_skip, ml_conv_w, ml_conv_b, ml_norm_w, w_branch, w_out, moe_rg_w, moe_rg_b, moe_re_w, moe_re_b, moe_w_gate, moe_w_up, moe_w_down, norm_f_w):
    nb, seq, d = x.shape
    lc = ctx.shape[1]
    depth = ada_w.shape[0]
    assert d == D_MODEL and seq % (GRID_W * 2) == 0 and lc % LANES == 0 and nb + 1 <= 8
    rows = seq // GRID_W
    a_lat = seq // LANES
    a_ctx = lc // LANES
    n_lat, n_ctx = nb * seq, nb * lc
    tm = 256
    tm_moe = {"lat": _pick(seq, (1024, 512, 256)), "ctx": _pick(n_ctx, (512, 256))}
    tm_mrg = {"lat": _pick(seq, (512, 256)), "ctx": _pick(n_ctx, (512, 256))}
    assert seq % tm == 0 and n_ctx % tm == 0
    seg_of = lambda s, t: (lambda i: i // (seq // t)) if s == "lat" else (lambda i: nb)
    streams = ("lat", "ctx")
    xs = {"lat": x.reshape(n_lat, d), "ctx": ctx.reshape(n_ctx, d)}
    xn = {}
    cvec = jnp.zeros((8, d), F32).at[:nb].set(c).at[nb].set(c_ctx)
    mods = _mods(cvec, ada_w, ada_b)

    o_fn, o_ml, o_mlg, o_gate = 3 * D_BR, 4 * D_BR, 8 * D_BR, 8 * D_BR + 4 * ML_HEADS
    pad_g = LANES - 4 * ML_HEADS

    for l in range(depth):
        lp = {"hy_f_w1": hy_f_w1[l], "hy_f_b1": hy_f_b1[l], "hy_f_w2": hy_f_w2[l], "hy_f_b2": hy_f_b2[l],
              "hy_f_w3": hy_f_w3[l], "hy_f_freq": hy_f_freq[l], "hy_decay": hy_decay[l]}
        mods3 = mods[l].reshape(8, 1, 6 * d)
        wl, bl = w_in[l], b_in[l]
        w_cm = jnp.concatenate([wl[:, :o_fn], wl[:, o_ml:o_ml + 2 * D_BR], wl[:, o_fn:o_ml]], axis=1)
        b_cm = jnp.concatenate([bl[:o_fn], bl[o_ml:o_ml + 2 * D_BR], bl[o_fn:o_ml]])
        w_tm = jnp.concatenate([wl[:, o_ml + 2 * D_BR:o_mlg], wl[:, o_mlg:o_gate], jnp.zeros((d, pad_g), F32)], axis=1)
        b_tm = jnp.concatenate([bl[o_ml + 2 * D_BR:o_mlg], bl[o_mlg:o_gate], jnp.zeros((pad_g,), F32)])
        c_hy, c_qk, c_fn = 0, 3 * D_BR, 5 * D_BR
        g_col = (2 * D_BR) // LANES

        last = l + 1 == depth
        live = ("lat",) if last else streams
        if l == 0:
            xn = {s: _norm_mod(xs[s], norm1_w[l], mods3, 0, 1, seg_of(s, tm), tm) for s in streams}
        w_tm_b, w_cm_t = w_tm.astype(BF16), w_cm.T.astype(BF16)
        z_tm = {s: _mm_tm(xn[s], w_tm_b, b_tm, gate=False) for s in streams}
        w_gate_b = wl[:, o_gate:].astype(BF16)
        gates = {s: _mm_tm(xn[s], w_gate_b, bl[o_gate:], gate=True) for s in live}
        z_s = {s: _mm_slab(xn[s], w_cm_t, b_cm) for s in streams}

        grid_kw = {"lat": dict(rows=rows, width=GRID_W), "ctx": dict(rows=1, width=lc)}
        hy_w, hy_b = hy_conv_w[l].reshape(9, 3 * D_BR), hy_conv_b[l]
        ml_w, ml_b = ml_conv_w[l].reshape(9, 2 * D_BR), ml_conv_b[l]
        conv = lambda s, w, b, lo, n, act: _conv(z_s[s], w, b, chan_lo=lo, chan_n=n, silu=act, n_batch=nb,
                                                 slab0=0, **grid_kw[s])
        u = {s: conv(s, hy_w, hy_b, c_hy, 3 * D_BR, False) for s in live}
        qk = {s: _slab_to_tm(conv(s, ml_w, ml_b, c_qk, 2 * D_BR, True)) for s in streams}

        (hf_lat, hb_lat), (hf_ctx, hb_ctx) = _mlstm(qk["lat"], z_tm["lat"], qk["ctx"], z_tm["ctx"],
                                                    n_batch=nb, v_col=0, g_col=g_col)
        h_f, h_b = {"lat": hf_lat, "ctx": hf_ctx}, {"lat": hb_lat, "ctx": hb_ctx}

        yh, yf = {}, {}
        yh["lat"] = _slab_to_tm(_hyena(u["lat"], _hyena_taps(seq, 2 * a_lat, lp), hy_skip[l],
                                       a_in=a_lat, na=2 * a_lat, n_batch=nb))
        pq = _fn_mix(z_s["lat"], a_n=a_lat, n_batch=nb, chan_lo=c_fn)
        yk = _fn_seq(pq, a_n=a_lat, n_batch=nb)
        yf["lat"] = jnp.transpose(yk, (0, 3, 2, 1)).reshape(n_lat, D_BR)
        if not last:
            yh["ctx"] = _hyena_short(u["ctx"], _hyena_taps(lc, 2 * a_ctx, lp), hy_skip[l],
                                     a_n=a_ctx, n_batch=nb).reshape(n_ctx, D_BR)
            yf["ctx"] = _fn_small(z_s["ctx"], a_n=a_ctx, n_batch=nb, slab0=0, chan_lo=c_fn).reshape(n_ctx, D_BR)

        rw = jnp.concatenate([moe_rg_w[l], moe_re_w[l], jnp.zeros((d, LANES - MOE_GROUPS - MOE_EXPERTS), F32)], axis=1)
        rb = jnp.concatenate([moe_rg_b[l], moe_re_b[l], jnp.zeros((LANES - MOE_GROUPS - MOE_EXPERTS,), F32)]).reshape(1, LANES)
        wb, wo = w_branch[l].astype(BF16), w_out[l].astype(BF16)
        experts = (moe_w_gate[l].astype(BF16), moe_w_up[l].astype(BF16), moe_w_down[l].astype(BF16))
        for s in live:
            xs[s], xn2, comb = _merge(yh[s], yf[s], h_f[s], h_b[s], z_tm[s], gates[s], xs[s], mods3, wb, wo,
                                      ml_norm_w[l], norm2_w[l], rw, rb, seg=seg_of(s, tm_mrg[s]), tm=tm_mrg[s],
                                      o_col=1)
            moe_kw = dict(seg=seg_of(s, tm_moe[s]), tm=tm_moe[s], n_keep_rows=xs[s].shape[0])
            if last:
                out, = _moe(xn2, comb, *experts, xs[s], mods3, norm_f_w, mods3, final=True, **moe_kw)
            else:
                xs[s], xn[s] = _moe(xn2, comb, *experts, xs[s], mods3, norm1_w[l + 1],
                                    mods[l + 1].reshape(8, 1, 6 * d), final=False, **moe_kw)

    return out.reshape(nb, seq, d)
```

```python
import functools
import math

import numpy as np
import jax
import jax.numpy as jnp
from jax import lax
from jax.experimental import pallas as pl
from jax.experimental.pallas import tpu as pltpu

F32 = jnp.float32
BF16 = jnp.bfloat16

D_MODEL = 1024
D_BR = 512
GRID_W = 64
LANES = 128
CB = 8
HY_ORDER = 2
HY_BANDS = 16
FN_GROUPS = 4
ML_HEADS = 4
MOE_GROUPS = 4
MOE_PER_GROUP = 4
MOE_EXPERTS = 16
EXPERT_HID = 256
EPS = 1e-6
VMEM_LIMIT = 56 * 1024 * 1024


def _params(sem):
    return pltpu.CompilerParams(dimension_semantics=sem, vmem_limit_bytes=VMEM_LIMIT)


def _bdot(a, b):
    return jnp.dot(a.astype(BF16), b.astype(BF16), preferred_element_type=F32)


def _split3(x):
    hi = x.astype(BF16)
    r1 = x - hi.astype(F32)
    mid = r1.astype(BF16)
    lo = (r1 - mid.astype(F32)).astype(BF16)
    return hi, mid, lo


def _dot_f32ish(x, w):
    xh, xm, xl = _split3(x)
    wh, wm, wl = _split3(w)
    d = lambda a, b: jnp.dot(a, b, preferred_element_type=F32)
    return (d(xh, wh) + (d(xh, wm) + d(xm, wh))) + (d(xm, wm) + d(xh, wl) + d(xl, wh))


def _dot_f32ish3(x, w):
    xh, xm, _ = _split3(x)
    wh, wm, _ = _split3(w)
    d = lambda a, b: jnp.dot(a, b, preferred_element_type=F32)
    return d(xh, wh) + (d(xh, wm) + d(xm, wh))


def _swap_halves(x):
    return jnp.concatenate([x[..., LANES:], x[..., :LANES]], axis=-1)


def _mods_kernel(c_ref, w_ref, b_ref, o_ref):
    c = c_ref[...]
    s = c * jax.nn.sigmoid(c)
    o_ref[...] = _dot_f32ish(s, w_ref[...]) + b_ref[...]


def _mods(cvec, ada_w, ada_b):
    depth, d, n6 = ada_w.shape
    tn = 1536
    return pl.pallas_call(
        _mods_kernel,
        grid=(depth, n6 // tn),
        in_specs=[pl.BlockSpec((8, d), lambda l, j: (0, 0)),
                  pl.BlockSpec((None, d, tn), lambda l, j: (l, 0, j)),
                  pl.BlockSpec((None, 1, tn), lambda l, j: (l, 0, j))],
        out_specs=pl.BlockSpec((None, 8, tn), lambda l, j: (l, 0, j)),
        out_shape=jax.ShapeDtypeStruct((depth, 8, n6), F32),
        compiler_params=_params(("arbitrary", "arbitrary")),
        name="adaln_mods",
    )(cvec, ada_w, ada_b.reshape(depth, 1, n6))


def _norm_mod_kernel(x_ref, w_ref, sh_ref, sc_ref, o_ref):
    x = x_ref[...]
    y = x * lax.rsqrt(jnp.mean(x * x, axis=-1, keepdims=True) + EPS) * w_ref[...]
    o_ref[...] = (y * (1.0 + sc_ref[...]) + sh_ref[...]).astype(o_ref.dtype)


def _norm_mod(x, w, mods3, col_shift, col_scale, seg, tm):
    nt, d = x.shape
    return pl.pallas_call(
        _norm_mod_kernel,
        grid=(nt // tm,),
        in_specs=[pl.BlockSpec((tm, d), lambda i: (i, 0)),
                  pl.BlockSpec((1, d), lambda i: (0, 0)),
                  pl.BlockSpec((None, 1, d), lambda i: (seg(i), 0, col_shift)),
                  pl.BlockSpec((None, 1, d), lambda i: (seg(i), 0, col_scale))],
        out_specs=pl.BlockSpec((tm, d), lambda i: (i, 0)),
        out_shape=jax.ShapeDtypeStruct((nt, d), BF16),
        compiler_params=_params(("arbitrary",)),
        name="norm_mod",
    )(x, w.reshape(1, d), mods3, mods3)


def _mm_tm_kernel(x_ref, w_ref, b_ref, o_ref, *, gate):
    y = jnp.dot(x_ref[...], w_ref[...], preferred_element_type=F32) + b_ref[...]
    o_ref[...] = (jax.nn.sigmoid(y) if gate else y).astype(o_ref.dtype)


def _pick(n, cands):
    for c in cands:
        if n % c == 0:
            return c
    raise ValueError(f"no tile for {n} in {cands}")


def _mm_tm(xn, w, b, *, gate):
    nt, k = xn.shape
    n = w.shape[1]
    tm = _pick(nt, (1056, 1024, 768, 512, 256))
    tn = _pick(n, (1536, 1408, 1152, 1024, 512, 384, 256, 128))
    return pl.pallas_call(
        functools.partial(_mm_tm_kernel, gate=gate),
        grid=(n // tn, nt // tm),
        in_specs=[pl.BlockSpec((tm, k), lambda j, i: (i, 0)),
                  pl.BlockSpec((k, tn), lambda j, i: (0, j)),
                  pl.BlockSpec((1, tn), lambda j, i: (0, j))],
        out_specs=pl.BlockSpec((tm, tn), lambda j, i: (i, j)),
        out_shape=jax.ShapeDtypeStruct((nt, n), BF16 if gate else F32),
        compiler_params=_params(("arbitrary", "arbitrary")),
        name="inproj_gates" if gate else "inproj_token_major",
    )(xn, w, b.reshape(1, n))


def _mm_slab_kernel(w_ref, x_ref, b_ref, o_ref, *, slabs):
    w = w_ref[...]
    b = b_ref[...]
    step = 2 if slabs % 2 == 0 else 1
    for s in range(0, slabs, step):
        xs = x_ref[s * LANES:(s + step) * LANES, :]
        y = lax.dot_general(w, xs, (((1,), (1,)), ((), ())), preferred_element_type=F32) + b
        for i in range(step):
            o_ref[s + i] = y[:, i * LANES:(i + 1) * LANES]


def _mm_slab(xn, wt, b):
    nt, k = xn.shape
    c = wt.shape[0]
    ns = nt // LANES
    ts = _pick(ns, (12, 11, 8, 6, 4, 3, 2, 1))
    tc = _pick(c, (1024, 896, 512, 256, 128))
    return pl.pallas_call(
        functools.partial(_mm_slab_kernel, slabs=ts),
        grid=(c // tc, ns // ts),
        in_specs=[pl.BlockSpec((tc, k), lambda j, i: (j, 0)),
                  pl.BlockSpec((ts * LANES, k), lambda j, i: (i, 0)),
                  pl.BlockSpec((tc, 1), lambda j, i: (j, 0))],
        out_specs=pl.BlockSpec((ts, tc, LANES), lambda j, i: (i, j, 0)),
        out_shape=jax.ShapeDtypeStruct((ns, c, LANES), F32),
        compiler_params=_params(("arbitrary", "arbitrary")),
        name="inproj_slab",
    )(wt, xn, b.reshape(c, 1))


def _conv_taps(rows, width):
    taps = []
    for dr in (-1, 0, 1):
        if rows == 1 and dr != 0:
            continue
        for dw in (-1, 0, 1):
            taps.append((dr, dw))
    return taps


def _silu(x):
    return x * jax.nn.sigmoid(x)


def _conv_kernel(x_ref, w_ref, b_ref, o_ref, *, taps, width, n_slabs, silu):
    ct = x_ref.shape[1]
    lane = lax.broadcasted_iota(jnp.int32, (ct, LANES), 1)
    bias = jnp.zeros((ct, LANES), F32) + b_ref[...]
    planes = []
    for t, (dr, dw) in enumerate(taps):
        w = w_ref[t]
        if width < LANES and dw != 0:
            col = lane % width + dw
            w = jnp.where((col >= 0) & (col < width), w, 0.0)
        planes.append((dr * width + dw, w))

    def body(a, carry):
        x0 = x_ref[a]
        xm = jnp.where(a > 0, x_ref[jnp.maximum(a - 1, 0)], 0.0)
        xp = jnp.where(a < n_slabs - 1, x_ref[jnp.minimum(a + 1, n_slabs - 1)], 0.0)
        acc = bias
        for delta, w in planes:
            if delta == 0:
                src = x0
            elif delta > 0:
                src = pltpu.roll(jnp.where(lane >= delta, x0, xp), LANES - delta, 1)
            else:
                src = pltpu.roll(jnp.where(lane < LANES + delta, x0, xm), -delta, 1)
            acc = acc + src * w
        o_ref[a] = _silu(acc) if silu else acc
        return carry

    lax.fori_loop(0, n_slabs, body, 0, unroll=2 if n_slabs % 2 == 0 else 1)


def _conv_grid_kernel(x_ref, w_ref, b_ref, o_ref, ym_ref, yp_ref, *, width, n_slabs, silu):
    ct = x_ref.shape[1]
    lane = lax.broadcasted_iota(jnp.int32, (ct, LANES), 1)
    col = lane % width
    bias = jnp.zeros((ct, LANES), F32) + b_ref[...]

    def row_sums(a, carry):
        x0 = x_ref[a]
        xl = jnp.where(col >= 1, pltpu.roll(x0, 1, 1), 0.0)
        xr = jnp.where(col < width - 1, pltpu.roll(x0, LANES - 1, 1), 0.0)
        ym_ref[a] = w_ref[0] * xl + w_ref[1] * x0 + w_ref[2] * xr
        o_ref[a] = bias + w_ref[3] * xl + w_ref[4] * x0 + w_ref[5] * xr
        yp_ref[a] = w_ref[6] * xl + w_ref[7] * x0 + w_ref[8] * xr
        return carry

    def combine(a, carry):
        up = jnp.where(a > 0, ym_ref[jnp.maximum(a - 1, 0)], 0.0)
        dn = jnp.where(a < n_slabs - 1, yp_ref[jnp.minimum(a + 1, n_slabs - 1)], 0.0)
        from_up = jnp.where(lane < LANES - width, ym_ref[a], up)
        from_dn = jnp.where(lane >= width, yp_ref[a], dn)
        if 2 * width == LANES:
            y = o_ref[a] + pltpu.roll(from_up + from_dn, width, 1)
        else:
            y = o_ref[a] + pltpu.roll(from_up, width, 1) + pltpu.roll(from_dn, LANES - width, 1)
        o_ref[a] = _silu(y) if silu else y
        return carry

    unroll = 8 if n_slabs % 8 == 0 else 1
    lax.fori_loop(0, n_slabs, row_sums, 0, unroll=unroll)
    lax.fori_loop(0, n_slabs, combine, 0, unroll=unroll)


def _conv(z_s, w9, bias, *, rows, width, n_batch, slab0, chan_lo, chan_n, silu):
    seq = rows * width
    a_n = seq // LANES
    taps = tuple(_conv_taps(rows, width))
    assert all(abs(dr * width + dw) < LANES for dr, dw in taps)
    assert LANES % width == 0 or (rows == 1 and width % LANES == 0)
    tap_ids = [(dr + 1) * 3 + (dw + 1) for dr, dw in taps]
    w_t = jnp.broadcast_to(w9[jnp.array(tap_ids)][:, :, None], (len(taps), chan_n, LANES))
    ct = 64 if rows > 1 else 256
    assert chan_lo % ct == 0 and chan_n % ct == 0 and slab0 % a_n == 0
    nt_ = len(taps)
    if rows > 1:
        assert LANES % width == 0 and nt_ == 9
        body = functools.partial(_conv_grid_kernel, width=width, n_slabs=a_n, silu=silu)
        scratch = [pltpu.VMEM((a_n, ct, LANES), F32), pltpu.VMEM((a_n, ct, LANES), F32)]
    else:
        body = functools.partial(_conv_kernel, taps=taps, width=width, n_slabs=a_n, silu=silu)
        scratch = []
    return pl.pallas_call(
        body,
        scratch_shapes=scratch,
        grid=(n_batch, chan_n // ct),
        in_specs=[pl.BlockSpec((a_n, ct, LANES), lambda b, j: (slab0 // a_n + b, chan_lo // ct + j, 0)),
                  pl.BlockSpec((nt_, ct, LANES), lambda b, j: (0, j, 0)),
                  pl.BlockSpec((ct, 1), lambda b, j: (j, 0))],
        out_specs=pl.BlockSpec((a_n, ct, LANES), lambda b, j: (b, j, 0)),
        out_shape=jax.ShapeDtypeStruct((n_batch * a_n, chan_n, LANES), F32),
        compiler_params=_params(("arbitrary", "arbitrary")),
        name=f"dwconv_{rows}x{width}",
    )(z_s, w_t, bias.reshape(chan_n, 1))


def _dft_consts(a_in, na):
    n = na * LANES
    k = np.arange(na)[:, None]
    a = np.arange(a_in)[None, :]
    ang = 2 * np.pi * (k * a % na) / na
    fa = np.concatenate([np.cos(ang), -np.sin(ang)], axis=0)
    r = np.arange(LANES)
    ang_t = 2 * np.pi * (np.arange(na)[:, None] * r[None, :] % n) / n
    tr, ti = np.cos(ang_t), -np.sin(ang_t)
    ta = np.concatenate([tr, tr], axis=1)
    tb = np.concatenate([-ti, ti], axis=1)
    ang2 = 2 * np.pi * (r[:, None] * r[None, :] % LANES) / LANES
    c2, s2 = np.cos(ang2), np.sin(ang2)
    g2 = np.block([[c2, -s2], [s2, c2]])
    g2i = np.block([[c2, s2], [-s2, c2]])
    ang_i = 2 * np.pi * (np.arange(a_in)[:, None] * np.arange(na)[None, :] % na) / na
    ci, si = np.cos(ang_i) / n, -np.sin(ang_i) / n
    f = lambda v, dt: jnp.asarray(v, dtype=dt)
    return dict(fa=f(fa, F32), ta=f(ta, F32), tb=f(tb, F32), g2=f(g2, F32), g2i=f(g2i, F32),
                ci=f(ci, F32), si=f(si, F32))


def _fwd_slab_stage(m, fa, ta, tb, na):
    pp = jnp.dot(fa, m.astype(BF16), preferred_element_type=F32)
    p = jnp.concatenate([pp[:na], pp[na:]], axis=1)
    return p * ta + _swap_halves(p) * tb


def _cmul(x, kf):
    kr, ki = kf[..., :LANES], kf[..., LANES:]
    ka = jnp.concatenate([kr, kr], axis=-1)
    kb = jnp.concatenate([-ki, ki], axis=-1)
    return x * ka + _swap_halves(x) * kb


def _chan_load(ref, c):
    n, cb, _ = ref.shape
    return ref.reshape(n * cb, LANES)[pl.ds(c, n, stride=cb), :]


def _chan_store(ref, c, val):
    ref[:, c, :] = val


def _dot_f32ish_k(w, h):
    wh, wm, _ = _split3(w)
    hh, hm, _ = _split3(h)
    lhs = jnp.concatenate([wh, wh, wm], axis=1)
    rhs = jnp.concatenate([hh, hm, hh], axis=0)
    return jnp.dot(lhs, rhs, preferred_element_type=F32)


def _taps_kernel(bands_ref, w1t_ref, w1c_ref, w1s_ref, b1_ref, w2_ref, b2_ref, fq_ref, w3_ref, dec_ref, o_ref,
                 *, seq, a_seq, na, spb):
    step = pl.program_id(0)
    width = spb * LANES
    n_total = na * LANES
    is_f = step * spb < a_seq
    n = step * width + lax.broadcasted_iota(jnp.int32, (1, width), 1)
    pos = jnp.where(is_f, n, n_total - n)
    lo, hi = jnp.where(is_f, -1, n_total - seq), jnp.where(is_f, seq, n_total)
    t = pos.astype(F32) / seq
    fq = fq_ref[...]
    ang = ((2 * math.pi) * t) * bands_ref[...]
    pre = (w1t_ref[...] * t + _dot_f32ish(w1c_ref[...], jnp.cos(ang))
           + _dot_f32ish(w1s_ref[...], jnp.sin(ang)))
    h = jnp.sin(fq * (pre + b1_ref[...]))
    h = jnp.sin(fq * (_dot_f32ish(w2_ref[...], h) + b2_ref[...]))
    d = jnp.where(is_f, 0, 1)
    dec = jnp.abs(dec_ref[d])
    k = _dot_f32ish_k(w3_ref[d], h)
    for i in range(spb):
        sl = slice(i * LANES, (i + 1) * LANES)
        live = (n[:, sl] > lo) & (n[:, sl] < hi)
        o_ref[i] = jnp.where(live, k[:, sl] * jnp.exp(-t[:, sl] * dec), 0.0)


def _hyena_taps(seq, na, lp):
    nc = HY_ORDER * D_BR
    hid = lp["hy_f_w2"].shape[0]
    w1 = lp["hy_f_w1"]
    col = lambda v: v.reshape(-1, 1)
    bands = col(jnp.linspace(1e-4, HY_BANDS - 1, HY_BANDS, dtype=F32))
    w3 = jnp.transpose(lp["hy_f_w3"].T.reshape(HY_ORDER, 2, D_BR, hid), (1, 0, 2, 3)).reshape(2, nc, hid)
    dec = jnp.broadcast_to(jnp.transpose(lp["hy_decay"], (1, 0, 2)).reshape(2, nc, 1), (2, nc, LANES))
    a_seq = seq // LANES
    spb = min(8, a_seq)
    assert na == 2 * a_seq and a_seq % spb == 0
    args = (bands, col(w1[0]), w1[1:1 + HY_BANDS].T, w1[1 + HY_BANDS:].T, col(lp["hy_f_b1"]), lp["hy_f_w2"].T,
            col(lp["hy_f_b2"]), col(lp["hy_f_freq"]), w3, dec)
    full = lambda v: pl.BlockSpec(v.shape, lambda s: (0,) * v.ndim)
    return pl.pallas_call(
        functools.partial(_taps_kernel, seq=seq, a_seq=a_seq, na=na, spb=spb),
        grid=(na // spb,),
        in_specs=[full(v) for v in args],
        out_specs=pl.BlockSpec((spb, nc, LANES), lambda s: (s, 0, 0)),
        out_shape=jax.ShapeDtypeStruct((na, nc, LANES), F32),
        compiler_params=_params(("arbitrary",)),
        name=f"hyena_filter_taps_{na}",
    )(*args)


def _hyena_kernel(v_ref, x1_ref, x2_ref, k0_ref, k1_ref, skip_ref, fa_ref, faf_ref, ta_ref, tb_ref, g2_ref, g2i_ref,
                  ci_ref, si_ref, o_ref, p_buf, z_buf, kf_buf, *, a_in, na):
    fa, ta, tb = fa_ref[...].astype(BF16), ta_ref[...], tb_ref[...]
    ci, si = ci_ref[...].astype(BF16), si_ref[...].astype(BF16)

    @pl.when(pl.program_id(1) == 0)
    def _():
        faf = faf_ref[...].astype(BF16)
        for order, k_ref in enumerate((k0_ref, k1_ref)):
            scales = []
            for c in range(CB):
                m = _chan_load(k_ref, c)
                ss = jnp.sum(jnp.sum(m * m, axis=1, keepdims=True), axis=0, keepdims=True)
                scales.append(lax.rsqrt(ss + EPS))
                p_buf[c] = _fwd_slab_stage(m, faf, ta, tb, na)
            x = _bdot(p_buf[...].reshape(CB * na, 2 * LANES), g2_ref[...]).reshape(CB, na, 2 * LANES)
            for c in range(CB):
                kf_buf[order, c] = x[c] * scales[c]

    def spectral(order):
        x = _bdot(p_buf[...].reshape(CB * na, 2 * LANES), g2_ref[...])
        y = _cmul(x, kf_buf[order].reshape(CB * na, 2 * LANES))
        bm = _bdot(y, g2i_ref[...]).reshape(CB, na, 2 * LANES)
        p_buf[...] = bm * ta - _swap_halves(bm) * tb

    def conv_out(c):
        bb = p_buf[c]
        return (jnp.dot(ci, bb[:, :LANES].astype(BF16), preferred_element_type=F32)
                + jnp.dot(si, bb[:, LANES:].astype(BF16), preferred_element_type=F32))

    for c in range(CB):
        p_buf[c] = _fwd_slab_stage(_chan_load(v_ref, c), fa, ta, tb, na)
    spectral(0)
    for c in range(CB):
        z_buf[c] = _chan_load(x1_ref, c) * (conv_out(c) + _chan_load(v_ref, c) * skip_ref[0, c])
    for c in range(CB):
        p_buf[c] = _fwd_slab_stage(z_buf[c], fa, ta, tb, na)
    spectral(1)
    for c in range(CB):
        _chan_store(o_ref, c, _chan_load(x2_ref, c) * (conv_out(c) + z_buf[c] * skip_ref[1, c]))


def _hyena(u_s, taps_s, skip, *, a_in, na, n_batch):
    cs = _dft_consts(a_in, na)
    faf = _dft_consts(na, na)["fa"]
    nblk = D_BR // CB
    const = lambda shp: pl.BlockSpec(shp, lambda j, b: (0,) * len(shp))
    skip_b = jnp.broadcast_to(skip[:, :, None, None], (HY_ORDER, D_BR, 1, LANES))
    return pl.pallas_call(
        functools.partial(_hyena_kernel, a_in=a_in, na=na),
        grid=(nblk, n_batch),
        in_specs=[pl.BlockSpec((a_in, CB, LANES), lambda j, b: (b, j, 0)),
                  pl.BlockSpec((a_in, CB, LANES), lambda j, b: (b, nblk + j, 0)),
                  pl.BlockSpec((a_in, CB, LANES), lambda j, b: (b, 2 * nblk + j, 0)),
                  pl.BlockSpec((na, CB, LANES), lambda j, b: (0, j, 0)),
                  pl.BlockSpec((na, CB, LANES), lambda j, b: (0, nblk + j, 0)),
                  pl.BlockSpec((HY_ORDER, CB, 1, LANES), lambda j, b: (0, j, 0, 0)),
                  const((2 * na, a_in)), const((2 * na, na)), const((na, 2 * LANES)), const((na, 2 * LANES)),
                  const((2 * LANES, 2 * LANES)), const((2 * LANES, 2 * LANES)),
                  const((a_in, na)), const((a_in, na))],
        out_specs=pl.BlockSpec((a_in, CB, LANES), lambda j, b: (b, j, 0)),
        out_shape=jax.ShapeDtypeStruct((n_batch * a_in, D_BR, LANES), F32),
        scratch_shapes=[pltpu.VMEM((CB, na, 2 * LANES), F32), pltpu.VMEM((CB, a_in, LANES), F32),
                        pltpu.VMEM((HY_ORDER, CB, na, 2 * LANES), F32)],
        compiler_params=_params(("arbitrary", "arbitrary")),
        name=f"hyena_longconv_{a_in}",
    )(u_s, u_s, u_s, taps_s, taps_s, skip_b, cs["fa"], faf, cs["ta"], cs["tb"], cs["g2"], cs["g2i"],
      cs["ci"], cs["si"])


def _slabs_to_rows(ref, n):
    return jnp.concatenate([ref[a].T for a in range(n)], axis=0)


def _hyena_short_kernel(v_ref, x1_ref, x2_ref, k0_ref, k1_ref, skip_ref, f_ref, g_ref, o_ref, *, a_n, a_k):
    seq, nf = a_n * LANES, a_k * LANES
    ff = f_ref[...].astype(BF16)
    gi = g_ref[...].astype(BF16)
    v, x1, x2 = _slabs_to_rows(v_ref, a_n), _slabs_to_rows(x1_ref, a_n), _slabs_to_rows(x2_ref, a_n)

    def longconv(u, k_ref):
        k = _slabs_to_rows(k_ref, a_k)
        s = lax.rsqrt(jnp.sum(k * k, axis=0, keepdims=True) + EPS)
        kf = jnp.dot(ff, k.astype(BF16), preferred_element_type=F32) * s
        x = jnp.dot(ff[:, :seq], u.astype(BF16), preferred_element_type=F32)
        xr, xi, kr, ki = x[:nf], x[nf:], kf[:nf], kf[nf:]
        y = jnp.concatenate([xr * kr - xi * ki, xr * ki + xi * kr], axis=0)
        return jnp.dot(gi, y.astype(BF16), preferred_element_type=F32)

    z = x1 * (longconv(v, k0_ref) + v * skip_ref[0])
    o_ref[...] = x2 * (longconv(z, k1_ref) + z * skip_ref[1])


def _hyena_short(u_s, taps_s, skip, *, a_n, n_batch):
    a_k = 2 * a_n
    seq, nf = a_n * LANES, a_k * LANES
    k = np.arange(nf)
    ang = 2 * np.pi * (k[:, None] * k[None, :] % nf) / nf
    f = np.concatenate([np.cos(ang), -np.sin(ang)], axis=0)
    g = np.concatenate([np.cos(ang[:seq]), -np.sin(ang[:seq])], axis=1) / nf
    nblk = D_BR // LANES
    const = lambda shp: pl.BlockSpec(shp, lambda b, j: (0,) * len(shp))
    return pl.pallas_call(
        functools.partial(_hyena_short_kernel, a_n=a_n, a_k=a_k),
        grid=(n_batch, nblk),
        in_specs=[pl.BlockSpec((a_n, LANES, LANES), lambda b, j: (b, j, 0)),
                  pl.BlockSpec((a_n, LANES, LANES), lambda b, j: (b, nblk + j, 0)),
                  pl.BlockSpec((a_n, LANES, LANES), lambda b, j: (b, 2 * nblk + j, 0)),
                  pl.BlockSpec((a_k, LANES, LANES), lambda b, j: (0, j, 0)),
                  pl.BlockSpec((a_k, LANES, LANES), lambda b, j: (0, nblk + j, 0)),
                  pl.BlockSpec((HY_ORDER, 1, LANES), lambda b, j: (0, 0, j)),
                  const((2 * nf, nf)), const((seq, 2 * nf))],
        out_specs=pl.BlockSpec((None, seq, LANES), lambda b, j: (b, 0, j)),
        out_shape=jax.ShapeDtypeStruct((n_batch, seq, D_BR), F32),
        compiler_params=_params(("arbitrary", "arbitrary")),
        name="hyena_short",
    )(u_s, u_s, u_s, taps_s, taps_s, skip.reshape(HY_ORDER, 1, D_BR), jnp.asarray(f, F32), jnp.asarray(g, F32))


def _chan_dft_mats():
    r = np.arange(LANES)
    ang = 2 * np.pi * (r[:, None] * r[None, :] % LANES) / LANES
    return np.cos(ang), np.sin(ang)


def _fn_fold_kernel(w_ref, cs_ref, o_ref):
    o_ref[...] = _dot_f32ish(w_ref[...], cs_ref[...])


def _fn_fold(w_fn, b_fn):
    d = w_fn.shape[0]
    c, s = _chan_dft_mats()
    cs = jnp.asarray(np.concatenate([c, s], axis=1), dtype=F32)
    rows = d + 8
    w_aug = jnp.concatenate([w_fn, b_fn[None, :], jnp.zeros((7, D_BR), F32)], axis=0)
    out = pl.pallas_call(
        _fn_fold_kernel,
        grid=(FN_GROUPS,),
        in_specs=[pl.BlockSpec((rows, LANES), lambda g: (0, g)),
                  pl.BlockSpec((LANES, 2 * LANES), lambda g: (0, 0))],
        out_specs=pl.BlockSpec((rows, 2 * LANES), lambda g: (0, g)),
        out_shape=jax.ShapeDtypeStruct((rows, 2 * D_BR), F32),
        compiler_params=_params(("arbitrary",)),
        name="fnet_fold_channel_dft",
    )(w_aug, cs)
    return out[:d], out[d]


def _fn_seq_kernel(p_ref, q_ref, fa_ref, tr_ref, ti_ref, g_ref, o_ref, a_buf, *, a_n, n_batch, scale):
    fa, tr, ti = fa_ref[...].astype(BF16), tr_ref[...], ti_ref[...]
    side = lambda m: jnp.concatenate([m[b * a_n:(b + 1) * a_n] for b in range(n_batch)], axis=1).astype(BF16)
    for c in range(CB):
        r1 = jnp.dot(fa, side(_chan_load(p_ref, c)), preferred_element_type=F32)
        r2 = jnp.dot(fa, side(_chan_load(q_ref, c)), preferred_element_type=F32)
        ar_all = r1[:a_n] - r2[a_n:]
        ai_all = -(r2[:a_n] + r1[a_n:])
        for b in range(n_batch):
            ar, ai = ar_all[:, b * LANES:(b + 1) * LANES], ai_all[:, b * LANES:(b + 1) * LANES]
            a_buf[b, c] = jnp.concatenate([ar * tr - ai * ti, ar * ti + ai * tr], axis=1)
    y = _bdot(a_buf[...].reshape(n_batch * CB * a_n, 2 * LANES), g_ref[...]) * scale
    o_ref[...] = y.reshape(n_batch, CB, a_n, LANES)


def _fn_seq(pq, *, a_n, n_batch, chan_lo):
    seq = a_n * LANES
    k = np.arange(a_n)
    ang = 2 * np.pi * (k[:, None] * k[None, :] % a_n) / a_n
    fa = np.concatenate([np.cos(ang), np.sin(ang)], axis=0)
    r = np.arange(LANES)
    ang_t = 2 * np.pi * (k[:, None] * r[None, :] % seq) / seq
    c2, s2 = _chan_dft_mats()
    g = np.concatenate([c2, s2], axis=0)
    nblk = LANES // CB
    const = lambda shp: pl.BlockSpec(shp, lambda j: (0,) * len(shp))

    def chan_blk(j, off):
        return chan_lo // CB + (j // nblk) * (2 * nblk) + off * nblk + j % nblk

    return pl.pallas_call(
        functools.partial(_fn_seq_kernel, a_n=a_n, n_batch=n_batch, scale=1.0 / math.sqrt(seq * LANES)),
        grid=(D_BR // CB,),
        in_specs=[pl.BlockSpec((n_batch * a_n, CB, LANES), lambda j: (0, chan_blk(j, 0), 0)),
                  pl.BlockSpec((n_batch * a_n, CB, LANES), lambda j: (0, chan_blk(j, 1), 0)),
                  const((2 * a_n, a_n)), const((a_n, LANES)), const((a_n, LANES)), const((2 * LANES, LANES))],
        out_specs=pl.BlockSpec((n_batch, CB, a_n, LANES), lambda j: (0, j, 0, 0)),
        out_shape=jax.ShapeDtypeStruct((n_batch, D_BR, a_n, LANES), F32),
        scratch_shapes=[pltpu.VMEM((n_batch, CB, a_n, 2 * LANES), F32)],
        compiler_params=_params(("arbitrary",)),
        name="fnet_sequence_dft",
    )(pq, pq, jnp.asarray(fa, F32), jnp.asarray(np.cos(ang_t), F32), jnp.asarray(-np.sin(ang_t), F32),
      jnp.asarray(g, F32))


def _fn_small_kernel(pq_ref, cl_ref, sl_ref, o_ref, *, a_n, scale):
    pq = jnp.concatenate([pq_ref[a].T for a in range(a_n)], axis=0)
    o_ref[...] = (_bdot(cl_ref[...], pq[:, :LANES]) - _bdot(sl_ref[...], pq[:, LANES:])) * scale


def _fn_small(pq, *, a_n, n_batch, chan_lo):
    seq = a_n * LANES
    n = np.arange(seq)
    ang = 2 * np.pi * (n[:, None] * n[None, :] % seq) / seq
    const = lambda shp: pl.BlockSpec(shp, lambda b, g: (0,) * len(shp))
    g0 = chan_lo // (2 * LANES)
    return pl.pallas_call(
        functools.partial(_fn_small_kernel, a_n=a_n, scale=1.0 / math.sqrt(seq * LANES)),
        grid=(n_batch, FN_GROUPS),
        in_specs=[pl.BlockSpec((a_n, 2 * LANES, LANES), lambda b, g: (b, g0 + g, 0)),
                  const((seq, seq)), const((seq, seq))],
        out_specs=pl.BlockSpec((None, seq, LANES), lambda b, g: (b, 0, g)),
        out_shape=jax.ShapeDtypeStruct((n_batch, seq, D_BR), F32),
        compiler_params=_params(("arbitrary", "arbitrary")),
        name="fnet_short",
    )(pq, jnp.asarray(np.cos(ang), F32), jnp.asarray(np.sin(ang), F32))


def _log_sigmoid(x):
    return jnp.minimum(x, 0.0) - jnp.log(1.0 + jnp.exp(-jnp.abs(x)))


def _exact_tri_dot(tri, x, tri_on_left):
    h, m, l = _split3(x)
    if tri_on_left:
        d = lambda p: jnp.dot(tri, p, preferred_element_type=F32)
    else:
        d = lambda p: jnp.dot(p, tri, preferred_element_type=F32)
    return d(h) + d(m) + d(l)


def _mlstm_step(inputs, c_st, n_st, m_st):
    t = hd = LANES
    n_dir, n_batch = len(inputs), len(inputs[0])
    n_grp = n_dir * n_batch * ML_HEADS
    row = lax.broadcasted_iota(jnp.int32, (t, t), 0)
    col = lax.broadcasted_iota(jnp.int32, (t, t), 1)
    tri = jnp.where(col <= row, 1.0, 0.0).astype(BF16)
    tri_t = jnp.where(col >= row, 1.0, 0.0).astype(BF16)
    qs, ks, vs, bcs, brs, ics, irs = [], [], [], [], [], [], []
    for d in range(n_dir):
        i_off = 2 * ML_HEADS * d
        f_off = i_off + ML_HEADS
        for b in range(n_batch):
            q_all, k_all, v_all, g = inputs[d][b]
            gt = g.T
            lf_c = _log_sigmoid(g)
            lf_r = lf_c.T
            if d == 1:
                b_c = _exact_tri_dot(tri_t, lf_c, True)
                b_r = _exact_tri_dot(tri, lf_r, False)
            else:
                b_c = _exact_tri_dot(tri, lf_c, True)
                b_r = _exact_tri_dot(tri_t, lf_r, False)
            for h in range(ML_HEADS):
                sl = slice(h * hd, (h + 1) * hd)
                qs.append(q_all[:, sl])
                ks.append(k_all[:, sl])
                vs.append(v_all[:, sl])
                bcs.append(b_c[:, f_off + h:f_off + h + 1])
                brs.append(b_r[f_off + h:f_off + h + 1, :])
                ics.append(g[:, i_off + h:i_off + h + 1])
                irs.append(gt[i_off + h:i_off + h + 1, :])
    q = jnp.stack(qs) * (hd ** -0.5)
    k, v = jnp.stack(ks), jnp.stack(vs)
    bc, br, ic, ir = jnp.stack(bcs), jnp.stack(brs), jnp.stack(ics), jnp.stack(irs)
    m_prev = m_st[...][:, :, :1]
    ct = c_st[...]
    n_prev = n_st[...]

    shp = (n_grp, t, t)
    grp = lax.broadcasted_iota(jnp.int32, shp, 0)
    r3, c3 = lax.broadcasted_iota(jnp.int32, shp, 1), lax.broadcasted_iota(jnp.int32, shp, 2)
    back = grp >= (n_grp // n_dir)
    mask = (back & (c3 >= r3)) | (jnp.logical_not(back) & (c3 <= r3))
    bdot = lambda a, b_, ca, cb: lax.dot_general(a.astype(BF16), b_.astype(BF16), (((ca,), (cb,)), ((0,), (0,))),
                                                 preferred_element_type=F32)
    e_ts = jnp.where(mask, ir - br, -jnp.inf)
    mm = jnp.maximum(m_prev, jnp.max(e_ts, axis=-1, keepdims=True))
    m_row = bc + mm
    w_intra = jnp.exp(e_ts - mm)
    w_inter = jnp.exp(m_prev - mm)
    s = bdot(q, k, 2, 2) * w_intra
    num = bdot(s, v, 2, 1) + w_inter * bdot(q, ct, 2, 1)
    den = jnp.sum(s, axis=-1, keepdims=True) + w_inter * jnp.sum(q * n_prev, axis=-1, keepdims=True)
    den = jnp.maximum(jnp.abs(den), jnp.exp(-m_row))
    h_all = num / den

    is_back = lax.broadcasted_iota(jnp.int32, (n_grp, 1, 1), 0) >= (n_grp // n_dir)
    b_tot_c = jnp.where(is_back, bc[:, :1, :], bc[:, t - 1:, :])
    b_tot_r = jnp.where(is_back, br[:, :, :1], br[:, :, t - 1:])
    a_c = b_tot_c - bc + ic
    a_r = b_tot_r - br + ir
    m_new = jnp.maximum(b_tot_c + m_prev, jnp.max(a_r, axis=-1, keepdims=True))
    sc = jnp.exp(a_c - m_new)
    decay = jnp.exp(b_tot_c + m_prev - m_new)
    k_sc = k * sc
    c_st[...] = decay * ct + bdot(k_sc, v, 1, 1)
    n_st[...] = decay * n_prev + jnp.sum(k_sc, axis=1, keepdims=True)
    m_st[...] = jnp.broadcast_to(m_new, (n_grp, 1, LANES))

    out = []
    for d in range(n_dir):
        out.append([jnp.concatenate([h_all[(d * n_batch + b) * ML_HEADS + h] for h in range(ML_HEADS)], axis=1)
                    for b in range(n_batch)])
    return out


def _mlstm_kernel(*refs, n_batch, ctx_chunks):
    lat_f, lat_b, ctx_f, ctx_b = refs[0:4], refs[4:8], refs[8:12], refs[12:16]
    hf_lat, hb_lat, hf_ctx, hb_ctx, c_st, n_st, m_st = refs[16:]
    j = pl.program_id(0)
    is_ctx = j < ctx_chunks

    @pl.when(j == 0)
    def _():
        c_st[...] = jnp.zeros(c_st.shape, F32)
        n_st[...] = jnp.zeros(n_st.shape, F32)
        m_st[...] = jnp.zeros(m_st.shape, F32)

    pick = lambda c_refs, l_refs, b: tuple(jnp.where(is_ctx, c[b], l[b]) for c, l in zip(c_refs, l_refs))
    inputs = [[pick(ctx_f, lat_f, b) for b in range(n_batch)], [pick(ctx_b, lat_b, b) for b in range(n_batch)]]
    hf, hb = _mlstm_step(inputs, c_st, n_st, m_st)

    @pl.when(is_ctx)
    def _():
        for b in range(n_batch):
            hf_ctx[b] = hf[b]
            hb_ctx[b] = hb[b]

    @pl.when(jnp.logical_not(is_ctx))
    def _():
        for b in range(n_batch):
            hf_lat[b] = hf[b]
            hb_lat[b] = hb[b]


def _mlstm(qk_lat, z_lat, qk_ctx, z_ctx, *, n_batch, v_col, g_col):
    lat_len, ctx_len = qk_lat.shape[0] // n_batch, qk_ctx.shape[0] // n_batch
    nlc, ncc = lat_len // LANES, ctx_len // LANES
    r3 = lambda a: a.reshape(n_batch, a.shape[0] // n_batch, a.shape[1])
    lf = lambda j: jnp.maximum(j - ncc, 0)
    lb = lambda j: jnp.where(j < ncc, nlc - 1, nlc - 1 - (j - ncc))
    cf = lambda j: jnp.minimum(j, ncc - 1)
    cb = lambda j: jnp.where(j < ncc, ncc - 1 - j, 0)

    def specs(ix):
        blk = lambda w, cidx: pl.BlockSpec((n_batch, LANES, w), lambda j: (0, ix(j), cidx))
        return [blk(D_BR, 0), blk(D_BR, 1), blk(D_BR, v_col), blk(LANES, g_col)]

    out = lambda ix: pl.BlockSpec((n_batch, LANES, D_BR), lambda j: (0, ix(j), 0))
    sd = lambda n: jax.ShapeDtypeStruct((n_batch, n, D_BR), F32)
    ql, zl, qc, zc = r3(qk_lat), r3(z_lat), r3(qk_ctx), r3(z_ctx)
    hf_lat, hb_lat, hf_ctx, hb_ctx = pl.pallas_call(
        functools.partial(_mlstm_kernel, n_batch=n_batch, ctx_chunks=ncc),
        grid=(ncc + nlc,),
        in_specs=specs(lf) + specs(lb) + specs(cf) + specs(cb),
        out_specs=[out(lf), out(lb), out(cf), out(cb)],
        out_shape=[sd(lat_len), sd(lat_len), sd(ctx_len), sd(ctx_len)],
        scratch_shapes=[pltpu.VMEM((2 * n_batch * ML_HEADS, LANES, LANES), F32),
                        pltpu.VMEM((2 * n_batch * ML_HEADS, 1, LANES), F32),
                        pltpu.VMEM((2 * n_batch * ML_HEADS, 1, LANES), F32)],
        compiler_params=_params(("arbitrary",)),
        name="mlstm_bidir",
    )(ql, ql, zl, zl, ql, ql, zl, zl, qc, qc, zc, zc, qc, qc, zc, zc)
    flat = lambda a: a.reshape(a.shape[0] * a.shape[1], D_BR)
    return (flat(hf_lat), flat(hb_lat)), (flat(hf_ctx), flat(hb_ctx))


def _rms_mod(x, w, shift, scale):
    y = x * lax.rsqrt(jnp.mean(x * x, axis=-1, keepdims=True) + EPS) * w
    return y * (1.0 + scale) + shift


def _route(t, rw, rb):
    logits = _dot_f32ish3(t, rw) + rb
    col = lax.broadcasted_iota(jnp.int32, logits.shape, 1)
    big = jnp.int32(1 << 20)
    ninf = -jnp.inf
    is_g = col < MOE_GROUPS
    gl = jnp.where(is_g, logits, ninf)
    gmax = jnp.max(gl, axis=-1, keepdims=True)
    g_sel = jnp.min(jnp.where(is_g & (gl == gmax), col, big), axis=-1, keepdims=True)
    p_top = 1.0 / jnp.sum(jnp.where(is_g, jnp.exp(gl - gmax), 0.0), axis=-1, keepdims=True)
    lo = MOE_GROUPS + g_sel * MOE_PER_GROUP
    in_grp = (col >= lo) & (col < lo + MOE_PER_GROUP)
    e1v = jnp.where(in_grp, logits, ninf)
    top1 = jnp.max(e1v, axis=-1, keepdims=True)
    idx1 = jnp.min(jnp.where(in_grp & (e1v == top1), col, big), axis=-1, keepdims=True)
    e2v = jnp.where(col == idx1, ninf, e1v)
    top2 = jnp.max(e2v, axis=-1, keepdims=True)
    idx2 = jnp.min(jnp.where(in_grp & (col != idx1) & (e2v == top2), col, big), axis=-1, keepdims=True)
    ex = jnp.exp(top2 - top1)
    s1 = 1.0 / (1.0 + ex)
    return jnp.where(col == idx1, p_top * s1, 0.0) + jnp.where(col == idx2, p_top * (ex * s1), 0.0)


def _merge_kernel(yh_ref, yf_ref, hf_ref, hb_ref, o_ref, g0_ref, g1_ref, g2_ref, x_ref, gate_ref,
                  wb_ref, wo_ref, nw_ref, n2_ref, sh_ref, sc_ref, rw_ref, rb_ref, out_ref, xn_ref, comb_ref):
    hd = LANES
    h = hf_ref[...] + hb_ref[...]
    parts = []
    for i in range(ML_HEADS):
        hh = h[:, i * hd:(i + 1) * hd]
        parts.append(hh * lax.rsqrt(jnp.mean(hh * hh, axis=-1, keepdims=True) + EPS))
    y_ml = jax.nn.sigmoid(o_ref[...]) * (jnp.concatenate(parts, axis=1) * nw_ref[...])
    acc = g0_ref[...].astype(F32) * _bdot(yh_ref[...], wb_ref[0])
    acc = acc + g1_ref[...].astype(F32) * _bdot(yf_ref[...], wb_ref[1])
    acc = acc + g2_ref[...].astype(F32) * _bdot(y_ml, wb_ref[2])
    x_new = x_ref[...] + gate_ref[...] * _bdot(acc, wo_ref[...])
    out_ref[...] = x_new
    t = _rms_mod(x_new, n2_ref[...], sh_ref[...], sc_ref[...])
    xn_ref[...] = t.astype(BF16)
    comb_ref[...] = _route(t, rw_ref[...], rb_ref[...])


def _merge(yh, yf, hf, hb, z_tm, gates, x, mods3, wb, wo, nw, n2w, rw, rb, *, seg, tm, o_col):
    nt, d = x.shape
    tok = lambda w, cidx: pl.BlockSpec((tm, w), lambda i: (i, cidx))
    mod = lambda k: pl.BlockSpec((None, 1, d), lambda i: (seg(i), 0, k))
    return pl.pallas_call(
        _merge_kernel,
        grid=(nt // tm,),
        in_specs=[tok(D_BR, 0), tok(D_BR, 0), tok(D_BR, 0), tok(D_BR, 0),
                  tok(D_BR, o_col),
                  tok(d, 0), tok(d, 1), tok(d, 2),
                  tok(d, 0),
                  mod(2),
                  pl.BlockSpec((3, D_BR, d), lambda i: (0, 0, 0)),
                  pl.BlockSpec((d, d), lambda i: (0, 0)),
                  pl.BlockSpec((1, D_BR), lambda i: (0, 0)),
                  pl.BlockSpec((1, d), lambda i: (0, 0)),
                  mod(3), mod(4),
                  pl.BlockSpec((d, LANES), lambda i: (0, 0)),
                  pl.BlockSpec((1, LANES), lambda i: (0, 0))],
        out_specs=[tok(d, 0), tok(d, 0), tok(LANES, 0)],
        out_shape=[jax.ShapeDtypeStruct((nt, d), F32), jax.ShapeDtypeStruct((nt, d), BF16),
                   jax.ShapeDtypeStruct((nt, LANES), F32)],
        compiler_params=_params(("arbitrary",)),
        name="merge_branches_router",
    )(yh, yf, hf, hb, z_tm, gates, gates, gates, x, mods3, wb, wo, nw.reshape(1, D_BR), n2w.reshape(1, d),
      mods3, mods3, rw, rb)


def _moe_kernel(xn_ref, comb_ref, wg_ref, wu_ref, wd_ref, x_ref, gate_ref, nw_ref, sh_ref, sc_ref, *out_and_scratch,
                final, n_keep):
    acc_ref = out_and_scratch[-1]
    e = pl.program_id(1)

    @pl.when(e == 0)
    def _():
        acc_ref[...] = jnp.zeros(acc_ref.shape, F32)

    xn = xn_ref[...]
    comb = comb_ref[...]
    col = lax.broadcasted_iota(jnp.int32, comb.shape, 1)
    acts = []
    for i in range(MOE_PER_GROUP):
        cw = jnp.sum(jnp.where(col == e * MOE_PER_GROUP + i + MOE_GROUPS, comb, 0.0), axis=-1, keepdims=True)
        hg = jnp.dot(xn, wg_ref[i], preferred_element_type=F32)
        hu = jnp.dot(xn, wu_ref[i], preferred_element_type=F32)
        acts.append(((hg * jax.nn.sigmoid(hg)) * hu * cw).astype(BF16))
    wd = wd_ref[...].reshape(MOE_PER_GROUP * EXPERT_HID, wd_ref.shape[-1])
    acc_ref[...] += jnp.dot(jnp.concatenate(acts, axis=1), wd, preferred_element_type=F32)

    n_steps = MOE_EXPERTS // MOE_PER_GROUP
    if final:
        y_ref, = out_and_scratch[:-1]

        @pl.when((e == n_steps - 1) & (pl.program_id(0) < n_keep))
        def _():
            x_new = x_ref[...] + gate_ref[...] * acc_ref[...]
            y_ref[...] = x_new * lax.rsqrt(jnp.mean(x_new * x_new, axis=-1, keepdims=True) + EPS) * nw_ref[...]
    else:
        o_ref, xn_next_ref = out_and_scratch[:-1]

        @pl.when(e == n_steps - 1)
        def _():
            x_new = x_ref[...] + gate_ref[...] * acc_ref[...]
            o_ref[...] = x_new
            xn_next_ref[...] = _rms_mod(x_new, nw_ref[...], sh_ref[...], sc_ref[...]).astype(BF16)


def _moe(xn, comb, wg, wu, wd, x, mods3, post_w, post_mods3, *, seg, tm, final, n_keep_rows):
    nt, d = x.shape
    n_keep = n_keep_rows // tm
    tok = pl.BlockSpec((tm, d), lambda i, e: (i, 0))
    if final:
        out_specs = [pl.BlockSpec((tm, d), lambda i, e: (jnp.minimum(i, n_keep - 1), 0))]
        out_shape = [jax.ShapeDtypeStruct((n_keep_rows, d), F32)]
    else:
        out_specs = [tok, tok]
        out_shape = [jax.ShapeDtypeStruct((nt, d), F32), jax.ShapeDtypeStruct((nt, d), BF16)]
    mod = lambda k: pl.BlockSpec((None, 1, d), lambda i, e: (seg(i), 0, k))
    return pl.pallas_call(
        functools.partial(_moe_kernel, final=final, n_keep=n_keep),
        grid=(nt // tm, MOE_EXPERTS // MOE_PER_GROUP),
        in_specs=[tok,
                  pl.BlockSpec((tm, LANES), lambda i, e: (i, 0)),
                  pl.BlockSpec((MOE_PER_GROUP, d, EXPERT_HID), lambda i, e: (e, 0, 0)),
                  pl.BlockSpec((MOE_PER_GROUP, d, EXPERT_HID), lambda i, e: (e, 0, 0)),
                  pl.BlockSpec((MOE_PER_GROUP, EXPERT_HID, d), lambda i, e: (e, 0, 0)),
                  tok,
                  mod(5),
                  pl.BlockSpec((1, d), lambda i, e: (0, 0)),
                  mod(0), mod(1)],
        out_specs=out_specs,
        out_shape=out_shape,
        scratch_shapes=[pltpu.VMEM((tm, d), F32)],
        compiler_params=_params(("arbitrary", "arbitrary")),
        name="moe_experts_final" if final else "moe_experts",
    )(xn, comb, wg, wu, wd, x, mods3, post_w.reshape(1, d), post_mods3, post_mods3)


def _slab_to_tm(y_s):
    ns, c, _ = y_s.shape
    return jnp.transpose(y_s, (0, 2, 1)).reshape(ns * LANES, c)


def kernel(x, c, ctx, c_ctx, ada_w, ada_b, norm1_w, norm2_w, w_in, b_in, hy_conv_w, hy_conv_b, hy_f_w1, hy_f_b1, hy_f_w2, hy_f_b2, hy_f_w3, hy_f_freq, hy_decay, hy_skip, ml_conv_w, ml_conv_b, ml_norm_w, w_branch, w_out, moe_rg_w, moe_rg_b, moe_re_w, moe_re_b, moe_w_gate, moe_w_up, moe_w_down, norm_f_w):
    nb, seq, d = x.shape
    lc = ctx.shape[1]
    depth = ada_w.shape[0]
    assert d == D_MODEL and seq % (GRID_W * 2) == 0 and lc % LANES == 0 and nb + 1 <= 8
    rows = seq // GRID_W
    a_lat = seq // LANES
    a_ctx = lc // LANES
    n_lat, n_ctx = nb * seq, nb * lc
    tm = 256
    tm_moe = {"lat": _pick(seq, (1024, 512, 256)), "ctx": _pick(n_ctx, (512, 256))}
    tm_mrg = {"lat": _pick(seq, (512, 256)), "ctx": _pick(n_ctx, (512, 256))}
    assert seq % tm == 0 and n_ctx % tm == 0
    seg_of = lambda s, t: (lambda i: i // (seq // t)) if s == "lat" else (lambda i: nb)
    streams = ("lat", "ctx")
    xs = {"lat": x.reshape(n_lat, d), "ctx": ctx.reshape(n_ctx, d)}
    xn = {}
    cvec = jnp.zeros((8, d), F32).at[:nb].set(c).at[nb].set(c_ctx)
    mods = _mods(cvec, ada_w, ada_b)

    o_fn, o_ml, o_mlg, o_gate = 3 * D_BR, 4 * D_BR, 8 * D_BR, 8 * D_BR + 4 * ML_HEADS
    pad_g = LANES - 4 * ML_HEADS

    for l in range(depth):
        lp = {"hy_f_w1": hy_f_w1[l], "hy_f_b1": hy_f_b1[l], "hy_f_w2": hy_f_w2[l], "hy_f_b2": hy_f_b2[l],
              "hy_f_w3": hy_f_w3[l], "hy_f_freq": hy_f_freq[l], "hy_decay": hy_decay[l]}
        mods3 = mods[l].reshape(8, 1, 6 * d)
        wl, bl = w_in[l], b_in[l]
        w_pq, b_pq = _fn_fold(wl[:, o_fn:o_ml], bl[o_fn:o_ml])
        w_cm = jnp.concatenate([wl[:, :o_fn], wl[:, o_ml:o_ml + 2 * D_BR], w_pq], axis=1)
        b_cm = jnp.concatenate([bl[:o_fn], bl[o_ml:o_ml + 2 * D_BR], b_pq])
        w_tm = jnp.concatenate([wl[:, o_ml + 2 * D_BR:o_mlg], wl[:, o_mlg:o_gate], jnp.zeros((d, pad_g), F32)], axis=1)
        b_tm = jnp.concatenate([bl[o_ml + 2 * D_BR:o_mlg], bl[o_mlg:o_gate], jnp.zeros((pad_g,), F32)])
        c_hy, c_qk, c_fn = 0, 3 * D_BR, 5 * D_BR
        g_col = (2 * D_BR) // LANES

        last = l + 1 == depth
        live = ("lat",) if last else streams
        if l == 0:
            xn = {s: _norm_mod(xs[s], norm1_w[l], mods3, 0, 1, seg_of(s, tm), tm) for s in streams}
        w_tm_b, w_cm_t = w_tm.astype(BF16), w_cm.T.astype(BF16)
        z_tm = {s: _mm_tm(xn[s], w_tm_b, b_tm, gate=False) for s in streams}
        w_gate_b = wl[:, o_gate:].astype(BF16)
        gates = {s: _mm_tm(xn[s], w_gate_b, bl[o_gate:], gate=True) for s in live}
        z_s = {s: _mm_slab(xn[s], w_cm_t, b_cm) for s in streams}

        grid_kw = {"lat": dict(rows=rows, width=GRID_W), "ctx": dict(rows=1, width=lc)}
        hy_w, hy_b = hy_conv_w[l].reshape(9, 3 * D_BR), hy_conv_b[l]
        ml_w, ml_b = ml_conv_w[l].reshape(9, 2 * D_BR), ml_conv_b[l]
        conv = lambda s, w, b, lo, n, act: _conv(z_s[s], w, b, chan_lo=lo, chan_n=n, silu=act, n_batch=nb,
                                                 slab0=0, **grid_kw[s])
        u = {s: conv(s, hy_w, hy_b, c_hy, 3 * D_BR, False) for s in live}
        qk = {s: _slab_to_tm(conv(s, ml_w, ml_b, c_qk, 2 * D_BR, True)) for s in streams}

        (hf_lat, hb_lat), (hf_ctx, hb_ctx) = _mlstm(qk["lat"], z_tm["lat"], qk["ctx"], z_tm["ctx"],
                                                    n_batch=nb, v_col=0, g_col=g_col)
        h_f, h_b = {"lat": hf_lat, "ctx": hf_ctx}, {"lat": hb_lat, "ctx": hb_ctx}

        yh, yf = {}, {}
        yh["lat"] = _slab_to_tm(_hyena(u["lat"], _hyena_taps(seq, 2 * a_lat, lp), hy_skip[l],
                                       a_in=a_lat, na=2 * a_lat, n_batch=nb))
        yk = _fn_seq(z_s["lat"], a_n=a_lat, n_batch=nb, chan_lo=c_fn)
        yf["lat"] = jnp.transpose(yk, (0, 3, 2, 1)).reshape(n_lat, D_BR)
        if not last:
            yh["ctx"] = _hyena_short(u["ctx"], _hyena_taps(lc, 2 * a_ctx, lp), hy_skip[l],
                                     a_n=a_ctx, n_batch=nb).reshape(n_ctx, D_BR)
            yf["ctx"] = _fn_small(z_s["ctx"], a_n=a_ctx, n_batch=nb, chan_lo=c_fn).reshape(n_ctx, D_BR)

        rw = jnp.concatenate([moe_rg_w[l], moe_re_w[l], jnp.zeros((d, LANES - MOE_GROUPS - MOE_EXPERTS), F32)], axis=1)
        rb = jnp.concatenate([moe_rg_b[l], moe_re_b[l], jnp.zeros((LANES - MOE_GROUPS - MOE_EXPERTS,), F32)]).reshape(1, LANES)
        wb, wo = w_branch[l].astype(BF16), w_out[l].astype(BF16)
        experts = (moe_w_gate[l].astype(BF16), moe_w_up[l].astype(BF16), moe_w_down[l].astype(BF16))
        for s in live:
            xs[s], xn2, comb = _merge(yh[s], yf[s], h_f[s], h_b[s], z_tm[s], gates[s], xs[s], mods3, wb, wo,
                                      ml_norm_w[l], norm2_w[l], rw, rb, seg=seg_of(s, tm_mrg[s]), tm=tm_mrg[s],
                                      o_col=1)
            moe_kw = dict(seg=seg_of(s, tm_moe[s]), tm=tm_moe[s], n_keep_rows=xs[s].shape[0])
            if last:
                out, = _moe(xn2, comb, *experts, xs[s], mods3, norm_f_w, mods3, final=True, **moe_kw)
            else:
                xs[s], xn[s] = _moe(xn2, comb, *experts, xs[s], mods3, norm1_w[l + 1],
                                    mods[l + 1].reshape(8, 1, 6 * d), final=False, **moe_kw)

    return out.reshape(nb, seq, d)
```

```python
import functools
import math

import numpy as np
import jax
import jax.numpy as jnp
from jax import lax
from jax.experimental import pallas as pl
from jax.experimental.pallas import tpu as pltpu

F32 = jnp.float32
BF16 = jnp.bfloat16

D_MODEL = 1024
D_BR = 512
GRID_W = 64
LANES = 128
CB = 8
HY_ORDER = 2
HY_BANDS = 16
FN_GROUPS = 4
ML_HEADS = 4
MOE_GROUPS = 4
MOE_PER_GROUP = 4
MOE_EXPERTS = 16
EXPERT_HID = 256
EPS = 1e-6
VMEM_LIMIT = 56 * 1024 * 1024


def _params(sem):
    return pltpu.CompilerParams(dimension_semantics=sem, vmem_limit_bytes=VMEM_LIMIT)


def _bdot(a, b):
    return jnp.dot(a.astype(BF16), b.astype(BF16), preferred_element_type=F32)


def _split3(x):
    hi = x.astype(BF16)
    r1 = x - hi.astype(F32)
    mid = r1.astype(BF16)
    lo = (r1 - mid.astype(F32)).astype(BF16)
    return hi, mid, lo


def _dot_f32ish(x, w):
    xh, xm, xl = _split3(x)
    wh, wm, wl = _split3(w)
    d = lambda a, b: jnp.dot(a, b, preferred_element_type=F32)
    return (d(xh, wh) + (d(xh, wm) + d(xm, wh))) + (d(xm, wm) + d(xh, wl) + d(xl, wh))


def _dot_f32ish3(x, w):
    xh, xm, _ = _split3(x)
    wh, wm, _ = _split3(w)
    d = lambda a, b: jnp.dot(a, b, preferred_element_type=F32)
    return d(xh, wh) + (d(xh, wm) + d(xm, wh))


def _swap_halves(x):
    return jnp.concatenate([x[..., LANES:], x[..., :LANES]], axis=-1)


def _mods_kernel(c_ref, w_ref, b_ref, o_ref):
    c = c_ref[...]
    s = c * jax.nn.sigmoid(c)
    o_ref[...] = _dot_f32ish(s, w_ref[...]) + b_ref[...]


def _mods(cvec, ada_w, ada_b):
    depth, d, n6 = ada_w.shape
    tn = 1536
    return pl.pallas_call(
        _mods_kernel,
        grid=(depth, n6 // tn),
        in_specs=[pl.BlockSpec((8, d), lambda l, j: (0, 0)),
                  pl.BlockSpec((None, d, tn), lambda l, j: (l, 0, j)),
                  pl.BlockSpec((None, 1, tn), lambda l, j: (l, 0, j))],
        out_specs=pl.BlockSpec((None, 8, tn), lambda l, j: (l, 0, j)),
        out_shape=jax.ShapeDtypeStruct((depth, 8, n6), F32),
        compiler_params=_params(("arbitrary", "arbitrary")),
        name="adaln_mods",
    )(cvec, ada_w, ada_b.reshape(depth, 1, n6))


def _norm_mod_kernel(x_ref, w_ref, sh_ref, sc_ref, o_ref):
    x = x_ref[...]
    y = x * lax.rsqrt(jnp.mean(x * x, axis=-1, keepdims=True) + EPS) * w_ref[...]
    o_ref[...] = (y * (1.0 + sc_ref[...]) + sh_ref[...]).astype(o_ref.dtype)


def _norm_mod(x, w, mods3, col_shift, col_scale, seg, tm):
    nt, d = x.shape
    return pl.pallas_call(
        _norm_mod_kernel,
        grid=(nt // tm,),
        in_specs=[pl.BlockSpec((tm, d), lambda i: (i, 0)),
                  pl.BlockSpec((1, d), lambda i: (0, 0)),
                  pl.BlockSpec((None, 1, d), lambda i: (seg(i), 0, col_shift)),
                  pl.BlockSpec((None, 1, d), lambda i: (seg(i), 0, col_scale))],
        out_specs=pl.BlockSpec((tm, d), lambda i: (i, 0)),
        out_shape=jax.ShapeDtypeStruct((nt, d), BF16),
        compiler_params=_params(("arbitrary",)),
        name="norm_mod",
    )(x, w.reshape(1, d), mods3, mods3)


def _mm_tm_kernel(x_ref, w_ref, b_ref, o_ref, *, gate):
    y = jnp.dot(x_ref[...], w_ref[...], preferred_element_type=F32) + b_ref[...]
    o_ref[...] = (jax.nn.sigmoid(y) if gate else y).astype(o_ref.dtype)


def _pick(n, cands):
    for c in cands:
        if n % c == 0:
            return c
    raise ValueError(f"no tile for {n} in {cands}")


def _mm_tm(xn, w, b, *, gate):
    nt, k = xn.shape
    n = w.shape[1]
    tm = _pick(nt, (1056, 1024, 768, 512, 256))
    tn = _pick(n, (1536, 1408, 1152, 1024, 640, 512, 384, 256, 128))
    return pl.pallas_call(
        functools.partial(_mm_tm_kernel, gate=gate),
        grid=(n // tn, nt // tm),
        in_specs=[pl.BlockSpec((tm, k), lambda j, i: (i, 0)),
                  pl.BlockSpec((k, tn), lambda j, i: (0, j)),
                  pl.BlockSpec((1, tn), lambda j, i: (0, j))],
        out_specs=pl.BlockSpec((tm, tn), lambda j, i: (i, j)),
        out_shape=jax.ShapeDtypeStruct((nt, n), BF16 if gate else F32),
        compiler_params=_params(("arbitrary", "arbitrary")),
        name="inproj_gates" if gate else "inproj_token_major",
    )(xn, w, b.reshape(1, n))


def _mm_slab_kernel(w_ref, x_ref, b_ref, o_ref, *, slabs):
    w = w_ref[...]
    b = b_ref[...]
    step = 2 if slabs % 2 == 0 else 1
    for s in range(0, slabs, step):
        xs = x_ref[s * LANES:(s + step) * LANES, :]
        y = lax.dot_general(w, xs, (((1,), (1,)), ((), ())), preferred_element_type=F32) + b
        for i in range(step):
            o_ref[s + i] = y[:, i * LANES:(i + 1) * LANES]


def _mm_slab(xn, wt, b):
    nt, k = xn.shape
    c = wt.shape[0]
    ns = nt // LANES
    ts = _pick(ns, (12, 11, 8, 6, 4, 3, 2, 1))
    tc = _pick(c, (1024, 896, 512, 256, 128))
    return pl.pallas_call(
        functools.partial(_mm_slab_kernel, slabs=ts),
        grid=(c // tc, ns // ts),
        in_specs=[pl.BlockSpec((tc, k), lambda j, i: (j, 0)),
                  pl.BlockSpec((ts * LANES, k), lambda j, i: (i, 0)),
                  pl.BlockSpec((tc, 1), lambda j, i: (j, 0))],
        out_specs=pl.BlockSpec((ts, tc, LANES), lambda j, i: (i, j, 0)),
        out_shape=jax.ShapeDtypeStruct((ns, c, LANES), F32),
        compiler_params=_params(("arbitrary", "arbitrary")),
        name="inproj_slab",
    )(wt, xn, b.reshape(c, 1))


def _conv_taps(rows, width):
    taps = []
    for dr in (-1, 0, 1):
        if rows == 1 and dr != 0:
            continue
        for dw in (-1, 0, 1):
            taps.append((dr, dw))
    return taps


def _silu(x):
    return x * jax.nn.sigmoid(x)


def _conv_kernel(x_ref, w_ref, b_ref, o_ref, *, taps, width, n_slabs, silu):
    ct = x_ref.shape[1]
    lane = lax.broadcasted_iota(jnp.int32, (ct, LANES), 1)
    bias = jnp.zeros((ct, LANES), F32) + b_ref[...]
    planes = []
    for t, (dr, dw) in enumerate(taps):
        w = w_ref[t]
        if width < LANES and dw != 0:
            col = lane % width + dw
            w = jnp.where((col >= 0) & (col < width), w, 0.0)
        planes.append((dr * width + dw, w))

    def body(a, carry):
        x0 = x_ref[a]
        xm = jnp.where(a > 0, x_ref[jnp.maximum(a - 1, 0)], 0.0)
        xp = jnp.where(a < n_slabs - 1, x_ref[jnp.minimum(a + 1, n_slabs - 1)], 0.0)
        acc = bias
        for delta, w in planes:
            if delta == 0:
                src = x0
            elif delta > 0:
                src = pltpu.roll(jnp.where(lane >= delta, x0, xp), LANES - delta, 1)
            else:
                src = pltpu.roll(jnp.where(lane < LANES + delta, x0, xm), -delta, 1)
            acc = acc + src * w
        o_ref[a] = _silu(acc) if silu else acc
        return carry

    lax.fori_loop(0, n_slabs, body, 0, unroll=2 if n_slabs % 2 == 0 else 1)


def _conv_grid_kernel(x_ref, w_ref, b_ref, o_ref, ym_ref, yp_ref, *, width, n_slabs, silu):
    ct = x_ref.shape[1]
    lane = lax.broadcasted_iota(jnp.int32, (ct, LANES), 1)
    col = lane % width
    bias = jnp.zeros((ct, LANES), F32) + b_ref[...]

    def row_sums(a, carry):
        x0 = x_ref[a]
        xl = jnp.where(col >= 1, pltpu.roll(x0, 1, 1), 0.0)
        xr = jnp.where(col < width - 1, pltpu.roll(x0, LANES - 1, 1), 0.0)
        ym_ref[a] = w_ref[0] * xl + w_ref[1] * x0 + w_ref[2] * xr
        o_ref[a] = bias + w_ref[3] * xl + w_ref[4] * x0 + w_ref[5] * xr
        yp_ref[a] = w_ref[6] * xl + w_ref[7] * x0 + w_ref[8] * xr
        return carry

    def combine(a, carry):
        up = jnp.where(a > 0, ym_ref[jnp.maximum(a - 1, 0)], 0.0)
        dn = jnp.where(a < n_slabs - 1, yp_ref[jnp.minimum(a + 1, n_slabs - 1)], 0.0)
        from_up = jnp.where(lane < LANES - width, ym_ref[a], up)
        from_dn = jnp.where(lane >= width, yp_ref[a], dn)
        if 2 * width == LANES:
            y = o_ref[a] + pltpu.roll(from_up + from_dn, width, 1)
        else:
            y = o_ref[a] + pltpu.roll(from_up, width, 1) + pltpu.roll(from_dn, LANES - width, 1)
        o_ref[a] = _silu(y) if silu else y
        return carry

    unroll = 8 if n_slabs % 8 == 0 else 1
    lax.fori_loop(0, n_slabs, row_sums, 0, unroll=unroll)
    lax.fori_loop(0, n_slabs, combine, 0, unroll=unroll)


def _conv(z_s, w9, bias, *, rows, width, n_batch, slab0, chan_lo, chan_n, silu):
    seq = rows * width
    a_n = seq // LANES
    taps = tuple(_conv_taps(rows, width))
    assert all(abs(dr * width + dw) < LANES for dr, dw in taps)
    assert LANES % width == 0 or (rows == 1 and width % LANES == 0)
    tap_ids = [(dr + 1) * 3 + (dw + 1) for dr, dw in taps]
    w_t = jnp.broadcast_to(w9[jnp.array(tap_ids)][:, :, None], (len(taps), chan_n, LANES))
    ct = 64 if rows > 1 else 256
    assert chan_lo % ct == 0 and chan_n % ct == 0 and slab0 % a_n == 0
    nt_ = len(taps)
    if rows > 1:
        assert LANES % width == 0 and nt_ == 9
        body = functools.partial(_conv_grid_kernel, width=width, n_slabs=a_n, silu=silu)
        scratch = [pltpu.VMEM((a_n, ct, LANES), F32), pltpu.VMEM((a_n, ct, LANES), F32)]
    else:
        body = functools.partial(_conv_kernel, taps=taps, width=width, n_slabs=a_n, silu=silu)
        scratch = []
    return pl.pallas_call(
        body,
        scratch_shapes=scratch,
        grid=(n_batch, chan_n // ct),
        in_specs=[pl.BlockSpec((a_n, ct, LANES), lambda b, j: (slab0 // a_n + b, chan_lo // ct + j, 0)),
                  pl.BlockSpec((nt_, ct, LANES), lambda b, j: (0, j, 0)),
                  pl.BlockSpec((ct, 1), lambda b, j: (j, 0))],
        out_specs=pl.BlockSpec((a_n, ct, LANES), lambda b, j: (b, j, 0)),
        out_shape=jax.ShapeDtypeStruct((n_batch * a_n, chan_n, LANES), F32),
        compiler_params=_params(("arbitrary", "arbitrary")),
        name=f"dwconv_{rows}x{width}",
    )(z_s, w_t, bias.reshape(chan_n, 1))


def _dft_consts(a_in, na):
    n = na * LANES
    k = np.arange(na)[:, None]
    a = np.arange(a_in)[None, :]
    ang = 2 * np.pi * (k * a % na) / na
    fa = np.concatenate([np.cos(ang), -np.sin(ang)], axis=0)
    r = np.arange(LANES)
    ang_t = 2 * np.pi * (np.arange(na)[:, None] * r[None, :] % n) / n
    tr, ti = np.cos(ang_t), -np.sin(ang_t)
    ta = np.concatenate([tr, tr], axis=1)
    tb = np.concatenate([-ti, ti], axis=1)
    ang2 = 2 * np.pi * (r[:, None] * r[None, :] % LANES) / LANES
    c2, s2 = np.cos(ang2), np.sin(ang2)
    g2 = np.block([[c2, -s2], [s2, c2]])
    g2i = np.block([[c2, s2], [-s2, c2]])
    ang_i = 2 * np.pi * (np.arange(a_in)[:, None] * np.arange(na)[None, :] % na) / na
    ci, si = np.cos(ang_i) / n, -np.sin(ang_i) / n
    f = lambda v, dt: jnp.asarray(v, dtype=dt)
    return dict(fa=f(fa, F32), ta=f(ta, F32), tb=f(tb, F32), g2=f(g2, F32), g2i=f(g2i, F32),
                ci=f(ci, F32), si=f(si, F32))


def _fwd_slab_stage(m, fa, ta, tb, na):
    pp = jnp.dot(fa, m.astype(BF16), preferred_element_type=F32)
    p = jnp.concatenate([pp[:na], pp[na:]], axis=1)
    return p * ta + _swap_halves(p) * tb


def _cmul(x, kf):
    kr, ki = kf[..., :LANES], kf[..., LANES:]
    ka = jnp.concatenate([kr, kr], axis=-1)
    kb = jnp.concatenate([-ki, ki], axis=-1)
    return x * ka + _swap_halves(x) * kb


def _chan_load(ref, c):
    n, cb, _ = ref.shape
    return ref.reshape(n * cb, LANES)[pl.ds(c, n, stride=cb), :]


def _chan_store(ref, c, val):
    ref[:, c, :] = val


def _dot_f32ish_k(w, h):
    wh, wm, _ = _split3(w)
    hh, hm, _ = _split3(h)
    lhs = jnp.concatenate([wh, wh, wm], axis=1)
    rhs = jnp.concatenate([hh, hm, hh], axis=0)
    return jnp.dot(lhs, rhs, preferred_element_type=F32)


def _taps_kernel(bands_ref, w1t_ref, w1c_ref, w1s_ref, b1_ref, w2_ref, b2_ref, fq_ref, w3_ref, dec_ref, o_ref,
                 *, seq, a_seq, na, spb):
    step = pl.program_id(0)
    width = spb * LANES
    n_total = na * LANES
    is_f = step * spb < a_seq
    n = step * width + lax.broadcasted_iota(jnp.int32, (1, width), 1)
    pos = jnp.where(is_f, n, n_total - n)
    lo, hi = jnp.where(is_f, -1, n_total - seq), jnp.where(is_f, seq, n_total)
    t = pos.astype(F32) / seq
    fq = fq_ref[...]
    ang = ((2 * math.pi) * t) * bands_ref[...]
    pre = (w1t_ref[...] * t + _dot_f32ish(w1c_ref[...], jnp.cos(ang))
           + _dot_f32ish(w1s_ref[...], jnp.sin(ang)))
    h = jnp.sin(fq * (pre + b1_ref[...]))
    h = jnp.sin(fq * (_dot_f32ish(w2_ref[...], h) + b2_ref[...]))
    d = jnp.where(is_f, 0, 1)
    dec = jnp.abs(dec_ref[d])
    k = _dot_f32ish_k(w3_ref[d], h)
    for i in range(spb):
        sl = slice(i * LANES, (i + 1) * LANES)
        live = (n[:, sl] > lo) & (n[:, sl] < hi)
        o_ref[i] = jnp.where(live, k[:, sl] * jnp.exp(-t[:, sl] * dec), 0.0)


def _hyena_taps(seq, na, lp):
    nc = HY_ORDER * D_BR
    hid = lp["hy_f_w2"].shape[0]
    w1 = lp["hy_f_w1"]
    col = lambda v: v.reshape(-1, 1)
    bands = col(jnp.linspace(1e-4, HY_BANDS - 1, HY_BANDS, dtype=F32))
    w3 = jnp.transpose(lp["hy_f_w3"].T.reshape(HY_ORDER, 2, D_BR, hid), (1, 0, 2, 3)).reshape(2, nc, hid)
    dec = jnp.broadcast_to(jnp.transpose(lp["hy_decay"], (1, 0, 2)).reshape(2, nc, 1), (2, nc, LANES))
    a_seq = seq // LANES
    spb = min(8, a_seq)
    assert na == 2 * a_seq and a_seq % spb == 0
    args = (bands, col(w1[0]), w1[1:1 + HY_BANDS].T, w1[1 + HY_BANDS:].T, col(lp["hy_f_b1"]), lp["hy_f_w2"].T,
            col(lp["hy_f_b2"]), col(lp["hy_f_freq"]), w3, dec)
    full = lambda v: pl.BlockSpec(v.shape, lambda s: (0,) * v.ndim)
    return pl.pallas_call(
        functools.partial(_taps_kernel, seq=seq, a_seq=a_seq, na=na, spb=spb),
        grid=(na // spb,),
        in_specs=[full(v) for v in args],
        out_specs=pl.BlockSpec((spb, nc, LANES), lambda s: (s, 0, 0)),
        out_shape=jax.ShapeDtypeStruct((na, nc, LANES), F32),
        compiler_params=_params(("arbitrary",)),
        name=f"hyena_filter_taps_{na}",
    )(*args)


def _hyena_kernel(v_ref, x1_ref, x2_ref, k0_ref, k1_ref, skip_ref, fa_ref, faf_ref, ta_ref, tb_ref, g2_ref, g2i_ref,
                  ci_ref, si_ref, o_ref, p_buf, z_buf, kf_buf, *, a_in, na):
    fa, ta, tb = fa_ref[...].astype(BF16), ta_ref[...], tb_ref[...]
    ci, si = ci_ref[...].astype(BF16), si_ref[...].astype(BF16)

    @pl.when(pl.program_id(1) == 0)
    def _():
        faf = faf_ref[...].astype(BF16)
        for order, k_ref in enumerate((k0_ref, k1_ref)):
            scales = []
            for c in range(CB):
                m = _chan_load(k_ref, c)
                ss = jnp.sum(jnp.sum(m * m, axis=1, keepdims=True), axis=0, keepdims=True)
                scales.append(lax.rsqrt(ss + EPS))
                p_buf[c] = _fwd_slab_stage(m, faf, ta, tb, na)
            x = _bdot(p_buf[...].reshape(CB * na, 2 * LANES), g2_ref[...]).reshape(CB, na, 2 * LANES)
            for c in range(CB):
                kf_buf[order, c] = x[c] * scales[c]

    def spectral(order):
        x = _bdot(p_buf[...].reshape(CB * na, 2 * LANES), g2_ref[...])
        y = _cmul(x, kf_buf[order].reshape(CB * na, 2 * LANES))
        bm = _bdot(y, g2i_ref[...]).reshape(CB, na, 2 * LANES)
        p_buf[...] = bm * ta - _swap_halves(bm) * tb

    def conv_out(c):
        bb = p_buf[c]
        return (jnp.dot(ci, bb[:, :LANES].astype(BF16), preferred_element_type=F32)
                + jnp.dot(si, bb[:, LANES:].astype(BF16), preferred_element_type=F32))

    for c in range(CB):
        p_buf[c] = _fwd_slab_stage(_chan_load(v_ref, c), fa, ta, tb, na)
    spectral(0)
    for c in range(CB):
        z_buf[c] = _chan_load(x1_ref, c) * (conv_out(c) + _chan_load(v_ref, c) * skip_ref[0, c])
    for c in range(CB):
        p_buf[c] = _fwd_slab_stage(z_buf[c], fa, ta, tb, na)
    spectral(1)
    for c in range(CB):
        _chan_store(o_ref, c, _chan_load(x2_ref, c) * (conv_out(c) + z_buf[c] * skip_ref[1, c]))


def _hyena(u_s, taps_s, skip, *, a_in, na, n_batch):
    cs = _dft_consts(a_in, na)
    faf = _dft_consts(na, na)["fa"]
    nblk = D_BR // CB
    const = lambda shp: pl.BlockSpec(shp, lambda j, b: (0,) * len(shp))
    skip_b = jnp.broadcast_to(skip[:, :, None, None], (HY_ORDER, D_BR, 1, LANES))
    return pl.pallas_call(
        functools.partial(_hyena_kernel, a_in=a_in, na=na),
        grid=(nblk, n_batch),
        in_specs=[pl.BlockSpec((a_in, CB, LANES), lambda j, b: (b, j, 0)),
                  pl.BlockSpec((a_in, CB, LANES), lambda j, b: (b, nblk + j, 0)),
                  pl.BlockSpec((a_in, CB, LANES), lambda j, b: (b, 2 * nblk + j, 0)),
                  pl.BlockSpec((na, CB, LANES), lambda j, b: (0, j, 0)),
                  pl.BlockSpec((na, CB, LANES), lambda j, b: (0, nblk + j, 0)),
                  pl.BlockSpec((HY_ORDER, CB, 1, LANES), lambda j, b: (0, j, 0, 0)),
                  const((2 * na, a_in)), const((2 * na, na)), const((na, 2 * LANES)), const((na, 2 * LANES)),
                  const((2 * LANES, 2 * LANES)), const((2 * LANES, 2 * LANES)),
                  const((a_in, na)), const((a_in, na))],
        out_specs=pl.BlockSpec((a_in, CB, LANES), lambda j, b: (b, j, 0)),
        out_shape=jax.ShapeDtypeStruct((n_batch * a_in, D_BR, LANES), F32),
        scratch_shapes=[pltpu.VMEM((CB, na, 2 * LANES), F32), pltpu.VMEM((CB, a_in, LANES), F32),
                        pltpu.VMEM((HY_ORDER, CB, na, 2 * LANES), F32)],
        compiler_params=_params(("arbitrary", "arbitrary")),
        name=f"hyena_longconv_{a_in}",
    )(u_s, u_s, u_s, taps_s, taps_s, skip_b, cs["fa"], faf, cs["ta"], cs["tb"], cs["g2"], cs["g2i"],
      cs["ci"], cs["si"])


def _slabs_to_rows(ref, n):
    return jnp.concatenate([ref[a].T for a in range(n)], axis=0)


def _hyena_short_kernel(v_ref, x1_ref, x2_ref, k0_ref, k1_ref, skip_ref, f_ref, g_ref, o_ref, *, a_n, a_k):
    seq, nf = a_n * LANES, a_k * LANES
    ff = f_ref[...].astype(BF16)
    gi = g_ref[...].astype(BF16)
    v, x1, x2 = _slabs_to_rows(v_ref, a_n), _slabs_to_rows(x1_ref, a_n), _slabs_to_rows(x2_ref, a_n)

    def longconv(u, k_ref):
        k = _slabs_to_rows(k_ref, a_k)
        s = lax.rsqrt(jnp.sum(k * k, axis=0, keepdims=True) + EPS)
        kf = jnp.dot(ff, k.astype(BF16), preferred_element_type=F32) * s
        x = jnp.dot(ff[:, :seq], u.astype(BF16), preferred_element_type=F32)
        xr, xi, kr, ki = x[:nf], x[nf:], kf[:nf], kf[nf:]
        y = jnp.concatenate([xr * kr - xi * ki, xr * ki + xi * kr], axis=0)
        return jnp.dot(gi, y.astype(BF16), preferred_element_type=F32)

    z = x1 * (longconv(v, k0_ref) + v * skip_ref[0])
    o_ref[...] = x2 * (longconv(z, k1_ref) + z * skip_ref[1])


def _hyena_short(u_s, taps_s, skip, *, a_n, n_batch):
    a_k = 2 * a_n
    seq, nf = a_n * LANES, a_k * LANES
    k = np.arange(nf)
    ang = 2 * np.pi * (k[:, None] * k[None, :] % nf) / nf
    f = np.concatenate([np.cos(ang), -np.sin(ang)], axis=0)
    g = np.concatenate([np.cos(ang[:seq]), -np.sin(ang[:seq])], axis=1) / nf
    nblk = D_BR // LANES
    const = lambda shp: pl.BlockSpec(shp, lambda b, j: (0,) * len(shp))
    return pl.pallas_call(
        functools.partial(_hyena_short_kernel, a_n=a_n, a_k=a_k),
        grid=(n_batch, nblk),
        in_specs=[pl.BlockSpec((a_n, LANES, LANES), lambda b, j: (b, j, 0)),
                  pl.BlockSpec((a_n, LANES, LANES), lambda b, j: (b, nblk + j, 0)),
                  pl.BlockSpec((a_n, LANES, LANES), lambda b, j: (b, 2 * nblk + j, 0)),
                  pl.BlockSpec((a_k, LANES, LANES), lambda b, j: (0, j, 0)),
                  pl.BlockSpec((a_k, LANES, LANES), lambda b, j: (0, nblk + j, 0)),
                  pl.BlockSpec((HY_ORDER, 1, LANES), lambda b, j: (0, 0, j)),
                  const((2 * nf, nf)), const((seq, 2 * nf))],
        out_specs=pl.BlockSpec((None, seq, LANES), lambda b, j: (b, 0, j)),
        out_shape=jax.ShapeDtypeStruct((n_batch, seq, D_BR), F32),
        compiler_params=_params(("arbitrary", "arbitrary")),
        name="hyena_short",
    )(u_s, u_s, u_s, taps_s, taps_s, skip.reshape(HY_ORDER, 1, D_BR), jnp.asarray(f, F32), jnp.asarray(g, F32))


def _chan_dft_mats():
    r = np.arange(LANES)
    ang = 2 * np.pi * (r[:, None] * r[None, :] % LANES) / LANES
    return np.cos(ang), np.sin(ang)


def _fn_fold_kernel(w_ref, cs_ref, o_ref):
    o_ref[...] = _dot_f32ish(w_ref[...], cs_ref[...])


def _fn_fold(w_fn, b_fn):
    d = w_fn.shape[0]
    c, s = _chan_dft_mats()
    cs = jnp.asarray(np.concatenate([c, s], axis=1), dtype=F32)
    rows = d + 8
    w_aug = jnp.concatenate([w_fn, b_fn[None, :], jnp.zeros((7, D_BR), F32)], axis=0)
    out = pl.pallas_call(
        _fn_fold_kernel,
        grid=(FN_GROUPS,),
        in_specs=[pl.BlockSpec((rows, LANES), lambda g: (0, g)),
                  pl.BlockSpec((LANES, 2 * LANES), lambda g: (0, 0))],
        out_specs=pl.BlockSpec((rows, 2 * LANES), lambda g: (0, g)),
        out_shape=jax.ShapeDtypeStruct((rows, 2 * D_BR), F32),
        compiler_params=_params(("arbitrary",)),
        name="fnet_fold_channel_dft",
    )(w_aug, cs)
    return out[:d], out[d]


def _fn_seq_kernel(p_ref, q_ref, fa_ref, tr_ref, ti_ref, g_ref, o_ref, a_buf, *, a_n, n_batch, scale):
    fa, tr, ti = fa_ref[...].astype(BF16), tr_ref[...], ti_ref[...]
    side = lambda m: jnp.concatenate([m[b * a_n:(b + 1) * a_n] for b in range(n_batch)], axis=1).astype(BF16)
    for c in range(CB):
        r1 = jnp.dot(fa, side(_chan_load(p_ref, c)), preferred_element_type=F32)
        r2 = jnp.dot(fa, side(_chan_load(q_ref, c)), preferred_element_type=F32)
        ar_all = r1[:a_n] - r2[a_n:]
        ai_all = -(r2[:a_n] + r1[a_n:])
        for b in range(n_batch):
            ar, ai = ar_all[:, b * LANES:(b + 1) * LANES], ai_all[:, b * LANES:(b + 1) * LANES]
            a_buf[b, c] = jnp.concatenate([ar * tr - ai * ti, ar * ti + ai * tr], axis=1)
    y = _bdot(a_buf[...].reshape(n_batch * CB * a_n, 2 * LANES), g_ref[...]) * scale
    o_ref[...] = y.reshape(n_batch, CB, a_n, LANES)


def _fn_seq(pq, *, a_n, n_batch, chan_lo):
    seq = a_n * LANES
    k = np.arange(a_n)
    ang = 2 * np.pi * (k[:, None] * k[None, :] % a_n) / a_n
    fa = np.concatenate([np.cos(ang), np.sin(ang)], axis=0)
    r = np.arange(LANES)
    ang_t = 2 * np.pi * (k[:, None] * r[None, :] % seq) / seq
    c2, s2 = _chan_dft_mats()
    g = np.concatenate([c2, s2], axis=0)
    nblk = LANES // CB
    const = lambda shp: pl.BlockSpec(shp, lambda j: (0,) * len(shp))

    def chan_blk(j, off):
        return chan_lo // CB + (j // nblk) * (2 * nblk) + off * nblk + j % nblk

    return pl.pallas_call(
        functools.partial(_fn_seq_kernel, a_n=a_n, n_batch=n_batch, scale=1.0 / math.sqrt(seq * LANES)),
        grid=(D_BR // CB,),
        in_specs=[pl.BlockSpec((n_batch * a_n, CB, LANES), lambda j: (0, chan_blk(j, 0), 0)),
                  pl.BlockSpec((n_batch * a_n, CB, LANES), lambda j: (0, chan_blk(j, 1), 0)),
                  const((2 * a_n, a_n)), const((a_n, LANES)), const((a_n, LANES)), const((2 * LANES, LANES))],
        out_specs=pl.BlockSpec((n_batch, CB, a_n, LANES), lambda j: (0, j, 0, 0)),
        out_shape=jax.ShapeDtypeStruct((n_batch, D_BR, a_n, LANES), F32),
        scratch_shapes=[pltpu.VMEM((n_batch, CB, a_n, 2 * LANES), F32)],
        compiler_params=_params(("arbitrary",)),
        name="fnet_sequence_dft",
    )(pq, pq, jnp.asarray(fa, F32), jnp.asarray(np.cos(ang_t), F32), jnp.asarray(-np.sin(ang_t), F32),
      jnp.asarray(g, F32))


def _fn_small_kernel(pq_ref, cl_ref, sl_ref, o_ref, *, a_n, scale):
    pq = jnp.concatenate([pq_ref[a].T for a in range(a_n)], axis=0)
    o_ref[...] = (_bdot(cl_ref[...], pq[:, :LANES]) - _bdot(sl_ref[...], pq[:, LANES:])) * scale


def _fn_small(pq, *, a_n, n_batch, chan_lo):
    seq = a_n * LANES
    n = np.arange(seq)
    ang = 2 * np.pi * (n[:, None] * n[None, :] % seq) / seq
    const = lambda shp: pl.BlockSpec(shp, lambda b, g: (0,) * len(shp))
    g0 = chan_lo // (2 * LANES)
    return pl.pallas_call(
        functools.partial(_fn_small_kernel, a_n=a_n, scale=1.0 / math.sqrt(seq * LANES)),
        grid=(n_batch, FN_GROUPS),
        in_specs=[pl.BlockSpec((a_n, 2 * LANES, LANES), lambda b, g: (b, g0 + g, 0)),
                  const((seq, seq)), const((seq, seq))],
        out_specs=pl.BlockSpec((None, seq, LANES), lambda b, g: (b, 0, g)),
        out_shape=jax.ShapeDtypeStruct((n_batch, seq, D_BR), F32),
        compiler_params=_params(("arbitrary", "arbitrary")),
        name="fnet_short",
    )(pq, jnp.asarray(np.cos(ang), F32), jnp.asarray(np.sin(ang), F32))


def _log_sigmoid(x):
    return jnp.minimum(x, 0.0) - jnp.log(1.0 + jnp.exp(-jnp.abs(x)))


def _exact_tri_dot(tri, x, tri_on_left):
    h, m, l = _split3(x)
    if tri_on_left:
        d = lambda p: jnp.dot(tri, p, preferred_element_type=F32)
    else:
        d = lambda p: jnp.dot(p, tri, preferred_element_type=F32)
    return d(h) + d(m) + d(l)


def _mlstm_step(inputs, c_st, n_st, m_st):
    t = hd = LANES
    n_dir, n_batch = len(inputs), len(inputs[0])
    n_grp = n_dir * n_batch * ML_HEADS
    row = lax.broadcasted_iota(jnp.int32, (t, t), 0)
    col = lax.broadcasted_iota(jnp.int32, (t, t), 1)
    tri = jnp.where(col <= row, 1.0, 0.0).astype(BF16)
    tri_t = jnp.where(col >= row, 1.0, 0.0).astype(BF16)
    qs, ks, vs, bcs, brs, ics, irs = [], [], [], [], [], [], []
    for d in range(n_dir):
        i_off = 2 * ML_HEADS * d
        f_off = i_off + ML_HEADS
        for b in range(n_batch):
            q_all, k_all, v_all, g = inputs[d][b]
            gt = g.T
            lf_c = _log_sigmoid(g)
            lf_r = lf_c.T
            if d == 1:
                b_c = _exact_tri_dot(tri_t, lf_c, True)
                b_r = _exact_tri_dot(tri, lf_r, False)
            else:
                b_c = _exact_tri_dot(tri, lf_c, True)
                b_r = _exact_tri_dot(tri_t, lf_r, False)
            for h in range(ML_HEADS):
                sl = slice(h * hd, (h + 1) * hd)
                qs.append(q_all[:, sl])
                ks.append(k_all[:, sl])
                vs.append(v_all[:, sl])
                bcs.append(b_c[:, f_off + h:f_off + h + 1])
                brs.append(b_r[f_off + h:f_off + h + 1, :])
                ics.append(g[:, i_off + h:i_off + h + 1])
                irs.append(gt[i_off + h:i_off + h + 1, :])
    q = jnp.stack(qs) * (hd ** -0.5)
    k, v = jnp.stack(ks), jnp.stack(vs)
    bc, br, ic, ir = jnp.stack(bcs), jnp.stack(brs), jnp.stack(ics), jnp.stack(irs)
    m_prev = m_st[...][:, :, :1]
    ct = c_st[...]
    n_prev = n_st[...]

    shp = (n_grp, t, t)
    grp = lax.broadcasted_iota(jnp.int32, shp, 0)
    r3, c3 = lax.broadcasted_iota(jnp.int32, shp, 1), lax.broadcasted_iota(jnp.int32, shp, 2)
    back = grp >= (n_grp // n_dir)
    mask = (back & (c3 >= r3)) | (jnp.logical_not(back) & (c3 <= r3))
    bdot = lambda a, b_, ca, cb: lax.dot_general(a.astype(BF16), b_.astype(BF16), (((ca,), (cb,)), ((0,), (0,))),
                                                 preferred_element_type=F32)
    e_ts = jnp.where(mask, ir - br, -jnp.inf)
    mm = jnp.maximum(m_prev, jnp.max(e_ts, axis=-1, keepdims=True))
    m_row = bc + mm
    w_intra = jnp.exp(e_ts - mm)
    w_inter = jnp.exp(m_prev - mm)
    s = bdot(q, k, 2, 2) * w_intra
    num = bdot(s, v, 2, 1) + w_inter * bdot(q, ct, 2, 1)
    den = jnp.sum(s, axis=-1, keepdims=True) + w_inter * jnp.sum(q * n_prev, axis=-1, keepdims=True)
    den = jnp.maximum(jnp.abs(den), jnp.exp(-m_row))
    h_all = num / den

    is_back = lax.broadcasted_iota(jnp.int32, (n_grp, 1, 1), 0) >= (n_grp // n_dir)
    b_tot_c = jnp.where(is_back, bc[:, :1, :], bc[:, t - 1:, :])
    b_tot_r = jnp.where(is_back, br[:, :, :1], br[:, :, t - 1:])
    a_c = b_tot_c - bc + ic
    a_r = b_tot_r - br + ir
    m_new = jnp.maximum(b_tot_c + m_prev, jnp.max(a_r, axis=-1, keepdims=True))
    sc = jnp.exp(a_c - m_new)
    decay = jnp.exp(b_tot_c + m_prev - m_new)
    k_sc = k * sc
    c_st[...] = decay * ct + bdot(k_sc, v, 1, 1)
    n_st[...] = decay * n_prev + jnp.sum(k_sc, axis=1, keepdims=True)
    m_st[...] = jnp.broadcast_to(m_new, (n_grp, 1, LANES))

    out = []
    for d in range(n_dir):
        out.append([jnp.concatenate([h_all[(d * n_batch + b) * ML_HEADS + h] for h in range(ML_HEADS)], axis=1)
                    for b in range(n_batch)])
    return out


def _mlstm_kernel(*refs, n_batch, ctx_chunks):
    lat_f, lat_b, ctx_f, ctx_b = refs[0:4], refs[4:8], refs[8:12], refs[12:16]
    hf_lat, hb_lat, hf_ctx, hb_ctx, c_st, n_st, m_st = refs[16:]
    j = pl.program_id(0)
    is_ctx = j < ctx_chunks

    @pl.when(j == 0)
    def _():
        c_st[...] = jnp.zeros(c_st.shape, F32)
        n_st[...] = jnp.zeros(n_st.shape, F32)
        m_st[...] = jnp.zeros(m_st.shape, F32)

    pick = lambda c_refs, l_refs, b: tuple(jnp.where(is_ctx, c[b], l[b]) for c, l in zip(c_refs, l_refs))
    inputs = [[pick(ctx_f, lat_f, b) for b in range(n_batch)], [pick(ctx_b, lat_b, b) for b in range(n_batch)]]
    hf, hb = _mlstm_step(inputs, c_st, n_st, m_st)

    @pl.when(is_ctx)
    def _():
        for b in range(n_batch):
            hf_ctx[b] = hf[b]
            hb_ctx[b] = hb[b]

    @pl.when(jnp.logical_not(is_ctx))
    def _():
        for b in range(n_batch):
            hf_lat[b] = hf[b]
            hb_lat[b] = hb[b]


def _mlstm(qk_lat, z_lat, qk_ctx, z_ctx, *, n_batch, v_col, g_col):
    lat_len, ctx_len = qk_lat.shape[0] // n_batch, qk_ctx.shape[0] // n_batch
    nlc, ncc = lat_len // LANES, ctx_len // LANES
    r3 = lambda a: a.reshape(n_batch, a.shape[0] // n_batch, a.shape[1])
    lf = lambda j: jnp.maximum(j - ncc, 0)
    lb = lambda j: jnp.where(j < ncc, nlc - 1, nlc - 1 - (j - ncc))
    cf = lambda j: jnp.minimum(j, ncc - 1)
    cb = lambda j: jnp.where(j < ncc, ncc - 1 - j, 0)

    def specs(ix):
        blk = lambda w, cidx: pl.BlockSpec((n_batch, LANES, w), lambda j: (0, ix(j), cidx))
        return [blk(D_BR, 0), blk(D_BR, 1), blk(D_BR, v_col), blk(LANES, g_col)]

    out = lambda ix: pl.BlockSpec((n_batch, LANES, D_BR), lambda j: (0, ix(j), 0))
    sd = lambda n: jax.ShapeDtypeStruct((n_batch, n, D_BR), F32)
    ql, zl, qc, zc = r3(qk_lat), r3(z_lat), r3(qk_ctx), r3(z_ctx)
    hf_lat, hb_lat, hf_ctx, hb_ctx = pl.pallas_call(
        functools.partial(_mlstm_kernel, n_batch=n_batch, ctx_chunks=ncc),
        grid=(ncc + nlc,),
        in_specs=specs(lf) + specs(lb) + specs(cf) + specs(cb),
        out_specs=[out(lf), out(lb), out(cf), out(cb)],
        out_shape=[sd(lat_len), sd(lat_len), sd(ctx_len), sd(ctx_len)],
        scratch_shapes=[pltpu.VMEM((2 * n_batch * ML_HEADS, LANES, LANES), F32),
                        pltpu.VMEM((2 * n_batch * ML_HEADS, 1, LANES), F32),
                        pltpu.VMEM((2 * n_batch * ML_HEADS, 1, LANES), F32)],
        compiler_params=_params(("arbitrary",)),
        name="mlstm_bidir",
    )(ql, ql, zl, zl, ql, ql, zl, zl, qc, qc, zc, zc, qc, qc, zc, zc)
    flat = lambda a: a.reshape(a.shape[0] * a.shape[1], D_BR)
    return (flat(hf_lat), flat(hb_lat)), (flat(hf_ctx), flat(hb_ctx))


def _mlstm_step_t(inputs, c_st, n_st, m_st):
    t = hd = LANES
    n_dir, n_batch = len(inputs), len(inputs[0])
    n_grp = n_dir * n_batch * ML_HEADS
    row = lax.broadcasted_iota(jnp.int32, (t, t), 0)
    col = lax.broadcasted_iota(jnp.int32, (t, t), 1)
    tri = jnp.where(col <= row, 1.0, 0.0).astype(BF16)
    tri_t = jnp.where(col >= row, 1.0, 0.0).astype(BF16)
    qts, kts, ks, vts, ecs, brs, irs = [], [], [], [], [], [], []
    for d in range(n_dir):
        i_off = 2 * ML_HEADS * d
        f_off = i_off + ML_HEADS
        for b in range(n_batch):
            qt_all, kt_all, k_all, vt_all, g = inputs[d][b]
            gt = g.T
            lf_c = _log_sigmoid(g)
            lf_r = lf_c.T
            if d == 1:
                b_c = _exact_tri_dot(tri_t, lf_c, True)
                b_r = _exact_tri_dot(tri, lf_r, False)
            else:
                b_c = _exact_tri_dot(tri, lf_c, True)
                b_r = _exact_tri_dot(tri_t, lf_r, False)
            for h in range(ML_HEADS):
                sl = slice(h * hd, (h + 1) * hd)
                qts.append(qt_all[sl])
                kts.append(kt_all[sl])
                vts.append(vt_all[sl])
                ks.append(k_all[:, sl])
                ecs.append(g[:, i_off + h:i_off + h + 1] - b_c[:, f_off + h:f_off + h + 1])
                brs.append(b_r[f_off + h:f_off + h + 1, :])
                irs.append(gt[i_off + h:i_off + h + 1, :])
    qt = jnp.stack(qts) * (hd ** -0.5)
    kt, vt, k = jnp.stack(kts), jnp.stack(vts), jnp.stack(ks)
    e_col, br, ir = jnp.stack(ecs), jnp.stack(brs), jnp.stack(irs)
    m_prev = m_st[...][:, :, :1]
    c_prev = c_st[...]
    n_prev = n_st[...]

    shp = (n_grp, t, t)
    grp = lax.broadcasted_iota(jnp.int32, shp, 0)
    s3, t3 = lax.broadcasted_iota(jnp.int32, shp, 1), lax.broadcasted_iota(jnp.int32, shp, 2)
    back = grp >= (n_grp // n_dir)
    mask = (back & (s3 >= t3)) | (jnp.logical_not(back) & (s3 <= t3))
    bdot = lambda a, b_, ca, cb: lax.dot_general(a.astype(BF16), b_.astype(BF16), (((ca,), (cb,)), ((0,), (0,))),
                                                 preferred_element_type=F32)
    e_st = jnp.where(mask, e_col, -jnp.inf)
    mm = jnp.maximum(m_prev, jnp.max(e_st, axis=1, keepdims=True))
    m_row = br + mm
    w_intra = jnp.exp(e_st - mm)
    w_inter = jnp.exp(m_prev - mm)
    s_t = bdot(k, qt, 2, 1) * w_intra
    num = bdot(vt, s_t, 2, 1) + w_inter * bdot(c_prev, qt, 2, 1)
    den = jnp.sum(s_t, axis=1, keepdims=True) + w_inter * bdot(n_prev, qt, 2, 1)
    den = jnp.maximum(jnp.abs(den), jnp.exp(-m_row))
    h_all = num / den

    is_back = lax.broadcasted_iota(jnp.int32, (n_grp, 1, 1), 0) >= (n_grp // n_dir)
    b_tot = jnp.where(is_back, br[:, :, :1], br[:, :, t - 1:])
    a_r = b_tot - br + ir
    m_new = jnp.maximum(b_tot + m_prev, jnp.max(a_r, axis=-1, keepdims=True))
    sc = jnp.exp(a_r - m_new)
    decay = jnp.exp(b_tot + m_prev - m_new)
    c_st[...] = decay * c_prev + bdot(vt * sc, kt, 2, 2)
    n_st[...] = decay * n_prev + bdot(sc, k, 2, 1)
    m_st[...] = jnp.broadcast_to(m_new, (n_grp, 1, LANES))

    out = []
    for d in range(n_dir):
        out.append([jnp.concatenate([h_all[(d * n_batch + b) * ML_HEADS + h] for h in range(ML_HEADS)], axis=0)
                    for b in range(n_batch)])
    return out


def _mlstm_kernel_t(*refs, n_batch, ctx_chunks):
    lat_f, lat_b, ctx_f, ctx_b = refs[0:5], refs[5:10], refs[10:15], refs[15:20]
    hf_lat, hb_lat, hf_ctx, hb_ctx, c_st, n_st, m_st = refs[20:]
    j = pl.program_id(0)
    is_ctx = j < ctx_chunks

    @pl.when(j == 0)
    def _():
        c_st[...] = jnp.zeros(c_st.shape, F32)
        n_st[...] = jnp.zeros(n_st.shape, F32)
        m_st[...] = jnp.zeros(m_st.shape, F32)

    def pick(c_refs, l_refs, b):
        ld = lambda r: r[b, 0] if len(r.shape) == 4 else r[b]
        return tuple(jnp.where(is_ctx, ld(c), ld(l)) for c, l in zip(c_refs, l_refs))

    inputs = [[pick(ctx_f, lat_f, b) for b in range(n_batch)], [pick(ctx_b, lat_b, b) for b in range(n_batch)]]
    hf, hb = _mlstm_step_t(inputs, c_st, n_st, m_st)

    @pl.when(is_ctx)
    def _():
        for b in range(n_batch):
            hf_ctx[b, 0] = hf[b]
            hb_ctx[b, 0] = hb[b]

    @pl.when(jnp.logical_not(is_ctx))
    def _():
        for b in range(n_batch):
            hf_lat[b, 0] = hf[b]
            hb_lat[b, 0] = hb[b]


def _mlstm_t(qk_s, k_tm, z_s, z_tm, *, n_batch, v_chan, g_col):
    nlc, ncc = qk_s["lat"].shape[0] // n_batch, qk_s["ctx"].shape[0] // n_batch
    r4 = lambda a: a.reshape(n_batch, a.shape[0] // n_batch, a.shape[1], a.shape[2])
    r3 = lambda a: a.reshape(n_batch, a.shape[0] // n_batch, a.shape[1])
    lf = lambda j: jnp.maximum(j - ncc, 0)
    lb = lambda j: jnp.where(j < ncc, nlc - 1, nlc - 1 - (j - ncc))
    cf = lambda j: jnp.minimum(j, ncc - 1)
    cb = lambda j: jnp.where(j < ncc, ncc - 1 - j, 0)
    slab = lambda ix, cidx: pl.BlockSpec((n_batch, 1, D_BR, LANES), lambda j: (0, ix(j), cidx, 0))
    rows = lambda w, ix, cidx: pl.BlockSpec((n_batch, LANES, w), lambda j: (0, ix(j), cidx))
    specs = lambda ix: [slab(ix, 0), slab(ix, 1), rows(D_BR, ix, 0), slab(ix, v_chan // D_BR), rows(LANES, ix, g_col)]
    args = lambda s: [r4(qk_s[s]), r4(qk_s[s]), r3(k_tm[s]), r4(z_s[s]), r3(z_tm[s])]
    n_grp = 2 * n_batch * ML_HEADS
    sd = lambda n: jax.ShapeDtypeStruct((n_batch, n, D_BR, LANES), F32)
    hf_lat, hb_lat, hf_ctx, hb_ctx = pl.pallas_call(
        functools.partial(_mlstm_kernel_t, n_batch=n_batch, ctx_chunks=ncc),
        grid=(ncc + nlc,),
        in_specs=specs(lf) + specs(lb) + specs(cf) + specs(cb),
        out_specs=[slab(lf, 0), slab(lb, 0), slab(cf, 0), slab(cb, 0)],
        out_shape=[sd(nlc), sd(nlc), sd(ncc), sd(ncc)],
        scratch_shapes=[pltpu.VMEM((n_grp, LANES, LANES), F32), pltpu.VMEM((n_grp, 1, LANES), F32),
                        pltpu.VMEM((n_grp, 1, LANES), F32)],
        compiler_params=_params(("arbitrary",)),
        name="mlstm_bidir",
    )(*(args("lat") + args("lat") + args("ctx") + args("ctx")))
    flat = lambda a: a.reshape(a.shape[0] * a.shape[1], D_BR, LANES)
    return {"lat": (flat(hf_lat), flat(hb_lat)), "ctx": (flat(hf_ctx), flat(hb_ctx))}


def _rms_mod(x, w, shift, scale):
    y = x * lax.rsqrt(jnp.mean(x * x, axis=-1, keepdims=True) + EPS) * w
    return y * (1.0 + scale) + shift


def _route(t, rw, rb):
    logits = _dot_f32ish3(t, rw) + rb
    col = lax.broadcasted_iota(jnp.int32, logits.shape, 1)
    big = jnp.int32(1 << 20)
    ninf = -jnp.inf
    is_g = col < MOE_GROUPS
    gl = jnp.where(is_g, logits, ninf)
    gmax = jnp.max(gl, axis=-1, keepdims=True)
    g_sel = jnp.min(jnp.where(is_g & (gl == gmax), col, big), axis=-1, keepdims=True)
    p_top = 1.0 / jnp.sum(jnp.where(is_g, jnp.exp(gl - gmax), 0.0), axis=-1, keepdims=True)
    lo = MOE_GROUPS + g_sel * MOE_PER_GROUP
    in_grp = (col >= lo) & (col < lo + MOE_PER_GROUP)
    e1v = jnp.where(in_grp, logits, ninf)
    top1 = jnp.max(e1v, axis=-1, keepdims=True)
    idx1 = jnp.min(jnp.where(in_grp & (e1v == top1), col, big), axis=-1, keepdims=True)
    e2v = jnp.where(col == idx1, ninf, e1v)
    top2 = jnp.max(e2v, axis=-1, keepdims=True)
    idx2 = jnp.min(jnp.where(in_grp & (col != idx1) & (e2v == top2), col, big), axis=-1, keepdims=True)
    ex = jnp.exp(top2 - top1)
    s1 = 1.0 / (1.0 + ex)
    return jnp.where(col == idx1, p_top * s1, 0.0) + jnp.where(col == idx2, p_top * (ex * s1), 0.0)


def _merge_kernel(yh_ref, yf_ref, h_ref, o_ref, g0_ref, g1_ref, g2_ref, x_ref, gate_ref,
                  wb_ref, wo_ref, nw_ref, n2_ref, sh_ref, sc_ref, rw_ref, rb_ref, out_ref, xn_ref, comb_ref):
    hd = LANES
    h = h_ref[...]
    parts = []
    for i in range(ML_HEADS):
        hh = h[:, i * hd:(i + 1) * hd]
        parts.append(hh * lax.rsqrt(jnp.mean(hh * hh, axis=-1, keepdims=True) + EPS))
    y_ml = jax.nn.sigmoid(o_ref[...]) * (jnp.concatenate(parts, axis=1) * nw_ref[...])
    acc = g0_ref[...].astype(F32) * _bdot(yh_ref[...], wb_ref[0])
    acc = acc + g1_ref[...].astype(F32) * _bdot(yf_ref[...], wb_ref[1])
    acc = acc + g2_ref[...].astype(F32) * _bdot(y_ml, wb_ref[2])
    x_new = x_ref[...] + gate_ref[...] * _bdot(acc, wo_ref[...])
    out_ref[...] = x_new
    t = _rms_mod(x_new, n2_ref[...], sh_ref[...], sc_ref[...])
    xn_ref[...] = t.astype(BF16)
    comb_ref[...] = _route(t, rw_ref[...], rb_ref[...])


def _merge(yh, yf, h, z_tm, gates, x, mods3, wb, wo, nw, n2w, rw, rb, *, seg, tm, o_col):
    nt, d = x.shape
    tok = lambda w, cidx: pl.BlockSpec((tm, w), lambda i: (i, cidx))
    mod = lambda k: pl.BlockSpec((None, 1, d), lambda i: (seg(i), 0, k))
    return pl.pallas_call(
        _merge_kernel,
        grid=(nt // tm,),
        in_specs=[tok(D_BR, 0), tok(D_BR, 0), tok(D_BR, 0),
                  tok(D_BR, o_col),
                  tok(d, 0), tok(d, 1), tok(d, 2),
                  tok(d, 0),
                  mod(2),
                  pl.BlockSpec((3, D_BR, d), lambda i: (0, 0, 0)),
                  pl.BlockSpec((d, d), lambda i: (0, 0)),
                  pl.BlockSpec((1, D_BR), lambda i: (0, 0)),
                  pl.BlockSpec((1, d), lambda i: (0, 0)),
                  mod(3), mod(4),
                  pl.BlockSpec((d, LANES), lambda i: (0, 0)),
                  pl.BlockSpec((1, LANES), lambda i: (0, 0))],
        out_specs=[tok(d, 0), tok(d, 0), tok(LANES, 0)],
        out_shape=[jax.ShapeDtypeStruct((nt, d), F32), jax.ShapeDtypeStruct((nt, d), BF16),
                   jax.ShapeDtypeStruct((nt, LANES), F32)],
        compiler_params=_params(("arbitrary",)),
        name="merge_branches_router",
    )(yh, yf, h, z_tm, gates, gates, gates, x, mods3, wb, wo, nw.reshape(1, D_BR), n2w.reshape(1, d),
      mods3, mods3, rw, rb)


def _moe_kernel(xn_ref, comb_ref, wg_ref, wu_ref, wd_ref, x_ref, gate_ref, nw_ref, sh_ref, sc_ref, *out_and_scratch,
                final, n_keep):
    acc_ref = out_and_scratch[-1]
    e = pl.program_id(1)

    @pl.when(e == 0)
    def _():
        acc_ref[...] = jnp.zeros(acc_ref.shape, F32)

    xn = xn_ref[...]
    comb = comb_ref[...]
    col = lax.broadcasted_iota(jnp.int32, comb.shape, 1)
    acts = []
    for i in range(MOE_PER_GROUP):
        cw = jnp.sum(jnp.where(col == e * MOE_PER_GROUP + i + MOE_GROUPS, comb, 0.0), axis=-1, keepdims=True)
        hg = jnp.dot(xn, wg_ref[i], preferred_element_type=F32)
        hu = jnp.dot(xn, wu_ref[i], preferred_element_type=F32)
        acts.append(((hg * jax.nn.sigmoid(hg)) * hu * cw).astype(BF16))
    wd = wd_ref[...].reshape(MOE_PER_GROUP * EXPERT_HID, wd_ref.shape[-1])
    acc_ref[...] += jnp.dot(jnp.concatenate(acts, axis=1), wd, preferred_element_type=F32)

    n_steps = MOE_EXPERTS // MOE_PER_GROUP
    if final:
        y_ref, = out_and_scratch[:-1]

        @pl.when((e == n_steps - 1) & (pl.program_id(0) < n_keep))
        def _():
            x_new = x_ref[...] + gate_ref[...] * acc_ref[...]
            y_ref[...] = x_new * lax.rsqrt(jnp.mean(x_new * x_new, axis=-1, keepdims=True) + EPS) * nw_ref[...]
    else:
        o_ref, xn_next_ref = out_and_scratch[:-1]

        @pl.when(e == n_steps - 1)
        def _():
            x_new = x_ref[...] + gate_ref[...] * acc_ref[...]
            o_ref[...] = x_new
            xn_next_ref[...] = _rms_mod(x_new, nw_ref[...], sh_ref[...], sc_ref[...]).astype(BF16)


def _moe(xn, comb, wg, wu, wd, x, mods3, post_w, post_mods3, *, seg, tm, final, n_keep_rows):
    nt, d = x.shape
    n_keep = n_keep_rows // tm
    tok = pl.BlockSpec((tm, d), lambda i, e: (i, 0))
    if final:
        out_specs = [pl.BlockSpec((tm, d), lambda i, e: (jnp.minimum(i, n_keep - 1), 0))]
        out_shape = [jax.ShapeDtypeStruct((n_keep_rows, d), F32)]
    else:
        out_specs = [tok, tok]
        out_shape = [jax.ShapeDtypeStruct((nt, d), F32), jax.ShapeDtypeStruct((nt, d), BF16)]
    mod = lambda k: pl.BlockSpec((None, 1, d), lambda i, e: (seg(i), 0, k))
    return pl.pallas_call(
        functools.partial(_moe_kernel, final=final, n_keep=n_keep),
        grid=(nt // tm, MOE_EXPERTS // MOE_PER_GROUP),
        in_specs=[tok,
                  pl.BlockSpec((tm, LANES), lambda i, e: (i, 0)),
                  pl.BlockSpec((MOE_PER_GROUP, d, EXPERT_HID), lambda i, e: (e, 0, 0)),
                  pl.BlockSpec((MOE_PER_GROUP, d, EXPERT_HID), lambda i, e: (e, 0, 0)),
                  pl.BlockSpec((MOE_PER_GROUP, EXPERT_HID, d), lambda i, e: (e, 0, 0)),
                  tok,
                  mod(5),
                  pl.BlockSpec((1, d), lambda i, e: (0, 0)),
                  mod(0), mod(1)],
        out_specs=out_specs,
        out_shape=out_shape,
        scratch_shapes=[pltpu.VMEM((tm, d), F32)],
        compiler_params=_params(("arbitrary", "arbitrary")),
        name="moe_experts_final" if final else "moe_experts",
    )(xn, comb, wg, wu, wd, x, mods3, post_w.reshape(1, d), post_mods3, post_mods3)


def _slab_to_tm(y_s):
    ns, c, _ = y_s.shape
    return jnp.transpose(y_s, (0, 2, 1)).reshape(ns * LANES, c)


def kernel(x, c, ctx, c_ctx, ada_w, ada_b, norm1_w, norm2_w, w_in, b_in, hy_conv_w, hy_conv_b, hy_f_w1, hy_f_b1, hy_f_w2, hy_f_b2, hy_f_w3, hy_f_freq, hy_decay, hy_skip, ml_conv_w, ml_conv_b, ml_norm_w, w_branch, w_out, moe_rg_w, moe_rg_b, moe_re_w, moe_re_b, moe_w_gate, moe_w_up, moe_w_down, norm_f_w):
    nb, seq, d = x.shape
    lc = ctx.shape[1]
    depth = ada_w.shape[0]
    assert d == D_MODEL and seq % (GRID_W * 2) == 0 and lc % LANES == 0 and nb + 1 <= 8
    rows = seq // GRID_W
    a_lat = seq // LANES
    a_ctx = lc // LANES
    n_lat, n_ctx = nb * seq, nb * lc
    tm = 256
    tm_moe = {"lat": _pick(seq, (1024, 512, 256)), "ctx": _pick(n_ctx, (512, 256))}
    tm_mrg = {"lat": _pick(seq, (512, 256)), "ctx": _pick(n_ctx, (512, 256))}
    assert seq % tm == 0 and n_ctx % tm == 0
    seg_of = lambda s, t: (lambda i: i // (seq // t)) if s == "lat" else (lambda i: nb)
    streams = ("lat", "ctx")
    xs = {"lat": x.reshape(n_lat, d), "ctx": ctx.reshape(n_ctx, d)}
    xn = {}
    cvec = jnp.zeros((8, d), F32).at[:nb].set(c).at[nb].set(c_ctx)
    mods = _mods(cvec, ada_w, ada_b)

    o_fn, o_ml, o_mlg, o_gate = 3 * D_BR, 4 * D_BR, 8 * D_BR, 8 * D_BR + 4 * ML_HEADS
    pad_g = LANES - 4 * ML_HEADS

    for l in range(depth):
        lp = {"hy_f_w1": hy_f_w1[l], "hy_f_b1": hy_f_b1[l], "hy_f_w2": hy_f_w2[l], "hy_f_b2": hy_f_b2[l],
              "hy_f_w3": hy_f_w3[l], "hy_f_freq": hy_f_freq[l], "hy_decay": hy_decay[l]}
        mods3 = mods[l].reshape(8, 1, 6 * d)
        wl, bl = w_in[l], b_in[l]
        w_pq, b_pq = _fn_fold(wl[:, o_fn:o_ml], bl[o_fn:o_ml])
        o_v, o_o = o_ml + 2 * D_BR, o_ml + 3 * D_BR
        w_cm = jnp.concatenate([wl[:, :o_fn], wl[:, o_ml:o_v], w_pq, wl[:, o_v:o_o]], axis=1)
        b_cm = jnp.concatenate([bl[:o_fn], bl[o_ml:o_v], b_pq, bl[o_v:o_o]])
        w_tm = jnp.concatenate([wl[:, o_o:o_mlg], wl[:, o_mlg:o_gate], jnp.zeros((d, pad_g), F32)], axis=1)
        b_tm = jnp.concatenate([bl[o_o:o_mlg], bl[o_mlg:o_gate], jnp.zeros((pad_g,), F32)])
        c_hy, c_qk, c_fn, c_v = 0, 3 * D_BR, 5 * D_BR, 7 * D_BR
        g_col = D_BR // LANES

        last = l + 1 == depth
        live = ("lat",) if last else streams
        if l == 0:
            xn = {s: _norm_mod(xs[s], norm1_w[l], mods3, 0, 1, seg_of(s, tm), tm) for s in streams}
        w_tm_b, w_cm_t = w_tm.astype(BF16), w_cm.T.astype(BF16)
        z_tm = {s: _mm_tm(xn[s], w_tm_b, b_tm, gate=False) for s in streams}
        w_gate_b = wl[:, o_gate:].astype(BF16)
        gates = {s: _mm_tm(xn[s], w_gate_b, bl[o_gate:], gate=True) for s in live}
        z_s = {s: _mm_slab(xn[s], w_cm_t, b_cm) for s in streams}

        grid_kw = {"lat": dict(rows=rows, width=GRID_W), "ctx": dict(rows=1, width=lc)}
        hy_w, hy_b = hy_conv_w[l].reshape(9, 3 * D_BR), hy_conv_b[l]
        ml_w, ml_b = ml_conv_w[l].reshape(9, 2 * D_BR), ml_conv_b[l]
        conv = lambda s, w, b, lo, n, act: _conv(z_s[s], w, b, chan_lo=lo, chan_n=n, silu=act, n_batch=nb,
                                                 slab0=0, **grid_kw[s])
        u = {s: conv(s, hy_w, hy_b, c_hy, 3 * D_BR, False) for s in live}
        qk = {s: conv(s, ml_w, ml_b, c_qk, 2 * D_BR, True) for s in streams}
        k_tm = {s: _slab_to_tm(qk[s][:, D_BR:]) for s in streams}
        h_s = _mlstm_t(qk, k_tm, z_s, z_tm, n_batch=nb, v_chan=c_v, g_col=g_col)
        h_sum = {s: _slab_to_tm(h_s[s][0] + h_s[s][1]) for s in live}

        yh, yf = {}, {}
        yh["lat"] = _slab_to_tm(_hyena(u["lat"], _hyena_taps(seq, 2 * a_lat, lp), hy_skip[l],
                                       a_in=a_lat, na=2 * a_lat, n_batch=nb))
        yk = _fn_seq(z_s["lat"], a_n=a_lat, n_batch=nb, chan_lo=c_fn)
        yf["lat"] = jnp.transpose(yk, (0, 3, 2, 1)).reshape(n_lat, D_BR)
        if not last:
            yh["ctx"] = _hyena_short(u["ctx"], _hyena_taps(lc, 2 * a_ctx, lp), hy_skip[l],
                                     a_n=a_ctx, n_batch=nb).reshape(n_ctx, D_BR)
            yf["ctx"] = _fn_small(z_s["ctx"], a_n=a_ctx, n_batch=nb, chan_lo=c_fn).reshape(n_ctx, D_BR)

        rw = jnp.concatenate([moe_rg_w[l], moe_re_w[l], jnp.zeros((d, LANES - MOE_GROUPS - MOE_EXPERTS), F32)], axis=1)
        rb = jnp.concatenate([moe_rg_b[l], moe_re_b[l], jnp.zeros((LANES - MOE_GROUPS - MOE_EXPERTS,), F32)]).reshape(1, LANES)
        wb, wo = w_branch[l].astype(BF16), w_out[l].astype(BF16)
        experts = (moe_w_gate[l].astype(BF16), moe_w_up[l].astype(BF16), moe_w_down[l].astype(BF16))
        for s in live:
            xs[s], xn2, comb = _merge(yh[s], yf[s], h_sum[s], z_tm[s], gates[s], xs[s], mods3, wb, wo,
                                      ml_norm_w[l], norm2_w[l], rw, rb, seg=seg_of(s, tm_mrg[s]), tm=tm_mrg[s],
                                      o_col=0)
            moe_kw = dict(seg=seg_of(s, tm_moe[s]), tm=tm_moe[s], n_keep_rows=xs[s].shape[0])
            if last:
                out, = _moe(xn2, comb, *experts, xs[s], mods3, norm_f_w, mods3, final=True, **moe_kw)
            else:
                xs[s], xn[s] = _moe(xn2, comb, *experts, xs[s], mods3, norm1_w[l + 1],
                                    mods[l + 1].reshape(8, 1, 6 * d), final=False, **moe_kw)

    return out.reshape(nb, seq, d)
```

```python
import functools
import math

import numpy as np
import jax
import jax.numpy as jnp
from jax import lax
from jax.experimental import pallas as pl
from jax.experimental.pallas import tpu as pltpu

F32 = jnp.float32
BF16 = jnp.bfloat16

D_MODEL = 1024
D_BR = 512
GRID_W = 64
LANES = 128
CB = 8
HY_ORDER = 2
HY_BANDS = 16
FN_GROUPS = 4
ML_HEADS = 4
MOE_GROUPS = 4
MOE_PER_GROUP = 4
MOE_EXPERTS = 16
EXPERT_HID = 256
EPS = 1e-6
VMEM_LIMIT = 56 * 1024 * 1024


def _params(sem):
    return pltpu.CompilerParams(dimension_semantics=sem, vmem_limit_bytes=VMEM_LIMIT)


def _bdot(a, b):
    return jnp.dot(a.astype(BF16), b.astype(BF16), preferred_element_type=F32)


def _split3(x):
    hi = x.astype(BF16)
    r1 = x - hi.astype(F32)
    mid = r1.astype(BF16)
    lo = (r1 - mid.astype(F32)).astype(BF16)
    return hi, mid, lo


def _dot_f32ish(x, w):
    xh, xm, xl = _split3(x)
    wh, wm, wl = _split3(w)
    d = lambda a, b: jnp.dot(a, b, preferred_element_type=F32)
    return (d(xh, wh) + (d(xh, wm) + d(xm, wh))) + (d(xm, wm) + d(xh, wl) + d(xl, wh))


def _dot_f32ish3(x, w):
    xh, xm, _ = _split3(x)
    wh, wm, _ = _split3(w)
    d = lambda a, b: jnp.dot(a, b, preferred_element_type=F32)
    return d(xh, wh) + (d(xh, wm) + d(xm, wh))


def _swap_halves(x):
    return jnp.concatenate([x[..., LANES:], x[..., :LANES]], axis=-1)


def _mods_kernel(c_ref, w_ref, b_ref, o_ref):
    c = c_ref[...]
    s = c * jax.nn.sigmoid(c)
    o_ref[...] = _dot_f32ish(s, w_ref[...]) + b_ref[...]


def _mods(cvec, ada_w, ada_b):
    depth, d, n6 = ada_w.shape
    tn = 1536
    return pl.pallas_call(
        _mods_kernel,
        grid=(depth, n6 // tn),
        in_specs=[pl.BlockSpec((8, d), lambda l, j: (0, 0)),
                  pl.BlockSpec((None, d, tn), lambda l, j: (l, 0, j)),
                  pl.BlockSpec((None, 1, tn), lambda l, j: (l, 0, j))],
        out_specs=pl.BlockSpec((None, 8, tn), lambda l, j: (l, 0, j)),
        out_shape=jax.ShapeDtypeStruct((depth, 8, n6), F32),
        compiler_params=_params(("arbitrary", "arbitrary")),
        name="adaln_mods",
    )(cvec, ada_w, ada_b.reshape(depth, 1, n6))


def _norm_mod_kernel(x_ref, w_ref, sh_ref, sc_ref, o_ref):
    x = x_ref[...]
    y = x * lax.rsqrt(jnp.mean(x * x, axis=-1, keepdims=True) + EPS) * w_ref[...]
    o_ref[...] = (y * (1.0 + sc_ref[...]) + sh_ref[...]).astype(o_ref.dtype)


def _norm_mod(x, w, mods3, col_shift, col_scale, seg, tm):
    nt, d = x.shape
    return pl.pallas_call(
        _norm_mod_kernel,
        grid=(nt // tm,),
        in_specs=[pl.BlockSpec((tm, d), lambda i: (i, 0)),
                  pl.BlockSpec((1, d), lambda i: (0, 0)),
                  pl.BlockSpec((None, 1, d), lambda i: (seg(i), 0, col_shift)),
                  pl.BlockSpec((None, 1, d), lambda i: (seg(i), 0, col_scale))],
        out_specs=pl.BlockSpec((tm, d), lambda i: (i, 0)),
        out_shape=jax.ShapeDtypeStruct((nt, d), BF16),
        compiler_params=_params(("arbitrary",)),
        name="norm_mod",
    )(x, w.reshape(1, d), mods3, mods3)


def _mm_tm_kernel(x_ref, w_ref, b_ref, o_ref, *, gate):
    y = jnp.dot(x_ref[...], w_ref[...], preferred_element_type=F32) + b_ref[...]
    o_ref[...] = (jax.nn.sigmoid(y) if gate else y).astype(o_ref.dtype)


def _pick(n, cands):
    for c in cands:
        if n % c == 0:
            return c
    raise ValueError(f"no tile for {n} in {cands}")


def _mm_tm(xn, w, b, *, gate):
    nt, k = xn.shape
    n = w.shape[1]
    tm = _pick(nt, (1056, 1024, 768, 512, 256))
    tn = _pick(n, (1536, 1408, 1152, 1024, 640, 512, 384, 256, 128))
    return pl.pallas_call(
        functools.partial(_mm_tm_kernel, gate=gate),
        grid=(n // tn, nt // tm),
        in_specs=[pl.BlockSpec((tm, k), lambda j, i: (i, 0)),
                  pl.BlockSpec((k, tn), lambda j, i: (0, j)),
                  pl.BlockSpec((1, tn), lambda j, i: (0, j))],
        out_specs=pl.BlockSpec((tm, tn), lambda j, i: (i, j)),
        out_shape=jax.ShapeDtypeStruct((nt, n), BF16 if gate else F32),
        compiler_params=_params(("arbitrary", "arbitrary")),
        name="inproj_gates" if gate else "inproj_token_major",
    )(xn, w, b.reshape(1, n))


def _mm_slab_kernel(w_ref, x_ref, b_ref, o_ref, *, slabs):
    w = w_ref[...]
    b = b_ref[...]
    step = 2 if slabs % 2 == 0 else 1
    for s in range(0, slabs, step):
        xs = x_ref[s * LANES:(s + step) * LANES, :]
        y = lax.dot_general(w, xs, (((1,), (1,)), ((), ())), preferred_element_type=F32) + b
        for i in range(step):
            o_ref[s + i] = y[:, i * LANES:(i + 1) * LANES]


def _mm_slab(xn, wt, b):
    nt, k = xn.shape
    c = wt.shape[0]
    ns = nt // LANES
    ts = _pick(ns, (12, 11, 8, 6, 4, 3, 2, 1))
    tc = _pick(c, (1024, 896, 512, 256, 128))
    return pl.pallas_call(
        functools.partial(_mm_slab_kernel, slabs=ts),
        grid=(c // tc, ns // ts),
        in_specs=[pl.BlockSpec((tc, k), lambda j, i: (j, 0)),
                  pl.BlockSpec((ts * LANES, k), lambda j, i: (i, 0)),
                  pl.BlockSpec((tc, 1), lambda j, i: (j, 0))],
        out_specs=pl.BlockSpec((ts, tc, LANES), lambda j, i: (i, j, 0)),
        out_shape=jax.ShapeDtypeStruct((ns, c, LANES), F32),
        compiler_params=_params(("arbitrary", "arbitrary")),
        name="inproj_slab",
    )(wt, xn, b.reshape(c, 1))


def _conv_taps(rows, width):
    taps = []
    for dr in (-1, 0, 1):
        if rows == 1 and dr != 0:
            continue
        for dw in (-1, 0, 1):
            taps.append((dr, dw))
    return taps


def _silu(x):
    return x * jax.nn.sigmoid(x)


def _conv_kernel(x_ref, w_ref, b_ref, o_ref, *, taps, width, n_slabs, silu):
    ct = x_ref.shape[1]
    lane = lax.broadcasted_iota(jnp.int32, (ct, LANES), 1)
    bias = jnp.zeros((ct, LANES), F32) + b_ref[...]
    planes = []
    for t, (dr, dw) in enumerate(taps):
        w = w_ref[t]
        if width < LANES and dw != 0:
            col = lane % width + dw
            w = jnp.where((col >= 0) & (col < width), w, 0.0)
        planes.append((dr * width + dw, w))

    def body(a, carry):
        x0 = x_ref[a]
        xm = jnp.where(a > 0, x_ref[jnp.maximum(a - 1, 0)], 0.0)
        xp = jnp.where(a < n_slabs - 1, x_ref[jnp.minimum(a + 1, n_slabs - 1)], 0.0)
        acc = bias
        for delta, w in planes:
            if delta == 0:
                src = x0
            elif delta > 0:
                src = pltpu.roll(jnp.where(lane >= delta, x0, xp), LANES - delta, 1)
            else:
                src = pltpu.roll(jnp.where(lane < LANES + delta, x0, xm), -delta, 1)
            acc = acc + src * w
        o_ref[a] = _silu(acc) if silu else acc
        return carry

    lax.fori_loop(0, n_slabs, body, 0, unroll=2 if n_slabs % 2 == 0 else 1)


def _conv_grid_kernel(x_ref, w_ref, b_ref, o_ref, ym_ref, yp_ref, *, width, n_slabs, silu):
    ct = x_ref.shape[1]
    lane = lax.broadcasted_iota(jnp.int32, (ct, LANES), 1)
    col = lane % width
    bias = jnp.zeros((ct, LANES), F32) + b_ref[...]

    def row_sums(a, carry):
        x0 = x_ref[a]
        xl = jnp.where(col >= 1, pltpu.roll(x0, 1, 1), 0.0)
        xr = jnp.where(col < width - 1, pltpu.roll(x0, LANES - 1, 1), 0.0)
        ym_ref[a] = w_ref[0] * xl + w_ref[1] * x0 + w_ref[2] * xr
        o_ref[a] = bias + w_ref[3] * xl + w_ref[4] * x0 + w_ref[5] * xr
        yp_ref[a] = w_ref[6] * xl + w_ref[7] * x0 + w_ref[8] * xr
        return carry

    def combine(a, carry):
        up = jnp.where(a > 0, ym_ref[jnp.maximum(a - 1, 0)], 0.0)
        dn = jnp.where(a < n_slabs - 1, yp_ref[jnp.minimum(a + 1, n_slabs - 1)], 0.0)
        from_up = jnp.where(lane < LANES - width, ym_ref[a], up)
        from_dn = jnp.where(lane >= width, yp_ref[a], dn)
        if 2 * width == LANES:
            y = o_ref[a] + pltpu.roll(from_up + from_dn, width, 1)
        else:
            y = o_ref[a] + pltpu.roll(from_up, width, 1) + pltpu.roll(from_dn, LANES - width, 1)
        o_ref[a] = _silu(y) if silu else y
        return carry

    unroll = 8 if n_slabs % 8 == 0 else 1
    lax.fori_loop(0, n_slabs, row_sums, 0, unroll=2 * unroll if n_slabs % (2 * unroll) == 0 else unroll)
    lax.fori_loop(0, n_slabs, combine, 0, unroll=unroll)


def _conv(z_s, w9, bias, *, rows, width, n_batch, slab0, chan_lo, chan_n, silu):
    seq = rows * width
    a_n = seq // LANES
    taps = tuple(_conv_taps(rows, width))
    assert all(abs(dr * width + dw) < LANES for dr, dw in taps)
    assert LANES % width == 0 or (rows == 1 and width % LANES == 0)
    tap_ids = [(dr + 1) * 3 + (dw + 1) for dr, dw in taps]
    w_t = jnp.broadcast_to(w9[jnp.array(tap_ids)][:, :, None], (len(taps), chan_n, LANES))
    ct = 64 if rows > 1 else 256
    assert chan_lo % ct == 0 and chan_n % ct == 0 and slab0 % a_n == 0
    nt_ = len(taps)
    if rows > 1:
        assert LANES % width == 0 and nt_ == 9
        body = functools.partial(_conv_grid_kernel, width=width, n_slabs=a_n, silu=silu)
        scratch = [pltpu.VMEM((a_n, ct, LANES), F32), pltpu.VMEM((a_n, ct, LANES), F32)]
    else:
        body = functools.partial(_conv_kernel, taps=taps, width=width, n_slabs=a_n, silu=silu)
        scratch = []
    return pl.pallas_call(
        body,
        scratch_shapes=scratch,
        grid=(n_batch, chan_n // ct),
        in_specs=[pl.BlockSpec((a_n, ct, LANES), lambda b, j: (slab0 // a_n + b, chan_lo // ct + j, 0)),
                  pl.BlockSpec((nt_, ct, LANES), lambda b, j: (0, j, 0)),
                  pl.BlockSpec((ct, 1), lambda b, j: (j, 0))],
        out_specs=pl.BlockSpec((a_n, ct, LANES), lambda b, j: (b, j, 0)),
        out_shape=jax.ShapeDtypeStruct((n_batch * a_n, chan_n, LANES), F32),
        compiler_params=_params(("arbitrary", "arbitrary")),
        name=f"dwconv_{rows}x{width}",
    )(z_s, w_t, bias.reshape(chan_n, 1))


def _dft_consts(a_in, na):
    n = na * LANES
    k = np.arange(na)[:, None]
    a = np.arange(a_in)[None, :]
    ang = 2 * np.pi * (k * a % na) / na
    fa = np.concatenate([np.cos(ang), -np.sin(ang)], axis=0)
    r = np.arange(LANES)
    ang_t = 2 * np.pi * (np.arange(na)[:, None] * r[None, :] % n) / n
    tr, ti = np.cos(ang_t), -np.sin(ang_t)
    ta = np.concatenate([tr, tr], axis=1)
    tb = np.concatenate([-ti, ti], axis=1)
    ang2 = 2 * np.pi * (r[:, None] * r[None, :] % LANES) / LANES
    c2, s2 = np.cos(ang2), np.sin(ang2)
    g2 = np.block([[c2, -s2], [s2, c2]])
    g2i = np.block([[c2, s2], [-s2, c2]])
    ang_i = 2 * np.pi * (np.arange(a_in)[:, None] * np.arange(na)[None, :] % na) / na
    ci, si = np.cos(ang_i) / n, -np.sin(ang_i) / n
    f = lambda v, dt: jnp.asarray(v, dtype=dt)
    return dict(fa=f(fa, F32), ta=f(ta, F32), tb=f(tb, F32), g2=f(g2, F32), g2i=f(g2i, F32),
                ci=f(ci, F32), si=f(si, F32))


def _fwd_slab_stage(m, fa, ta, tb, na):
    pp = jnp.dot(fa, m.astype(BF16), preferred_element_type=F32)
    p = jnp.concatenate([pp[:na], pp[na:]], axis=1)
    return p * ta + _swap_halves(p) * tb


def _cmul(x, kf):
    kr, ki = kf[..., :LANES], kf[..., LANES:]
    ka = jnp.concatenate([kr, kr], axis=-1)
    kb = jnp.concatenate([-ki, ki], axis=-1)
    return x * ka + _swap_halves(x) * kb


def _chan_load(ref, c):
    n, cb, _ = ref.shape
    return ref.reshape(n * cb, LANES)[pl.ds(c, n, stride=cb), :]


def _chan_store(ref, c, val):
    ref[:, c, :] = val


def _dot_f32ish_k(w, h):
    wh, wm, _ = _split3(w)
    hh, hm, _ = _split3(h)
    lhs = jnp.concatenate([wh, wh, wm], axis=1)
    rhs = jnp.concatenate([hh, hm, hh], axis=0)
    return jnp.dot(lhs, rhs, preferred_element_type=F32)


def _taps_kernel(bands_ref, w1t_ref, w1c_ref, w1s_ref, b1_ref, w2_ref, b2_ref, fq_ref, w3_ref, dec_ref, o_ref,
                 *, seq, a_seq, na, spb):
    step = pl.program_id(0)
    width = spb * LANES
    n_total = na * LANES
    is_f = step * spb < a_seq
    n = step * width + lax.broadcasted_iota(jnp.int32, (1, width), 1)
    pos = jnp.where(is_f, n, n_total - n)
    lo, hi = jnp.where(is_f, -1, n_total - seq), jnp.where(is_f, seq, n_total)
    t = pos.astype(F32) / seq
    fq = fq_ref[...]
    ang = ((2 * math.pi) * t) * bands_ref[...]
    pre = (w1t_ref[...] * t + _dot_f32ish(w1c_ref[...], jnp.cos(ang))
           + _dot_f32ish(w1s_ref[...], jnp.sin(ang)))
    h = jnp.sin(fq * (pre + b1_ref[...]))
    h = jnp.sin(fq * (_dot_f32ish(w2_ref[...], h) + b2_ref[...]))
    d = jnp.where(is_f, 0, 1)
    dec = jnp.abs(dec_ref[d])
    k = _dot_f32ish_k(w3_ref[d], h)
    for i in range(spb):
        sl = slice(i * LANES, (i + 1) * LANES)
        live = (n[:, sl] > lo) & (n[:, sl] < hi)
        o_ref[i] = jnp.where(live, k[:, sl] * jnp.exp(-t[:, sl] * dec), 0.0)


def _hyena_taps(seq, na, lp):
    nc = HY_ORDER * D_BR
    hid = lp["hy_f_w2"].shape[0]
    w1 = lp["hy_f_w1"]
    col = lambda v: v.reshape(-1, 1)
    bands = col(jnp.linspace(1e-4, HY_BANDS - 1, HY_BANDS, dtype=F32))
    w3 = jnp.transpose(lp["hy_f_w3"].T.reshape(HY_ORDER, 2, D_BR, hid), (1, 0, 2, 3)).reshape(2, nc, hid)
    dec = jnp.broadcast_to(jnp.transpose(lp["hy_decay"], (1, 0, 2)).reshape(2, nc, 1), (2, nc, LANES))
    a_seq = seq // LANES
    spb = min(8, a_seq)
    assert na == 2 * a_seq and a_seq % spb == 0
    args = (bands, col(w1[0]), w1[1:1 + HY_BANDS].T, w1[1 + HY_BANDS:].T, col(lp["hy_f_b1"]), lp["hy_f_w2"].T,
            col(lp["hy_f_b2"]), col(lp["hy_f_freq"]), w3, dec)
    full = lambda v: pl.BlockSpec(v.shape, lambda s: (0,) * v.ndim)
    return pl.pallas_call(
        functools.partial(_taps_kernel, seq=seq, a_seq=a_seq, na=na, spb=spb),
        grid=(na // spb,),
        in_specs=[full(v) for v in args],
        out_specs=pl.BlockSpec((spb, nc, LANES), lambda s: (s, 0, 0)),
        out_shape=jax.ShapeDtypeStruct((na, nc, LANES), F32),
        compiler_params=_params(("arbitrary",)),
        name=f"hyena_filter_taps_{na}",
    )(*args)


def _hyena_kernel(v_ref, x1_ref, x2_ref, k0_ref, k1_ref, skip_ref, fa_ref, faf_ref, ta_ref, tb_ref, g2_ref, g2i_ref,
                  ci_ref, si_ref, o_ref, p_buf, z_buf, kf_buf, *, a_in, na):
    fa, ta, tb = fa_ref[...].astype(BF16), ta_ref[...], tb_ref[...]
    ci, si = ci_ref[...].astype(BF16), si_ref[...].astype(BF16)

    @pl.when(pl.program_id(1) == 0)
    def _():
        faf = faf_ref[...].astype(BF16)
        for order, k_ref in enumerate((k0_ref, k1_ref)):
            scales = []
            for c in range(CB):
                m = _chan_load(k_ref, c)
                ss = jnp.sum(jnp.sum(m * m, axis=1, keepdims=True), axis=0, keepdims=True)
                scales.append(lax.rsqrt(ss + EPS))
                p_buf[c] = _fwd_slab_stage(m, faf, ta, tb, na)
            x = _bdot(p_buf[...].reshape(CB * na, 2 * LANES), g2_ref[...]).reshape(CB, na, 2 * LANES)
            for c in range(CB):
                kf_buf[order, c] = x[c] * scales[c]

    def spectral(order):
        x = _bdot(p_buf[...].reshape(CB * na, 2 * LANES), g2_ref[...])
        y = _cmul(x, kf_buf[order].reshape(CB * na, 2 * LANES))
        bm = _bdot(y, g2i_ref[...]).reshape(CB, na, 2 * LANES)
        p_buf[...] = bm * ta - _swap_halves(bm) * tb

    def conv_out(c):
        bb = p_buf[c]
        return (jnp.dot(ci, bb[:, :LANES].astype(BF16), preferred_element_type=F32)
                + jnp.dot(si, bb[:, LANES:].astype(BF16), preferred_element_type=F32))

    for c in range(CB):
        p_buf[c] = _fwd_slab_stage(_chan_load(v_ref, c), fa, ta, tb, na)
    spectral(0)
    for c in range(CB):
        z_buf[c] = _chan_load(x1_ref, c) * (conv_out(c) + _chan_load(v_ref, c) * skip_ref[0, c])
    for c in range(CB):
        p_buf[c] = _fwd_slab_stage(z_buf[c], fa, ta, tb, na)
    spectral(1)
    for c in range(CB):
        _chan_store(o_ref, c, _chan_load(x2_ref, c) * (conv_out(c) + z_buf[c] * skip_ref[1, c]))


def _hyena(u_s, taps_s, skip, *, a_in, na, n_batch):
    cs = _dft_consts(a_in, na)
    faf = _dft_consts(na, na)["fa"]
    nblk = D_BR // CB
    const = lambda shp: pl.BlockSpec(shp, lambda j, b: (0,) * len(shp))
    skip_b = jnp.broadcast_to(skip[:, :, None, None], (HY_ORDER, D_BR, 1, LANES))
    return pl.pallas_call(
        functools.partial(_hyena_kernel, a_in=a_in, na=na),
        grid=(nblk, n_batch),
        in_specs=[pl.BlockSpec((a_in, CB, LANES), lambda j, b: (b, j, 0)),
                  pl.BlockSpec((a_in, CB, LANES), lambda j, b: (b, nblk + j, 0)),
                  pl.BlockSpec((a_in, CB, LANES), lambda j, b: (b, 2 * nblk + j, 0)),
                  pl.BlockSpec((na, CB, LANES), lambda j, b: (0, j, 0)),
                  pl.BlockSpec((na, CB, LANES), lambda j, b: (0, nblk + j, 0)),
                  pl.BlockSpec((HY_ORDER, CB, 1, LANES), lambda j, b: (0, j, 0, 0)),
                  const((2 * na, a_in)), const((2 * na, na)), const((na, 2 * LANES)), const((na, 2 * LANES)),
                  const((2 * LANES, 2 * LANES)), const((2 * LANES, 2 * LANES)),
                  const((a_in, na)), const((a_in, na))],
        out_specs=pl.BlockSpec((a_in, CB, LANES), lambda j, b: (b, j, 0)),
        out_shape=jax.ShapeDtypeStruct((n_batch * a_in, D_BR, LANES), F32),
        scratch_shapes=[pltpu.VMEM((CB, na, 2 * LANES), F32), pltpu.VMEM((CB, a_in, LANES), F32),
                        pltpu.VMEM((HY_ORDER, CB, na, 2 * LANES), F32)],
        compiler_params=_params(("arbitrary", "arbitrary")),
        name=f"hyena_longconv_{a_in}",
    )(u_s, u_s, u_s, taps_s, taps_s, skip_b, cs["fa"], faf, cs["ta"], cs["tb"], cs["g2"], cs["g2i"],
      cs["ci"], cs["si"])


def _slabs_to_rows(ref, n):
    return jnp.concatenate([ref[a].T for a in range(n)], axis=0)


def _hyena_short_kernel(v_ref, x1_ref, x2_ref, k0_ref, k1_ref, skip_ref, f_ref, g_ref, o_ref, *, a_n, a_k):
    seq, nf = a_n * LANES, a_k * LANES
    ff = f_ref[...].astype(BF16)
    gi = g_ref[...].astype(BF16)
    v, x1, x2 = _slabs_to_rows(v_ref, a_n), _slabs_to_rows(x1_ref, a_n), _slabs_to_rows(x2_ref, a_n)

    def longconv(u, k_ref):
        k = _slabs_to_rows(k_ref, a_k)
        s = lax.rsqrt(jnp.sum(k * k, axis=0, keepdims=True) + EPS)
        kf = jnp.dot(ff, k.astype(BF16), preferred_element_type=F32) * s
        x = jnp.dot(ff[:, :seq], u.astype(BF16), preferred_element_type=F32)
        xr, xi, kr, ki = x[:nf], x[nf:], kf[:nf], kf[nf:]
        y = jnp.concatenate([xr * kr - xi * ki, xr * ki + xi * kr], axis=0)
        return jnp.dot(gi, y.astype(BF16), preferred_element_type=F32)

    z = x1 * (longconv(v, k0_ref) + v * skip_ref[0])
    o_ref[...] = x2 * (longconv(z, k1_ref) + z * skip_ref[1])


def _hyena_short(u_s, taps_s, skip, *, a_n, n_batch):
    a_k = 2 * a_n
    seq, nf = a_n * LANES, a_k * LANES
    k = np.arange(nf)
    ang = 2 * np.pi * (k[:, None] * k[None, :] % nf) / nf
    f = np.concatenate([np.cos(ang), -np.sin(ang)], axis=0)
    g = np.concatenate([np.cos(ang[:seq]), -np.sin(ang[:seq])], axis=1) / nf
    nblk = D_BR // LANES
    const = lambda shp: pl.BlockSpec(shp, lambda b, j: (0,) * len(shp))
    return pl.pallas_call(
        functools.partial(_hyena_short_kernel, a_n=a_n, a_k=a_k),
        grid=(n_batch, nblk),
        in_specs=[pl.BlockSpec((a_n, LANES, LANES), lambda b, j: (b, j, 0)),
                  pl.BlockSpec((a_n, LANES, LANES), lambda b, j: (b, nblk + j, 0)),
                  pl.BlockSpec((a_n, LANES, LANES), lambda b, j: (b, 2 * nblk + j, 0)),
                  pl.BlockSpec((a_k, LANES, LANES), lambda b, j: (0, j, 0)),
                  pl.BlockSpec((a_k, LANES, LANES), lambda b, j: (0, nblk + j, 0)),
                  pl.BlockSpec((HY_ORDER, 1, LANES), lambda b, j: (0, 0, j)),
                  const((2 * nf, nf)), const((seq, 2 * nf))],
        out_specs=pl.BlockSpec((None, seq, LANES), lambda b, j: (b, 0, j)),
        out_shape=jax.ShapeDtypeStruct((n_batch, seq, D_BR), F32),
        compiler_params=_params(("arbitrary", "arbitrary")),
        name="hyena_short",
    )(u_s, u_s, u_s, taps_s, taps_s, skip.reshape(HY_ORDER, 1, D_BR), jnp.asarray(f, F32), jnp.asarray(g, F32))


def _chan_dft_mats():
    r = np.arange(LANES)
    ang = 2 * np.pi * (r[:, None] * r[None, :] % LANES) / LANES
    return np.cos(ang), np.sin(ang)


def _fn_fold_kernel(w_ref, cs_ref, o_ref):
    o_ref[...] = _dot_f32ish(w_ref[...], cs_ref[...])


def _fn_fold(w_fn, b_fn):
    d = w_fn.shape[0]
    c, s = _chan_dft_mats()
    cs = jnp.asarray(np.concatenate([c, s], axis=1), dtype=F32)
    rows = d + 8
    w_aug = jnp.concatenate([w_fn, b_fn[None, :], jnp.zeros((7, D_BR), F32)], axis=0)
    out = pl.pallas_call(
        _fn_fold_kernel,
        grid=(FN_GROUPS,),
        in_specs=[pl.BlockSpec((rows, LANES), lambda g: (0, g)),
                  pl.BlockSpec((LANES, 2 * LANES), lambda g: (0, 0))],
        out_specs=pl.BlockSpec((rows, 2 * LANES), lambda g: (0, g)),
        out_shape=jax.ShapeDtypeStruct((rows, 2 * D_BR), F32),
        compiler_params=_params(("arbitrary",)),
        name="fnet_fold_channel_dft",
    )(w_aug, cs)
    return out[:d], out[d]


def _fn_seq_kernel(p_ref, q_ref, fa_ref, tr_ref, ti_ref, g_ref, o_ref, a_buf, *, a_n, n_batch, scale):
    fa, tr, ti = fa_ref[...].astype(BF16), tr_ref[...], ti_ref[...]
    side = lambda m: jnp.concatenate([m[b * a_n:(b + 1) * a_n] for b in range(n_batch)], axis=1).astype(BF16)
    for c in range(CB):
        r1 = jnp.dot(fa, side(_chan_load(p_ref, c)), preferred_element_type=F32)
        r2 = jnp.dot(fa, side(_chan_load(q_ref, c)), preferred_element_type=F32)
        ar_all = r1[:a_n] - r2[a_n:]
        ai_all = -(r2[:a_n] + r1[a_n:])
        for b in range(n_batch):
            ar, ai = ar_all[:, b * LANES:(b + 1) * LANES], ai_all[:, b * LANES:(b + 1) * LANES]
            a_buf[b, c] = jnp.concatenate([ar * tr - ai * ti, ar * ti + ai * tr], axis=1)
    y = _bdot(a_buf[...].reshape(n_batch * CB * a_n, 2 * LANES), g_ref[...]) * scale
    o_ref[...] = y.reshape(n_batch, CB, a_n, LANES)


def _fn_seq(pq, *, a_n, n_batch, chan_lo):
    seq = a_n * LANES
    k = np.arange(a_n)
    ang = 2 * np.pi * (k[:, None] * k[None, :] % a_n) / a_n
    fa = np.concatenate([np.cos(ang), np.sin(ang)], axis=0)
    r = np.arange(LANES)
    ang_t = 2 * np.pi * (k[:, None] * r[None, :] % seq) / seq
    c2, s2 = _chan_dft_mats()
    g = np.concatenate([c2, s2], axis=0)
    nblk = LANES // CB
    const = lambda shp: pl.BlockSpec(shp, lambda j: (0,) * len(shp))

    def chan_blk(j, off):
        return chan_lo // CB + (j // nblk) * (2 * nblk) + off * nblk + j % nblk

    return pl.pallas_call(
        functools.partial(_fn_seq_kernel, a_n=a_n, n_batch=n_batch, scale=1.0 / math.sqrt(seq * LANES)),
        grid=(D_BR // CB,),
        in_specs=[pl.BlockSpec((n_batch * a_n, CB, LANES), lambda j: (0, chan_blk(j, 0), 0)),
                  pl.BlockSpec((n_batch * a_n, CB, LANES), lambda j: (0, chan_blk(j, 1), 0)),
                  const((2 * a_n, a_n)), const((a_n, LANES)), const((a_n, LANES)), const((2 * LANES, LANES))],
        out_specs=pl.BlockSpec((n_batch, CB, a_n, LANES), lambda j: (0, j, 0, 0)),
        out_shape=jax.ShapeDtypeStruct((n_batch, D_BR, a_n, LANES), F32),
        scratch_shapes=[pltpu.VMEM((n_batch, CB, a_n, 2 * LANES), F32)],
        compiler_params=_params(("arbitrary",)),
        name="fnet_sequence_dft",
    )(pq, pq, jnp.asarray(fa, F32), jnp.asarray(np.cos(ang_t), F32), jnp.asarray(-np.sin(ang_t), F32),
      jnp.asarray(g, F32))


def _fn_small_kernel(pq_ref, cl_ref, sl_ref, o_ref, *, a_n, scale):
    pq = jnp.concatenate([pq_ref[a].T for a in range(a_n)], axis=0)
    o_ref[...] = (_bdot(cl_ref[...], pq[:, :LANES]) - _bdot(sl_ref[...], pq[:, LANES:])) * scale


def _fn_small(pq, *, a_n, n_batch, chan_lo):
    seq = a_n * LANES
    n = np.arange(seq)
    ang = 2 * np.pi * (n[:, None] * n[None, :] % seq) / seq
    const = lambda shp: pl.BlockSpec(shp, lambda b, g: (0,) * len(shp))
    g0 = chan_lo // (2 * LANES)
    return pl.pallas_call(
        functools.partial(_fn_small_kernel, a_n=a_n, scale=1.0 / math.sqrt(seq * LANES)),
        grid=(n_batch, FN_GROUPS),
        in_specs=[pl.BlockSpec((a_n, 2 * LANES, LANES), lambda b, g: (b, g0 + g, 0)),
                  const((seq, seq)), const((seq, seq))],
        out_specs=pl.BlockSpec((None, seq, LANES), lambda b, g: (b, 0, g)),
        out_shape=jax.ShapeDtypeStruct((n_batch, seq, D_BR), F32),
        compiler_params=_params(("arbitrary", "arbitrary")),
        name="fnet_short",
    )(pq, jnp.asarray(np.cos(ang), F32), jnp.asarray(np.sin(ang), F32))


def _log_sigmoid(x):
    return jnp.minimum(x, 0.0) - jnp.log(1.0 + jnp.exp(-jnp.abs(x)))


def _exact_tri_dot(tri, x, tri_on_left):
    h, m, l = _split3(x)
    if tri_on_left:
        d = lambda p: jnp.dot(tri, p, preferred_element_type=F32)
    else:
        d = lambda p: jnp.dot(p, tri, preferred_element_type=F32)
    return d(h) + d(m) + d(l)


def _mlstm_step_t(inputs, c_st, n_st, m_st):
    t = hd = LANES
    n_dir, n_batch = len(inputs), len(inputs[0])
    n_grp = n_dir * n_batch * ML_HEADS
    row = lax.broadcasted_iota(jnp.int32, (t, t), 0)
    col = lax.broadcasted_iota(jnp.int32, (t, t), 1)
    tri = jnp.where(col <= row, 1.0, 0.0).astype(BF16)
    tri_t = jnp.where(col >= row, 1.0, 0.0).astype(BF16)
    qts, kts, ks, vts, ecs, brs, irs = [], [], [], [], [], [], []
    for d in range(n_dir):
        i_off = 2 * ML_HEADS * d
        f_off = i_off + ML_HEADS
        for b in range(n_batch):
            qt_all, kt_all, k_all, vt_all, g = inputs[d][b]
            gt = g.T
            lf_c = _log_sigmoid(g)
            lf_r = lf_c.T
            if d == 1:
                b_c = _exact_tri_dot(tri_t, lf_c, True)
                b_r = _exact_tri_dot(tri, lf_r, False)
            else:
                b_c = _exact_tri_dot(tri, lf_c, True)
                b_r = _exact_tri_dot(tri_t, lf_r, False)
            for h in range(ML_HEADS):
                sl = slice(h * hd, (h + 1) * hd)
                qts.append(qt_all[sl])
                kts.append(kt_all[sl])
                vts.append(vt_all[sl])
                ks.append(k_all[:, sl])
                ecs.append(g[:, i_off + h:i_off + h + 1] - b_c[:, f_off + h:f_off + h + 1])
                brs.append(b_r[f_off + h:f_off + h + 1, :])
                irs.append(gt[i_off + h:i_off + h + 1, :])
    qt = jnp.stack(qts) * (hd ** -0.5)
    kt, vt, k = jnp.stack(kts), jnp.stack(vts), jnp.stack(ks)
    e_col, br, ir = jnp.stack(ecs), jnp.stack(brs), jnp.stack(irs)
    m_prev = m_st[...][:, :, :1]
    c_prev = c_st[...]
    n_prev = n_st[...]

    shp = (n_grp, t, t)
    grp = lax.broadcasted_iota(jnp.int32, shp, 0)
    s3, t3 = lax.broadcasted_iota(jnp.int32, shp, 1), lax.broadcasted_iota(jnp.int32, shp, 2)
    back = grp >= (n_grp // n_dir)
    mask = (back & (s3 >= t3)) | (jnp.logical_not(back) & (s3 <= t3))
    bdot = lambda a, b_, ca, cb: lax.dot_general(a.astype(BF16), b_.astype(BF16), (((ca,), (cb,)), ((0,), (0,))),
                                                 preferred_element_type=F32)
    e_st = jnp.where(mask, e_col, -jnp.inf)
    mm = jnp.maximum(m_prev, jnp.max(e_st, axis=1, keepdims=True))
    m_row = br + mm
    w_intra = jnp.exp(e_st - mm)
    w_inter = jnp.exp(m_prev - mm)
    s_t = bdot(k, qt, 2, 1) * w_intra
    num = bdot(vt, s_t, 2, 1) + w_inter * bdot(c_prev, qt, 2, 1)
    den = jnp.sum(s_t, axis=1, keepdims=True) + w_inter * bdot(n_prev, qt, 2, 1)
    den = jnp.maximum(jnp.abs(den), jnp.exp(-m_row))
    h_all = num / den

    is_back = lax.broadcasted_iota(jnp.int32, (n_grp, 1, 1), 0) >= (n_grp // n_dir)
    b_tot = jnp.where(is_back, br[:, :, :1], br[:, :, t - 1:])
    a_r = b_tot - br + ir
    m_new = jnp.maximum(b_tot + m_prev, jnp.max(a_r, axis=-1, keepdims=True))
    sc = jnp.exp(a_r - m_new)
    decay = jnp.exp(b_tot + m_prev - m_new)
    c_st[...] = decay * c_prev + bdot(vt * sc, kt, 2, 2)
    n_st[...] = decay * n_prev + bdot(sc, k, 2, 1)
    m_st[...] = jnp.broadcast_to(m_new, (n_grp, 1, LANES))

    out = []
    for d in range(n_dir):
        out.append([jnp.concatenate([h_all[(d * n_batch + b) * ML_HEADS + h] for h in range(ML_HEADS)], axis=0)
                    for b in range(n_batch)])
    return out


def _mlstm_kernel_t(*refs, n_batch, ctx_chunks):
    lat_f, lat_b, ctx_f, ctx_b = refs[0:5], refs[5:10], refs[10:15], refs[15:20]
    hf_lat, hb_lat, hf_ctx, hb_ctx, c_st, n_st, m_st = refs[20:]
    j = pl.program_id(0)
    is_ctx = j < ctx_chunks

    @pl.when(j == 0)
    def _():
        c_st[...] = jnp.zeros(c_st.shape, F32)
        n_st[...] = jnp.zeros(n_st.shape, F32)
        m_st[...] = jnp.zeros(m_st.shape, F32)

    def pick(c_refs, l_refs, b):
        ld = lambda r: r[b, 0] if len(r.shape) == 4 else r[b]
        return tuple(jnp.where(is_ctx, ld(c), ld(l)) for c, l in zip(c_refs, l_refs))

    inputs = [[pick(ctx_f, lat_f, b) for b in range(n_batch)], [pick(ctx_b, lat_b, b) for b in range(n_batch)]]
    hf, hb = _mlstm_step_t(inputs, c_st, n_st, m_st)

    @pl.when(is_ctx)
    def _():
        for b in range(n_batch):
            hf_ctx[b, 0] = hf[b]
            hb_ctx[b, 0] = hb[b]

    @pl.when(jnp.logical_not(is_ctx))
    def _():
        for b in range(n_batch):
            hf_lat[b, 0] = hf[b]
            hb_lat[b, 0] = hb[b]


def _mlstm_t(qk_s, k_tm, z_s, z_tm, *, n_batch, v_chan, g_col):
    nlc, ncc = qk_s["lat"].shape[0] // n_batch, qk_s["ctx"].shape[0] // n_batch
    r4 = lambda a: a.reshape(n_batch, a.shape[0] // n_batch, a.shape[1], a.shape[2])
    r3 = lambda a: a.reshape(n_batch, a.shape[0] // n_batch, a.shape[1])
    lf = lambda j: jnp.maximum(j - ncc, 0)
    lb = lambda j: jnp.where(j < ncc, nlc - 1, nlc - 1 - (j - ncc))
    cf = lambda j: jnp.minimum(j, ncc - 1)
    cb = lambda j: jnp.where(j < ncc, ncc - 1 - j, 0)
    slab = lambda ix, cidx: pl.BlockSpec((n_batch, 1, D_BR, LANES), lambda j: (0, ix(j), cidx, 0))
    rows = lambda w, ix, cidx: pl.BlockSpec((n_batch, LANES, w), lambda j: (0, ix(j), cidx))
    specs = lambda ix: [slab(ix, 0), slab(ix, 1), rows(D_BR, ix, 0), slab(ix, v_chan // D_BR), rows(LANES, ix, g_col)]
    args = lambda s: [r4(qk_s[s]), r4(qk_s[s]), r3(k_tm[s]), r4(z_s[s]), r3(z_tm[s])]
    n_grp = 2 * n_batch * ML_HEADS
    sd = lambda n: jax.ShapeDtypeStruct((n_batch, n, D_BR, LANES), F32)
    hf_lat, hb_lat, hf_ctx, hb_ctx = pl.pallas_call(
        functools.partial(_mlstm_kernel_t, n_batch=n_batch, ctx_chunks=ncc),
        grid=(ncc + nlc,),
        in_specs=specs(lf) + specs(lb) + specs(cf) + specs(cb),
        out_specs=[slab(lf, 0), slab(lb, 0), slab(cf, 0), slab(cb, 0)],
        out_shape=[sd(nlc), sd(nlc), sd(ncc), sd(ncc)],
        scratch_shapes=[pltpu.VMEM((n_grp, LANES, LANES), F32), pltpu.VMEM((n_grp, 1, LANES), F32),
                        pltpu.VMEM((n_grp, 1, LANES), F32)],
        compiler_params=_params(("arbitrary",)),
        name="mlstm_bidir",
    )(*(args("lat") + args("lat") + args("ctx") + args("ctx")))
    flat = lambda a: a.reshape(a.shape[0] * a.shape[1], D_BR, LANES)
    return {"lat": (flat(hf_lat), flat(hb_lat)), "ctx": (flat(hf_ctx), flat(hb_ctx))}


def _rms_mod(x, w, shift, scale):
    y = x * lax.rsqrt(jnp.mean(x * x, axis=-1, keepdims=True) + EPS) * w
    return y * (1.0 + scale) + shift


def _route(t, rw, rb):
    logits = _dot_f32ish3(t, rw) + rb
    col = lax.broadcasted_iota(jnp.int32, logits.shape, 1)
    big = jnp.int32(1 << 20)
    ninf = -jnp.inf
    is_g = col < MOE_GROUPS
    gl = jnp.where(is_g, logits, ninf)
    gmax = jnp.max(gl, axis=-1, keepdims=True)
    g_sel = jnp.min(jnp.where(is_g & (gl == gmax), col, big), axis=-1, keepdims=True)
    p_top = 1.0 / jnp.sum(jnp.where(is_g, jnp.exp(gl - gmax), 0.0), axis=-1, keepdims=True)
    lo = MOE_GROUPS + g_sel * MOE_PER_GROUP
    in_grp = (col >= lo) & (col < lo + MOE_PER_GROUP)
    e1v = jnp.where(in_grp, logits, ninf)
    top1 = jnp.max(e1v, axis=-1, keepdims=True)
    idx1 = jnp.min(jnp.where(in_grp & (e1v == top1), col, big), axis=-1, keepdims=True)
    e2v = jnp.where(col == idx1, ninf, e1v)
    top2 = jnp.max(e2v, axis=-1, keepdims=True)
    idx2 = jnp.min(jnp.where(in_grp & (col != idx1) & (e2v == top2), col, big), axis=-1, keepdims=True)
    ex = jnp.exp(top2 - top1)
    s1 = 1.0 / (1.0 + ex)
    return jnp.where(col == idx1, p_top * s1, 0.0) + jnp.where(col == idx2, p_top * (ex * s1), 0.0)


def _merge_kernel(yh_ref, yf_ref, h_ref, o_ref, g0_ref, g1_ref, g2_ref, x_ref, gate_ref,
                  wb_ref, wo_ref, nw_ref, n2_ref, sh_ref, sc_ref, rw_ref, rb_ref, out_ref, xn_ref, comb_ref):
    hd = LANES
    h = h_ref[...]
    parts = []
    for i in range(ML_HEADS):
        hh = h[:, i * hd:(i + 1) * hd]
        parts.append(hh * lax.rsqrt(jnp.mean(hh * hh, axis=-1, keepdims=True) + EPS))
    y_ml = jax.nn.sigmoid(o_ref[...]) * (jnp.concatenate(parts, axis=1) * nw_ref[...])
    acc = g0_ref[...].astype(F32) * _bdot(yh_ref[...], wb_ref[0])
    acc = acc + g1_ref[...].astype(F32) * _bdot(yf_ref[...], wb_ref[1])
    acc = acc + g2_ref[...].astype(F32) * _bdot(y_ml, wb_ref[2])
    x_new = x_ref[...] + gate_ref[...] * _bdot(acc, wo_ref[...])
    out_ref[...] = x_new
    t = _rms_mod(x_new, n2_ref[...], sh_ref[...], sc_ref[...])
    xn_ref[...] = t.astype(BF16)
    comb_ref[...] = _route(t, rw_ref[...], rb_ref[...])


def _merge(yh, yf, h, z_tm, gates, x, mods3, wb, wo, nw, n2w, rw, rb, *, seg, tm, o_col):
    nt, d = x.shape
    tok = lambda w, cidx: pl.BlockSpec((tm, w), lambda i: (i, cidx))
    mod = lambda k: pl.BlockSpec((None, 1, d), lambda i: (seg(i), 0, k))
    return pl.pallas_call(
        _merge_kernel,
        grid=(nt // tm,),
        in_specs=[tok(D_BR, 0), tok(D_BR, 0), tok(D_BR, 0),
                  tok(D_BR, o_col),
                  tok(d, 0), tok(d, 1), tok(d, 2),
                  tok(d, 0),
                  mod(2),
                  pl.BlockSpec((3, D_BR, d), lambda i: (0, 0, 0)),
                  pl.BlockSpec((d, d), lambda i: (0, 0)),
                  pl.BlockSpec((1, D_BR), lambda i: (0, 0)),
                  pl.BlockSpec((1, d), lambda i: (0, 0)),
                  mod(3), mod(4),
                  pl.BlockSpec((d, LANES), lambda i: (0, 0)),
                  pl.BlockSpec((1, LANES), lambda i: (0, 0))],
        out_specs=[tok(d, 0), tok(d, 0), tok(LANES, 0)],
        out_shape=[jax.ShapeDtypeStruct((nt, d), F32), jax.ShapeDtypeStruct((nt, d), BF16),
                   jax.ShapeDtypeStruct((nt, LANES), F32)],
        compiler_params=_params(("arbitrary",)),
        name="merge_branches_router",
    )(yh, yf, h, z_tm, gates, gates, gates, x, mods3, wb, wo, nw.reshape(1, D_BR), n2w.reshape(1, d),
      mods3, mods3, rw, rb)


def _moe_kernel(xn_ref, comb_ref, wg_ref, wu_ref, wd_ref, x_ref, gate_ref, nw_ref, sh_ref, sc_ref, *out_and_scratch,
                final, n_keep):
    acc_ref = out_and_scratch[-1]
    e = pl.program_id(1)

    @pl.when(e == 0)
    def _():
        acc_ref[...] = jnp.zeros(acc_ref.shape, F32)

    xn = xn_ref[...]
    comb = comb_ref[...]
    col = lax.broadcasted_iota(jnp.int32, comb.shape, 1)
    acts = []
    for i in range(MOE_PER_GROUP):
        cw = jnp.sum(jnp.where(col == e * MOE_PER_GROUP + i + MOE_GROUPS, comb, 0.0), axis=-1, keepdims=True)
        hg = jnp.dot(xn, wg_ref[i], preferred_element_type=F32)
        hu = jnp.dot(xn, wu_ref[i], preferred_element_type=F32)
        acts.append(((hg * jax.nn.sigmoid(hg)) * hu * cw).astype(BF16))
    wd = wd_ref[...].reshape(MOE_PER_GROUP * EXPERT_HID, wd_ref.shape[-1])
    acc_ref[...] += jnp.dot(jnp.concatenate(acts, axis=1), wd, preferred_element_type=F32)

    n_steps = MOE_EXPERTS // MOE_PER_GROUP
    if final:
        y_ref, = out_and_scratch[:-1]

        @pl.when((e == n_steps - 1) & (pl.program_id(0) < n_keep))
        def _():
            x_new = x_ref[...] + gate_ref[...] * acc_ref[...]
            y_ref[...] = x_new * lax.rsqrt(jnp.mean(x_new * x_new, axis=-1, keepdims=True) + EPS) * nw_ref[...]
    else:
        o_ref, xn_next_ref = out_and_scratch[:-1]

        @pl.when(e == n_steps - 1)
        def _():
            x_new = x_ref[...] + gate_ref[...] * acc_ref[...]
            o_ref[...] = x_new
            xn_next_ref[...] = _rms_mod(x_new, nw_ref[...], sh_ref[...], sc_ref[...]).astype(BF16)


def _moe(xn, comb, wg, wu, wd, x, mods3, post_w, post_mods3, *, seg, tm, final, n_keep_rows):
    nt, d = x.shape
    n_keep = n_keep_rows // tm
    tok = pl.BlockSpec((tm, d), lambda i, e: (i, 0))
    if final:
        out_specs = [pl.BlockSpec((tm, d), lambda i, e: (jnp.minimum(i, n_keep - 1), 0))]
        out_shape = [jax.ShapeDtypeStruct((n_keep_rows, d), F32)]
    else:
        out_specs = [tok, tok]
        out_shape = [jax.ShapeDtypeStruct((nt, d), F32), jax.ShapeDtypeStruct((nt, d), BF16)]
    mod = lambda k: pl.BlockSpec((None, 1, d), lambda i, e: (seg(i), 0, k))
    return pl.pallas_call(
        functools.partial(_moe_kernel, final=final, n_keep=n_keep),
        grid=(nt // tm, MOE_EXPERTS // MOE_PER_GROUP),
        in_specs=[tok,
                  pl.BlockSpec((tm, LANES), lambda i, e: (i, 0)),
                  pl.BlockSpec((MOE_PER_GROUP, d, EXPERT_HID), lambda i, e: (e, 0, 0)),
                  pl.BlockSpec((MOE_PER_GROUP, d, EXPERT_HID), lambda i, e: (e, 0, 0)),
                  pl.BlockSpec((MOE_PER_GROUP, EXPERT_HID, d), lambda i, e: (e, 0, 0)),
                  tok,
                  mod(5),
                  pl.BlockSpec((1, d), lambda i, e: (0, 0)),
                  mod(0), mod(1)],
        out_specs=out_specs,
        out_shape=out_shape,
        scratch_shapes=[pltpu.VMEM((tm, d), F32)],
        compiler_params=_params(("arbitrary", "arbitrary")),
        name="moe_experts_final" if final else "moe_experts",
    )(xn, comb, wg, wu, wd, x, mods3, post_w.reshape(1, d), post_mods3, post_mods3)


def _slab_to_tm(y_s):
    ns, c, _ = y_s.shape
    return jnp.transpose(y_s, (0, 2, 1)).reshape(ns * LANES, c)


def kernel(x, c, ctx, c_ctx, ada_w, ada_b, norm1_w, norm2_w, w_in, b_in, hy_conv_w, hy_conv_b, hy_f_w1, hy_f_b1, hy_f_w2, hy_f_b2, hy_f_w3, hy_f_freq, hy_decay, hy_skip, ml_conv_w, ml_conv_b, ml_norm_w, w_branch, w_out, moe_rg_w, moe_rg_b, moe_re_w, moe_re_b, moe_w_gate, moe_w_up, moe_w_down, norm_f_w):
    nb, seq, d = x.shape
    lc = ctx.shape[1]
    depth = ada_w.shape[0]
    assert d == D_MODEL and seq % (GRID_W * 2) == 0 and lc % LANES == 0 and nb + 1 <= 8
    rows = seq // GRID_W
    a_lat = seq // LANES
    a_ctx = lc // LANES
    n_lat, n_ctx = nb * seq, nb * lc
    tm = 256
    tm_moe = {"lat": _pick(seq, (1024, 512, 256)), "ctx": _pick(n_ctx, (512, 256))}
    tm_mrg = {"lat": _pick(seq, (512, 256)), "ctx": _pick(n_ctx, (512, 256))}
    assert seq % tm == 0 and n_ctx % tm == 0
    seg_of = lambda s, t: (lambda i: i // (seq // t)) if s == "lat" else (lambda i: nb)
    streams = ("lat", "ctx")
    xs = {"lat": x.reshape(n_lat, d), "ctx": ctx.reshape(n_ctx, d)}
    xn = {}
    cvec = jnp.zeros((8, d), F32).at[:nb].set(c).at[nb].set(c_ctx)
    mods = _mods(cvec, ada_w, ada_b)

    o_fn, o_ml, o_mlg, o_gate = 3 * D_BR, 4 * D_BR, 8 * D_BR, 8 * D_BR + 4 * ML_HEADS
    pad_g = LANES - 4 * ML_HEADS

    for l in range(depth):
        lp = {"hy_f_w1": hy_f_w1[l], "hy_f_b1": hy_f_b1[l], "hy_f_w2": hy_f_w2[l], "hy_f_b2": hy_f_b2[l],
              "hy_f_w3": hy_f_w3[l], "hy_f_freq": hy_f_freq[l], "hy_decay": hy_decay[l]}
        mods3 = mods[l].reshape(8, 1, 6 * d)
        wl, bl = w_in[l], b_in[l]
        w_pq, b_pq = _fn_fold(wl[:, o_fn:o_ml], bl[o_fn:o_ml])
        o_v, o_o = o_ml + 2 * D_BR, o_ml + 3 * D_BR
        w_cm = jnp.concatenate([wl[:, :o_fn], wl[:, o_ml:o_v], w_pq, wl[:, o_v:o_o]], axis=1)
        b_cm = jnp.concatenate([bl[:o_fn], bl[o_ml:o_v], b_pq, bl[o_v:o_o]])
        w_tm = jnp.concatenate([wl[:, o_o:o_mlg], wl[:, o_mlg:o_gate], jnp.zeros((d, pad_g), F32)], axis=1)
        b_tm = jnp.concatenate([bl[o_o:o_mlg], bl[o_mlg:o_gate], jnp.zeros((pad_g,), F32)])
        c_hy, c_qk, c_fn, c_v = 0, 3 * D_BR, 5 * D_BR, 7 * D_BR
        g_col = D_BR // LANES

        last = l + 1 == depth
        live = ("lat",) if last else streams
        if l == 0:
            xn = {s: _norm_mod(xs[s], norm1_w[l], mods3, 0, 1, seg_of(s, tm), tm) for s in streams}
        w_tm_b, w_cm_t = w_tm.astype(BF16), w_cm.T.astype(BF16)
        z_tm = {s: _mm_tm(xn[s], w_tm_b, b_tm, gate=False) for s in streams}
        w_gate_b = wl[:, o_gate:].astype(BF16)
        gates = {s: _mm_tm(xn[s], w_gate_b, bl[o_gate:], gate=True) for s in live}
        z_s = {s: _mm_slab(xn[s], w_cm_t, b_cm) for s in streams}

        grid_kw = {"lat": dict(rows=rows, width=GRID_W), "ctx": dict(rows=1, width=lc)}
        hy_w, hy_b = hy_conv_w[l].reshape(9, 3 * D_BR), hy_conv_b[l]
        ml_w, ml_b = ml_conv_w[l].reshape(9, 2 * D_BR), ml_conv_b[l]
        conv = lambda s, w, b, lo, n, act: _conv(z_s[s], w, b, chan_lo=lo, chan_n=n, silu=act, n_batch=nb,
                                                 slab0=0, **grid_kw[s])
        u = {s: conv(s, hy_w, hy_b, c_hy, 3 * D_BR, False) for s in live}
        qk = {s: conv(s, ml_w, ml_b, c_qk, 2 * D_BR, True) for s in streams}
        k_tm = {s: _slab_to_tm(qk[s][:, D_BR:]).astype(BF16) for s in streams}
        h_s = _mlstm_t(qk, k_tm, z_s, z_tm, n_batch=nb, v_chan=c_v, g_col=g_col)
        h_sum = {s: _slab_to_tm(h_s[s][0] + h_s[s][1]) for s in live}

        yh, yf = {}, {}
        yh["lat"] = _slab_to_tm(_hyena(u["lat"], _hyena_taps(seq, 2 * a_lat, lp), hy_skip[l],
                                       a_in=a_lat, na=2 * a_lat, n_batch=nb)).astype(BF16)
        yk = _fn_seq(z_s["lat"], a_n=a_lat, n_batch=nb, chan_lo=c_fn)
        yf["lat"] = jnp.transpose(yk, (0, 3, 2, 1)).reshape(n_lat, D_BR).astype(BF16)
        if not last:
            yh["ctx"] = _hyena_short(u["ctx"], _hyena_taps(lc, 2 * a_ctx, lp), hy_skip[l],
                                     a_n=a_ctx, n_batch=nb).reshape(n_ctx, D_BR)
            yf["ctx"] = _fn_small(z_s["ctx"], a_n=a_ctx, n_batch=nb, chan_lo=c_fn).reshape(n_ctx, D_BR)

        rw = jnp.concatenate([moe_rg_w[l], moe_re_w[l], jnp.zeros((d, LANES - MOE_GROUPS - MOE_EXPERTS), F32)], axis=1)
        rb = jnp.concatenate([moe_rg_b[l], moe_re_b[l], jnp.zeros((LANES - MOE_GROUPS - MOE_EXPERTS,), F32)]).reshape(1, LANES)
        wb, wo = w_branch[l].astype(BF16), w_out[l].astype(BF16)
        experts = (moe_w_gate[l].astype(BF16), moe_w_up[l].astype(BF16), moe_w_down[l].astype(BF16))
        for s in live:
            xs[s], xn2, comb = _merge(yh[s], yf[s], h_sum[s], z_tm[s], gates[s], xs[s], mods3, wb, wo,
                                      ml_norm_w[l], norm2_w[l], rw, rb, seg=seg_of(s, tm_mrg[s]), tm=tm_mrg[s],
                                      o_col=0)
            moe_kw = dict(seg=seg_of(s, tm_moe[s]), tm=tm_moe[s], n_keep_rows=xs[s].shape[0])
            if last:
                out, = _moe(xn2, comb, *experts, xs[s], mods3, norm_f_w, mods3, final=True, **moe_kw)
            else:
                xs[s], xn[s] = _moe(xn2, comb, *experts, xs[s], mods3, norm1_w[l + 1],
                                    mods[l + 1].reshape(8, 1, 6 * d), final=False, **moe_kw)

    return out.reshape(nb, seq, d)
```

```python
import functools
import math

import numpy as np
import jax
import jax.numpy as jnp
from jax import lax
from jax.experimental import pallas as pl
from jax.experimental.pallas import tpu as pltpu

F32 = jnp.float32
BF16 = jnp.bfloat16

D_MODEL = 1024
D_BR = 512
GRID_W = 64
LANES = 128
CB = 8
HY_ORDER = 2
HY_BANDS = 16
FN_GROUPS = 4
ML_HEADS = 4
MOE_GROUPS = 4
MOE_PER_GROUP = 4
MOE_EXPERTS = 16
EXPERT_HID = 256
EPS = 1e-6
VMEM_LIMIT = 56 * 1024 * 1024


def _params(sem):
    return pltpu.CompilerParams(dimension_semantics=sem, vmem_limit_bytes=VMEM_LIMIT)


def _bdot(a, b):
    return jnp.dot(a.astype(BF16), b.astype(BF16), preferred_element_type=F32)


def _split3(x):
    hi = x.astype(BF16)
    r1 = x - hi.astype(F32)
    mid = r1.astype(BF16)
    lo = (r1 - mid.astype(F32)).astype(BF16)
    return hi, mid, lo


def _dot_f32ish(x, w):
    xh, xm, xl = _split3(x)
    wh, wm, wl = _split3(w)
    d = lambda a, b: jnp.dot(a, b, preferred_element_type=F32)
    return (d(xh, wh) + (d(xh, wm) + d(xm, wh))) + (d(xm, wm) + d(xh, wl) + d(xl, wh))


def _dot_f32ish3(x, w):
    xh, xm, _ = _split3(x)
    wh, wm, _ = _split3(w)
    d = lambda a, b: jnp.dot(a, b, preferred_element_type=F32)
    return d(xh, wh) + (d(xh, wm) + d(xm, wh))


def _swap_halves(x):
    return jnp.concatenate([x[..., LANES:], x[..., :LANES]], axis=-1)


def _mods_kernel(c_ref, w_ref, b_ref, o_ref):
    c = c_ref[...]
    s = c * jax.nn.sigmoid(c)
    o_ref[...] = _dot_f32ish(s, w_ref[...]) + b_ref[...]


def _mods(cvec, ada_w, ada_b):
    depth, d, n6 = ada_w.shape
    tn = 1536
    return pl.pallas_call(
        _mods_kernel,
        grid=(depth, n6 // tn),
        in_specs=[pl.BlockSpec((8, d), lambda l, j: (0, 0)),
                  pl.BlockSpec((None, d, tn), lambda l, j: (l, 0, j)),
                  pl.BlockSpec((None, 1, tn), lambda l, j: (l, 0, j))],
        out_specs=pl.BlockSpec((None, 8, tn), lambda l, j: (l, 0, j)),
        out_shape=jax.ShapeDtypeStruct((depth, 8, n6), F32),
        compiler_params=_params(("arbitrary", "arbitrary")),
        name="adaln_mods",
    )(cvec, ada_w, ada_b.reshape(depth, 1, n6))


def _norm_mod_kernel(x_ref, w_ref, sh_ref, sc_ref, o_ref):
    x = x_ref[...]
    y = x * lax.rsqrt(jnp.mean(x * x, axis=-1, keepdims=True) + EPS) * w_ref[...]
    o_ref[...] = (y * (1.0 + sc_ref[...]) + sh_ref[...]).astype(o_ref.dtype)


def _norm_mod(x, w, mods3, col_shift, col_scale, seg, tm):
    nt, d = x.shape
    return pl.pallas_call(
        _norm_mod_kernel,
        grid=(nt // tm,),
        in_specs=[pl.BlockSpec((tm, d), lambda i: (i, 0)),
                  pl.BlockSpec((1, d), lambda i: (0, 0)),
                  pl.BlockSpec((None, 1, d), lambda i: (seg(i), 0, col_shift)),
                  pl.BlockSpec((None, 1, d), lambda i: (seg(i), 0, col_scale))],
        out_specs=pl.BlockSpec((tm, d), lambda i: (i, 0)),
        out_shape=jax.ShapeDtypeStruct((nt, d), BF16),
        compiler_params=_params(("arbitrary",)),
        name="norm_mod",
    )(x, w.reshape(1, d), mods3, mods3)


def _mm_tm_kernel(x_ref, w_ref, b_ref, o_ref, *, gate):
    y = jnp.dot(x_ref[...], w_ref[...], preferred_element_type=F32) + b_ref[...]
    o_ref[...] = (jax.nn.sigmoid(y) if gate else y).astype(o_ref.dtype)


def _pick(n, cands):
    for c in cands:
        if n % c == 0:
            return c
    raise ValueError(f"no tile for {n} in {cands}")


def _mm_tm(xn, w, b, *, gate):
    nt, k = xn.shape
    n = w.shape[1]
    tm = _pick(nt, (1056, 1024, 768, 512, 256))
    tn = _pick(n, (1536, 1408, 1152, 1024, 640, 512, 384, 256, 128))
    return pl.pallas_call(
        functools.partial(_mm_tm_kernel, gate=gate),
        grid=(n // tn, nt // tm),
        in_specs=[pl.BlockSpec((tm, k), lambda j, i: (i, 0)),
                  pl.BlockSpec((k, tn), lambda j, i: (0, j)),
                  pl.BlockSpec((1, tn), lambda j, i: (0, j))],
        out_specs=pl.BlockSpec((tm, tn), lambda j, i: (i, j)),
        out_shape=jax.ShapeDtypeStruct((nt, n), BF16 if gate else F32),
        compiler_params=_params(("arbitrary", "arbitrary")),
        name="inproj_gates" if gate else "inproj_token_major",
    )(xn, w, b.reshape(1, n))


def _mm_slab_kernel(w_ref, x_ref, b_ref, o_ref, *, slabs):
    w = w_ref[...]
    b = b_ref[...]
    step = 2 if slabs % 2 == 0 else 1
    for s in range(0, slabs, step):
        xs = x_ref[s * LANES:(s + step) * LANES, :]
        y = lax.dot_general(w, xs, (((1,), (1,)), ((), ())), preferred_element_type=F32) + b
        for i in range(step):
            o_ref[s + i] = y[:, i * LANES:(i + 1) * LANES]


def _mm_slab(xn, wt, b):
    nt, k = xn.shape
    c = wt.shape[0]
    ns = nt // LANES
    ts = _pick(ns, (12, 11, 8, 6, 4, 3, 2, 1))
    tc = _pick(c, (1024, 896, 512, 256, 128))
    return pl.pallas_call(
        functools.partial(_mm_slab_kernel, slabs=ts),
        grid=(c // tc, ns // ts),
        in_specs=[pl.BlockSpec((tc, k), lambda j, i: (j, 0)),
                  pl.BlockSpec((ts * LANES, k), lambda j, i: (i, 0)),
                  pl.BlockSpec((tc, 1), lambda j, i: (j, 0))],
        out_specs=pl.BlockSpec((ts, tc, LANES), lambda j, i: (i, j, 0)),
        out_shape=jax.ShapeDtypeStruct((ns, c, LANES), F32),
        compiler_params=_params(("arbitrary", "arbitrary")),
        name="inproj_slab",
    )(wt, xn, b.reshape(c, 1))


def _conv_taps(rows, width):
    taps = []
    for dr in (-1, 0, 1):
        if rows == 1 and dr != 0:
            continue
        for dw in (-1, 0, 1):
            taps.append((dr, dw))
    return taps


def _silu(x):
    return x * jax.nn.sigmoid(x)


def _conv_kernel(x_ref, w_ref, b_ref, o_ref, *, taps, width, n_slabs, silu):
    ct = x_ref.shape[1]
    lane = lax.broadcasted_iota(jnp.int32, (ct, LANES), 1)
    bias = jnp.zeros((ct, LANES), F32) + b_ref[...]
    planes = []
    for t, (dr, dw) in enumerate(taps):
        w = w_ref[t]
        if width < LANES and dw != 0:
            col = lane % width + dw
            w = jnp.where((col >= 0) & (col < width), w, 0.0)
        planes.append((dr * width + dw, w))

    def body(a, carry):
        x0 = x_ref[a]
        xm = jnp.where(a > 0, x_ref[jnp.maximum(a - 1, 0)], 0.0)
        xp = jnp.where(a < n_slabs - 1, x_ref[jnp.minimum(a + 1, n_slabs - 1)], 0.0)
        acc = bias
        for delta, w in planes:
            if delta == 0:
                src = x0
            elif delta > 0:
                src = pltpu.roll(jnp.where(lane >= delta, x0, xp), LANES - delta, 1)
            else:
                src = pltpu.roll(jnp.where(lane < LANES + delta, x0, xm), -delta, 1)
            acc = acc + src * w
        o_ref[a] = _silu(acc) if silu else acc
        return carry

    lax.fori_loop(0, n_slabs, body, 0, unroll=2 if n_slabs % 2 == 0 else 1)


def _conv_grid_kernel(x_ref, w_ref, b_ref, o_ref, ym_ref, yp_ref, *, width, n_slabs, silu):
    ct = x_ref.shape[1]
    lane = lax.broadcasted_iota(jnp.int32, (ct, LANES), 1)
    col = lane % width
    bias = jnp.zeros((ct, LANES), F32) + b_ref[...]

    def row_sums(a, carry):
        x0 = x_ref[a]
        xl = jnp.where(col >= 1, pltpu.roll(x0, 1, 1), 0.0)
        xr = jnp.where(col < width - 1, pltpu.roll(x0, LANES - 1, 1), 0.0)
        ym_ref[a] = w_ref[0] * xl + w_ref[1] * x0 + w_ref[2] * xr
        o_ref[a] = bias + w_ref[3] * xl + w_ref[4] * x0 + w_ref[5] * xr
        yp_ref[a] = w_ref[6] * xl + w_ref[7] * x0 + w_ref[8] * xr
        return carry

    def combine(a, carry):
        up = jnp.where(a > 0, ym_ref[jnp.maximum(a - 1, 0)], 0.0)
        dn = jnp.where(a < n_slabs - 1, yp_ref[jnp.minimum(a + 1, n_slabs - 1)], 0.0)
        from_up = jnp.where(lane < LANES - width, ym_ref[a], up)
        from_dn = jnp.where(lane >= width, yp_ref[a], dn)
        if 2 * width == LANES:
            y = o_ref[a] + pltpu.roll(from_up + from_dn, width, 1)
        else:
            y = o_ref[a] + pltpu.roll(from_up, width, 1) + pltpu.roll(from_dn, LANES - width, 1)
        o_ref[a] = _silu(y) if silu else y
        return carry

    unroll = 8 if n_slabs % 8 == 0 else 1
    lax.fori_loop(0, n_slabs, row_sums, 0, unroll=2 * unroll if n_slabs % (2 * unroll) == 0 else unroll)
    lax.fori_loop(0, n_slabs, combine, 0, unroll=unroll)


def _conv(z_s, w9, bias, *, rows, width, n_batch, slab0, chan_lo, chan_n, silu):
    seq = rows * width
    a_n = seq // LANES
    taps = tuple(_conv_taps(rows, width))
    assert all(abs(dr * width + dw) < LANES for dr, dw in taps)
    assert LANES % width == 0 or (rows == 1 and width % LANES == 0)
    tap_ids = [(dr + 1) * 3 + (dw + 1) for dr, dw in taps]
    w_t = jnp.broadcast_to(w9[jnp.array(tap_ids)][:, :, None], (len(taps), chan_n, LANES))
    ct = 64 if rows > 1 else 256
    assert chan_lo % ct == 0 and chan_n % ct == 0 and slab0 % a_n == 0
    nt_ = len(taps)
    if rows > 1:
        assert LANES % width == 0 and nt_ == 9
        body = functools.partial(_conv_grid_kernel, width=width, n_slabs=a_n, silu=silu)
        scratch = [pltpu.VMEM((a_n, ct, LANES), F32), pltpu.VMEM((a_n, ct, LANES), F32)]
    else:
        body = functools.partial(_conv_kernel, taps=taps, width=width, n_slabs=a_n, silu=silu)
        scratch = []
    return pl.pallas_call(
        body,
        scratch_shapes=scratch,
        grid=(n_batch, chan_n // ct),
        in_specs=[pl.BlockSpec((a_n, ct, LANES), lambda b, j: (slab0 // a_n + b, chan_lo // ct + j, 0)),
                  pl.BlockSpec((nt_, ct, LANES), lambda b, j: (0, j, 0)),
                  pl.BlockSpec((ct, 1), lambda b, j: (j, 0))],
        out_specs=pl.BlockSpec((a_n, ct, LANES), lambda b, j: (b, j, 0)),
        out_shape=jax.ShapeDtypeStruct((n_batch * a_n, chan_n, LANES), F32),
        compiler_params=_params(("arbitrary", "arbitrary")),
        name=f"dwconv_{rows}x{width}",
    )(z_s, w_t, bias.reshape(chan_n, 1))


def _dft_consts(a_in, na):
    n = na * LANES
    k = np.arange(na)[:, None]
    a = np.arange(a_in)[None, :]
    ang = 2 * np.pi * (k * a % na) / na
    fa = np.concatenate([np.cos(ang), -np.sin(ang)], axis=0)
    r = np.arange(LANES)
    ang_t = 2 * np.pi * (np.arange(na)[:, None] * r[None, :] % n) / n
    tr, ti = np.cos(ang_t), -np.sin(ang_t)
    ta = np.concatenate([tr, tr], axis=1)
    tb = np.concatenate([-ti, ti], axis=1)
    ang2 = 2 * np.pi * (r[:, None] * r[None, :] % LANES) / LANES
    c2, s2 = np.cos(ang2), np.sin(ang2)
    g2 = np.block([[c2, -s2], [s2, c2]])
    g2i = np.block([[c2, s2], [-s2, c2]])
    ang_i = 2 * np.pi * (np.arange(a_in)[:, None] * np.arange(na)[None, :] % na) / na
    ci, si = np.cos(ang_i) / n, -np.sin(ang_i) / n
    f = lambda v, dt: jnp.asarray(v, dtype=dt)
    return dict(fa=f(fa, F32), ta=f(ta, F32), tb=f(tb, F32), g2=f(g2, F32), g2i=f(g2i, F32),
                ci=f(ci, F32), si=f(si, F32))


def _fwd_slab_stage(ms, fa, ta, tb, na):
    pp = jnp.dot(fa, jnp.concatenate(ms, axis=1).astype(BF16), preferred_element_type=F32)
    out = []
    for c in range(len(ms)):
        sl = slice(c * LANES, (c + 1) * LANES)
        p = jnp.concatenate([pp[:na, sl], pp[na:, sl]], axis=1)
        out.append(p * ta + _swap_halves(p) * tb)
    return out


def _chan_load(ref, c):
    n, cb, _ = ref.shape
    return ref.reshape(n * cb, LANES)[pl.ds(c, n, stride=cb), :]


def _chan_store(ref, c, val):
    ref[:, c, :] = val


def _dot_f32ish_k(w, h):
    wh, wm, _ = _split3(w)
    hh, hm, _ = _split3(h)
    lhs = jnp.concatenate([wh, wh, wm], axis=1)
    rhs = jnp.concatenate([hh, hm, hh], axis=0)
    return jnp.dot(lhs, rhs, preferred_element_type=F32)


def _taps_kernel(bands_ref, w1t_ref, w1c_ref, w1s_ref, b1_ref, w2_ref, b2_ref, fq_ref, w3_ref, dec_ref, o_ref,
                 *, seq, a_seq, na, spb):
    step = pl.program_id(0)
    width = spb * LANES
    n_total = na * LANES
    is_f = step * spb < a_seq
    n = step * width + lax.broadcasted_iota(jnp.int32, (1, width), 1)
    pos = jnp.where(is_f, n, n_total - n)
    lo, hi = jnp.where(is_f, -1, n_total - seq), jnp.where(is_f, seq, n_total)
    t = pos.astype(F32) / seq
    fq = fq_ref[...]
    ang = ((2 * math.pi) * t) * bands_ref[...]
    pre = (w1t_ref[...] * t + _dot_f32ish(w1c_ref[...], jnp.cos(ang))
           + _dot_f32ish(w1s_ref[...], jnp.sin(ang)))
    h = jnp.sin(fq * (pre + b1_ref[...]))
    h = jnp.sin(fq * (_dot_f32ish(w2_ref[...], h) + b2_ref[...]))
    d = jnp.where(is_f, 0, 1)
    dec = jnp.abs(dec_ref[d])
    k = _dot_f32ish_k(w3_ref[d], h)
    for i in range(spb):
        sl = slice(i * LANES, (i + 1) * LANES)
        live = (n[:, sl] > lo) & (n[:, sl] < hi)
        o_ref[i] = jnp.where(live, k[:, sl] * jnp.exp(-t[:, sl] * dec), 0.0)


def _hyena_taps(seq, na, lp):
    nc = HY_ORDER * D_BR
    hid = lp["hy_f_w2"].shape[0]
    w1 = lp["hy_f_w1"]
    col = lambda v: v.reshape(-1, 1)
    bands = col(jnp.linspace(1e-4, HY_BANDS - 1, HY_BANDS, dtype=F32))
    w3 = jnp.transpose(lp["hy_f_w3"].T.reshape(HY_ORDER, 2, D_BR, hid), (1, 0, 2, 3)).reshape(2, nc, hid)
    dec = jnp.broadcast_to(jnp.transpose(lp["hy_decay"], (1, 0, 2)).reshape(2, nc, 1), (2, nc, LANES))
    a_seq = seq // LANES
    spb = min(8, a_seq)
    assert na == 2 * a_seq and a_seq % spb == 0
    args = (bands, col(w1[0]), w1[1:1 + HY_BANDS].T, w1[1 + HY_BANDS:].T, col(lp["hy_f_b1"]), lp["hy_f_w2"].T,
            col(lp["hy_f_b2"]), col(lp["hy_f_freq"]), w3, dec)
    full = lambda v: pl.BlockSpec(v.shape, lambda s: (0,) * v.ndim)
    return pl.pallas_call(
        functools.partial(_taps_kernel, seq=seq, a_seq=a_seq, na=na, spb=spb),
        grid=(na // spb,),
        in_specs=[full(v) for v in args],
        out_specs=pl.BlockSpec((spb, nc, LANES), lambda s: (s, 0, 0)),
        out_shape=jax.ShapeDtypeStruct((na, nc, LANES), F32),
        compiler_params=_params(("arbitrary",)),
        name=f"hyena_filter_taps_{na}",
    )(*args)


def _hyena_kernel(v_ref, x1_ref, x2_ref, k0_ref, k1_ref, skip_ref, fa_ref, faf_ref, ta_ref, tb_ref, g2_ref, g2i_ref,
                  ci_ref, si_ref, o_ref, p_buf, z_buf, kf_buf, *, a_in, na):
    fa, ta, tb = fa_ref[...].astype(BF16), ta_ref[...], tb_ref[...]
    ci, si = ci_ref[...].astype(BF16), si_ref[...].astype(BF16)

    @pl.when(pl.program_id(1) == 0)
    def _():
        faf = faf_ref[...].astype(BF16)
        for order, k_ref in enumerate((k0_ref, k1_ref)):
            ms = [_chan_load(k_ref, c) for c in range(CB)]
            scales = [lax.rsqrt(jnp.sum(jnp.sum(m * m, axis=1, keepdims=True), axis=0, keepdims=True) + EPS)
                      for m in ms]
            for c, p in enumerate(_fwd_slab_stage(ms, faf, ta, tb, na)):
                p_buf[c] = p
            x = _bdot(p_buf[...].reshape(CB * na, 2 * LANES), g2_ref[...]).reshape(CB, na, 2 * LANES)
            for c in range(CB):
                kf = x[c] * scales[c]
                kr, ki = kf[:, :LANES], kf[:, LANES:]
                kf_buf[0, order, c] = jnp.concatenate([kr, kr], axis=1)
                kf_buf[1, order, c] = jnp.concatenate([-ki, ki], axis=1)

    def spectral(order):
        x = _bdot(p_buf[...].reshape(CB * na, 2 * LANES), g2_ref[...])
        ka = kf_buf[0, order].reshape(CB * na, 2 * LANES)
        kb = kf_buf[1, order].reshape(CB * na, 2 * LANES)
        y = x * ka + _swap_halves(x) * kb
        bm = _bdot(y, g2i_ref[...]).reshape(CB, na, 2 * LANES)
        p_buf[...] = bm * ta - _swap_halves(bm) * tb

    def conv_out():
        br = jnp.concatenate([p_buf[c][:, :LANES] for c in range(CB)], axis=1).astype(BF16)
        bi = jnp.concatenate([p_buf[c][:, LANES:] for c in range(CB)], axis=1).astype(BF16)
        y = jnp.dot(ci, br, preferred_element_type=F32) + jnp.dot(si, bi, preferred_element_type=F32)
        return [y[:, c * LANES:(c + 1) * LANES] for c in range(CB)]

    def forward(ms):
        for c, p in enumerate(_fwd_slab_stage(ms, fa, ta, tb, na)):
            p_buf[c] = p

    vs = [_chan_load(v_ref, c) for c in range(CB)]
    forward(vs)
    spectral(0)
    zs = [_chan_load(x1_ref, c) * (y + vs[c] * skip_ref[0, c]) for c, y in enumerate(conv_out())]
    for c in range(CB):
        z_buf[c] = zs[c]
    forward(zs)
    spectral(1)
    for c, y in enumerate(conv_out()):
        _chan_store(o_ref, c, _chan_load(x2_ref, c) * (y + z_buf[c] * skip_ref[1, c]))


def _hyena(u_s, taps_s, skip, *, a_in, na, n_batch):
    cs = _dft_consts(a_in, na)
    faf = _dft_consts(na, na)["fa"]
    nblk = D_BR // CB
    const = lambda shp: pl.BlockSpec(shp, lambda j, b: (0,) * len(shp))
    skip_b = jnp.broadcast_to(skip[:, :, None, None], (HY_ORDER, D_BR, 1, LANES))
    return pl.pallas_call(
        functools.partial(_hyena_kernel, a_in=a_in, na=na),
        grid=(nblk, n_batch),
        in_specs=[pl.BlockSpec((a_in, CB, LANES), lambda j, b: (b, j, 0)),
                  pl.BlockSpec((a_in, CB, LANES), lambda j, b: (b, nblk + j, 0)),
                  pl.BlockSpec((a_in, CB, LANES), lambda j, b: (b, 2 * nblk + j, 0)),
                  pl.BlockSpec((na, CB, LANES), lambda j, b: (0, j, 0)),
                  pl.BlockSpec((na, CB, LANES), lambda j, b: (0, nblk + j, 0)),
                  pl.BlockSpec((HY_ORDER, CB, 1, LANES), lambda j, b: (0, j, 0, 0)),
                  const((2 * na, a_in)), const((2 * na, na)), const((na, 2 * LANES)), const((na, 2 * LANES)),
                  const((2 * LANES, 2 * LANES)), const((2 * LANES, 2 * LANES)),
                  const((a_in, na)), const((a_in, na))],
        out_specs=pl.BlockSpec((a_in, CB, LANES), lambda j, b: (b, j, 0)),
        out_shape=jax.ShapeDtypeStruct((n_batch * a_in, D_BR, LANES), F32),
        scratch_shapes=[pltpu.VMEM((CB, na, 2 * LANES), F32), pltpu.VMEM((CB, a_in, LANES), F32),
                        pltpu.VMEM((2, HY_ORDER, CB, na, 2 * LANES), F32)],
        compiler_params=_params(("arbitrary", "arbitrary")),
        name=f"hyena_longconv_{a_in}",
    )(u_s, u_s, u_s, taps_s, taps_s, skip_b, cs["fa"], faf, cs["ta"], cs["tb"], cs["g2"], cs["g2i"],
      cs["ci"], cs["si"])


def _slabs_to_rows(ref, n):
    return jnp.concatenate([ref[a].T for a in range(n)], axis=0)


def _hyena_short_kernel(v_ref, x1_ref, x2_ref, k0_ref, k1_ref, skip_ref, f_ref, g_ref, o_ref, *, a_n, a_k):
    seq, nf = a_n * LANES, a_k * LANES
    ff = f_ref[...].astype(BF16)
    gi = g_ref[...].astype(BF16)
    v, x1, x2 = _slabs_to_rows(v_ref, a_n), _slabs_to_rows(x1_ref, a_n), _slabs_to_rows(x2_ref, a_n)

    def longconv(u, k_ref):
        k = _slabs_to_rows(k_ref, a_k)
        s = lax.rsqrt(jnp.sum(k * k, axis=0, keepdims=True) + EPS)
        kf = jnp.dot(ff, k.astype(BF16), preferred_element_type=F32) * s
        x = jnp.dot(ff[:, :seq], u.astype(BF16), preferred_element_type=F32)
        xr, xi, kr, ki = x[:nf], x[nf:], kf[:nf], kf[nf:]
        y = jnp.concatenate([xr * kr - xi * ki, xr * ki + xi * kr], axis=0)
        return jnp.dot(gi, y.astype(BF16), preferred_element_type=F32)

    z = x1 * (longconv(v, k0_ref) + v * skip_ref[0])
    o_ref[...] = x2 * (longconv(z, k1_ref) + z * skip_ref[1])


def _hyena_short(u_s, taps_s, skip, *, a_n, n_batch):
    a_k = 2 * a_n
    seq, nf = a_n * LANES, a_k * LANES
    k = np.arange(nf)
    ang = 2 * np.pi * (k[:, None] * k[None, :] % nf) / nf
    f = np.concatenate([np.cos(ang), -np.sin(ang)], axis=0)
    g = np.concatenate([np.cos(ang[:seq]), -np.sin(ang[:seq])], axis=1) / nf
    nblk = D_BR // LANES
    const = lambda shp: pl.BlockSpec(shp, lambda b, j: (0,) * len(shp))
    return pl.pallas_call(
        functools.partial(_hyena_short_kernel, a_n=a_n, a_k=a_k),
        grid=(n_batch, nblk),
        in_specs=[pl.BlockSpec((a_n, LANES, LANES), lambda b, j: (b, j, 0)),
                  pl.BlockSpec((a_n, LANES, LANES), lambda b, j: (b, nblk + j, 0)),
                  pl.BlockSpec((a_n, LANES, LANES), lambda b, j: (b, 2 * nblk + j, 0)),
                  pl.BlockSpec((a_k, LANES, LANES), lambda b, j: (0, j, 0)),
                  pl.BlockSpec((a_k, LANES, LANES), lambda b, j: (0, nblk + j, 0)),
                  pl.BlockSpec((HY_ORDER, 1, LANES), lambda b, j: (0, 0, j)),
                  const((2 * nf, nf)), const((seq, 2 * nf))],
        out_specs=pl.BlockSpec((None, seq, LANES), lambda b, j: (b, 0, j)),
        out_shape=jax.ShapeDtypeStruct((n_batch, seq, D_BR), F32),
        compiler_params=_params(("arbitrary", "arbitrary")),
        name="hyena_short",
    )(u_s, u_s, u_s, taps_s, taps_s, skip.reshape(HY_ORDER, 1, D_BR), jnp.asarray(f, F32), jnp.asarray(g, F32))


def _chan_dft_mats():
    r = np.arange(LANES)
    ang = 2 * np.pi * (r[:, None] * r[None, :] % LANES) / LANES
    return np.cos(ang), np.sin(ang)


def _fn_fold_kernel(w_ref, cs_ref, o_ref):
    o_ref[...] = _dot_f32ish(w_ref[...], cs_ref[...])


def _fn_fold(w_fn, b_fn):
    d = w_fn.shape[0]
    c, s = _chan_dft_mats()
    cs = jnp.asarray(np.concatenate([c, s], axis=1), dtype=F32)
    rows = d + 8
    w_aug = jnp.concatenate([w_fn, b_fn[None, :], jnp.zeros((7, D_BR), F32)], axis=0)
    out = pl.pallas_call(
        _fn_fold_kernel,
        grid=(FN_GROUPS,),
        in_specs=[pl.BlockSpec((rows, LANES), lambda g: (0, g)),
                  pl.BlockSpec((LANES, 2 * LANES), lambda g: (0, 0))],
        out_specs=pl.BlockSpec((rows, 2 * LANES), lambda g: (0, g)),
        out_shape=jax.ShapeDtypeStruct((rows, 2 * D_BR), F32),
        compiler_params=_params(("arbitrary",)),
        name="fnet_fold_channel_dft",
    )(w_aug, cs)
    return out[:d], out[d]


def _fn_seq_kernel(p_ref, q_ref, fa_ref, tr_ref, ti_ref, g_ref, o_ref, a_buf, *, a_n, n_batch, scale):
    fa, tr, ti = fa_ref[...].astype(BF16), tr_ref[...], ti_ref[...]
    side = lambda m: jnp.concatenate([m[b * a_n:(b + 1) * a_n] for b in range(n_batch)], axis=1).astype(BF16)
    for c in range(CB):
        r1 = jnp.dot(fa, side(_chan_load(p_ref, c)), preferred_element_type=F32)
        r2 = jnp.dot(fa, side(_chan_load(q_ref, c)), preferred_element_type=F32)
        ar_all = r1[:a_n] - r2[a_n:]
        ai_all = -(r2[:a_n] + r1[a_n:])
        for b in range(n_batch):
            ar, ai = ar_all[:, b * LANES:(b + 1) * LANES], ai_all[:, b * LANES:(b + 1) * LANES]
            a_buf[b, c] = jnp.concatenate([ar * tr - ai * ti, ar * ti + ai * tr], axis=1)
    y = _bdot(a_buf[...].reshape(n_batch * CB * a_n, 2 * LANES), g_ref[...]) * scale
    o_ref[...] = y.reshape(n_batch, CB, a_n, LANES)


def _fn_seq(pq, *, a_n, n_batch, chan_lo):
    seq = a_n * LANES
    k = np.arange(a_n)
    ang = 2 * np.pi * (k[:, None] * k[None, :] % a_n) / a_n
    fa = np.concatenate([np.cos(ang), np.sin(ang)], axis=0)
    r = np.arange(LANES)
    ang_t = 2 * np.pi * (k[:, None] * r[None, :] % seq) / seq
    c2, s2 = _chan_dft_mats()
    g = np.concatenate([c2, s2], axis=0)
    nblk = LANES // CB
    const = lambda shp: pl.BlockSpec(shp, lambda j: (0,) * len(shp))

    def chan_blk(j, off):
        return chan_lo // CB + (j // nblk) * (2 * nblk) + off * nblk + j % nblk

    return pl.pallas_call(
        functools.partial(_fn_seq_kernel, a_n=a_n, n_batch=n_batch, scale=1.0 / math.sqrt(seq * LANES)),
        grid=(D_BR // CB,),
        in_specs=[pl.BlockSpec((n_batch * a_n, CB, LANES), lambda j: (0, chan_blk(j, 0), 0)),
                  pl.BlockSpec((n_batch * a_n, CB, LANES), lambda j: (0, chan_blk(j, 1), 0)),
                  const((2 * a_n, a_n)), const((a_n, LANES)), const((a_n, LANES)), const((2 * LANES, LANES))],
        out_specs=pl.BlockSpec((n_batch, CB, a_n, LANES), lambda j: (0, j, 0, 0)),
        out_shape=jax.ShapeDtypeStruct((n_batch, D_BR, a_n, LANES), F32),
        scratch_shapes=[pltpu.VMEM((n_batch, CB, a_n, 2 * LANES), F32)],
        compiler_params=_params(("arbitrary",)),
        name="fnet_sequence_dft",
    )(pq, pq, jnp.asarray(fa, F32), jnp.asarray(np.cos(ang_t), F32), jnp.asarray(-np.sin(ang_t), F32),
      jnp.asarray(g, F32))


def _fn_small_kernel(pq_ref, cl_ref, sl_ref, o_ref, *, a_n, scale):
    pq = jnp.concatenate([pq_ref[a].T for a in range(a_n)], axis=0)
    o_ref[...] = (_bdot(cl_ref[...], pq[:, :LANES]) - _bdot(sl_ref[...], pq[:, LANES:])) * scale


def _fn_small(pq, *, a_n, n_batch, chan_lo):
    seq = a_n * LANES
    n = np.arange(seq)
    ang = 2 * np.pi * (n[:, None] * n[None, :] % seq) / seq
    const = lambda shp: pl.BlockSpec(shp, lambda b, g: (0,) * len(shp))
    g0 = chan_lo // (2 * LANES)
    return pl.pallas_call(
        functools.partial(_fn_small_kernel, a_n=a_n, scale=1.0 / math.sqrt(seq * LANES)),
        grid=(n_batch, FN_GROUPS),
        in_specs=[pl.BlockSpec((a_n, 2 * LANES, LANES), lambda b, g: (b, g0 + g, 0)),
                  const((seq, seq)), const((seq, seq))],
        out_specs=pl.BlockSpec((None, seq, LANES), lambda b, g: (b, 0, g)),
        out_shape=jax.ShapeDtypeStruct((n_batch, seq, D_BR), F32),
        compiler_params=_params(("arbitrary", "arbitrary")),
        name="fnet_short",
    )(pq, jnp.asarray(np.cos(ang), F32), jnp.asarray(np.sin(ang), F32))


def _log_sigmoid(x):
    return jnp.minimum(x, 0.0) - jnp.log(1.0 + jnp.exp(-jnp.abs(x)))


def _exact_tri_dot(tri, x, tri_on_left):
    h, m, l = _split3(x)
    if tri_on_left:
        d = lambda p: jnp.dot(tri, p, preferred_element_type=F32)
    else:
        d = lambda p: jnp.dot(p, tri, preferred_element_type=F32)
    return d(h) + d(m) + d(l)


def _mlstm_step_t(inputs, c_st, n_st, m_st):
    t = hd = LANES
    n_dir, n_batch = len(inputs), len(inputs[0])
    n_grp = n_dir * n_batch * ML_HEADS
    row = lax.broadcasted_iota(jnp.int32, (t, t), 0)
    col = lax.broadcasted_iota(jnp.int32, (t, t), 1)
    tri = jnp.where(col <= row, 1.0, 0.0).astype(BF16)
    tri_t = jnp.where(col >= row, 1.0, 0.0).astype(BF16)
    qts, kts, ks, vts, ecs, brs, irs = [], [], [], [], [], [], []
    for d in range(n_dir):
        i_off = 2 * ML_HEADS * d
        f_off = i_off + ML_HEADS
        for b in range(n_batch):
            qt_all, kt_all, k_all, vt_all, g = inputs[d][b]
            gt = g.T
            lf_c = _log_sigmoid(g)
            lf_r = lf_c.T
            if d == 1:
                b_c = _exact_tri_dot(tri_t, lf_c, True)
                b_r = _exact_tri_dot(tri, lf_r, False)
            else:
                b_c = _exact_tri_dot(tri, lf_c, True)
                b_r = _exact_tri_dot(tri_t, lf_r, False)
            for h in range(ML_HEADS):
                sl = slice(h * hd, (h + 1) * hd)
                qts.append(qt_all[sl])
                kts.append(kt_all[sl])
                vts.append(vt_all[sl])
                ks.append(k_all[:, sl])
                ecs.append(g[:, i_off + h:i_off + h + 1] - b_c[:, f_off + h:f_off + h + 1])
                brs.append(b_r[f_off + h:f_off + h + 1, :])
                irs.append(gt[i_off + h:i_off + h + 1, :])
    qt = jnp.stack(qts) * (hd ** -0.5)
    kt, vt, k = jnp.stack(kts), jnp.stack(vts), jnp.stack(ks)
    e_col, br, ir = jnp.stack(ecs), jnp.stack(brs), jnp.stack(irs)
    m_prev = m_st[...][:, :, :1]
    c_prev = c_st[...]
    n_prev = n_st[...]

    shp = (n_grp, t, t)
    grp = lax.broadcasted_iota(jnp.int32, shp, 0)
    s3, t3 = lax.broadcasted_iota(jnp.int32, shp, 1), lax.broadcasted_iota(jnp.int32, shp, 2)
    back = grp >= (n_grp // n_dir)
    mask = (back & (s3 >= t3)) | (jnp.logical_not(back) & (s3 <= t3))
    bdot = lambda a, b_, ca, cb: lax.dot_general(a.astype(BF16), b_.astype(BF16), (((ca,), (cb,)), ((0,), (0,))),
                                                 preferred_element_type=F32)
    e_st = jnp.where(mask, e_col, -jnp.inf)
    mm = jnp.maximum(m_prev, jnp.max(e_st, axis=1, keepdims=True))
    m_row = br + mm
    w_intra = jnp.exp(e_st - mm)
    w_inter = jnp.exp(m_prev - mm)
    s_t = bdot(k, qt, 2, 1) * w_intra
    num = bdot(vt, s_t, 2, 1) + w_inter * bdot(c_prev, qt, 2, 1)
    den = jnp.sum(s_t, axis=1, keepdims=True) + w_inter * bdot(n_prev, qt, 2, 1)
    den = jnp.maximum(jnp.abs(den), jnp.exp(-m_row))
    h_all = num / den

    is_back = lax.broadcasted_iota(jnp.int32, (n_grp, 1, 1), 0) >= (n_grp // n_dir)
    b_tot = jnp.where(is_back, br[:, :, :1], br[:, :, t - 1:])
    a_r = b_tot - br + ir
    m_new = jnp.maximum(b_tot + m_prev, jnp.max(a_r, axis=-1, keepdims=True))
    sc = jnp.exp(a_r - m_new)
    decay = jnp.exp(b_tot + m_prev - m_new)
    c_st[...] = decay * c_prev + bdot(vt * sc, kt, 2, 2)
    n_st[...] = decay * n_prev + bdot(sc, k, 2, 1)
    m_st[...] = jnp.broadcast_to(m_new, (n_grp, 1, LANES))

    out = []
    for d in range(n_dir):
        out.append([jnp.concatenate([h_all[(d * n_batch + b) * ML_HEADS + h] for h in range(ML_HEADS)], axis=0)
                    for b in range(n_batch)])
    return out


def _mlstm_kernel_t(*refs, n_batch, ctx_chunks):
    lat_f, lat_b, ctx_f, ctx_b = refs[0:5], refs[5:10], refs[10:15], refs[15:20]
    hf_lat, hb_lat, hf_ctx, hb_ctx, c_st, n_st, m_st = refs[20:]
    j = pl.program_id(0)
    is_ctx = j < ctx_chunks

    @pl.when(j == 0)
    def _():
        c_st[...] = jnp.zeros(c_st.shape, F32)
        n_st[...] = jnp.zeros(n_st.shape, F32)
        m_st[...] = jnp.zeros(m_st.shape, F32)

    def pick(c_refs, l_refs, b):
        ld = lambda r: r[b, 0] if len(r.shape) == 4 else r[b]
        return tuple(jnp.where(is_ctx, ld(c), ld(l)) for c, l in zip(c_refs, l_refs))

    inputs = [[pick(ctx_f, lat_f, b) for b in range(n_batch)], [pick(ctx_b, lat_b, b) for b in range(n_batch)]]
    hf, hb = _mlstm_step_t(inputs, c_st, n_st, m_st)

    @pl.when(is_ctx)
    def _():
        for b in range(n_batch):
            hf_ctx[b, 0] = hf[b]
            hb_ctx[b, 0] = hb[b]

    @pl.when(jnp.logical_not(is_ctx))
    def _():
        for b in range(n_batch):
            hf_lat[b, 0] = hf[b]
            hb_lat[b, 0] = hb[b]


def _mlstm_t(qk_s, k_tm, z_s, z_tm, *, n_batch, v_chan, g_col):
    nlc, ncc = qk_s["lat"].shape[0] // n_batch, qk_s["ctx"].shape[0] // n_batch
    r4 = lambda a: a.reshape(n_batch, a.shape[0] // n_batch, a.shape[1], a.shape[2])
    r3 = lambda a: a.reshape(n_batch, a.shape[0] // n_batch, a.shape[1])
    lf = lambda j: jnp.maximum(j - ncc, 0)
    lb = lambda j: jnp.where(j < ncc, nlc - 1, nlc - 1 - (j - ncc))
    cf = lambda j: jnp.minimum(j, ncc - 1)
    cb = lambda j: jnp.where(j < ncc, ncc - 1 - j, 0)
    slab = lambda ix, cidx: pl.BlockSpec((n_batch, 1, D_BR, LANES), lambda j: (0, ix(j), cidx, 0))
    rows = lambda w, ix, cidx: pl.BlockSpec((n_batch, LANES, w), lambda j: (0, ix(j), cidx))
    specs = lambda ix: [slab(ix, 0), slab(ix, 1), rows(D_BR, ix, 0), slab(ix, v_chan // D_BR), rows(LANES, ix, g_col)]
    args = lambda s: [r4(qk_s[s]), r4(qk_s[s]), r3(k_tm[s]), r4(z_s[s]), r3(z_tm[s])]
    n_grp = 2 * n_batch * ML_HEADS
    sd = lambda n: jax.ShapeDtypeStruct((n_batch, n, D_BR, LANES), F32)
    hf_lat, hb_lat, hf_ctx, hb_ctx = pl.pallas_call(
        functools.partial(_mlstm_kernel_t, n_batch=n_batch, ctx_chunks=ncc),
        grid=(ncc + nlc,),
        in_specs=specs(lf) + specs(lb) + specs(cf) + specs(cb),
        out_specs=[slab(lf, 0), slab(lb, 0), slab(cf, 0), slab(cb, 0)],
        out_shape=[sd(nlc), sd(nlc), sd(ncc), sd(ncc)],
        scratch_shapes=[pltpu.VMEM((n_grp, LANES, LANES), F32), pltpu.VMEM((n_grp, 1, LANES), F32),
                        pltpu.VMEM((n_grp, 1, LANES), F32)],
        compiler_params=_params(("arbitrary",)),
        name="mlstm_bidir",
    )(*(args("lat") + args("lat") + args("ctx") + args("ctx")))
    flat = lambda a: a.reshape(a.shape[0] * a.shape[1], D_BR, LANES)
    return {"lat": (flat(hf_lat), flat(hb_lat)), "ctx": (flat(hf_ctx), flat(hb_ctx))}


def _rms_mod(x, w, shift, scale):
    y = x * lax.rsqrt(jnp.mean(x * x, axis=-1, keepdims=True) + EPS) * w
    return y * (1.0 + scale) + shift


def _route(t, rw, rb):
    logits = _dot_f32ish3(t, rw) + rb
    col = lax.broadcasted_iota(jnp.int32, logits.shape, 1)
    big = jnp.int32(1 << 20)
    ninf = -jnp.inf
    is_g = col < MOE_GROUPS
    gl = jnp.where(is_g, logits, ninf)
    gmax = jnp.max(gl, axis=-1, keepdims=True)
    g_sel = jnp.min(jnp.where(is_g & (gl == gmax), col, big), axis=-1, keepdims=True)
    p_top = 1.0 / jnp.sum(jnp.where(is_g, jnp.exp(gl - gmax), 0.0), axis=-1, keepdims=True)
    lo = MOE_GROUPS + g_sel * MOE_PER_GROUP
    in_grp = (col >= lo) & (col < lo + MOE_PER_GROUP)
    e1v = jnp.where(in_grp, logits, ninf)
    top1 = jnp.max(e1v, axis=-1, keepdims=True)
    idx1 = jnp.min(jnp.where(in_grp & (e1v == top1), col, big), axis=-1, keepdims=True)
    e2v = jnp.where(col == idx1, ninf, e1v)
    top2 = jnp.max(e2v, axis=-1, keepdims=True)
    idx2 = jnp.min(jnp.where(in_grp & (col != idx1) & (e2v == top2), col, big), axis=-1, keepdims=True)
    ex = jnp.exp(top2 - top1)
    s1 = 1.0 / (1.0 + ex)
    return jnp.where(col == idx1, p_top * s1, 0.0) + jnp.where(col == idx2, p_top * (ex * s1), 0.0)


def _merge_kernel(yh_ref, yf_ref, h_ref, o_ref, g0_ref, g1_ref, g2_ref, x_ref, gate_ref,
                  wb_ref, wo_ref, nw_ref, n2_ref, sh_ref, sc_ref, rw_ref, rb_ref, out_ref, xn_ref, comb_ref):
    hd = LANES
    h = h_ref[...]
    parts = []
    for i in range(ML_HEADS):
        hh = h[:, i * hd:(i + 1) * hd]
        parts.append(hh * lax.rsqrt(jnp.mean(hh * hh, axis=-1, keepdims=True) + EPS))
    y_ml = jax.nn.sigmoid(o_ref[...]) * (jnp.concatenate(parts, axis=1) * nw_ref[...])
    acc = g0_ref[...].astype(F32) * _bdot(yh_ref[...], wb_ref[0])
    acc = acc + g1_ref[...].astype(F32) * _bdot(yf_ref[...], wb_ref[1])
    acc = acc + g2_ref[...].astype(F32) * _bdot(y_ml, wb_ref[2])
    x_new = x_ref[...] + gate_ref[...] * _bdot(acc, wo_ref[...])
    out_ref[...] = x_new
    t = _rms_mod(x_new, n2_ref[...], sh_ref[...], sc_ref[...])
    xn_ref[...] = t.astype(BF16)
    comb_ref[...] = _route(t, rw_ref[...], rb_ref[...])


def _merge(yh, yf, h, z_tm, gates, x, mods3, wb, wo, nw, n2w, rw, rb, *, seg, tm, o_col):
    nt, d = x.shape
    tok = lambda w, cidx: pl.BlockSpec((tm, w), lambda i: (i, cidx))
    mod = lambda k: pl.BlockSpec((None, 1, d), lambda i: (seg(i), 0, k))
    return pl.pallas_call(
        _merge_kernel,
        grid=(nt // tm,),
        in_specs=[tok(D_BR, 0), tok(D_BR, 0), tok(D_BR, 0),
                  tok(D_BR, o_col),
                  tok(d, 0), tok(d, 1), tok(d, 2),
                  tok(d, 0),
                  mod(2),
                  pl.BlockSpec((3, D_BR, d), lambda i: (0, 0, 0)),
                  pl.BlockSpec((d, d), lambda i: (0, 0)),
                  pl.BlockSpec((1, D_BR), lambda i: (0, 0)),
                  pl.BlockSpec((1, d), lambda i: (0, 0)),
                  mod(3), mod(4),
                  pl.BlockSpec((d, LANES), lambda i: (0, 0)),
                  pl.BlockSpec((1, LANES), lambda i: (0, 0))],
        out_specs=[tok(d, 0), tok(d, 0), tok(LANES, 0)],
        out_shape=[jax.ShapeDtypeStruct((nt, d), F32), jax.ShapeDtypeStruct((nt, d), BF16),
                   jax.ShapeDtypeStruct((nt, LANES), F32)],
        compiler_params=_params(("arbitrary",)),
        name="merge_branches_router",
    )(yh, yf, h, z_tm, gates, gates, gates, x, mods3, wb, wo, nw.reshape(1, D_BR), n2w.reshape(1, d),
      mods3, mods3, rw, rb)


def _moe_kernel(xn_ref, comb_ref, wg_ref, wu_ref, wd_ref, x_ref, gate_ref, nw_ref, sh_ref, sc_ref, *out_and_scratch,
                final, n_keep):
    acc_ref = out_and_scratch[-1]
    e = pl.program_id(1)

    @pl.when(e == 0)
    def _():
        acc_ref[...] = jnp.zeros(acc_ref.shape, F32)

    xn = xn_ref[...]
    comb = comb_ref[...]
    col = lax.broadcasted_iota(jnp.int32, comb.shape, 1)
    acts = []
    for i in range(MOE_PER_GROUP):
        cw = jnp.sum(jnp.where(col == e * MOE_PER_GROUP + i + MOE_GROUPS, comb, 0.0), axis=-1, keepdims=True)
        hg = jnp.dot(xn, wg_ref[i], preferred_element_type=F32)
        hu = jnp.dot(xn, wu_ref[i], preferred_element_type=F32)
        acts.append(((hg * jax.nn.sigmoid(hg)) * hu * cw).astype(BF16))
    wd = wd_ref[...].reshape(MOE_PER_GROUP * EXPERT_HID, wd_ref.shape[-1])
    acc_ref[...] += jnp.dot(jnp.concatenate(acts, axis=1), wd, preferred_element_type=F32)

    n_steps = MOE_EXPERTS // MOE_PER_GROUP
    if final:
        y_ref, = out_and_scratch[:-1]

        @pl.when((e == n_steps - 1) & (pl.program_id(0) < n_keep))
        def _():
            x_new = x_ref[...] + gate_ref[...] * acc_ref[...]
            y_ref[...] = x_new * lax.rsqrt(jnp.mean(x_new * x_new, axis=-1, keepdims=True) + EPS) * nw_ref[...]
    else:
        o_ref, xn_next_ref = out_and_scratch[:-1]

        @pl.when(e == n_steps - 1)
        def _():
            x_new = x_ref[...] + gate_ref[...] * acc_ref[...]
            o_ref[...] = x_new
            xn_next_ref[...] = _rms_mod(x_new, nw_ref[...], sh_ref[...], sc_ref[...]).astype(BF16)


def _moe(xn, comb, wg, wu, wd, x, mods3, post_w, post_mods3, *, seg, tm, final, n_keep_rows):
    nt, d = x.shape
    n_keep = n_keep_rows // tm
    tok = pl.BlockSpec((tm, d), lambda i, e: (i, 0))
    if final:
        out_specs = [pl.BlockSpec((tm, d), lambda i, e: (jnp.minimum(i, n_keep - 1), 0))]
        out_shape = [jax.ShapeDtypeStruct((n_keep_rows, d), F32)]
    else:
        out_specs = [tok, tok]
        out_shape = [jax.ShapeDtypeStruct((nt, d), F32), jax.ShapeDtypeStruct((nt, d), BF16)]
    mod = lambda k: pl.BlockSpec((None, 1, d), lambda i, e: (seg(i), 0, k))
    return pl.pallas_call(
        functools.partial(_moe_kernel, final=final, n_keep=n_keep),
        grid=(nt // tm, MOE_EXPERTS // MOE_PER_GROUP),
        in_specs=[tok,
                  pl.BlockSpec((tm, LANES), lambda i, e: (i, 0)),
                  pl.BlockSpec((MOE_PER_GROUP, d, EXPERT_HID), lambda i, e: (e, 0, 0)),
                  pl.BlockSpec((MOE_PER_GROUP, d, EXPERT_HID), lambda i, e: (e, 0, 0)),
                  pl.BlockSpec((MOE_PER_GROUP, EXPERT_HID, d), lambda i, e: (e, 0, 0)),
                  tok,
                  mod(5),
                  pl.BlockSpec((1, d), lambda i, e: (0, 0)),
                  mod(0), mod(1)],
        out_specs=out_specs,
        out_shape=out_shape,
        scratch_shapes=[pltpu.VMEM((tm, d), F32)],
        compiler_params=_params(("arbitrary", "arbitrary")),
        name="moe_experts_final" if final else "moe_experts",
    )(xn, comb, wg, wu, wd, x, mods3, post_w.reshape(1, d), post_mods3, post_mods3)


def _slab_to_tm(y_s):
    ns, c, _ = y_s.shape
    return jnp.transpose(y_s, (0, 2, 1)).reshape(ns * LANES, c)


def kernel(x, c, ctx, c_ctx, ada_w, ada_b, norm1_w, norm2_w, w_in, b_in, hy_conv_w, hy_conv_b, hy_f_w1, hy_f_b1, hy_f_w2, hy_f_b2, hy_f_w3, hy_f_freq, hy_decay, hy_skip, ml_conv_w, ml_conv_b, ml_norm_w, w_branch, w_out, moe_rg_w, moe_rg_b, moe_re_w, moe_re_b, moe_w_gate, moe_w_up, moe_w_down, norm_f_w):
    nb, seq, d = x.shape
    lc = ctx.shape[1]
    depth = ada_w.shape[0]
    assert d == D_MODEL and seq % (GRID_W * 2) == 0 and lc % LANES == 0 and nb + 1 <= 8
    rows = seq // GRID_W
    a_lat = seq // LANES
    a_ctx = lc // LANES
    n_lat, n_ctx = nb * seq, nb * lc
    tm = 256
    tm_moe = {"lat": _pick(seq, (1024, 512, 256)), "ctx": _pick(n_ctx, (512, 256))}
    tm_mrg = {"lat": _pick(seq, (512, 256)), "ctx": _pick(n_ctx, (512, 256))}
    assert seq % tm == 0 and n_ctx % tm == 0
    seg_of = lambda s, t: (lambda i: i // (seq // t)) if s == "lat" else (lambda i: nb)
    streams = ("lat", "ctx")
    xs = {"lat": x.reshape(n_lat, d), "ctx": ctx.reshape(n_ctx, d)}
    xn = {}
    cvec = jnp.zeros((8, d), F32).at[:nb].set(c).at[nb].set(c_ctx)
    mods = _mods(cvec, ada_w, ada_b)

    o_fn, o_ml, o_mlg, o_gate = 3 * D_BR, 4 * D_BR, 8 * D_BR, 8 * D_BR + 4 * ML_HEADS
    pad_g = LANES - 4 * ML_HEADS

    for l in range(depth):
        lp = {"hy_f_w1": hy_f_w1[l], "hy_f_b1": hy_f_b1[l], "hy_f_w2": hy_f_w2[l], "hy_f_b2": hy_f_b2[l],
              "hy_f_w3": hy_f_w3[l], "hy_f_freq": hy_f_freq[l], "hy_decay": hy_decay[l]}
        mods3 = mods[l].reshape(8, 1, 6 * d)
        wl, bl = w_in[l], b_in[l]
        w_pq, b_pq = _fn_fold(wl[:, o_fn:o_ml], bl[o_fn:o_ml])
        o_v, o_o = o_ml + 2 * D_BR, o_ml + 3 * D_BR
        w_cm = jnp.concatenate([wl[:, :o_fn], wl[:, o_ml:o_v], w_pq, wl[:, o_v:o_o]], axis=1)
        b_cm = jnp.concatenate([bl[:o_fn], bl[o_ml:o_v], b_pq, bl[o_v:o_o]])
        w_tm = jnp.concatenate([wl[:, o_o:o_mlg], wl[:, o_mlg:o_gate], jnp.zeros((d, pad_g), F32)], axis=1)
        b_tm = jnp.concatenate([bl[o_o:o_mlg], bl[o_mlg:o_gate], jnp.zeros((pad_g,), F32)])
        c_hy, c_qk, c_fn, c_v = 0, 3 * D_BR, 5 * D_BR, 7 * D_BR
        g_col = D_BR // LANES

        last = l + 1 == depth
        live = ("lat",) if last else streams
        if l == 0:
            xn = {s: _norm_mod(xs[s], norm1_w[l], mods3, 0, 1, seg_of(s, tm), tm) for s in streams}
        w_tm_b, w_cm_t = w_tm.astype(BF16), w_cm.T.astype(BF16)
        z_tm = {s: _mm_tm(xn[s], w_tm_b, b_tm, gate=False) for s in streams}
        w_gate_b = wl[:, o_gate:].astype(BF16)
        gates = {s: _mm_tm(xn[s], w_gate_b, bl[o_gate:], gate=True) for s in live}
        z_s = {s: _mm_slab(xn[s], w_cm_t, b_cm) for s in streams}

        grid_kw = {"lat": dict(rows=rows, width=GRID_W), "ctx": dict(rows=1, width=lc)}
        hy_w, hy_b = hy_conv_w[l].reshape(9, 3 * D_BR), hy_conv_b[l]
        ml_w, ml_b = ml_conv_w[l].reshape(9, 2 * D_BR), ml_conv_b[l]
        conv = lambda s, w, b, lo, n, act: _conv(z_s[s], w, b, chan_lo=lo, chan_n=n, silu=act, n_batch=nb,
                                                 slab0=0, **grid_kw[s])
        u = {s: conv(s, hy_w, hy_b, c_hy, 3 * D_BR, False) for s in live}
        qk = {s: conv(s, ml_w, ml_b, c_qk, 2 * D_BR, True) for s in streams}
        k_tm = {s: _slab_to_tm(qk[s][:, D_BR:]).astype(BF16) for s in streams}
        h_s = _mlstm_t(qk, k_tm, z_s, z_tm, n_batch=nb, v_chan=c_v, g_col=g_col)
        h_sum = {s: _slab_to_tm(h_s[s][0] + h_s[s][1]) for s in live}

        yh, yf = {}, {}
        yh["lat"] = _slab_to_tm(_hyena(u["lat"], _hyena_taps(seq, 2 * a_lat, lp), hy_skip[l],
                                       a_in=a_lat, na=2 * a_lat, n_batch=nb)).astype(BF16)
        yk = _fn_seq(z_s["lat"], a_n=a_lat, n_batch=nb, chan_lo=c_fn)
        yf["lat"] = jnp.transpose(yk, (0, 3, 2, 1)).reshape(n_lat, D_BR).astype(BF16)
        if not last:
            yh["ctx"] = _hyena_short(u["ctx"], _hyena_taps(lc, 2 * a_ctx, lp), hy_skip[l],
                                     a_n=a_ctx, n_batch=nb).reshape(n_ctx, D_BR)
            yf["ctx"] = _fn_small(z_s["ctx"], a_n=a_ctx, n_batch=nb, chan_lo=c_fn).reshape(n_ctx, D_BR)

        rw = jnp.concatenate([moe_rg_w[l], moe_re_w[l], jnp.zeros((d, LANES - MOE_GROUPS - MOE_EXPERTS), F32)], axis=1)
        rb = jnp.concatenate([moe_rg_b[l], moe_re_b[l], jnp.zeros((LANES - MOE_GROUPS - MOE_EXPERTS,), F32)]).reshape(1, LANES)
        wb, wo = w_branch[l].astype(BF16), w_out[l].astype(BF16)
        experts = (moe_w_gate[l].astype(BF16), moe_w_up[l].astype(BF16), moe_w_down[l].astype(BF16))
        for s in live:
            xs[s], xn2, comb = _merge(yh[s], yf[s], h_sum[s], z_tm[s], gates[s], xs[s], mods3, wb, wo,
                                      ml_norm_w[l], norm2_w[l], rw, rb, seg=seg_of(s, tm_mrg[s]), tm=tm_mrg[s],
                                      o_col=0)
            moe_kw = dict(seg=seg_of(s, tm_moe[s]), tm=tm_moe[s], n_keep_rows=xs[s].shape[0])
            if last:
                out, = _moe(xn2, comb, *experts, xs[s], mods3, norm_f_w, mods3, final=True, **moe_kw)
            else:
                xs[s], xn[s] = _moe(xn2, comb, *experts, xs[s], mods3, norm1_w[l + 1],
                                    mods[l + 1].reshape(8, 1, 6 * d), final=False, **moe_kw)

    return out.reshape(nb, seq, d)
```

```python
import functools
import math

import numpy as np
import jax
import jax.numpy as jnp
from jax import lax
from jax.experimental import pallas as pl
from jax.experimental.pallas import tpu as pltpu

F32 = jnp.float32
BF16 = jnp.bfloat16

D_MODEL = 1024
D_BR = 512
GRID_W = 64
LANES = 128
CB = 8
HY_ORDER = 2
HY_BANDS = 16
FN_GROUPS = 4
ML_HEADS = 4
MOE_GROUPS = 4
MOE_PER_GROUP = 4
MOE_EXPERTS = 16
EXPERT_HID = 256
EPS = 1e-6
VMEM_LIMIT = 56 * 1024 * 1024


def _params(sem):
    return pltpu.CompilerParams(dimension_semantics=sem, vmem_limit_bytes=VMEM_LIMIT)


def _bdot(a, b):
    return jnp.dot(a.astype(BF16), b.astype(BF16), preferred_element_type=F32)


def _split3(x):
    hi = x.astype(BF16)
    r1 = x - hi.astype(F32)
    mid = r1.astype(BF16)
    lo = (r1 - mid.astype(F32)).astype(BF16)
    return hi, mid, lo


def _dot_f32ish(x, w):
    xh, xm, xl = _split3(x)
    wh, wm, wl = _split3(w)
    d = lambda a, b: jnp.dot(a, b, preferred_element_type=F32)
    return (d(xh, wh) + (d(xh, wm) + d(xm, wh))) + (d(xm, wm) + d(xh, wl) + d(xl, wh))


def _dot_f32ish3(x, w):
    xh, xm, _ = _split3(x)
    wh, wm, _ = _split3(w)
    d = lambda a, b: jnp.dot(a, b, preferred_element_type=F32)
    return d(xh, wh) + (d(xh, wm) + d(xm, wh))


def _swap_halves(x):
    return jnp.concatenate([x[..., LANES:], x[..., :LANES]], axis=-1)


def _mods_kernel(c_ref, w_ref, b_ref, o_ref):
    c = c_ref[...]
    s = c * jax.nn.sigmoid(c)
    o_ref[...] = _dot_f32ish(s, w_ref[...]) + b_ref[...]


def _mods(cvec, ada_w, ada_b):
    depth, d, n6 = ada_w.shape
    tn = 1536
    return pl.pallas_call(
        _mods_kernel,
        grid=(depth, n6 // tn),
        in_specs=[pl.BlockSpec((8, d), lambda l, j: (0, 0)),
                  pl.BlockSpec((None, d, tn), lambda l, j: (l, 0, j)),
                  pl.BlockSpec((None, 1, tn), lambda l, j: (l, 0, j))],
        out_specs=pl.BlockSpec((None, 8, tn), lambda l, j: (l, 0, j)),
        out_shape=jax.ShapeDtypeStruct((depth, 8, n6), F32),
        compiler_params=_params(("arbitrary", "arbitrary")),
        name="adaln_mods",
    )(cvec, ada_w, ada_b.reshape(depth, 1, n6))


def _norm_mod_kernel(x_ref, w_ref, sh_ref, sc_ref, o_ref):
    x = x_ref[...]
    y = x * lax.rsqrt(jnp.mean(x * x, axis=-1, keepdims=True) + EPS) * w_ref[...]
    o_ref[...] = (y * (1.0 + sc_ref[...]) + sh_ref[...]).astype(o_ref.dtype)


def _norm_mod(x, w, mods3, col_shift, col_scale, seg, tm):
    nt, d = x.shape
    return pl.pallas_call(
        _norm_mod_kernel,
        grid=(nt // tm,),
        in_specs=[pl.BlockSpec((tm, d), lambda i: (i, 0)),
                  pl.BlockSpec((1, d), lambda i: (0, 0)),
                  pl.BlockSpec((None, 1, d), lambda i: (seg(i), 0, col_shift)),
                  pl.BlockSpec((None, 1, d), lambda i: (seg(i), 0, col_scale))],
        out_specs=pl.BlockSpec((tm, d), lambda i: (i, 0)),
        out_shape=jax.ShapeDtypeStruct((nt, d), BF16),
        compiler_params=_params(("arbitrary",)),
        name="norm_mod",
    )(x, w.reshape(1, d), mods3, mods3)


def _mm_tm_kernel(x_ref, w_ref, b_ref, o_ref, *, gate):
    y = jnp.dot(x_ref[...], w_ref[...], preferred_element_type=F32) + b_ref[...]
    o_ref[...] = (jax.nn.sigmoid(y) if gate else y).astype(o_ref.dtype)


def _pick(n, cands):
    for c in cands:
        if n % c == 0:
            return c
    raise ValueError(f"no tile for {n} in {cands}")


def _mm_tm(xn, w, b, *, gate):
    nt, k = xn.shape
    n = w.shape[1]
    tm = _pick(nt, (1056, 1024, 768, 512, 256))
    tn = _pick(n, (1536, 1408, 1152, 1024, 640, 512, 384, 256, 128))
    return pl.pallas_call(
        functools.partial(_mm_tm_kernel, gate=gate),
        grid=(n // tn, nt // tm),
        in_specs=[pl.BlockSpec((tm, k), lambda j, i: (i, 0)),
                  pl.BlockSpec((k, tn), lambda j, i: (0, j)),
                  pl.BlockSpec((1, tn), lambda j, i: (0, j))],
        out_specs=pl.BlockSpec((tm, tn), lambda j, i: (i, j)),
        out_shape=jax.ShapeDtypeStruct((nt, n), BF16 if gate else F32),
        compiler_params=_params(("arbitrary", "arbitrary")),
        name="inproj_gates" if gate else "inproj_token_major",
    )(xn, w, b.reshape(1, n))


def _mm_slab_kernel(w_ref, x_ref, b_ref, o_ref, *, slabs):
    w = w_ref[...]
    b = b_ref[...]
    step = 2 if slabs % 2 == 0 else 1
    for s in range(0, slabs, step):
        xs = x_ref[s * LANES:(s + step) * LANES, :]
        y = lax.dot_general(w, xs, (((1,), (1,)), ((), ())), preferred_element_type=F32) + b
        for i in range(step):
            o_ref[s + i] = y[:, i * LANES:(i + 1) * LANES]


def _mm_slab(xn, wt, b):
    nt, k = xn.shape
    c = wt.shape[0]
    ns = nt // LANES
    ts = _pick(ns, (12, 11, 8, 6, 4, 3, 2, 1))
    tc = _pick(c, (1024, 896, 512, 256, 128))
    return pl.pallas_call(
        functools.partial(_mm_slab_kernel, slabs=ts),
        grid=(c // tc, ns // ts),
        in_specs=[pl.BlockSpec((tc, k), lambda j, i: (j, 0)),
                  pl.BlockSpec((ts * LANES, k), lambda j, i: (i, 0)),
                  pl.BlockSpec((tc, 1), lambda j, i: (j, 0))],
        out_specs=pl.BlockSpec((ts, tc, LANES), lambda j, i: (i, j, 0)),
        out_shape=jax.ShapeDtypeStruct((ns, c, LANES), F32),
        compiler_params=_params(("arbitrary", "arbitrary")),
        name="inproj_slab",
    )(wt, xn, b.reshape(c, 1))


def _conv_taps(rows, width):
    taps = []
    for dr in (-1, 0, 1):
        if rows == 1 and dr != 0:
            continue
        for dw in (-1, 0, 1):
            taps.append((dr, dw))
    return taps


def _silu(x):
    return x * jax.nn.sigmoid(x)


def _conv_kernel(x_ref, w_ref, b_ref, o_ref, *, taps, width, n_slabs, silu):
    ct = x_ref.shape[1]
    lane = lax.broadcasted_iota(jnp.int32, (ct, LANES), 1)
    bias = jnp.zeros((ct, LANES), F32) + b_ref[...]
    planes = []
    for t, (dr, dw) in enumerate(taps):
        w = w_ref[t]
        if width < LANES and dw != 0:
            col = lane % width + dw
            w = jnp.where((col >= 0) & (col < width), w, 0.0)
        planes.append((dr * width + dw, w))

    def body(a, carry):
        x0 = x_ref[a]
        xm = jnp.where(a > 0, x_ref[jnp.maximum(a - 1, 0)], 0.0)
        xp = jnp.where(a < n_slabs - 1, x_ref[jnp.minimum(a + 1, n_slabs - 1)], 0.0)
        acc = bias
        for delta, w in planes:
            if delta == 0:
                src = x0
            elif delta > 0:
                src = pltpu.roll(jnp.where(lane >= delta, x0, xp), LANES - delta, 1)
            else:
                src = pltpu.roll(jnp.where(lane < LANES + delta, x0, xm), -delta, 1)
            acc = acc + src * w
        o_ref[a] = _silu(acc) if silu else acc
        return carry

    lax.fori_loop(0, n_slabs, body, 0, unroll=2 if n_slabs % 2 == 0 else 1)


def _conv_grid_kernel(x_ref, w_ref, b_ref, o_ref, ym_ref, yp_ref, *, width, n_slabs, silu):
    ct = x_ref.shape[1]
    lane = lax.broadcasted_iota(jnp.int32, (ct, LANES), 1)
    col = lane % width
    bias = jnp.zeros((ct, LANES), F32) + b_ref[...]

    def row_sums(a, carry):
        x0 = x_ref[a]
        xl = jnp.where(col >= 1, pltpu.roll(x0, 1, 1), 0.0)
        xr = jnp.where(col < width - 1, pltpu.roll(x0, LANES - 1, 1), 0.0)
        ym_ref[a] = w_ref[0] * xl + w_ref[1] * x0 + w_ref[2] * xr
        o_ref[a] = bias + w_ref[3] * xl + w_ref[4] * x0 + w_ref[5] * xr
        yp_ref[a] = w_ref[6] * xl + w_ref[7] * x0 + w_ref[8] * xr
        return carry

    def combine(a, carry):
        up = jnp.where(a > 0, ym_ref[jnp.maximum(a - 1, 0)], 0.0)
        dn = jnp.where(a < n_slabs - 1, yp_ref[jnp.minimum(a + 1, n_slabs - 1)], 0.0)
        from_up = jnp.where(lane < LANES - width, ym_ref[a], up)
        from_dn = jnp.where(lane >= width, yp_ref[a], dn)
        if 2 * width == LANES:
            y = o_ref[a] + pltpu.roll(from_up + from_dn, width, 1)
        else:
            y = o_ref[a] + pltpu.roll(from_up, width, 1) + pltpu.roll(from_dn, LANES - width, 1)
        o_ref[a] = _silu(y) if silu else y
        return carry

    unroll = max(u for u in (16, 8, 4, 2, 1) if n_slabs % u == 0)
    lax.fori_loop(0, n_slabs, row_sums, 0, unroll=unroll)
    lax.fori_loop(0, n_slabs, combine, 0, unroll=unroll)


def _conv(z_s, w9, bias, *, rows, width, n_batch, slab0, chan_lo, chan_n, silu):
    seq = rows * width
    a_n = seq // LANES
    taps = tuple(_conv_taps(rows, width))
    assert all(abs(dr * width + dw) < LANES for dr, dw in taps)
    assert LANES % width == 0 or (rows == 1 and width % LANES == 0)
    tap_ids = [(dr + 1) * 3 + (dw + 1) for dr, dw in taps]
    w_t = jnp.broadcast_to(w9[jnp.array(tap_ids)][:, :, None], (len(taps), chan_n, LANES))
    ct = 64 if rows > 1 else 256
    assert chan_lo % ct == 0 and chan_n % ct == 0 and slab0 % a_n == 0
    nt_ = len(taps)
    if rows > 1:
        assert LANES % width == 0 and nt_ == 9
        body = functools.partial(_conv_grid_kernel, width=width, n_slabs=a_n, silu=silu)
        scratch = [pltpu.VMEM((a_n, ct, LANES), F32), pltpu.VMEM((a_n, ct, LANES), F32)]
    else:
        body = functools.partial(_conv_kernel, taps=taps, width=width, n_slabs=a_n, silu=silu)
        scratch = []
    return pl.pallas_call(
        body,
        scratch_shapes=scratch,
        grid=(n_batch, chan_n // ct),
        in_specs=[pl.BlockSpec((a_n, ct, LANES), lambda b, j: (slab0 // a_n + b, chan_lo // ct + j, 0)),
                  pl.BlockSpec((nt_, ct, LANES), lambda b, j: (0, j, 0)),
                  pl.BlockSpec((ct, 1), lambda b, j: (j, 0))],
        out_specs=pl.BlockSpec((a_n, ct, LANES), lambda b, j: (b, j, 0)),
        out_shape=jax.ShapeDtypeStruct((n_batch * a_n, chan_n, LANES), F32),
        compiler_params=_params(("arbitrary", "arbitrary")),
        name=f"dwconv_{rows}x{width}",
    )(z_s, w_t, bias.reshape(chan_n, 1))


def _dft_consts(a_in, na):
    n = na * LANES
    k = np.arange(na)[:, None]
    a = np.arange(a_in)[None, :]
    ang = 2 * np.pi * (k * a % na) / na
    fa = np.concatenate([np.cos(ang), -np.sin(ang)], axis=0)
    r = np.arange(LANES)
    ang_t = 2 * np.pi * (np.arange(na)[:, None] * r[None, :] % n) / n
    tr, ti = np.cos(ang_t), -np.sin(ang_t)
    ta = np.concatenate([tr, tr], axis=1)
    tb = np.concatenate([-ti, ti], axis=1)
    ang2 = 2 * np.pi * (r[:, None] * r[None, :] % LANES) / LANES
    c2, s2 = np.cos(ang2), np.sin(ang2)
    g2 = np.block([[c2, -s2], [s2, c2]])
    g2i = np.block([[c2, s2], [-s2, c2]])
    ang_i = 2 * np.pi * (np.arange(a_in)[:, None] * np.arange(na)[None, :] % na) / na
    ci, si = np.cos(ang_i) / n, -np.sin(ang_i) / n
    f = lambda v, dt: jnp.asarray(v, dtype=dt)
    return dict(fa=f(fa, F32), ta=f(ta, F32), tb=f(tb, F32), g2=f(g2, F32), g2i=f(g2i, F32),
                ci=f(ci, F32), si=f(si, F32))


def _fwd_slab_stage(m, fa, ta, tb, na):
    pp = jnp.dot(fa, m.astype(BF16), preferred_element_type=F32)
    p = jnp.concatenate([pp[:na], pp[na:]], axis=1)
    return p * ta + _swap_halves(p) * tb


def _cmul(x, kf):
    kr, ki = kf[..., :LANES], kf[..., LANES:]
    ka = jnp.concatenate([kr, kr], axis=-1)
    kb = jnp.concatenate([-ki, ki], axis=-1)
    return x * ka + _swap_halves(x) * kb


def _chan_load(ref, c):
    n, cb, _ = ref.shape
    return ref.reshape(n * cb, LANES)[pl.ds(c, n, stride=cb), :]


def _chan_store(ref, c, val):
    ref[:, c, :] = val


def _dot_f32ish_k(w, h):
    wh, wm, _ = _split3(w)
    hh, hm, _ = _split3(h)
    lhs = jnp.concatenate([wh, wh, wm], axis=1)
    rhs = jnp.concatenate([hh, hm, hh], axis=0)
    return jnp.dot(lhs, rhs, preferred_element_type=F32)


def _taps_kernel(bands_ref, w1t_ref, w1c_ref, w1s_ref, b1_ref, w2_ref, b2_ref, fq_ref, w3_ref, dec_ref, o_ref,
                 *, seq, a_seq, na, spb):
    step = pl.program_id(0)
    width = spb * LANES
    n_total = na * LANES
    is_f = step * spb < a_seq
    n = step * width + lax.broadcasted_iota(jnp.int32, (1, width), 1)
    pos = jnp.where(is_f, n, n_total - n)
    lo, hi = jnp.where(is_f, -1, n_total - seq), jnp.where(is_f, seq, n_total)
    t = pos.astype(F32) / seq
    fq = fq_ref[...]
    ang = ((2 * math.pi) * t) * bands_ref[...]
    pre = (w1t_ref[...] * t + _dot_f32ish(w1c_ref[...], jnp.cos(ang))
           + _dot_f32ish(w1s_ref[...], jnp.sin(ang)))
    h = jnp.sin(fq * (pre + b1_ref[...]))
    h = jnp.sin(fq * (_dot_f32ish(w2_ref[...], h) + b2_ref[...]))
    d = jnp.where(is_f, 0, 1)
    dec = jnp.abs(dec_ref[d])
    k = _dot_f32ish_k(w3_ref[d], h)
    for i in range(spb):
        sl = slice(i * LANES, (i + 1) * LANES)
        live = (n[:, sl] > lo) & (n[:, sl] < hi)
        o_ref[i] = jnp.where(live, k[:, sl] * jnp.exp(-t[:, sl] * dec), 0.0)


def _hyena_taps(seq, na, lp):
    nc = HY_ORDER * D_BR
    hid = lp["hy_f_w2"].shape[0]
    w1 = lp["hy_f_w1"]
    col = lambda v: v.reshape(-1, 1)
    bands = col(jnp.linspace(1e-4, HY_BANDS - 1, HY_BANDS, dtype=F32))
    w3 = jnp.transpose(lp["hy_f_w3"].T.reshape(HY_ORDER, 2, D_BR, hid), (1, 0, 2, 3)).reshape(2, nc, hid)
    dec = jnp.broadcast_to(jnp.transpose(lp["hy_decay"], (1, 0, 2)).reshape(2, nc, 1), (2, nc, LANES))
    a_seq = seq // LANES
    spb = min(8, a_seq)
    assert na == 2 * a_seq and a_seq % spb == 0
    args = (bands, col(w1[0]), w1[1:1 + HY_BANDS].T, w1[1 + HY_BANDS:].T, col(lp["hy_f_b1"]), lp["hy_f_w2"].T,
            col(lp["hy_f_b2"]), col(lp["hy_f_freq"]), w3, dec)
    full = lambda v: pl.BlockSpec(v.shape, lambda s: (0,) * v.ndim)
    return pl.pallas_call(
        functools.partial(_taps_kernel, seq=seq, a_seq=a_seq, na=na, spb=spb),
        grid=(na // spb,),
        in_specs=[full(v) for v in args],
        out_specs=pl.BlockSpec((spb, nc, LANES), lambda s: (s, 0, 0)),
        out_shape=jax.ShapeDtypeStruct((na, nc, LANES), F32),
        compiler_params=_params(("arbitrary",)),
        name=f"hyena_filter_taps_{na}",
    )(*args)


def _hyena_kernel(v_ref, x1_ref, x2_ref, k0_ref, k1_ref, skip_ref, fa_ref, faf_ref, ta_ref, tb_ref, g2_ref, g2i_ref,
                  ci_ref, si_ref, o_ref, p_buf, z_buf, kf_buf, *, a_in, na):
    fa, ta, tb = fa_ref[...].astype(BF16), ta_ref[...], tb_ref[...]
    ci, si = ci_ref[...].astype(BF16), si_ref[...].astype(BF16)

    @pl.when(pl.program_id(1) == 0)
    def _():
        faf = faf_ref[...].astype(BF16)
        for order, k_ref in enumerate((k0_ref, k1_ref)):
            scales = []
            for c in range(CB):
                m = _chan_load(k_ref, c)
                ss = jnp.sum(jnp.sum(m * m, axis=1, keepdims=True), axis=0, keepdims=True)
                scales.append(lax.rsqrt(ss + EPS))
                p_buf[c] = _fwd_slab_stage(m, faf, ta, tb, na)
            x = _bdot(p_buf[...].reshape(CB * na, 2 * LANES), g2_ref[...]).reshape(CB, na, 2 * LANES)
            for c in range(CB):
                kf_buf[order, c] = x[c] * scales[c]

    def spectral(order):
        x = _bdot(p_buf[...].reshape(CB * na, 2 * LANES), g2_ref[...])
        y = _cmul(x, kf_buf[order].reshape(CB * na, 2 * LANES))
        bm = _bdot(y, g2i_ref[...]).reshape(CB, na, 2 * LANES)
        p_buf[...] = bm * ta - _swap_halves(bm) * tb

    def conv_out(c):
        bb = p_buf[c]
        return (jnp.dot(ci, bb[:, :LANES].astype(BF16), preferred_element_type=F32)
                + jnp.dot(si, bb[:, LANES:].astype(BF16), preferred_element_type=F32))

    for c in range(CB):
        p_buf[c] = _fwd_slab_stage(_chan_load(v_ref, c), fa, ta, tb, na)
    spectral(0)
    for c in range(CB):
        z_buf[c] = _chan_load(x1_ref, c) * (conv_out(c) + _chan_load(v_ref, c) * skip_ref[0, c])
    for c in range(CB):
        p_buf[c] = _fwd_slab_stage(z_buf[c], fa, ta, tb, na)
    spectral(1)
    for c in range(CB):
        _chan_store(o_ref, c, _chan_load(x2_ref, c) * (conv_out(c) + z_buf[c] * skip_ref[1, c]))


def _hyena(u_s, taps_s, skip, *, a_in, na, n_batch):
    cs = _dft_consts(a_in, na)
    faf = _dft_consts(na, na)["fa"]
    nblk = D_BR // CB
    const = lambda shp: pl.BlockSpec(shp, lambda j, b: (0,) * len(shp))
    skip_b = jnp.broadcast_to(skip[:, :, None, None], (HY_ORDER, D_BR, 1, LANES))
    return pl.pallas_call(
        functools.partial(_hyena_kernel, a_in=a_in, na=na),
        grid=(nblk, n_batch),
        in_specs=[pl.BlockSpec((a_in, CB, LANES), lambda j, b: (b, j, 0)),
                  pl.BlockSpec((a_in, CB, LANES), lambda j, b: (b, nblk + j, 0)),
                  pl.BlockSpec((a_in, CB, LANES), lambda j, b: (b, 2 * nblk + j, 0)),
                  pl.BlockSpec((na, CB, LANES), lambda j, b: (0, j, 0)),
                  pl.BlockSpec((na, CB, LANES), lambda j, b: (0, nblk + j, 0)),
                  pl.BlockSpec((HY_ORDER, CB, 1, LANES), lambda j, b: (0, j, 0, 0)),
                  const((2 * na, a_in)), const((2 * na, na)), const((na, 2 * LANES)), const((na, 2 * LANES)),
                  const((2 * LANES, 2 * LANES)), const((2 * LANES, 2 * LANES)),
                  const((a_in, na)), const((a_in, na))],
        out_specs=pl.BlockSpec((a_in, CB, LANES), lambda j, b: (b, j, 0)),
        out_shape=jax.ShapeDtypeStruct((n_batch * a_in, D_BR, LANES), F32),
        scratch_shapes=[pltpu.VMEM((CB, na, 2 * LANES), F32), pltpu.VMEM((CB, a_in, LANES), F32),
                        pltpu.VMEM((HY_ORDER, CB, na, 2 * LANES), F32)],
        compiler_params=_params(("arbitrary", "arbitrary")),
        name=f"hyena_longconv_{a_in}",
    )(u_s, u_s, u_s, taps_s, taps_s, skip_b, cs["fa"], faf, cs["ta"], cs["tb"], cs["g2"], cs["g2i"],
      cs["ci"], cs["si"])


def _slabs_to_rows(ref, n):
    return jnp.concatenate([ref[a].T for a in range(n)], axis=0)


def _hyena_short_kernel(v_ref, x1_ref, x2_ref, k0_ref, k1_ref, skip_ref, f_ref, g_ref, o_ref, *, a_n, a_k):
    seq, nf = a_n * LANES, a_k * LANES
    ff = f_ref[...].astype(BF16)
    gi = g_ref[...].astype(BF16)
    v, x1, x2 = _slabs_to_rows(v_ref, a_n), _slabs_to_rows(x1_ref, a_n), _slabs_to_rows(x2_ref, a_n)

    def longconv(u, k_ref):
        k = _slabs_to_rows(k_ref, a_k)
        s = lax.rsqrt(jnp.sum(k * k, axis=0, keepdims=True) + EPS)
        kf = jnp.dot(ff, k.astype(BF16), preferred_element_type=F32) * s
        x = jnp.dot(ff[:, :seq], u.astype(BF16), preferred_element_type=F32)
        xr, xi, kr, ki = x[:nf], x[nf:], kf[:nf], kf[nf:]
        y = jnp.concatenate([xr * kr - xi * ki, xr * ki + xi * kr], axis=0)
        return jnp.dot(gi, y.astype(BF16), preferred_element_type=F32)

    z = x1 * (longconv(v, k0_ref) + v * skip_ref[0])
    o_ref[...] = x2 * (longconv(z, k1_ref) + z * skip_ref[1])


def _hyena_short(u_s, taps_s, skip, *, a_n, n_batch):
    a_k = 2 * a_n
    seq, nf = a_n * LANES, a_k * LANES
    k = np.arange(nf)
    ang = 2 * np.pi * (k[:, None] * k[None, :] % nf) / nf
    f = np.concatenate([np.cos(ang), -np.sin(ang)], axis=0)
    g = np.concatenate([np.cos(ang[:seq]), -np.sin(ang[:seq])], axis=1) / nf
    nblk = D_BR // LANES
    const = lambda shp: pl.BlockSpec(shp, lambda b, j: (0,) * len(shp))
    return pl.pallas_call(
        functools.partial(_hyena_short_kernel, a_n=a_n, a_k=a_k),
        grid=(n_batch, nblk),
        in_specs=[pl.BlockSpec((a_n, LANES, LANES), lambda b, j: (b, j, 0)),
                  pl.BlockSpec((a_n, LANES, LANES), lambda b, j: (b, nblk + j, 0)),
                  pl.BlockSpec((a_n, LANES, LANES), lambda b, j: (b, 2 * nblk + j, 0)),
                  pl.BlockSpec((a_k, LANES, LANES), lambda b, j: (0, j, 0)),
                  pl.BlockSpec((a_k, LANES, LANES), lambda b, j: (0, nblk + j, 0)),
                  pl.BlockSpec((HY_ORDER, 1, LANES), lambda b, j: (0, 0, j)),
                  const((2 * nf, nf)), const((seq, 2 * nf))],
        out_specs=pl.BlockSpec((None, seq, LANES), lambda b, j: (b, 0, j)),
        out_shape=jax.ShapeDtypeStruct((n_batch, seq, D_BR), F32),
        compiler_params=_params(("arbitrary", "arbitrary")),
        name="hyena_short",
    )(u_s, u_s, u_s, taps_s, taps_s, skip.reshape(HY_ORDER, 1, D_BR), jnp.asarray(f, F32), jnp.asarray(g, F32))


def _chan_dft_mats():
    r = np.arange(LANES)
    ang = 2 * np.pi * (r[:, None] * r[None, :] % LANES) / LANES
    return np.cos(ang), np.sin(ang)


def _fn_fold_kernel(w_ref, cs_ref, o_ref):
    o_ref[...] = _dot_f32ish(w_ref[...], cs_ref[...])


def _fn_fold(w_fn, b_fn):
    d = w_fn.shape[0]
    c, s = _chan_dft_mats()
    cs = jnp.asarray(np.concatenate([c, s], axis=1), dtype=F32)
    rows = d + 8
    w_aug = jnp.concatenate([w_fn, b_fn[None, :], jnp.zeros((7, D_BR), F32)], axis=0)
    out = pl.pallas_call(
        _fn_fold_kernel,
        grid=(FN_GROUPS,),
        in_specs=[pl.BlockSpec((rows, LANES), lambda g: (0, g)),
                  pl.BlockSpec((LANES, 2 * LANES), lambda g: (0, 0))],
        out_specs=pl.BlockSpec((rows, 2 * LANES), lambda g: (0, g)),
        out_shape=jax.ShapeDtypeStruct((rows, 2 * D_BR), F32),
        compiler_params=_params(("arbitrary",)),
        name="fnet_fold_channel_dft",
    )(w_aug, cs)
    return out[:d], out[d]


def _fn_seq_kernel(p_ref, q_ref, fa_ref, tr_ref, ti_ref, g_ref, o_ref, a_buf, *, a_n, n_batch, scale):
    fa, tr, ti = fa_ref[...].astype(BF16), tr_ref[...], ti_ref[...]
    side = lambda m: jnp.concatenate([m[b * a_n:(b + 1) * a_n] for b in range(n_batch)], axis=1).astype(BF16)
    for c in range(CB):
        r1 = jnp.dot(fa, side(_chan_load(p_ref, c)), preferred_element_type=F32)
        r2 = jnp.dot(fa, side(_chan_load(q_ref, c)), preferred_element_type=F32)
        ar_all = r1[:a_n] - r2[a_n:]
        ai_all = -(r2[:a_n] + r1[a_n:])
        for b in range(n_batch):
            ar, ai = ar_all[:, b * LANES:(b + 1) * LANES], ai_all[:, b * LANES:(b + 1) * LANES]
            a_buf[b, c] = jnp.concatenate([ar * tr - ai * ti, ar * ti + ai * tr], axis=1)
    y = _bdot(a_buf[...].reshape(n_batch * CB * a_n, 2 * LANES), g_ref[...]) * scale
    o_ref[...] = y.reshape(n_batch, CB, a_n, LANES)


def _fn_seq(pq, *, a_n, n_batch, chan_lo):
    seq = a_n * LANES
    k = np.arange(a_n)
    ang = 2 * np.pi * (k[:, None] * k[None, :] % a_n) / a_n
    fa = np.concatenate([np.cos(ang), np.sin(ang)], axis=0)
    r = np.arange(LANES)
    ang_t = 2 * np.pi * (k[:, None] * r[None, :] % seq) / seq
    c2, s2 = _chan_dft_mats()
    g = np.concatenate([c2, s2], axis=0)
    nblk = LANES // CB
    const = lambda shp: pl.BlockSpec(shp, lambda j: (0,) * len(shp))

    def chan_blk(j, off):
        return chan_lo // CB + (j // nblk) * (2 * nblk) + off * nblk + j % nblk

    return pl.pallas_call(
        functools.partial(_fn_seq_kernel, a_n=a_n, n_batch=n_batch, scale=1.0 / math.sqrt(seq * LANES)),
        grid=(D_BR // CB,),
        in_specs=[pl.BlockSpec((n_batch * a_n, CB, LANES), lambda j: (0, chan_blk(j, 0), 0)),
                  pl.BlockSpec((n_batch * a_n, CB, LANES), lambda j: (0, chan_blk(j, 1), 0)),
                  const((2 * a_n, a_n)), const((a_n, LANES)), const((a_n, LANES)), const((2 * LANES, LANES))],
        out_specs=pl.BlockSpec((n_batch, CB, a_n, LANES), lambda j: (0, j, 0, 0)),
        out_shape=jax.ShapeDtypeStruct((n_batch, D_BR, a_n, LANES), F32),
        scratch_shapes=[pltpu.VMEM((n_batch, CB, a_n, 2 * LANES), F32)],
        compiler_params=_params(("arbitrary",)),
        name="fnet_sequence_dft",
    )(pq, pq, jnp.asarray(fa, F32), jnp.asarray(np.cos(ang_t), F32), jnp.asarray(-np.sin(ang_t), F32),
      jnp.asarray(g, F32))


def _fn_small_kernel(pq_ref, cl_ref, sl_ref, o_ref, *, a_n, scale):
    pq = jnp.concatenate([pq_ref[a].T for a in range(a_n)], axis=0)
    o_ref[...] = (_bdot(cl_ref[...], pq[:, :LANES]) - _bdot(sl_ref[...], pq[:, LANES:])) * scale


def _fn_small(pq, *, a_n, n_batch, chan_lo):
    seq = a_n * LANES
    n = np.arange(seq)
    ang = 2 * np.pi * (n[:, None] * n[None, :] % seq) / seq
    const = lambda shp: pl.BlockSpec(shp, lambda b, g: (0,) * len(shp))
    g0 = chan_lo // (2 * LANES)
    return pl.pallas_call(
        functools.partial(_fn_small_kernel, a_n=a_n, scale=1.0 / math.sqrt(seq * LANES)),
        grid=(n_batch, FN_GROUPS),
        in_specs=[pl.BlockSpec((a_n, 2 * LANES, LANES), lambda b, g: (b, g0 + g, 0)),
                  const((seq, seq)), const((seq, seq))],
        out_specs=pl.BlockSpec((None, seq, LANES), lambda b, g: (b, 0, g)),
        out_shape=jax.ShapeDtypeStruct((n_batch, seq, D_BR), F32),
        compiler_params=_params(("arbitrary", "arbitrary")),
        name="fnet_short",
    )(pq, jnp.asarray(np.cos(ang), F32), jnp.asarray(np.sin(ang), F32))


def _log_sigmoid(x):
    return jnp.minimum(x, 0.0) - jnp.log(1.0 + jnp.exp(-jnp.abs(x)))


def _exact_tri_dot(tri, x, tri_on_left):
    h, m, l = _split3(x)
    if tri_on_left:
        d = lambda p: jnp.dot(tri, p, preferred_element_type=F32)
    else:
        d = lambda p: jnp.dot(p, tri, preferred_element_type=F32)
    return d(h) + d(m) + d(l)


def _mlstm_step_t(inputs, c_st, n_st, m_st):
    t = hd = LANES
    n_dir, n_batch = len(inputs), len(inputs[0])
    n_grp = n_dir * n_batch * ML_HEADS
    row = lax.broadcasted_iota(jnp.int32, (t, t), 0)
    col = lax.broadcasted_iota(jnp.int32, (t, t), 1)
    tri = jnp.where(col <= row, 1.0, 0.0).astype(BF16)
    tri_t = jnp.where(col >= row, 1.0, 0.0).astype(BF16)
    qts, kts, ks, vts, ecs, brs, irs = [], [], [], [], [], [], []
    for d in range(n_dir):
        i_off = 2 * ML_HEADS * d
        f_off = i_off + ML_HEADS
        for b in range(n_batch):
            qt_all, kt_all, k_all, vt_all, g = inputs[d][b]
            gt = g.T
            lf_c = _log_sigmoid(g)
            lf_r = lf_c.T
            if d == 1:
                b_c = _exact_tri_dot(tri_t, lf_c, True)
                b_r = _exact_tri_dot(tri, lf_r, False)
            else:
                b_c = _exact_tri_dot(tri, lf_c, True)
                b_r = _exact_tri_dot(tri_t, lf_r, False)
            for h in range(ML_HEADS):
                sl = slice(h * hd, (h + 1) * hd)
                qts.append(qt_all[sl])
                kts.append(kt_all[sl])
                vts.append(vt_all[sl])
                ks.append(k_all[:, sl])
                ecs.append(g[:, i_off + h:i_off + h + 1] - b_c[:, f_off + h:f_off + h + 1])
                brs.append(b_r[f_off + h:f_off + h + 1, :])
                irs.append(gt[i_off + h:i_off + h + 1, :])
    qt = jnp.stack(qts) * (hd ** -0.5)
    kt, vt, k = jnp.stack(kts), jnp.stack(vts), jnp.stack(ks)
    e_col, br, ir = jnp.stack(ecs), jnp.stack(brs), jnp.stack(irs)
    m_prev = m_st[...][:, :, :1]
    c_prev = c_st[...]
    n_prev = n_st[...]

    shp = (n_grp, t, t)
    grp = lax.broadcasted_iota(jnp.int32, shp, 0)
    s3, t3 = lax.broadcasted_iota(jnp.int32, shp, 1), lax.broadcasted_iota(jnp.int32, shp, 2)
    back = grp >= (n_grp // n_dir)
    mask = (back & (s3 >= t3)) | (jnp.logical_not(back) & (s3 <= t3))
    bdot = lambda a, b_, ca, cb: lax.dot_general(a.astype(BF16), b_.astype(BF16), (((ca,), (cb,)), ((0,), (0,))),
                                                 preferred_element_type=F32)
    e_st = jnp.where(mask, e_col, -jnp.inf)
    mm = jnp.maximum(m_prev, jnp.max(e_st, axis=1, keepdims=True))
    m_row = br + mm
    w_intra = jnp.exp(e_st - mm)
    w_inter = jnp.exp(m_prev - mm)
    s_t = bdot(k, qt, 2, 1) * w_intra
    num = bdot(vt, s_t, 2, 1) + w_inter * bdot(c_prev, qt, 2, 1)
    den = jnp.sum(s_t, axis=1, keepdims=True) + w_inter * bdot(n_prev, qt, 2, 1)
    den = jnp.maximum(jnp.abs(den), jnp.exp(-m_row))
    h_all = num / den

    is_back = lax.broadcasted_iota(jnp.int32, (n_grp, 1, 1), 0) >= (n_grp // n_dir)
    b_tot = jnp.where(is_back, br[:, :, :1], br[:, :, t - 1:])
    a_r = b_tot - br + ir
    m_new = jnp.maximum(b_tot + m_prev, jnp.max(a_r, axis=-1, keepdims=True))
    sc = jnp.exp(a_r - m_new)
    decay = jnp.exp(b_tot + m_prev - m_new)
    c_st[...] = decay * c_prev + bdot(vt * sc, kt, 2, 2)
    n_st[...] = decay * n_prev + bdot(sc, k, 2, 1)
    m_st[...] = jnp.broadcast_to(m_new, (n_grp, 1, LANES))

    out = []
    for d in range(n_dir):
        out.append([jnp.concatenate([h_all[(d * n_batch + b) * ML_HEADS + h] for h in range(ML_HEADS)], axis=0)
                    for b in range(n_batch)])
    return out


def _mlstm_kernel_t(*refs, n_batch, ctx_chunks):
    lat_f, lat_b, ctx_f, ctx_b = refs[0:5], refs[5:10], refs[10:15], refs[15:20]
    hf_lat, hb_lat, hf_ctx, hb_ctx, c_st, n_st, m_st = refs[20:]
    j = pl.program_id(0)
    is_ctx = j < ctx_chunks

    @pl.when(j == 0)
    def _():
        c_st[...] = jnp.zeros(c_st.shape, F32)
        n_st[...] = jnp.zeros(n_st.shape, F32)
        m_st[...] = jnp.zeros(m_st.shape, F32)

    def pick(c_refs, l_refs, b):
        ld = lambda r: r[b, 0] if len(r.shape) == 4 else r[b]
        return tuple(jnp.where(is_ctx, ld(c), ld(l)) for c, l in zip(c_refs, l_refs))

    inputs = [[pick(ctx_f, lat_f, b) for b in range(n_batch)], [pick(ctx_b, lat_b, b) for b in range(n_batch)]]
    hf, hb = _mlstm_step_t(inputs, c_st, n_st, m_st)

    @pl.when(is_ctx)
    def _():
        for b in range(n_batch):
            hf_ctx[b, 0] = hf[b]
            hb_ctx[b, 0] = hb[b]

    @pl.when(jnp.logical_not(is_ctx))
    def _():
        for b in range(n_batch):
            hf_lat[b, 0] = hf[b]
            hb_lat[b, 0] = hb[b]


def _mlstm_t(qk_s, k_tm, z_s, z_tm, *, n_batch, v_chan, g_col):
    nlc, ncc = qk_s["lat"].shape[0] // n_batch, qk_s["ctx"].shape[0] // n_batch
    r4 = lambda a: a.reshape(n_batch, a.shape[0] // n_batch, a.shape[1], a.shape[2])
    r3 = lambda a: a.reshape(n_batch, a.shape[0] // n_batch, a.shape[1])
    lf = lambda j: jnp.maximum(j - ncc, 0)
    lb = lambda j: jnp.where(j < ncc, nlc - 1, nlc - 1 - (j - ncc))
    cf = lambda j: jnp.minimum(j, ncc - 1)
    cb = lambda j: jnp.where(j < ncc, ncc - 1 - j, 0)
    slab = lambda ix, cidx: pl.BlockSpec((n_batch, 1, D_BR, LANES), lambda j: (0, ix(j), cidx, 0))
    rows = lambda w, ix, cidx: pl.BlockSpec((n_batch, LANES, w), lambda j: (0, ix(j), cidx))
    specs = lambda ix: [slab(ix, 0), slab(ix, 1), rows(D_BR, ix, 0), slab(ix, v_chan // D_BR), rows(LANES, ix, g_col)]
    args = lambda s: [r4(qk_s[s]), r4(qk_s[s]), r3(k_tm[s]), r4(z_s[s]), r3(z_tm[s])]
    n_grp = 2 * n_batch * ML_HEADS
    sd = lambda n: jax.ShapeDtypeStruct((n_batch, n, D_BR, LANES), F32)
    hf_lat, hb_lat, hf_ctx, hb_ctx = pl.pallas_call(
        functools.partial(_mlstm_kernel_t, n_batch=n_batch, ctx_chunks=ncc),
        grid=(ncc + nlc,),
        in_specs=specs(lf) + specs(lb) + specs(cf) + specs(cb),
        out_specs=[slab(lf, 0), slab(lb, 0), slab(cf, 0), slab(cb, 0)],
        out_shape=[sd(nlc), sd(nlc), sd(ncc), sd(ncc)],
        scratch_shapes=[pltpu.VMEM((n_grp, LANES, LANES), F32), pltpu.VMEM((n_grp, 1, LANES), F32),
                        pltpu.VMEM((n_grp, 1, LANES), F32)],
        compiler_params=_params(("arbitrary",)),
        name="mlstm_bidir",
    )(*(args("lat") + args("lat") + args("ctx") + args("ctx")))
    flat = lambda a: a.reshape(a.shape[0] * a.shape[1], D_BR, LANES)
    return {"lat": (flat(hf_lat), flat(hb_lat)), "ctx": (flat(hf_ctx), flat(hb_ctx))}


def _rms_mod(x, w, shift, scale):
    y = x * lax.rsqrt(jnp.mean(x * x, axis=-1, keepdims=True) + EPS) * w
    return y * (1.0 + scale) + shift


def _route(t, rw, rb):
    logits = _dot_f32ish3(t, rw) + rb
    col = lax.broadcasted_iota(jnp.int32, logits.shape, 1)
    big = jnp.int32(1 << 20)
    ninf = -jnp.inf
    is_g = col < MOE_GROUPS
    gl = jnp.where(is_g, logits, ninf)
    gmax = jnp.max(gl, axis=-1, keepdims=True)
    g_sel = jnp.min(jnp.where(is_g & (gl == gmax), col, big), axis=-1, keepdims=True)
    p_top = 1.0 / jnp.sum(jnp.where(is_g, jnp.exp(gl - gmax), 0.0), axis=-1, keepdims=True)
    lo = MOE_GROUPS + g_sel * MOE_PER_GROUP
    in_grp = (col >= lo) & (col < lo + MOE_PER_GROUP)
    e1v = jnp.where(in_grp, logits, ninf)
    top1 = jnp.max(e1v, axis=-1, keepdims=True)
    idx1 = jnp.min(jnp.where(in_grp & (e1v == top1), col, big), axis=-1, keepdims=True)
    e2v = jnp.where(col == idx1, ninf, e1v)
    top2 = jnp.max(e2v, axis=-1, keepdims=True)
    idx2 = jnp.min(jnp.where(in_grp & (col != idx1) & (e2v == top2), col, big), axis=-1, keepdims=True)
    ex = jnp.exp(top2 - top1)
    s1 = 1.0 / (1.0 + ex)
    return jnp.where(col == idx1, p_top * s1, 0.0) + jnp.where(col == idx2, p_top * (ex * s1), 0.0)


def _merge_kernel(yh_ref, yf_ref, h_ref, o_ref, g0_ref, g1_ref, g2_ref, x_ref, gate_ref,
                  wb_ref, wo_ref, nw_ref, n2_ref, sh_ref, sc_ref, rw_ref, rb_ref, out_ref, xn_ref, comb_ref):
    hd = LANES
    h = h_ref[...]
    parts = []
    for i in range(ML_HEADS):
        hh = h[:, i * hd:(i + 1) * hd]
        parts.append(hh * lax.rsqrt(jnp.mean(hh * hh, axis=-1, keepdims=True) + EPS))
    y_ml = jax.nn.sigmoid(o_ref[...]) * (jnp.concatenate(parts, axis=1) * nw_ref[...])
    acc = g0_ref[...].astype(F32) * _bdot(yh_ref[...], wb_ref[0])
    acc = acc + g1_ref[...].astype(F32) * _bdot(yf_ref[...], wb_ref[1])
    acc = acc + g2_ref[...].astype(F32) * _bdot(y_ml, wb_ref[2])
    x_new = x_ref[...] + gate_ref[...] * _bdot(acc, wo_ref[...])
    out_ref[...] = x_new
    t = _rms_mod(x_new, n2_ref[...], sh_ref[...], sc_ref[...])
    xn_ref[...] = t.astype(BF16)
    comb_ref[...] = _route(t, rw_ref[...], rb_ref[...])


def _merge(yh, yf, h, z_tm, gates, x, mods3, wb, wo, nw, n2w, rw, rb, *, seg, tm, o_col):
    nt, d = x.shape
    tok = lambda w, cidx: pl.BlockSpec((tm, w), lambda i: (i, cidx))
    mod = lambda k: pl.BlockSpec((None, 1, d), lambda i: (seg(i), 0, k))
    return pl.pallas_call(
        _merge_kernel,
        grid=(nt // tm,),
        in_specs=[tok(D_BR, 0), tok(D_BR, 0), tok(D_BR, 0),
                  tok(D_BR, o_col),
                  tok(d, 0), tok(d, 1), tok(d, 2),
                  tok(d, 0),
                  mod(2),
                  pl.BlockSpec((3, D_BR, d), lambda i: (0, 0, 0)),
                  pl.BlockSpec((d, d), lambda i: (0, 0)),
                  pl.BlockSpec((1, D_BR), lambda i: (0, 0)),
                  pl.BlockSpec((1, d), lambda i: (0, 0)),
                  mod(3), mod(4),
                  pl.BlockSpec((d, LANES), lambda i: (0, 0)),
                  pl.BlockSpec((1, LANES), lambda i: (0, 0))],
        out_specs=[tok(d, 0), tok(d, 0), tok(LANES, 0)],
        out_shape=[jax.ShapeDtypeStruct((nt, d), F32), jax.ShapeDtypeStruct((nt, d), BF16),
                   jax.ShapeDtypeStruct((nt, LANES), F32)],
        compiler_params=_params(("arbitrary",)),
        name="merge_branches_router",
    )(yh, yf, h, z_tm, gates, gates, gates, x, mods3, wb, wo, nw.reshape(1, D_BR), n2w.reshape(1, d),
      mods3, mods3, rw, rb)


def _moe_kernel(xn_ref, comb_ref, wg_ref, wu_ref, wd_ref, x_ref, gate_ref, nw_ref, sh_ref, sc_ref, *out_and_scratch,
                final, n_keep):
    acc_ref = out_and_scratch[-1]
    e = pl.program_id(1)

    @pl.when(e == 0)
    def _():
        acc_ref[...] = jnp.zeros(acc_ref.shape, F32)

    xn = xn_ref[...]
    comb = comb_ref[...]
    col = lax.broadcasted_iota(jnp.int32, comb.shape, 1)
    acts = []
    for i in range(MOE_PER_GROUP):
        cw = jnp.sum(jnp.where(col == e * MOE_PER_GROUP + i + MOE_GROUPS, comb, 0.0), axis=-1, keepdims=True)
        hg = jnp.dot(xn, wg_ref[i], preferred_element_type=F32)
        hu = jnp.dot(xn, wu_ref[i], preferred_element_type=F32)
        acts.append(((hg * jax.nn.sigmoid(hg)) * hu * cw).astype(BF16))
    wd = wd_ref[...].reshape(MOE_PER_GROUP * EXPERT_HID, wd_ref.shape[-1])
    acc_ref[...] += jnp.dot(jnp.concatenate(acts, axis=1), wd, preferred_element_type=F32)

    n_steps = MOE_EXPERTS // MOE_PER_GROUP
    if final:
        y_ref, = out_and_scratch[:-1]

        @pl.when((e == n_steps - 1) & (pl.program_id(0) < n_keep))
        def _():
            x_new = x_ref[...] + gate_ref[...] * acc_ref[...]
            y_ref[...] = x_new * lax.rsqrt(jnp.mean(x_new * x_new, axis=-1, keepdims=True) + EPS) * nw_ref[...]
    else:
        o_ref, xn_next_ref = out_and_scratch[:-1]

        @pl.when(e == n_steps - 1)
        def _():
            x_new = x_ref[...] + gate_ref[...] * acc_ref[...]
            o_ref[...] = x_new
            xn_next_ref[...] = _rms_mod(x_new, nw_ref[...], sh_ref[...], sc_ref[...]).astype(BF16)


def _moe(xn, comb, wg, wu, wd, x, mods3, post_w, post_mods3, *, seg, tm, final, n_keep_rows):
    nt, d = x.shape
    n_keep = n_keep_rows // tm
    tok = pl.BlockSpec((tm, d), lambda i, e: (i, 0))
    if final:
        out_specs = [pl.BlockSpec((tm, d), lambda i, e: (jnp.minimum(i, n_keep - 1), 0))]
        out_shape = [jax.ShapeDtypeStruct((n_keep_rows, d), F32)]
    else:
        out_specs = [tok, tok]
        out_shape = [jax.ShapeDtypeStruct((nt, d), F32), jax.ShapeDtypeStruct((nt, d), BF16)]
    mod = lambda k: pl.BlockSpec((None, 1, d), lambda i, e: (seg(i), 0, k))
    return pl.pallas_call(
        functools.partial(_moe_kernel, final=final, n_keep=n_keep),
        grid=(nt // tm, MOE_EXPERTS // MOE_PER_GROUP),
        in_specs=[tok,
                  pl.BlockSpec((tm, LANES), lambda i, e: (i, 0)),
                  pl.BlockSpec((MOE_PER_GROUP, d, EXPERT_HID), lambda i, e: (e, 0, 0)),
                  pl.BlockSpec((MOE_PER_GROUP, d, EXPERT_HID), lambda i, e: (e, 0, 0)),
                  pl.BlockSpec((MOE_PER_GROUP, EXPERT_HID, d), lambda i, e: (e, 0, 0)),
                  tok,
                  mod(5),
                  pl.BlockSpec((1, d), lambda i, e: (0, 0)),
                  mod(0), mod(1)],
        out_specs=out_specs,
        out_shape=out_shape,
        scratch_shapes=[pltpu.VMEM((tm, d), F32)],
        compiler_params=_params(("arbitrary", "arbitrary")),
        name="moe_experts_final" if final else "moe_experts",
    )(xn, comb, wg, wu, wd, x, mods3, post_w.reshape(1, d), post_mods3, post_mods3)


def _slab_to_tm(y_s):
    ns, c, _ = y_s.shape
    return jnp.transpose(y_s, (0, 2, 1)).reshape(ns * LANES, c)


def kernel(x, c, ctx, c_ctx, ada_w, ada_b, norm1_w, norm2_w, w_in, b_in, hy_conv_w, hy_conv_b, hy_f_w1, hy_f_b1, hy_f_w2, hy_f_b2, hy_f_w3, hy_f_freq, hy_decay, hy_skip, ml_conv_w, ml_conv_b, ml_norm_w, w_branch, w_out, moe_rg_w, moe_rg_b, moe_re_w, moe_re_b, moe_w_gate, moe_w_up, moe_w_down, norm_f_w):
    nb, seq, d = x.shape
    lc = ctx.shape[1]
    depth = ada_w.shape[0]
    assert d == D_MODEL and seq % (GRID_W * 2) == 0 and lc % LANES == 0 and nb + 1 <= 8
    rows = seq // GRID_W
    a_lat = seq // LANES
    a_ctx = lc // LANES
    n_lat, n_ctx = nb * seq, nb * lc
    tm = 256
    tm_moe = {"lat": _pick(seq, (1024, 512, 256)), "ctx": _pick(n_ctx, (512, 256))}
    tm_mrg = {"lat": _pick(seq, (512, 256)), "ctx": _pick(n_ctx, (512, 256))}
    assert seq % tm == 0 and n_ctx % tm == 0
    seg_of = lambda s, t: (lambda i: i // (seq // t)) if s == "lat" else (lambda i: nb)
    streams = ("lat", "ctx")
    xs = {"lat": x.reshape(n_lat, d), "ctx": ctx.reshape(n_ctx, d)}
    xn = {}
    cvec = jnp.zeros((8, d), F32).at[:nb].set(c).at[nb].set(c_ctx)
    mods = _mods(cvec, ada_w, ada_b)

    o_fn, o_ml, o_mlg, o_gate = 3 * D_BR, 4 * D_BR, 8 * D_BR, 8 * D_BR + 4 * ML_HEADS
    pad_g = LANES - 4 * ML_HEADS

    for l in range(depth):
        lp = {"hy_f_w1": hy_f_w1[l], "hy_f_b1": hy_f_b1[l], "hy_f_w2": hy_f_w2[l], "hy_f_b2": hy_f_b2[l],
              "hy_f_w3": hy_f_w3[l], "hy_f_freq": hy_f_freq[l], "hy_decay": hy_decay[l]}
        mods3 = mods[l].reshape(8, 1, 6 * d)
        wl, bl = w_in[l], b_in[l]
        w_pq, b_pq = _fn_fold(wl[:, o_fn:o_ml], bl[o_fn:o_ml])
        o_v, o_o = o_ml + 2 * D_BR, o_ml + 3 * D_BR
        w_cm = jnp.concatenate([wl[:, :o_fn], wl[:, o_ml:o_v], w_pq, wl[:, o_v:o_o]], axis=1)
        b_cm = jnp.concatenate([bl[:o_fn], bl[o_ml:o_v], b_pq, bl[o_v:o_o]])
        w_tm = jnp.concatenate([wl[:, o_o:o_mlg], wl[:, o_mlg:o_gate], jnp.zeros((d, pad_g), F32)], axis=1)
        b_tm = jnp.concatenate([bl[o_o:o_mlg], bl[o_mlg:o_gate], jnp.zeros((pad_g,), F32)])
        c_hy, c_qk, c_fn, c_v = 0, 3 * D_BR, 5 * D_BR, 7 * D_BR
        g_col = D_BR // LANES

        last = l + 1 == depth
        live = ("lat",) if last else streams
        if l == 0:
            xn = {s: _norm_mod(xs[s], norm1_w[l], mods3, 0, 1, seg_of(s, tm), tm) for s in streams}
        w_tm_b, w_cm_t = w_tm.astype(BF16), w_cm.T.astype(BF16)
        z_tm = {s: _mm_tm(xn[s], w_tm_b, b_tm, gate=False) for s in streams}
        w_gate_b = wl[:, o_gate:].astype(BF16)
        gates = {s: _mm_tm(xn[s], w_gate_b, bl[o_gate:], gate=True) for s in live}
        z_s = {s: _mm_slab(xn[s], w_cm_t, b_cm) for s in streams}

        grid_kw = {"lat": dict(rows=rows, width=GRID_W), "ctx": dict(rows=1, width=lc)}
        hy_w, hy_b = hy_conv_w[l].reshape(9, 3 * D_BR), hy_conv_b[l]
        ml_w, ml_b = ml_conv_w[l].reshape(9, 2 * D_BR), ml_conv_b[l]
        conv = lambda s, w, b, lo, n, act: _conv(z_s[s], w, b, chan_lo=lo, chan_n=n, silu=act, n_batch=nb,
                                                 slab0=0, **grid_kw[s])
        u = {s: conv(s, hy_w, hy_b, c_hy, 3 * D_BR, False) for s in live}
        qk = {s: conv(s, ml_w, ml_b, c_qk, 2 * D_BR, True) for s in streams}
        k_tm = {s: _slab_to_tm(qk[s][:, D_BR:]).astype(BF16) for s in streams}
        h_s = _mlstm_t(qk, k_tm, z_s, z_tm, n_batch=nb, v_chan=c_v, g_col=g_col)
        h_sum = {s: _slab_to_tm(h_s[s][0] + h_s[s][1]) for s in live}

        yh, yf = {}, {}
        yh["lat"] = _slab_to_tm(_hyena(u["lat"], _hyena_taps(seq, 2 * a_lat, lp), hy_skip[l],
                                       a_in=a_lat, na=2 * a_lat, n_batch=nb)).astype(BF16)
        yk = _fn_seq(z_s["lat"], a_n=a_lat, n_batch=nb, chan_lo=c_fn)
        yf["lat"] = jnp.transpose(yk, (0, 3, 2, 1)).reshape(n_lat, D_BR).astype(BF16)
        if not last:
            yh["ctx"] = _hyena_short(u["ctx"], _hyena_taps(lc, 2 * a_ctx, lp), hy_skip[l],
                                     a_n=a_ctx, n_batch=nb).reshape(n_ctx, D_BR)
            yf["ctx"] = _fn_small(z_s["ctx"], a_n=a_ctx, n_batch=nb, chan_lo=c_fn).reshape(n_ctx, D_BR)

        rw = jnp.concatenate([moe_rg_w[l], moe_re_w[l], jnp.zeros((d, LANES - MOE_GROUPS - MOE_EXPERTS), F32)], axis=1)
        rb = jnp.concatenate([moe_rg_b[l], moe_re_b[l], jnp.zeros((LANES - MOE_GROUPS - MOE_EXPERTS,), F32)]).reshape(1, LANES)
        wb, wo = w_branch[l].astype(BF16), w_out[l].astype(BF16)
        experts = (moe_w_gate[l].astype(BF16), moe_w_up[l].astype(BF16), moe_w_down[l].astype(BF16))
        for s in live:
            xs[s], xn2, comb = _merge(yh[s], yf[s], h_sum[s], z_tm[s], gates[s], xs[s], mods3, wb, wo,
                                      ml_norm_w[l], norm2_w[l], rw, rb, seg=seg_of(s, tm_mrg[s]), tm=tm_mrg[s],
                                      o_col=0)
            moe_kw = dict(seg=seg_of(s, tm_moe[s]), tm=tm_moe[s], n_keep_rows=xs[s].shape[0])
            if last:
                out, = _moe(xn2, comb, *experts, xs[s], mods3, norm_f_w, mods3, final=True, **moe_kw)
            else:
                xs[s], xn[s] = _moe(xn2, comb, *experts, xs[s], mods3, norm1_w[l + 1],
                                    mods[l + 1].reshape(8, 1, 6 * d), final=False, **moe_kw)

    return out.reshape(nb, seq, d)
```

```python
import functools
import math

import numpy as np
import jax
import jax.numpy as jnp
from jax import lax
from jax.experimental import pallas as pl
from jax.experimental.pallas import tpu as pltpu

F32 = jnp.float32
BF16 = jnp.bfloat16

D_MODEL = 1024
D_BR = 512
GRID_W = 64
LANES = 128
CB = 8
HY_ORDER = 2
HY_BANDS = 16
FN_GROUPS = 4
ML_HEADS = 4
MOE_GROUPS = 4
MOE_PER_GROUP = 4
MOE_EXPERTS = 16
EXPERT_HID = 256
EPS = 1e-6
VMEM_LIMIT = 56 * 1024 * 1024


def _params(sem):
    return pltpu.CompilerParams(dimension_semantics=sem, vmem_limit_bytes=VMEM_LIMIT)


def _bdot(a, b):
    return jnp.dot(a.astype(BF16), b.astype(BF16), preferred_element_type=F32)


def _split3(x):
    hi = x.astype(BF16)
    r1 = x - hi.astype(F32)
    mid = r1.astype(BF16)
    lo = (r1 - mid.astype(F32)).astype(BF16)
    return hi, mid, lo


def _dot_f32ish(x, w):
    xh, xm, xl = _split3(x)
    wh, wm, wl = _split3(w)
    d = lambda a, b: jnp.dot(a, b, preferred_element_type=F32)
    return (d(xh, wh) + (d(xh, wm) + d(xm, wh))) + (d(xm, wm) + d(xh, wl) + d(xl, wh))


def _dot_f32ish3(x, w):
    xh, xm, _ = _split3(x)
    wh, wm, _ = _split3(w)
    d = lambda a, b: jnp.dot(a, b, preferred_element_type=F32)
    return d(xh, wh) + (d(xh, wm) + d(xm, wh))


def _swap_halves(x):
    return jnp.concatenate([x[..., LANES:], x[..., :LANES]], axis=-1)


def _mods_kernel(c_ref, w_ref, b_ref, o_ref):
    c = c_ref[...]
    s = c * jax.nn.sigmoid(c)
    o_ref[...] = _dot_f32ish(s, w_ref[...]) + b_ref[...]


def _mods(cvec, ada_w, ada_b):
    depth, d, n6 = ada_w.shape
    tn = 1536
    return pl.pallas_call(
        _mods_kernel,
        grid=(depth, n6 // tn),
        in_specs=[pl.BlockSpec((8, d), lambda l, j: (0, 0)),
                  pl.BlockSpec((None, d, tn), lambda l, j: (l, 0, j)),
                  pl.BlockSpec((None, 1, tn), lambda l, j: (l, 0, j))],
        out_specs=pl.BlockSpec((None, 8, tn), lambda l, j: (l, 0, j)),
        out_shape=jax.ShapeDtypeStruct((depth, 8, n6), F32),
        compiler_params=_params(("arbitrary", "arbitrary")),
        name="adaln_mods",
    )(cvec, ada_w, ada_b.reshape(depth, 1, n6))


def _norm_mod_kernel(x_ref, w_ref, sh_ref, sc_ref, o_ref):
    x = x_ref[...]
    y = x * lax.rsqrt(jnp.mean(x * x, axis=-1, keepdims=True) + EPS) * w_ref[...]
    o_ref[...] = (y * (1.0 + sc_ref[...]) + sh_ref[...]).astype(o_ref.dtype)


def _norm_mod(x, w, mods3, col_shift, col_scale, seg, tm):
    nt, d = x.shape
    return pl.pallas_call(
        _norm_mod_kernel,
        grid=(nt // tm,),
        in_specs=[pl.BlockSpec((tm, d), lambda i: (i, 0)),
                  pl.BlockSpec((1, d), lambda i: (0, 0)),
                  pl.BlockSpec((None, 1, d), lambda i: (seg(i), 0, col_shift)),
                  pl.BlockSpec((None, 1, d), lambda i: (seg(i), 0, col_scale))],
        out_specs=pl.BlockSpec((tm, d), lambda i: (i, 0)),
        out_shape=jax.ShapeDtypeStruct((nt, d), BF16),
        compiler_params=_params(("arbitrary",)),
        name="norm_mod",
    )(x, w.reshape(1, d), mods3, mods3)


def _mm_tm_kernel(x_ref, w_ref, b_ref, o_ref, *, gate):
    y = jnp.dot(x_ref[...], w_ref[...], preferred_element_type=F32) + b_ref[...]
    o_ref[...] = (jax.nn.sigmoid(y) if gate else y).astype(o_ref.dtype)


def _pick(n, cands):
    for c in cands:
        if n % c == 0:
            return c
    raise ValueError(f"no tile for {n} in {cands}")


def _mm_tm(xn, w, b, *, gate):
    nt, k = xn.shape
    n = w.shape[1]
    tm = _pick(nt, (1056, 1024, 768, 512, 256))
    tn = _pick(n, (1536, 1408, 1152, 1024, 640, 512, 384, 256, 128))
    return pl.pallas_call(
        functools.partial(_mm_tm_kernel, gate=gate),
        grid=(n // tn, nt // tm),
        in_specs=[pl.BlockSpec((tm, k), lambda j, i: (i, 0)),
                  pl.BlockSpec((k, tn), lambda j, i: (0, j)),
                  pl.BlockSpec((1, tn), lambda j, i: (0, j))],
        out_specs=pl.BlockSpec((tm, tn), lambda j, i: (i, j)),
        out_shape=jax.ShapeDtypeStruct((nt, n), BF16 if gate else F32),
        compiler_params=_params(("arbitrary", "arbitrary")),
        name="inproj_gates" if gate else "inproj_token_major",
    )(xn, w, b.reshape(1, n))


def _mm_slab_kernel(w_ref, x_ref, b_ref, o_ref, *, slabs):
    w = w_ref[...]
    b = b_ref[...]
    step = 2 if slabs % 2 == 0 else 1
    for s in range(0, slabs, step):
        xs = x_ref[s * LANES:(s + step) * LANES, :]
        y = lax.dot_general(w, xs, (((1,), (1,)), ((), ())), preferred_element_type=F32) + b
        for i in range(step):
            o_ref[s + i] = y[:, i * LANES:(i + 1) * LANES]


def _mm_slab(xn, wt, b):
    nt, k = xn.shape
    c = wt.shape[0]
    ns = nt // LANES
    ts = _pick(ns, (12, 11, 8, 6, 4, 3, 2, 1))
    tc = _pick(c, (1024, 896, 512, 256, 128))
    return pl.pallas_call(
        functools.partial(_mm_slab_kernel, slabs=ts),
        grid=(c // tc, ns // ts),
        in_specs=[pl.BlockSpec((tc, k), lambda j, i: (j, 0)),
                  pl.BlockSpec((ts * LANES, k), lambda j, i: (i, 0)),
                  pl.BlockSpec((tc, 1), lambda j, i: (j, 0))],
        out_specs=pl.BlockSpec((ts, tc, LANES), lambda j, i: (i, j, 0)),
        out_shape=jax.ShapeDtypeStruct((ns, c, LANES), F32),
        compiler_params=_params(("arbitrary", "arbitrary")),
        name="inproj_slab",
    )(wt, xn, b.reshape(c, 1))


def _conv_taps(rows, width):
    taps = []
    for dr in (-1, 0, 1):
        if rows == 1 and dr != 0:
            continue
        for dw in (-1, 0, 1):
            taps.append((dr, dw))
    return taps


def _silu(x):
    return x * jax.nn.sigmoid(x)


def _conv_kernel(x_ref, w_ref, b_ref, o_ref, *, taps, width, n_slabs, silu):
    ct = x_ref.shape[1]
    lane = lax.broadcasted_iota(jnp.int32, (ct, LANES), 1)
    bias = jnp.zeros((ct, LANES), F32) + b_ref[...]
    planes = []
    for t, (dr, dw) in enumerate(taps):
        w = w_ref[t]
        if width < LANES and dw != 0:
            col = lane % width + dw
            w = jnp.where((col >= 0) & (col < width), w, 0.0)
        planes.append((dr * width + dw, w))

    def body(a, carry):
        x0 = x_ref[a]
        xm = jnp.where(a > 0, x_ref[jnp.maximum(a - 1, 0)], 0.0)
        xp = jnp.where(a < n_slabs - 1, x_ref[jnp.minimum(a + 1, n_slabs - 1)], 0.0)
        acc = bias
        for delta, w in planes:
            if delta == 0:
                src = x0
            elif delta > 0:
                src = pltpu.roll(jnp.where(lane >= delta, x0, xp), LANES - delta, 1)
            else:
                src = pltpu.roll(jnp.where(lane < LANES + delta, x0, xm), -delta, 1)
            acc = acc + src * w
        o_ref[a] = _silu(acc) if silu else acc
        return carry

    lax.fori_loop(0, n_slabs, body, 0, unroll=2 if n_slabs % 2 == 0 else 1)


def _conv_grid_kernel(x_ref, w_ref, b_ref, o_ref, ym_ref, yp_ref, *, width, n_slabs, silu):
    ct = x_ref.shape[1]
    lane = lax.broadcasted_iota(jnp.int32, (ct, LANES), 1)
    col = lane % width
    bias = jnp.zeros((ct, LANES), F32) + b_ref[...]

    def row_sums(a, carry):
        x0 = x_ref[a]
        xl = jnp.where(col >= 1, pltpu.roll(x0, 1, 1), 0.0)
        xr = jnp.where(col < width - 1, pltpu.roll(x0, LANES - 1, 1), 0.0)
        ym_ref[a] = w_ref[0] * xl + w_ref[1] * x0 + w_ref[2] * xr
        o_ref[a] = bias + w_ref[3] * xl + w_ref[4] * x0 + w_ref[5] * xr
        yp_ref[a] = w_ref[6] * xl + w_ref[7] * x0 + w_ref[8] * xr
        return carry

    def combine(a, carry):
        up = jnp.where(a > 0, ym_ref[jnp.maximum(a - 1, 0)], 0.0)
        dn = jnp.where(a < n_slabs - 1, yp_ref[jnp.minimum(a + 1, n_slabs - 1)], 0.0)
        from_up = jnp.where(lane < LANES - width, ym_ref[a], up)
        from_dn = jnp.where(lane >= width, yp_ref[a], dn)
        if 2 * width == LANES:
            y = o_ref[a] + pltpu.roll(from_up + from_dn, width, 1)
        else:
            y = o_ref[a] + pltpu.roll(from_up, width, 1) + pltpu.roll(from_dn, LANES - width, 1)
        o_ref[a] = _silu(y) if silu else y
        return carry

    unroll = max(u for u in (32, 16, 8, 4, 2, 1) if n_slabs % u == 0)
    lax.fori_loop(0, n_slabs, row_sums, 0, unroll=unroll)
    lax.fori_loop(0, n_slabs, combine, 0, unroll=unroll)


def _conv(z_s, w9, bias, *, rows, width, n_batch, slab0, chan_lo, chan_n, silu):
    seq = rows * width
    a_n = seq // LANES
    taps = tuple(_conv_taps(rows, width))
    assert all(abs(dr * width + dw) < LANES for dr, dw in taps)
    assert LANES % width == 0 or (rows == 1 and width % LANES == 0)
    tap_ids = [(dr + 1) * 3 + (dw + 1) for dr, dw in taps]
    w_t = jnp.broadcast_to(w9[jnp.array(tap_ids)][:, :, None], (len(taps), chan_n, LANES))
    ct = 64 if rows > 1 else 256
    assert chan_lo % ct == 0 and chan_n % ct == 0 and slab0 % a_n == 0
    nt_ = len(taps)
    if rows > 1:
        assert LANES % width == 0 and nt_ == 9
        body = functools.partial(_conv_grid_kernel, width=width, n_slabs=a_n, silu=silu)
        scratch = [pltpu.VMEM((a_n, ct, LANES), F32), pltpu.VMEM((a_n, ct, LANES), F32)]
    else:
        body = functools.partial(_conv_kernel, taps=taps, width=width, n_slabs=a_n, silu=silu)
        scratch = []
    return pl.pallas_call(
        body,
        scratch_shapes=scratch,
        grid=(n_batch, chan_n // ct),
        in_specs=[pl.BlockSpec((a_n, ct, LANES), lambda b, j: (slab0 // a_n + b, chan_lo // ct + j, 0)),
                  pl.BlockSpec((nt_, ct, LANES), lambda b, j: (0, j, 0)),
                  pl.BlockSpec((ct, 1), lambda b, j: (j, 0))],
        out_specs=pl.BlockSpec((a_n, ct, LANES), lambda b, j: (b, j, 0)),
        out_shape=jax.ShapeDtypeStruct((n_batch * a_n, chan_n, LANES), F32),
        compiler_params=_params(("arbitrary", "arbitrary")),
        name=f"dwconv_{rows}x{width}",
    )(z_s, w_t, bias.reshape(chan_n, 1))


def _dft_consts(a_in, na):
    n = na * LANES
    k = np.arange(na)[:, None]
    a = np.arange(a_in)[None, :]
    ang = 2 * np.pi * (k * a % na) / na
    fa = np.concatenate([np.cos(ang), -np.sin(ang)], axis=0)
    r = np.arange(LANES)
    ang_t = 2 * np.pi * (np.arange(na)[:, None] * r[None, :] % n) / n
    tr, ti = np.cos(ang_t), -np.sin(ang_t)
    ta = np.concatenate([tr, tr], axis=1)
    tb = np.concatenate([-ti, ti], axis=1)
    ang2 = 2 * np.pi * (r[:, None] * r[None, :] % LANES) / LANES
    c2, s2 = np.cos(ang2), np.sin(ang2)
    g2 = np.block([[c2, -s2], [s2, c2]])
    g2i = np.block([[c2, s2], [-s2, c2]])
    ang_i = 2 * np.pi * (np.arange(a_in)[:, None] * np.arange(na)[None, :] % na) / na
    ci, si = np.cos(ang_i) / n, -np.sin(ang_i) / n
    f = lambda v, dt: jnp.asarray(v, dtype=dt)
    return dict(fa=f(fa, F32), ta=f(ta, F32), tb=f(tb, F32), g2=f(g2, F32), g2i=f(g2i, F32),
                ci=f(ci, F32), si=f(si, F32))


def _fwd_slab_stage(m, fa, ta, tb, na):
    pp = jnp.dot(fa, m.astype(BF16), preferred_element_type=F32)
    p = jnp.concatenate([pp[:na], pp[na:]], axis=1)
    return p * ta + _swap_halves(p) * tb


def _cmul(x, kf):
    kr, ki = kf[..., :LANES], kf[..., LANES:]
    ka = jnp.concatenate([kr, kr], axis=-1)
    kb = jnp.concatenate([-ki, ki], axis=-1)
    return x * ka + _swap_halves(x) * kb


def _chan_load(ref, c):
    n, cb, _ = ref.shape
    return ref.reshape(n * cb, LANES)[pl.ds(c, n, stride=cb), :]


def _chan_store(ref, c, val):
    ref[:, c, :] = val


def _dot_f32ish_k(w, h):
    wh, wm, _ = _split3(w)
    hh, hm, _ = _split3(h)
    lhs = jnp.concatenate([wh, wh, wm], axis=1)
    rhs = jnp.concatenate([hh, hm, hh], axis=0)
    return jnp.dot(lhs, rhs, preferred_element_type=F32)


def _taps_kernel(bands_ref, w1t_ref, w1c_ref, w1s_ref, b1_ref, w2_ref, b2_ref, fq_ref, w3_ref, dec_ref, o_ref,
                 *, seq, a_seq, na, spb):
    step = pl.program_id(0)
    width = spb * LANES
    n_total = na * LANES
    is_f = step * spb < a_seq
    n = step * width + lax.broadcasted_iota(jnp.int32, (1, width), 1)
    pos = jnp.where(is_f, n, n_total - n)
    lo, hi = jnp.where(is_f, -1, n_total - seq), jnp.where(is_f, seq, n_total)
    t = pos.astype(F32) / seq
    fq = fq_ref[...]
    ang = ((2 * math.pi) * t) * bands_ref[...]
    pre = (w1t_ref[...] * t + _dot_f32ish(w1c_ref[...], jnp.cos(ang))
           + _dot_f32ish(w1s_ref[...], jnp.sin(ang)))
    h = jnp.sin(fq * (pre + b1_ref[...]))
    h = jnp.sin(fq * (_dot_f32ish(w2_ref[...], h) + b2_ref[...]))
    d = jnp.where(is_f, 0, 1)
    dec = jnp.abs(dec_ref[d])
    k = _dot_f32ish_k(w3_ref[d], h)
    for i in range(spb):
        sl = slice(i * LANES, (i + 1) * LANES)
        live = (n[:, sl] > lo) & (n[:, sl] < hi)
        o_ref[i] = jnp.where(live, k[:, sl] * jnp.exp(-t[:, sl] * dec), 0.0)


def _hyena_taps(seq, na, lp):
    nc = HY_ORDER * D_BR
    hid = lp["hy_f_w2"].shape[0]
    w1 = lp["hy_f_w1"]
    col = lambda v: v.reshape(-1, 1)
    bands = col(jnp.linspace(1e-4, HY_BANDS - 1, HY_BANDS, dtype=F32))
    w3 = jnp.transpose(lp["hy_f_w3"].T.reshape(HY_ORDER, 2, D_BR, hid), (1, 0, 2, 3)).reshape(2, nc, hid)
    dec = jnp.broadcast_to(jnp.transpose(lp["hy_decay"], (1, 0, 2)).reshape(2, nc, 1), (2, nc, LANES))
    a_seq = seq // LANES
    spb = min(8, a_seq)
    assert na == 2 * a_seq and a_seq % spb == 0
    args = (bands, col(w1[0]), w1[1:1 + HY_BANDS].T, w1[1 + HY_BANDS:].T, col(lp["hy_f_b1"]), lp["hy_f_w2"].T,
            col(lp["hy_f_b2"]), col(lp["hy_f_freq"]), w3, dec)
    full = lambda v: pl.BlockSpec(v.shape, lambda s: (0,) * v.ndim)
    return pl.pallas_call(
        functools.partial(_taps_kernel, seq=seq, a_seq=a_seq, na=na, spb=spb),
        grid=(na // spb,),
        in_specs=[full(v) for v in args],
        out_specs=pl.BlockSpec((spb, nc, LANES), lambda s: (s, 0, 0)),
        out_shape=jax.ShapeDtypeStruct((na, nc, LANES), F32),
        compiler_params=_params(("arbitrary",)),
        name=f"hyena_filter_taps_{na}",
    )(*args)


def _hyena_kernel(v_ref, x1_ref, x2_ref, k0_ref, k1_ref, skip_ref, fa_ref, faf_ref, ta_ref, tb_ref, g2_ref, g2i_ref,
                  ci_ref, si_ref, o_ref, p_buf, z_buf, kf_buf, *, a_in, na):
    fa, ta, tb = fa_ref[...].astype(BF16), ta_ref[...], tb_ref[...]
    ci, si = ci_ref[...].astype(BF16), si_ref[...].astype(BF16)

    @pl.when(pl.program_id(1) == 0)
    def _():
        faf = faf_ref[...].astype(BF16)
        for order, k_ref in enumerate((k0_ref, k1_ref)):
            scales = []
            for c in range(CB):
                m = _chan_load(k_ref, c)
                ss = jnp.sum(jnp.sum(m * m, axis=1, keepdims=True), axis=0, keepdims=True)
                scales.append(lax.rsqrt(ss + EPS))
                p_buf[c] = _fwd_slab_stage(m, faf, ta, tb, na)
            x = _bdot(p_buf[...].reshape(CB * na, 2 * LANES), g2_ref[...]).reshape(CB, na, 2 * LANES)
            for c in range(CB):
                kf_buf[order, c] = x[c] * scales[c]

    def spectral(order):
        x = _bdot(p_buf[...].reshape(CB * na, 2 * LANES), g2_ref[...])
        y = _cmul(x, kf_buf[order].reshape(CB * na, 2 * LANES))
        bm = _bdot(y, g2i_ref[...]).reshape(CB, na, 2 * LANES)
        p_buf[...] = bm * ta - _swap_halves(bm) * tb

    def conv_out(c):
        bb = p_buf[c]
        return (jnp.dot(ci, bb[:, :LANES].astype(BF16), preferred_element_type=F32)
                + jnp.dot(si, bb[:, LANES:].astype(BF16), preferred_element_type=F32))

    for c in range(CB):
        p_buf[c] = _fwd_slab_stage(_chan_load(v_ref, c), fa, ta, tb, na)
    spectral(0)
    for c in range(CB):
        z_buf[c] = _chan_load(x1_ref, c) * (conv_out(c) + _chan_load(v_ref, c) * skip_ref[0, c])
    for c in range(CB):
        p_buf[c] = _fwd_slab_stage(z_buf[c], fa, ta, tb, na)
    spectral(1)
    for c in range(CB):
        _chan_store(o_ref, c, _chan_load(x2_ref, c) * (conv_out(c) + z_buf[c] * skip_ref[1, c]))


def _hyena(u_s, taps_s, skip, *, a_in, na, n_batch):
    cs = _dft_consts(a_in, na)
    faf = _dft_consts(na, na)["fa"]
    nblk = D_BR // CB
    const = lambda shp: pl.BlockSpec(shp, lambda j, b: (0,) * len(shp))
    skip_b = jnp.broadcast_to(skip[:, :, None, None], (HY_ORDER, D_BR, 1, LANES))
    return pl.pallas_call(
        functools.partial(_hyena_kernel, a_in=a_in, na=na),
        grid=(nblk, n_batch),
        in_specs=[pl.BlockSpec((a_in, CB, LANES), lambda j, b: (b, j, 0)),
                  pl.BlockSpec((a_in, CB, LANES), lambda j, b: (b, nblk + j, 0)),
                  pl.BlockSpec((a_in, CB, LANES), lambda j, b: (b, 2 * nblk + j, 0)),
                  pl.BlockSpec((na, CB, LANES), lambda j, b: (0, j, 0)),
                  pl.BlockSpec((na, CB, LANES), lambda j, b: (0, nblk + j, 0)),
                  pl.BlockSpec((HY_ORDER, CB, 1, LANES), lambda j, b: (0, j, 0, 0)),
                  const((2 * na, a_in)), const((2 * na, na)), const((na, 2 * LANES)), const((na, 2 * LANES)),
                  const((2 * LANES, 2 * LANES)), const((2 * LANES, 2 * LANES)),
                  const((a_in, na)), const((a_in, na))],
        out_specs=pl.BlockSpec((a_in, CB, LANES), lambda j, b: (b, j, 0)),
        out_shape=jax.ShapeDtypeStruct((n_batch * a_in, D_BR, LANES), F32),
        scratch_shapes=[pltpu.VMEM((CB, na, 2 * LANES), F32), pltpu.VMEM((CB, a_in, LANES), F32),
                        pltpu.VMEM((HY_ORDER, CB, na, 2 * LANES), F32)],
        compiler_params=_params(("arbitrary", "arbitrary")),
        name=f"hyena_longconv_{a_in}",
    )(u_s, u_s, u_s, taps_s, taps_s, skip_b, cs["fa"], faf, cs["ta"], cs["tb"], cs["g2"], cs["g2i"],
      cs["ci"], cs["si"])


def _slabs_to_rows(ref, n):
    return jnp.concatenate([ref[a].T for a in range(n)], axis=0)


def _hyena_short_kernel(v_ref, x1_ref, x2_ref, k0_ref, k1_ref, skip_ref, f_ref, g_ref, o_ref, *, a_n, a_k):
    seq, nf = a_n * LANES, a_k * LANES
    ff = f_ref[...].astype(BF16)
    gi = g_ref[...].astype(BF16)
    v, x1, x2 = _slabs_to_rows(v_ref, a_n), _slabs_to_rows(x1_ref, a_n), _slabs_to_rows(x2_ref, a_n)

    def longconv(u, k_ref):
        k = _slabs_to_rows(k_ref, a_k)
        s = lax.rsqrt(jnp.sum(k * k, axis=0, keepdims=True) + EPS)
        kf = jnp.dot(ff, k.astype(BF16), preferred_element_type=F32) * s
        x = jnp.dot(ff[:, :seq], u.astype(BF16), preferred_element_type=F32)
        xr, xi, kr, ki = x[:nf], x[nf:], kf[:nf], kf[nf:]
        y = jnp.concatenate([xr * kr - xi * ki, xr * ki + xi * kr], axis=0)
        return jnp.dot(gi, y.astype(BF16), preferred_element_type=F32)

    z = x1 * (longconv(v, k0_ref) + v * skip_ref[0])
    o_ref[...] = x2 * (longconv(z, k1_ref) + z * skip_ref[1])


def _hyena_short(u_s, taps_s, skip, *, a_n, n_batch):
    a_k = 2 * a_n
    seq, nf = a_n * LANES, a_k * LANES
    k = np.arange(nf)
    ang = 2 * np.pi * (k[:, None] * k[None, :] % nf) / nf
    f = np.concatenate([np.cos(ang), -np.sin(ang)], axis=0)
    g = np.concatenate([np.cos(ang[:seq]), -np.sin(ang[:seq])], axis=1) / nf
    nblk = D_BR // LANES
    const = lambda shp: pl.BlockSpec(shp, lambda b, j: (0,) * len(shp))
    return pl.pallas_call(
        functools.partial(_hyena_short_kernel, a_n=a_n, a_k=a_k),
        grid=(n_batch, nblk),
        in_specs=[pl.BlockSpec((a_n, LANES, LANES), lambda b, j: (b, j, 0)),
                  pl.BlockSpec((a_n, LANES, LANES), lambda b, j: (b, nblk + j, 0)),
                  pl.BlockSpec((a_n, LANES, LANES), lambda b, j: (b, 2 * nblk + j, 0)),
                  pl.BlockSpec((a_k, LANES, LANES), lambda b, j: (0, j, 0)),
                  pl.BlockSpec((a_k, LANES, LANES), lambda b, j: (0, nblk + j, 0)),
                  pl.BlockSpec((HY_ORDER, 1, LANES), lambda b, j: (0, 0, j)),
                  const((2 * nf, nf)), const((seq, 2 * nf))],
        out_specs=pl.BlockSpec((None, seq, LANES), lambda b, j: (b, 0, j)),
        out_shape=jax.ShapeDtypeStruct((n_batch, seq, D_BR), F32),
        compiler_params=_params(("arbitrary", "arbitrary")),
        name="hyena_short",
    )(u_s, u_s, u_s, taps_s, taps_s, skip.reshape(HY_ORDER, 1, D_BR), jnp.asarray(f, F32), jnp.asarray(g, F32))


def _chan_dft_mats():
    r = np.arange(LANES)
    ang = 2 * np.pi * (r[:, None] * r[None, :] % LANES) / LANES
    return np.cos(ang), np.sin(ang)


def _fn_fold_kernel(w_ref, cs_ref, o_ref):
    o_ref[...] = _dot_f32ish(w_ref[...], cs_ref[...])


def _fn_fold(w_fn, b_fn):
    d = w_fn.shape[0]
    c, s = _chan_dft_mats()
    cs = jnp.asarray(np.concatenate([c, s], axis=1), dtype=F32)
    rows = d + 8
    w_aug = jnp.concatenate([w_fn, b_fn[None, :], jnp.zeros((7, D_BR), F32)], axis=0)
    out = pl.pallas_call(
        _fn_fold_kernel,
        grid=(FN_GROUPS,),
        in_specs=[pl.BlockSpec((rows, LANES), lambda g: (0, g)),
                  pl.BlockSpec((LANES, 2 * LANES), lambda g: (0, 0))],
        out_specs=pl.BlockSpec((rows, 2 * LANES), lambda g: (0, g)),
        out_shape=jax.ShapeDtypeStruct((rows, 2 * D_BR), F32),
        compiler_params=_params(("arbitrary",)),
        name="fnet_fold_channel_dft",
    )(w_aug, cs)
    return out[:d], out[d]


def _fn_seq_kernel(p_ref, q_ref, fa_ref, tr_ref, ti_ref, g_ref, o_ref, a_buf, *, a_n, n_batch, scale):
    fa, tr, ti = fa_ref[...].astype(BF16), tr_ref[...], ti_ref[...]
    side = lambda m: jnp.concatenate([m[b * a_n:(b + 1) * a_n] for b in range(n_batch)], axis=1).astype(BF16)
    for c in range(CB):
        r1 = jnp.dot(fa, side(_chan_load(p_ref, c)), preferred_element_type=F32)
        r2 = jnp.dot(fa, side(_chan_load(q_ref, c)), preferred_element_type=F32)
        ar_all = r1[:a_n] - r2[a_n:]
        ai_all = -(r2[:a_n] + r1[a_n:])
        for b in range(n_batch):
            ar, ai = ar_all[:, b * LANES:(b + 1) * LANES], ai_all[:, b * LANES:(b + 1) * LANES]
            a_buf[b, c] = jnp.concatenate([ar * tr - ai * ti, ar * ti + ai * tr], axis=1)
    y = _bdot(a_buf[...].reshape(n_batch * CB * a_n, 2 * LANES), g_ref[...]) * scale
    o_ref[...] = y.reshape(n_batch, CB, a_n, LANES)


def _fn_seq(pq, *, a_n, n_batch, chan_lo):
    seq = a_n * LANES
    k = np.arange(a_n)
    ang = 2 * np.pi * (k[:, None] * k[None, :] % a_n) / a_n
    fa = np.concatenate([np.cos(ang), np.sin(ang)], axis=0)
    r = np.arange(LANES)
    ang_t = 2 * np.pi * (k[:, None] * r[None, :] % seq) / seq
    c2, s2 = _chan_dft_mats()
    g = np.concatenate([c2, s2], axis=0)
    nblk = LANES // CB
    const = lambda shp: pl.BlockSpec(shp, lambda j: (0,) * len(shp))

    def chan_blk(j, off):
        return chan_lo // CB + (j // nblk) * (2 * nblk) + off * nblk + j % nblk

    return pl.pallas_call(
        functools.partial(_fn_seq_kernel, a_n=a_n, n_batch=n_batch, scale=1.0 / math.sqrt(seq * LANES)),
        grid=(D_BR // CB,),
        in_specs=[pl.BlockSpec((n_batch * a_n, CB, LANES), lambda j: (0, chan_blk(j, 0), 0)),
                  pl.BlockSpec((n_batch * a_n, CB, LANES), lambda j: (0, chan_blk(j, 1), 0)),
                  const((2 * a_n, a_n)), const((a_n, LANES)), const((a_n, LANES)), const((2 * LANES, LANES))],
        out_specs=pl.BlockSpec((n_batch, CB, a_n, LANES), lambda j: (0, j, 0, 0)),
        out_shape=jax.ShapeDtypeStruct((n_batch, D_BR, a_n, LANES), F32),
        scratch_shapes=[pltpu.VMEM((n_batch, CB, a_n, 2 * LANES), F32)],
        compiler_params=_params(("arbitrary",)),
        name="fnet_sequence_dft",
    )(pq, pq, jnp.asarray(fa, F32), jnp.asarray(np.cos(ang_t), F32), jnp.asarray(-np.sin(ang_t), F32),
      jnp.asarray(g, F32))


def _fn_small_kernel(pq_ref, cl_ref, sl_ref, o_ref, *, a_n, scale):
    pq = jnp.concatenate([pq_ref[a].T for a in range(a_n)], axis=0)
    o_ref[...] = (_bdot(cl_ref[...], pq[:, :LANES]) - _bdot(sl_ref[...], pq[:, LANES:])) * scale


def _fn_small(pq, *, a_n, n_batch, chan_lo):
    seq = a_n * LANES
    n = np.arange(seq)
    ang = 2 * np.pi * (n[:, None] * n[None, :] % seq) / seq
    const = lambda shp: pl.BlockSpec(shp, lambda b, g: (0,) * len(shp))
    g0 = chan_lo // (2 * LANES)
    return pl.pallas_call(
        functools.partial(_fn_small_kernel, a_n=a_n, scale=1.0 / math.sqrt(seq * LANES)),
        grid=(n_batch, FN_GROUPS),
        in_specs=[pl.BlockSpec((a_n, 2 * LANES, LANES), lambda b, g: (b, g0 + g, 0)),
                  const((seq, seq)), const((seq, seq))],
        out_specs=pl.BlockSpec((None, seq, LANES), lambda b, g: (b, 0, g)),
        out_shape=jax.ShapeDtypeStruct((n_batch, seq, D_BR), F32),
        compiler_params=_params(("arbitrary", "arbitrary")),
        name="fnet_short",
    )(pq, jnp.asarray(np.cos(ang), F32), jnp.asarray(np.sin(ang), F32))


def _log_sigmoid(x):
    return jnp.minimum(x, 0.0) - jnp.log(1.0 + jnp.exp(-jnp.abs(x)))


def _exact_tri_dot(tri, x, tri_on_left):
    h, m, l = _split3(x)
    if tri_on_left:
        d = lambda p: jnp.dot(tri, p, preferred_element_type=F32)
    else:
        d = lambda p: jnp.dot(p, tri, preferred_element_type=F32)
    return d(h) + d(m) + d(l)


def _mlstm_step_t(inputs, c_st, n_st, m_st):
    t = hd = LANES
    n_dir, n_batch = len(inputs), len(inputs[0])
    n_grp = n_dir * n_batch * ML_HEADS
    row = lax.broadcasted_iota(jnp.int32, (t, t), 0)
    col = lax.broadcasted_iota(jnp.int32, (t, t), 1)
    tri = jnp.where(col <= row, 1.0, 0.0).astype(BF16)
    tri_t = jnp.where(col >= row, 1.0, 0.0).astype(BF16)
    qts, kts, ks, vts, ecs, brs, irs = [], [], [], [], [], [], []
    for d in range(n_dir):
        i_off = 2 * ML_HEADS * d
        f_off = i_off + ML_HEADS
        for b in range(n_batch):
            qt_all, kt_all, k_all, vt_all, g = inputs[d][b]
            gt = g.T
            lf_c = _log_sigmoid(g)
            lf_r = lf_c.T
            if d == 1:
                b_c = _exact_tri_dot(tri_t, lf_c, True)
                b_r = _exact_tri_dot(tri, lf_r, False)
            else:
                b_c = _exact_tri_dot(tri, lf_c, True)
                b_r = _exact_tri_dot(tri_t, lf_r, False)
            for h in range(ML_HEADS):
                sl = slice(h * hd, (h + 1) * hd)
                qts.append(qt_all[sl])
                kts.append(kt_all[sl])
                vts.append(vt_all[sl])
                ks.append(k_all[:, sl])
                ecs.append(g[:, i_off + h:i_off + h + 1] - b_c[:, f_off + h:f_off + h + 1])
                brs.append(b_r[f_off + h:f_off + h + 1, :])
                irs.append(gt[i_off + h:i_off + h + 1, :])
    qt = jnp.stack(qts) * (hd ** -0.5)
    kt, vt, k = jnp.stack(kts), jnp.stack(vts), jnp.stack(ks)
    e_col, br, ir = jnp.stack(ecs), jnp.stack(brs), jnp.stack(irs)
    m_prev = m_st[...][:, :, :1]
    c_prev = c_st[...]
    n_prev = n_st[...]

    shp = (n_grp, t, t)
    grp = lax.broadcasted_iota(jnp.int32, shp, 0)
    s3, t3 = lax.broadcasted_iota(jnp.int32, shp, 1), lax.broadcasted_iota(jnp.int32, shp, 2)
    back = grp >= (n_grp // n_dir)
    mask = (back & (s3 >= t3)) | (jnp.logical_not(back) & (s3 <= t3))
    bdot = lambda a, b_, ca, cb: lax.dot_general(a.astype(BF16), b_.astype(BF16), (((ca,), (cb,)), ((0,), (0,))),
                                                 preferred_element_type=F32)
    e_st = jnp.where(mask, e_col, -jnp.inf)
    mm = jnp.maximum(m_prev, jnp.max(e_st, axis=1, keepdims=True))
    m_row = br + mm
    w_intra = jnp.exp(e_st - mm)
    w_inter = jnp.exp(m_prev - mm)
    s_t = bdot(k, qt, 2, 1) * w_intra
    num = bdot(vt, s_t, 2, 1) + w_inter * bdot(c_prev, qt, 2, 1)
    den = jnp.sum(s_t, axis=1, keepdims=True) + w_inter * bdot(n_prev, qt, 2, 1)
    den = jnp.maximum(jnp.abs(den), jnp.exp(-m_row))
    h_all = num / den

    is_back = lax.broadcasted_iota(jnp.int32, (n_grp, 1, 1), 0) >= (n_grp // n_dir)
    b_tot = jnp.where(is_back, br[:, :, :1], br[:, :, t - 1:])
    a_r = b_tot - br + ir
    m_new = jnp.maximum(b_tot + m_prev, jnp.max(a_r, axis=-1, keepdims=True))
    sc = jnp.exp(a_r - m_new)
    decay = jnp.exp(b_tot + m_prev - m_new)
    c_st[...] = decay * c_prev + bdot(vt * sc, kt, 2, 2)
    n_st[...] = decay * n_prev + bdot(sc, k, 2, 1)
    m_st[...] = jnp.broadcast_to(m_new, (n_grp, 1, LANES))

    out = []
    for d in range(n_dir):
        out.append([jnp.concatenate([h_all[(d * n_batch + b) * ML_HEADS + h] for h in range(ML_HEADS)], axis=0)
                    for b in range(n_batch)])
    return out


def _mlstm_kernel_t(*refs, n_batch, ctx_chunks):
    lat_f, lat_b, ctx_f, ctx_b = refs[0:5], refs[5:10], refs[10:15], refs[15:20]
    hf_lat, hb_lat, hf_ctx, hb_ctx, c_st, n_st, m_st = refs[20:]
    j = pl.program_id(0)
    is_ctx = j < ctx_chunks

    @pl.when(j == 0)
    def _():
        c_st[...] = jnp.zeros(c_st.shape, F32)
        n_st[...] = jnp.zeros(n_st.shape, F32)
        m_st[...] = jnp.zeros(m_st.shape, F32)

    def pick(c_refs, l_refs, b):
        ld = lambda r: r[b, 0] if len(r.shape) == 4 else r[b]
        return tuple(jnp.where(is_ctx, ld(c), ld(l)) for c, l in zip(c_refs, l_refs))

    inputs = [[pick(ctx_f, lat_f, b) for b in range(n_batch)], [pick(ctx_b, lat_b, b) for b in range(n_batch)]]
    hf, hb = _mlstm_step_t(inputs, c_st, n_st, m_st)

    @pl.when(is_ctx)
    def _():
        for b in range(n_batch):
            hf_ctx[b, 0] = hf[b]
            hb_ctx[b, 0] = hb[b]

    @pl.when(jnp.logical_not(is_ctx))
    def _():
        for b in range(n_batch):
            hf_lat[b, 0] = hf[b]
            hb_lat[b, 0] = hb[b]


def _mlstm_t(qk_s, k_tm, z_s, z_tm, *, n_batch, v_chan, g_col):
    nlc, ncc = qk_s["lat"].shape[0] // n_batch, qk_s["ctx"].shape[0] // n_batch
    r4 = lambda a: a.reshape(n_batch, a.shape[0] // n_batch, a.shape[1], a.shape[2])
    r3 = lambda a: a.reshape(n_batch, a.shape[0] // n_batch, a.shape[1])
    lf = lambda j: jnp.maximum(j - ncc, 0)
    lb = lambda j: jnp.where(j < ncc, nlc - 1, nlc - 1 - (j - ncc))
    cf = lambda j: jnp.minimum(j, ncc - 1)
    cb = lambda j: jnp.where(j < ncc, ncc - 1 - j, 0)
    slab = lambda ix, cidx: pl.BlockSpec((n_batch, 1, D_BR, LANES), lambda j: (0, ix(j), cidx, 0))
    rows = lambda w, ix, cidx: pl.BlockSpec((n_batch, LANES, w), lambda j: (0, ix(j), cidx))
    specs = lambda ix: [slab(ix, 0), slab(ix, 1), rows(D_BR, ix, 0), slab(ix, v_chan // D_BR), rows(LANES, ix, g_col)]
    args = lambda s: [r4(qk_s[s]), r4(qk_s[s]), r3(k_tm[s]), r4(z_s[s]), r3(z_tm[s])]
    n_grp = 2 * n_batch * ML_HEADS
    sd = lambda n: jax.ShapeDtypeStruct((n_batch, n, D_BR, LANES), F32)
    hf_lat, hb_lat, hf_ctx, hb_ctx = pl.pallas_call(
        functools.partial(_mlstm_kernel_t, n_batch=n_batch, ctx_chunks=ncc),
        grid=(ncc + nlc,),
        in_specs=specs(lf) + specs(lb) + specs(cf) + specs(cb),
        out_specs=[slab(lf, 0), slab(lb, 0), slab(cf, 0), slab(cb, 0)],
        out_shape=[sd(nlc), sd(nlc), sd(ncc), sd(ncc)],
        scratch_shapes=[pltpu.VMEM((n_grp, LANES, LANES), F32), pltpu.VMEM((n_grp, 1, LANES), F32),
                        pltpu.VMEM((n_grp, 1, LANES), F32)],
        compiler_params=_params(("arbitrary",)),
        name="mlstm_bidir",
    )(*(args("lat") + args("lat") + args("ctx") + args("ctx")))
    flat = lambda a: a.reshape(a.shape[0] * a.shape[1], D_BR, LANES)
    return {"lat": (flat(hf_lat), flat(hb_lat)), "ctx": (flat(hf_ctx), flat(hb_ctx))}


def _rms_mod(x, w, shift, scale):
    y = x * lax.rsqrt(jnp.mean(x * x, axis=-1, keepdims=True) + EPS) * w
    return y * (1.0 + scale) + shift


def _route(t, rw, rb):
    logits = _dot_f32ish3(t, rw) + rb
    col = lax.broadcasted_iota(jnp.int32, logits.shape, 1)
    big = jnp.int32(1 << 20)
    ninf = -jnp.inf
    is_g = col < MOE_GROUPS
    gl = jnp.where(is_g, logits, ninf)
    gmax = jnp.max(gl, axis=-1, keepdims=True)
    g_sel = jnp.min(jnp.where(is_g & (gl == gmax), col, big), axis=-1, keepdims=True)
    p_top = 1.0 / jnp.sum(jnp.where(is_g, jnp.exp(gl - gmax), 0.0), axis=-1, keepdims=True)
    lo = MOE_GROUPS + g_sel * MOE_PER_GROUP
    in_grp = (col >= lo) & (col < lo + MOE_PER_GROUP)
    e1v = jnp.where(in_grp, logits, ninf)
    top1 = jnp.max(e1v, axis=-1, keepdims=True)
    idx1 = jnp.min(jnp.where(in_grp & (e1v == top1), col, big), axis=-1, keepdims=True)
    e2v = jnp.where(col == idx1, ninf, e1v)
    top2 = jnp.max(e2v, axis=-1, keepdims=True)
    idx2 = jnp.min(jnp.where(in_grp & (col != idx1) & (e2v == top2), col, big), axis=-1, keepdims=True)
    ex = jnp.exp(top2 - top1)
    s1 = 1.0 / (1.0 + ex)
    return jnp.where(col == idx1, p_top * s1, 0.0) + jnp.where(col == idx2, p_top * (ex * s1), 0.0)


def _merge_kernel(yh_ref, yf_ref, h_ref, o_ref, g0_ref, g1_ref, g2_ref, x_ref, gate_ref,
                  wb_ref, wo_ref, nw_ref, n2_ref, sh_ref, sc_ref, rw_ref, rb_ref, out_ref, xn_ref, comb_ref):
    hd = LANES
    h = h_ref[...]
    parts = []
    for i in range(ML_HEADS):
        hh = h[:, i * hd:(i + 1) * hd]
        parts.append(hh * lax.rsqrt(jnp.mean(hh * hh, axis=-1, keepdims=True) + EPS))
    y_ml = jax.nn.sigmoid(o_ref[...]) * (jnp.concatenate(parts, axis=1) * nw_ref[...])
    acc = g0_ref[...].astype(F32) * _bdot(yh_ref[...], wb_ref[0])
    acc = acc + g1_ref[...].astype(F32) * _bdot(yf_ref[...], wb_ref[1])
    acc = acc + g2_ref[...].astype(F32) * _bdot(y_ml, wb_ref[2])
    x_new = x_ref[...] + gate_ref[...] * _bdot(acc, wo_ref[...])
    out_ref[...] = x_new
    t = _rms_mod(x_new, n2_ref[...], sh_ref[...], sc_ref[...])
    xn_ref[...] = t.astype(BF16)
    comb_ref[...] = _route(t, rw_ref[...], rb_ref[...])


def _merge(yh, yf, h, z_tm, gates, x, mods3, wb, wo, nw, n2w, rw, rb, *, seg, tm, o_col):
    nt, d = x.shape
    tok = lambda w, cidx: pl.BlockSpec((tm, w), lambda i: (i, cidx))
    mod = lambda k: pl.BlockSpec((None, 1, d), lambda i: (seg(i), 0, k))
    return pl.pallas_call(
        _merge_kernel,
        grid=(nt // tm,),
        in_specs=[tok(D_BR, 0), tok(D_BR, 0), tok(D_BR, 0),
                  tok(D_BR, o_col),
                  tok(d, 0), tok(d, 1), tok(d, 2),
                  tok(d, 0),
                  mod(2),
                  pl.BlockSpec((3, D_BR, d), lambda i: (0, 0, 0)),
                  pl.BlockSpec((d, d), lambda i: (0, 0)),
                  pl.BlockSpec((1, D_BR), lambda i: (0, 0)),
                  pl.BlockSpec((1, d), lambda i: (0, 0)),
                  mod(3), mod(4),
                  pl.BlockSpec((d, LANES), lambda i: (0, 0)),
                  pl.BlockSpec((1, LANES), lambda i: (0, 0))],
        out_specs=[tok(d, 0), tok(d, 0), tok(LANES, 0)],
        out_shape=[jax.ShapeDtypeStruct((nt, d), F32), jax.ShapeDtypeStruct((nt, d), BF16),
                   jax.ShapeDtypeStruct((nt, LANES), F32)],
        compiler_params=_params(("arbitrary",)),
        name="merge_branches_router",
    )(yh, yf, h, z_tm, gates, gates, gates, x, mods3, wb, wo, nw.reshape(1, D_BR), n2w.reshape(1, d),
      mods3, mods3, rw, rb)


def _moe_kernel(xn_ref, comb_ref, wg_ref, wu_ref, wd_ref, x_ref, gate_ref, nw_ref, sh_ref, sc_ref, *out_and_scratch,
                final, n_keep):
    acc_ref = out_and_scratch[-1]
    e = pl.program_id(1)

    @pl.when(e == 0)
    def _():
        acc_ref[...] = jnp.zeros(acc_ref.shape, F32)

    xn = xn_ref[...]
    comb = comb_ref[...]
    col = lax.broadcasted_iota(jnp.int32, comb.shape, 1)
    acts = []
    for i in range(MOE_PER_GROUP):
        cw = jnp.sum(jnp.where(col == e * MOE_PER_GROUP + i + MOE_GROUPS, comb, 0.0), axis=-1, keepdims=True)
        hg = jnp.dot(xn, wg_ref[i], preferred_element_type=F32)
        hu = jnp.dot(xn, wu_ref[i], preferred_element_type=F32)
        acts.append(((hg * jax.nn.sigmoid(hg)) * hu * cw).astype(BF16))
    wd = wd_ref[...].reshape(MOE_PER_GROUP * EXPERT_HID, wd_ref.shape[-1])
    acc_ref[...] += jnp.dot(jnp.concatenate(acts, axis=1), wd, preferred_element_type=F32)

    n_steps = MOE_EXPERTS // MOE_PER_GROUP
    if final:
        y_ref, = out_and_scratch[:-1]

        @pl.when((e == n_steps - 1) & (pl.program_id(0) < n_keep))
        def _():
            x_new = x_ref[...] + gate_ref[...] * acc_ref[...]
            y_ref[...] = x_new * lax.rsqrt(jnp.mean(x_new * x_new, axis=-1, keepdims=True) + EPS) * nw_ref[...]
    else:
        o_ref, xn_next_ref = out_and_scratch[:-1]

        @pl.when(e == n_steps - 1)
        def _():
            x_new = x_ref[...] + gate_ref[...] * acc_ref[...]
            o_ref[...] = x_new
            xn_next_ref[...] = _rms_mod(x_new, nw_ref[...], sh_ref[...], sc_ref[...]).astype(BF16)


def _moe(xn, comb, wg, wu, wd, x, mods3, post_w, post_mods3, *, seg, tm, final, n_keep_rows):
    nt, d = x.shape
    n_keep = n_keep_rows // tm
    tok = pl.BlockSpec((tm, d), lambda i, e: (i, 0))
    if final:
        out_specs = [pl.BlockSpec((tm, d), lambda i, e: (jnp.minimum(i, n_keep - 1), 0))]
        out_shape = [jax.ShapeDtypeStruct((n_keep_rows, d), F32)]
    else:
        out_specs = [tok, tok]
        out_shape = [jax.ShapeDtypeStruct((nt, d), F32), jax.ShapeDtypeStruct((nt, d), BF16)]
    mod = lambda k: pl.BlockSpec((None, 1, d), lambda i, e: (seg(i), 0, k))
    return pl.pallas_call(
        functools.partial(_moe_kernel, final=final, n_keep=n_keep),
        grid=(nt // tm, MOE_EXPERTS // MOE_PER_GROUP),
        in_specs=[tok,
                  pl.BlockSpec((tm, LANES), lambda i, e: (i, 0)),
                  pl.BlockSpec((MOE_PER_GROUP, d, EXPERT_HID), lambda i, e: (e, 0, 0)),
                  pl.BlockSpec((MOE_PER_GROUP, d, EXPERT_HID), lambda i, e: (e, 0, 0)),
                  pl.BlockSpec((MOE_PER_GROUP, EXPERT_HID, d), lambda i, e: (e, 0, 0)),
                  tok,
                  mod(5),
                  pl.BlockSpec((1, d), lambda i, e: (0, 0)),
                  mod(0), mod(1)],
        out_specs=out_specs,
        out_shape=out_shape,
        scratch_shapes=[pltpu.VMEM((tm, d), F32)],
        compiler_params=_params(("arbitrary", "arbitrary")),
        name="moe_experts_final" if final else "moe_experts",
    )(xn, comb, wg, wu, wd, x, mods3, post_w.reshape(1, d), post_mods3, post_mods3)


def _slab_to_tm(y_s):
    ns, c, _ = y_s.shape
    return jnp.transpose(y_s, (0, 2, 1)).reshape(ns * LANES, c)


def kernel(x, c, ctx, c_ctx, ada_w, ada_b, norm1_w, norm2_w, w_in, b_in, hy_conv_w, hy_conv_b, hy_f_w1, hy_f_b1, hy_f_w2, hy_f_b2, hy_f_w3, hy_f_freq, hy_decay, hy_skip, ml_conv_w, ml_conv_b, ml_norm_w, w_branch, w_out, moe_rg_w, moe_rg_b, moe_re_w, moe_re_b, moe_w_gate, moe_w_up, moe_w_down, norm_f_w):
    nb, seq, d = x.shape
    lc = ctx.shape[1]
    depth = ada_w.shape[0]
    assert d == D_MODEL and seq % (GRID_W * 2) == 0 and lc % LANES == 0 and nb + 1 <= 8
    rows = seq // GRID_W
    a_lat = seq // LANES
    a_ctx = lc // LANES
    n_lat, n_ctx = nb * seq, nb * lc
    tm = 256
    tm_moe = {"lat": _pick(seq, (1024, 512, 256)), "ctx": _pick(n_ctx, (512, 256))}
    tm_mrg = {"lat": _pick(seq, (512, 256)), "ctx": _pick(n_ctx, (512, 256))}
    assert seq % tm == 0 and n_ctx % tm == 0
    seg_of = lambda s, t: (lambda i: i // (seq // t)) if s == "lat" else (lambda i: nb)
    streams = ("lat", "ctx")
    xs = {"lat": x.reshape(n_lat, d), "ctx": ctx.reshape(n_ctx, d)}
    xn = {}
    cvec = jnp.zeros((8, d), F32).at[:nb].set(c).at[nb].set(c_ctx)
    mods = _mods(cvec, ada_w, ada_b)

    o_fn, o_ml, o_mlg, o_gate = 3 * D_BR, 4 * D_BR, 8 * D_BR, 8 * D_BR + 4 * ML_HEADS
    pad_g = LANES - 4 * ML_HEADS

    for l in range(depth):
        lp = {"hy_f_w1": hy_f_w1[l], "hy_f_b1": hy_f_b1[l], "hy_f_w2": hy_f_w2[l], "hy_f_b2": hy_f_b2[l],
              "hy_f_w3": hy_f_w3[l], "hy_f_freq": hy_f_freq[l], "hy_decay": hy_decay[l]}
        mods3 = mods[l].reshape(8, 1, 6 * d)
        wl, bl = w_in[l], b_in[l]
        w_pq, b_pq = _fn_fold(wl[:, o_fn:o_ml], bl[o_fn:o_ml])
        o_v, o_o = o_ml + 2 * D_BR, o_ml + 3 * D_BR
        w_cm = jnp.concatenate([wl[:, :o_fn], wl[:, o_ml:o_v], w_pq, wl[:, o_v:o_o]], axis=1)
        b_cm = jnp.concatenate([bl[:o_fn], bl[o_ml:o_v], b_pq, bl[o_v:o_o]])
        w_tm = jnp.concatenate([wl[:, o_o:o_mlg], wl[:, o_mlg:o_gate], jnp.zeros((d, pad_g), F32)], axis=1)
        b_tm = jnp.concatenate([bl[o_o:o_mlg], bl[o_mlg:o_gate], jnp.zeros((pad_g,), F32)])
        c_hy, c_qk, c_fn, c_v = 0, 3 * D_BR, 5 * D_BR, 7 * D_BR
        g_col = D_BR // LANES

        last = l + 1 == depth
        live = ("lat",) if last else streams
        if l == 0:
            xn = {s: _norm_mod(xs[s], norm1_w[l], mods3, 0, 1, seg_of(s, tm), tm) for s in streams}
        w_tm_b, w_cm_t = w_tm.astype(BF16), w_cm.T.astype(BF16)
        z_tm = {s: _mm_tm(xn[s], w_tm_b, b_tm, gate=False) for s in streams}
        w_gate_b = wl[:, o_gate:].astype(BF16)
        gates = {s: _mm_tm(xn[s], w_gate_b, bl[o_gate:], gate=True) for s in live}
        z_s = {s: _mm_slab(xn[s], w_cm_t, b_cm) for s in streams}

        grid_kw = {"lat": dict(rows=rows, width=GRID_W), "ctx": dict(rows=1, width=lc)}
        hy_w, hy_b = hy_conv_w[l].reshape(9, 3 * D_BR), hy_conv_b[l]
        ml_w, ml_b = ml_conv_w[l].reshape(9, 2 * D_BR), ml_conv_b[l]
        conv = lambda s, w, b, lo, n, act: _conv(z_s[s], w, b, chan_lo=lo, chan_n=n, silu=act, n_batch=nb,
                                                 slab0=0, **grid_kw[s])
        u = {s: conv(s, hy_w, hy_b, c_hy, 3 * D_BR, False) for s in live}
        qk = {s: conv(s, ml_w, ml_b, c_qk, 2 * D_BR, True) for s in streams}
        k_tm = {s: _slab_to_tm(qk[s][:, D_BR:]).astype(BF16) for s in streams}
        h_s = _mlstm_t(qk, k_tm, z_s, z_tm, n_batch=nb, v_chan=c_v, g_col=g_col)
        h_sum = {s: _slab_to_tm(h_s[s][0] + h_s[s][1]) for s in live}

        yh, yf = {}, {}
        yh["lat"] = _slab_to_tm(_hyena(u["lat"], _hyena_taps(seq, 2 * a_lat, lp), hy_skip[l],
                                       a_in=a_lat, na=2 * a_lat, n_batch=nb)).astype(BF16)
        yk = _fn_seq(z_s["lat"], a_n=a_lat, n_batch=nb, chan_lo=c_fn)
        yf["lat"] = jnp.transpose(yk, (0, 3, 2, 1)).reshape(n_lat, D_BR).astype(BF16)
        if not last:
            yh["ctx"] = _hyena_short(u["ctx"], _hyena_taps(lc, 2 * a_ctx, lp), hy_skip[l],
                                     a_n=a_ctx, n_batch=nb).reshape(n_ctx, D_BR)
            yf["ctx"] = _fn_small(z_s["ctx"], a_n=a_ctx, n_batch=nb, chan_lo=c_fn).reshape(n_ctx, D_BR)

        rw = jnp.concatenate([moe_rg_w[l], moe_re_w[l], jnp.zeros((d, LANES - MOE_GROUPS - MOE_EXPERTS), F32)], axis=1)
        rb = jnp.concatenate([moe_rg_b[l], moe_re_b[l], jnp.zeros((LANES - MOE_GROUPS - MOE_EXPERTS,), F32)]).reshape(1, LANES)
        wb, wo = w_branch[l].astype(BF16), w_out[l].astype(BF16)
        experts = (moe_w_gate[l].astype(BF16), moe_w_up[l].astype(BF16), moe_w_down[l].astype(BF16))
        for s in live:
            xs[s], xn2, comb = _merge(yh[s], yf[s], h_sum[s], z_tm[s], gates[s], xs[s], mods3, wb, wo,
                                      ml_norm_w[l], norm2_w[l], rw, rb, seg=seg_of(s, tm_mrg[s]), tm=tm_mrg[s],
                                      o_col=0)
            moe_kw = dict(seg=seg_of(s, tm_moe[s]), tm=tm_moe[s], n_keep_rows=xs[s].shape[0])
            if last:
                out, = _moe(xn2, comb, *experts, xs[s], mods3, norm_f_w, mods3, final=True, **moe_kw)
            else:
                xs[s], xn[s] = _moe(xn2, comb, *experts, xs[s], mods3, norm1_w[l + 1],
                                    mods[l + 1].reshape(8, 1, 6 * d), final=False, **moe_kw)

    return out.reshape(nb, seq, d)
```
